```python
import math
import numpy as np
import jax
import jax.numpy as jnp
from jax import lax

D_MODEL = 1024
BATCH = 2
SEQ = 16384
DEPTH = 2

GRID_W = 64
CTX_LEN = 256
EPS = 1e-6
Q_BLOCK = 128
ROPE_THETA = 10000.0

HEAD_DIM = 64
AXIS_DIM = HEAD_DIM // 2

A_HEADS = 8
A_KV_HEADS = 2
A_GROUP = A_HEADS // A_KV_HEADS

DN_HEADS = 8
DN_DK = 64
DN_DV = 64
DN_CHUNK = 64
DN_CONV = 3

DF_HEADS = 4
DF_HD = 64
DF_VD = 2 * DF_HD

N_EXPERTS = 64
TOP_K = 6
N_GROUPS = 8
TOPK_GROUPS = 4
MOE_FF = 256
ROUTED_SCALE = 2.5
MOE_BLOCK = 128

A_QW = A_HEADS * HEAD_DIM
A_KVW = A_KV_HEADS * HEAD_DIM
DN_QKVW = DN_HEADS * (2 * DN_DK + DN_DV)
DN_ZW = DN_HEADS * DN_DV
DN_GW = 2 * DN_HEADS
DF_QKW = DF_HEADS * 2 * DF_HD
DF_VW = DF_HEADS * DF_VD
GATE_W = 3 * D_MODEL
PROJ_SPLITS = (A_QW, A_KVW, A_KVW, DN_QKVW, DN_ZW, DN_GW, DN_GW, DF_QKW, DF_QKW, DF_VW, GATE_W)
IN_W = A_QW + 2 * A_KVW + DN_QKVW + DN_ZW + 2 * DN_GW + 2 * DF_QKW + DF_VW + GATE_W
BR_W = 512

kernel_name = "hybrid_gqa_deltanet_diffattn_moe_dit"


def _rmsnorm(x, g):
    xf = x.astype(jnp.float32)
    y = xf * lax.rsqrt(jnp.mean(xf * xf, axis=-1, keepdims=True) + EPS)
    return (y * g.astype(jnp.float32)).astype(x.dtype)


def _l2norm(x):
    xf = x.astype(jnp.float32)
    return (xf * lax.rsqrt(jnp.sum(xf * xf, axis=-1, keepdims=True) + EPS)).astype(x.dtype)


def _axial_rope(n_tokens, dtype):
    rows = n_tokens // GRID_W
    row = jnp.repeat(jnp.arange(rows, dtype=jnp.float32), GRID_W)
    col = (jnp.arange(n_tokens) % GRID_W).astype(jnp.float32)
    inv = 1.0 / (ROPE_THETA ** (jnp.arange(0, AXIS_DIM, 2, dtype=jnp.float32) / AXIS_DIM))
    ang = jnp.concatenate([row[:, None] * inv, col[:, None] * inv], axis=-1)
    ang = jnp.concatenate([ang, ang], axis=-1)
    return jnp.cos(ang).astype(dtype), jnp.sin(ang).astype(dtype)


def _apply_rope(x, cos, sin):
    shape = (1, x.shape[1]) + (1,) * (x.ndim - 3) + (x.shape[-1],)
    cos = cos.reshape(shape)
    sin = sin.reshape(shape)
    x1, x2 = jnp.split(x, 2, axis=-1)
    return x * cos + jnp.concatenate([-x2, x1], axis=-1) * sin


def _sweep_query_blocks(fn, q):
    B, T = q.shape[:2]
    nb = T // Q_BLOCK
    qb = jnp.moveaxis(q.reshape((B, nb, Q_BLOCK) + q.shape[2:]), 1, 0)
    ob = lax.map(fn, qb)
    return jnp.moveaxis(ob, 0, 1).reshape((B, T) + ob.shape[3:])


def _gqa_attend(q, k, v):
    s = jnp.einsum("bqhgd,bkhd->bhgqk", q, k, preferred_element_type=jnp.float32) * (HEAD_DIM ** -0.5)
    p = jax.nn.softmax(s, axis=-1).astype(v.dtype)
    return jnp.einsum("bhgqk,bkhd->bqhgd", p, v)


def _diff_attend(q, k, v, lam):
    s = jnp.einsum("bqhcd,bkhcd->bhcqk", q, k, preferred_element_type=jnp.float32) * (DF_HD ** -0.5)
    p = jax.nn.softmax(s, axis=-1)
    a = (p[:, :, 0] - lam * p[:, :, 1]).astype(v.dtype)
    return jnp.einsum("bhqk,bkhe->bqhe", a, v)


def _short_conv(x, w):
    T = x.shape[1]
    pad = DN_CONV // 2
    xp = jnp.pad(x, ((0, 0), (pad, DN_CONV - 1 - pad), (0, 0)))
    out = xp[:, 0:T] * w[0]
    for j in range(1, DN_CONV):
        out = out + xp[:, j:j + T] * w[j]
    return out


def _gated_delta_rule(q, k, v, g, beta, s0):
    out_dtype = v.dtype
    B, T, H, dk = q.shape
    dv = v.shape[-1]
    n = T // DN_CHUNK
    f32 = jnp.float32

    def chunks(t):
        t = t.astype(f32).reshape((B, n, DN_CHUNK, H) + t.shape[3:])
        return jnp.moveaxis(t, 3, 1)

    q = chunks(q) * (dk ** -0.5)
    k = chunks(k)
    v = chunks(v)
    g = jnp.cumsum(chunks(g), axis=-1)
    beta = chunks(beta)
    kb = k * beta[..., None]
    vb = v * beta[..., None]
    causal = jnp.tril(jnp.ones((DN_CHUNK, DN_CHUNK), bool))
    strict = jnp.tril(jnp.ones((DN_CHUNK, DN_CHUNK), bool), -1)
    decay = jnp.exp(jnp.where(causal, g[..., :, None] - g[..., None, :], -jnp.inf))
    lower = jnp.where(strict, jnp.einsum("bhncd,bhnsd->bhncs", kb, k) * decay, 0.0)
    eye = jnp.eye(DN_CHUNK, dtype=f32)
    tinv = lax.linalg.triangular_solve(lower + eye, jnp.broadcast_to(eye, lower.shape),
                                       left_side=True, lower=True, unit_diagonal=True)
    vals = tinv @ vb
    wk = tinv @ (kb * jnp.exp(g)[..., None])
    qk = jnp.einsum("bhncd,bhnsd->bhncs", q, k) * decay
    q_dec = q * jnp.exp(g)[..., None]
    k_dec = k * jnp.exp(g[..., -1:] - g)[..., None]
    g_last = jnp.exp(g[..., -1])
    xs = tuple(jnp.moveaxis(t, 2, 0) for t in (q_dec, k_dec, vals, wk, qk, g_last))

    def step(S, inp):
        q_i, k_i, u_i, w_i, qk_i, gl_i = inp
        v_new = u_i - w_i @ S
        o_i = q_i @ S + qk_i @ v_new
        S = S * gl_i[..., None, None] + jnp.swapaxes(k_i, -1, -2) @ v_new
        return S, o_i

    S, o = lax.scan(step, s0, xs)
    o = jnp.moveaxis(jnp.moveaxis(o, 0, 2), 1, 3).reshape(B, T, H, dv)
    return o.astype(out_dtype), S


def _split_proj(p):
    idx = np.cumsum(PROJ_SPLITS)[:-1].tolist()
    return jnp.split(p, idx, axis=-1)


def _mixer_gqa(q_c, k_c, v_c, q_l, k_l, v_l, cos, sin, g_q, g_k, with_ctx):
    B, S = q_l.shape[:2]
    n_c = k_c.shape[1]
    q_l = _apply_rope(_rmsnorm(q_l.reshape(B, S, A_KV_HEADS, A_GROUP, HEAD_DIM), g_q), cos, sin)
    k_l = _apply_rope(_rmsnorm(k_l.reshape(B, S, A_KV_HEADS, HEAD_DIM), g_k), cos, sin)
    v_l = v_l.reshape(B, S, A_KV_HEADS, HEAD_DIM)
    k_c = _rmsnorm(k_c.reshape(B, n_c, A_KV_HEADS, HEAD_DIM), g_k)
    v_c = v_c.reshape(B, n_c, A_KV_HEADS, HEAD_DIM)
    k_all = jnp.concatenate([k_c, k_l], axis=1)
    v_all = jnp.concatenate([v_c, v_l], axis=1)
    o_l = _sweep_query_blocks(lambda qb: _gqa_attend(qb, k_all, v_all), q_l).reshape(B, S, A_QW)
    if not with_ctx:
        return None, o_l
    q_c = _rmsnorm(q_c.reshape(B, n_c, A_KV_HEADS, A_GROUP, HEAD_DIM), g_q)
    return _gqa_attend(q_c, k_c, v_c).reshape(B, n_c, A_QW), o_l


def _dn_prepare(qkv, a, b, w_conv, a_log, dt_bias):
    B, T = qkv.shape[:2]
    qkv = jax.nn.silu(_short_conv(qkv, w_conv))
    q, k, v = jnp.split(qkv, [DN_HEADS * DN_DK, 2 * DN_HEADS * DN_DK], axis=-1)
    q = _l2norm(q.reshape(B, T, DN_HEADS, DN_DK))
    k = _l2norm(k.reshape(B, T, DN_HEADS, DN_DK))
    v = v.reshape(B, T, DN_HEADS, DN_DV)
    a = a.astype(jnp.float32).reshape(B, T, 2, DN_HEADS)
    b = b.astype(jnp.float32).reshape(B, T, 2, DN_HEADS)
    g = -jnp.exp(a_log.astype(jnp.float32)) * jax.nn.softplus(a + dt_bias.astype(jnp.float32))
    return q, k, v, g, jax.nn.sigmoid(b)


def _flip(t):
    return jnp.flip(t, axis=1)


def _mixer_deltanet(qkv_c, z_c, a_c, b_c, qkv_l, z_l, a_l, b_l, w_conv, a_log, dt_bias, g_out, with_ctx):
    qc, kc, vc, gc, bc = _dn_prepare(qkv_c, a_c, b_c, w_conv, a_log, dt_bias)
    ql, kl, vl, gl, bl = _dn_prepare(qkv_l, a_l, b_l, w_conv, a_log, dt_bias)
    s0 = jnp.zeros((qc.shape[0], DN_HEADS, DN_DK, DN_DV), jnp.float32)
    oc_f, s_f = _gated_delta_rule(qc, kc, vc, gc[:, :, 0], bc[:, :, 0], s0)
    ol_f, _ = _gated_delta_rule(ql, kl, vl, gl[:, :, 0], bl[:, :, 0], s_f)
    oc_b, s_b = _gated_delta_rule(_flip(qc), _flip(kc), _flip(vc), _flip(gc[:, :, 1]), _flip(bc[:, :, 1]), s0)
    ol_b, _ = _gated_delta_rule(_flip(ql), _flip(kl), _flip(vl), _flip(gl[:, :, 1]), _flip(bl[:, :, 1]), s_b)

    def gated_out(o, z):
        B, T = o.shape[:2]
        zz = z.reshape(B, T, DN_HEADS, DN_DV)
        return (_rmsnorm(o, g_out) * jax.nn.silu(zz)).reshape(B, T, DN_ZW)

    o_l = gated_out(ol_f + _flip(ol_b), z_l)
    if not with_ctx:
        return None, o_l
    return gated_out(oc_f + _flip(oc_b), z_c), o_l


def _mixer_diff(q_c, k_c, v_c, q_l, k_l, v_l, cos, sin, lam_qk, g_subln, lambda_init, with_ctx):
    B, S = q_l.shape[:2]
    n_c = k_c.shape[1]
    lq = lam_qk.astype(jnp.float32)
    lam = jnp.exp(jnp.sum(lq[0] * lq[1])) - jnp.exp(jnp.sum(lq[2] * lq[3])) + lambda_init
    q_l = _apply_rope(q_l.reshape(B, S, DF_HEADS, 2, DF_HD), cos, sin)
    k_l = _apply_rope(k_l.reshape(B, S, DF_HEADS, 2, DF_HD), cos, sin)
    v_l = v_l.reshape(B, S, DF_HEADS, DF_VD)
    k_c = k_c.reshape(B, n_c, DF_HEADS, 2, DF_HD)
    v_c = v_c.reshape(B, n_c, DF_HEADS, DF_VD)
    k_all = jnp.concatenate([k_c, k_l], axis=1)
    v_all = jnp.concatenate([v_c, v_l], axis=1)

    def finish(o):
        return (_rmsnorm(o, g_subln) * (1.0 - lambda_init)).reshape(o.shape[:2] + (DF_VW,))

    o_l = finish(_sweep_query_blocks(lambda qb: _diff_attend(qb, k_all, v_all, lam), q_l))
    if not with_ctx:
        return None, o_l
    q_c = q_c.reshape(B, n_c, DF_HEADS, 2, DF_HD)
    return finish(_diff_attend(q_c, k_c, v_c, lam)), o_l


def _merge(br_a, br_b, br_c, gates, w_branch, w_out):
    g_a, g_b, g_c = jnp.split(gates, 3, axis=-1)
    m = (jax.nn.sigmoid(g_a) * (br_a @ w_branch[0])
         + jax.nn.sigmoid(g_b) * (br_b @ w_branch[1])
         + jax.nn.sigmoid(g_c) * (br_c @ w_branch[2]))
    return m @ w_out


def _moe(u, w_router, b_router, w_e_gate, w_e_up, w_e_down, w_s_gate, w_s_up, w_s_down):
    N, D = u.shape
    scores = jax.nn.sigmoid(jnp.dot(u, w_router, preferred_element_type=jnp.float32))
    sel = scores + b_router.astype(jnp.float32)
    grouped = sel.reshape(N, N_GROUPS, N_EXPERTS // N_GROUPS)
    group_score = jnp.sum(lax.top_k(grouped, 2)[0], axis=-1)
    gidx = lax.top_k(group_score, TOPK_GROUPS)[1]
    gmask = jnp.sum(jax.nn.one_hot(gidx, N_GROUPS, dtype=jnp.float32), axis=1) > 0
    emask = jnp.repeat(gmask, N_EXPERTS // N_GROUPS, axis=1)
    eidx = lax.top_k(jnp.where(emask, sel, -jnp.inf), TOP_K)[1]
    wts = jnp.take_along_axis(scores, eidx, axis=1)
    wts = wts / (jnp.sum(wts, axis=-1, keepdims=True) + 1e-20) * ROUTED_SCALE
    combine = jnp.sum(jax.nn.one_hot(eidx, N_EXPERTS, dtype=jnp.float32) * wts[..., None], axis=1)
    nb = N // MOE_BLOCK
    ub = u.reshape(nb, MOE_BLOCK, D)
    cb = combine.astype(u.dtype).reshape(nb, MOE_BLOCK, N_EXPERTS)

    def blk(args):
        xb, gb = args
        h = jax.nn.silu(jnp.einsum("nd,edf->nef", xb, w_e_gate)) * jnp.einsum("nd,edf->nef", xb, w_e_up)
        return jnp.einsum("nef,efd->nd", h * gb[..., None], w_e_down)

    routed = lax.map(blk, (ub, cb)).reshape(N, D)
    shared = (jax.nn.silu(u @ w_s_gate) * (u @ w_s_up)) @ w_s_down
    return routed + shared


def _layer(h_ctx, h_lat, cond_ctx, cond_lat, cos, sin, layer_idx, update_ctx,
           w_mod, b_mod, g_norm1, g_norm2, w_in, g_qnorm, g_knorm, w_conv, a_log, dt_bias,
           g_dn_out, lam_qk, g_subln, w_branch, w_out, w_router, b_router,
           w_e_gate, w_e_up, w_e_down, w_s_gate, w_s_up, w_s_down):
    D = h_lat.shape[-1]
    lambda_init = 0.8 - 0.6 * math.exp(-0.3 * layer_idx)
    mod_l = (cond_lat @ w_mod + b_mod)[:, None, :]
    mod_c = cond_ctx @ w_mod + b_mod
    sh1_l, sc1_l, gt1_l, sh2_l, sc2_l, gt2_l = jnp.split(mod_l, 6, axis=-1)
    sh1_c, sc1_c, gt1_c, sh2_c, sc2_c, gt2_c = jnp.split(mod_c, 6, axis=-1)

    u_l = _rmsnorm(h_lat, g_norm1) * (1 + sc1_l) + sh1_l
    u_c = _rmsnorm(h_ctx, g_norm1) * (1 + sc1_c) + sh1_c
    (aq_l, ak_l, av_l, bqkv_l, bz_l, ba_l, bb_l, cq_l, ck_l, cv_l, gate_l) = _split_proj(u_l @ w_in)
    (aq_c, ak_c, av_c, bqkv_c, bz_c, ba_c, bb_c, cq_c, ck_c, cv_c, gate_c) = _split_proj(u_c @ w_in)
    a_c, a_l = _mixer_gqa(aq_c, ak_c, av_c, aq_l, ak_l, av_l, cos, sin, g_qnorm, g_knorm, update_ctx)
    b_c, b_l = _mixer_deltanet(bqkv_c, bz_c, ba_c, bb_c, bqkv_l, bz_l, ba_l, bb_l,
                               w_conv, a_log, dt_bias, g_dn_out, update_ctx)
    c_c, c_l = _mixer_diff(cq_c, ck_c, cv_c, cq_l, ck_l, cv_l, cos, sin, lam_qk, g_subln,
                           lambda_init, update_ctx)
    h_lat = h_lat + gt1_l * _merge(a_l, b_l, c_l, gate_l, w_branch, w_out)
    v_l = _rmsnorm(h_lat, g_norm2) * (1 + sc2_l) + sh2_l

    moe_args = (w_router, b_router, w_e_gate, w_e_up, w_e_down, w_s_gate, w_s_up, w_s_down)
    if update_ctx:
        h_ctx = h_ctx + gt1_c * _merge(a_c, b_c, c_c, gate_c, w_branch, w_out)
        v_c = _rmsnorm(h_ctx, g_norm2) * (1 + sc2_c) + sh2_c
        n_c = v_c.shape[0] * v_c.shape[1]
        f = _moe(jnp.concatenate([v_c.reshape(-1, D), v_l.reshape(-1, D)], axis=0), *moe_args)
        h_ctx = h_ctx + gt2_c * f[:n_c].reshape(h_ctx.shape)
        h_lat = h_lat + gt2_l * f[n_c:].reshape(h_lat.shape)
    else:
        h_lat = h_lat + gt2_l * _moe(v_l.reshape(-1, D), *moe_args).reshape(h_lat.shape)
    return h_ctx, h_lat


def setup_inputs(seed: int = 0) -> dict:
    key = jax.random.key(seed)
    ks = jax.random.split(key, 32)
    L, D, E, F = DEPTH, D_MODEL, N_EXPERTS, MOE_FF

    def nrm(k, shape, scale):
        return jax.random.normal(k, shape, jnp.float32) * scale

    return {
        "x": nrm(ks[0], (BATCH, SEQ, D), 1.0),
        "c": nrm(ks[1], (BATCH, D), 1.0),
        "ctx": nrm(ks[2], (BATCH, CTX_LEN, D), 1.0),
        "c_ctx": nrm(ks[3], (D,), 1.0),
        "w_mod": nrm(ks[4], (L, D, 6 * D), 0.5 * D ** -0.5),
        "b_mod": nrm(ks[5], (L, 6 * D), 0.02),
        "g_norm1": 1.0 + nrm(ks[6], (L, D), 0.02),
        "g_norm2": 1.0 + nrm(ks[7], (L, D), 0.02),
        "w_in": nrm(ks[8], (L, D, IN_W), D ** -0.5),
        "g_qnorm": 1.0 + nrm(ks[9], (L, HEAD_DIM), 0.02),
        "g_knorm": 1.0 + nrm(ks[10], (L, HEAD_DIM), 0.02),
        "w_conv": nrm(ks[11], (L, DN_CONV, DN_QKVW), DN_CONV ** -0.5),
        "a_log": jnp.log(jax.random.uniform(ks[12], (L, 2, DN_HEADS), jnp.float32, 1.0, 16.0)),
        "dt_bias": 1.0 + nrm(ks[13], (L, 2, DN_HEADS), 0.1),
        "g_dn_out": 1.0 + nrm(ks[14], (L, DN_DV), 0.02),
        "lam_qk": nrm(ks[15], (L, 4, DF_HD), 0.1),
        "g_subln": 1.0 + nrm(ks[16], (L, DF_VD), 0.02),
        "w_branch": nrm(ks[17], (L, 3, BR_W, D), BR_W ** -0.5),
        "w_out": nrm(ks[18], (L, D, D), D ** -0.5),
        "w_router": nrm(ks[19], (L, D, E), D ** -0.5),
        "b_router": nrm(ks[20], (L, E), 0.01),
        "w_e_gate": nrm(ks[21], (L, E, D, F), D ** -0.5),
        "w_e_up": nrm(ks[22], (L, E, D, F), D ** -0.5),
        "w_e_down": nrm(ks[23], (L, E, F, D), F ** -0.5),
        "w_s_gate": nrm(ks[24], (L, D, F), D ** -0.5),
        "w_s_up": nrm(ks[25], (L, D, F), D ** -0.5),
        "w_s_down": nrm(ks[26], (L, F, D), F ** -0.5),
        "g_final": 1.0 + nrm(ks[27], (D,), 0.02),
    }


def reference(x, c, ctx, c_ctx, w_mod, b_mod, g_norm1, g_norm2, w_in, g_qnorm, g_knorm, w_conv,
              a_log, dt_bias, g_dn_out, lam_qk, g_subln, w_branch, w_out, w_router, b_router,
              w_e_gate, w_e_up, w_e_down, w_s_gate, w_s_up, w_s_down, g_final):
    cos, sin = _axial_rope(x.shape[1], x.dtype)
    cond_lat = jax.nn.silu(c)
    cond_ctx = jax.nn.silu(c_ctx)
    h_ctx, h_lat = ctx, x
    for l in range(DEPTH):
        h_ctx, h_lat = _layer(
            h_ctx, h_lat, cond_ctx, cond_lat, cos, sin, l, l < DEPTH - 1,
            w_mod[l], b_mod[l], g_norm1[l], g_norm2[l], w_in[l], g_qnorm[l], g_knorm[l],
            w_conv[l], a_log[l], dt_bias[l], g_dn_out[l], lam_qk[l], g_subln[l],
            w_branch[l], w_out[l], w_router[l], b_router[l],
            w_e_gate[l], w_e_up[l], w_e_down[l], w_s_gate[l], w_s_up[l], w_s_down[l])
    return _rmsnorm(h_lat, g_final)
```

```python
import functools
import math

import numpy as np
import jax
import jax.numpy as jnp
from jax import lax
from jax.experimental import pallas as pl
from jax.experimental.pallas import tpu as pltpu

F32 = jnp.float32
BF16 = jnp.bfloat16

D_MODEL = 1024
GRID_W = 64
EPS = 1e-6
ROPE_THETA = 10000.0
HEAD_DIM = 64
LANES = 128
A_HEADS = 8
A_KV_HEADS = 2
A_GROUP = A_HEADS // A_KV_HEADS
DN_HEADS = 8
DN_DK = 64
DN_CHUNK = 64
DN_BLOCK = 256
DF_HEADS = 4
N_EXPERTS = 64
TOP_K = 6
N_GROUPS = 8
TOPK_GROUPS = 4
GROUP_SIZE = N_EXPERTS // N_GROUPS
MOE_FF = 256
ROUTED_SCALE = 2.5
BR_W = 512

COL_AQ, COL_CQ, COL_CK, COL_CV, COL_BZ = 0, 512, 1024, 1536, 2048
COL_BQ, COL_BK, COL_BV = 2560, 3072, 3584
COL_GATE = 4096
COL_AK, COL_AV, COL_AB = 7168, 7296, 7424
IN_W_PAD = 7680
IN_TN = 1280

VMEM_LIMIT = 56 * 1024 * 1024


def _cparams(sem):
    return pltpu.CompilerParams(dimension_semantics=sem, vmem_limit_bytes=VMEM_LIMIT)


def _dot(a, b):
    return jnp.dot(a, b, preferred_element_type=F32)


def _dot_nt(a, b):
    return lax.dot_general(a, b, (((1,), (1,)), ((), ())), preferred_element_type=F32)


def _dot_tn(a, b):
    return lax.dot_general(a, b, (((0,), (0,)), ((), ())), preferred_element_type=F32)


def _split3(x):
    hi = x.astype(BF16)
    r = x - hi.astype(F32)
    mid = r.astype(BF16)
    lo = (r - mid.astype(F32)).astype(BF16)
    return hi, mid, lo


def _dot3_l(x, m):
    hi, mid, lo = _split3(x)
    return _dot(hi, m) + _dot(mid, m) + _dot(lo, m)


def _dot3_r(m, x):
    hi, mid, lo = _split3(x)
    return _dot(m, hi) + _dot(m, mid) + _dot(m, lo)


def _silu(x):
    return x * jax.nn.sigmoid(x)


def _mod_kernel(cond_ref, w_ref, b_ref, o_ref):
    a = _silu(cond_ref[...]).astype(BF16)
    o_ref[...] = _dot(a, w_ref[...].astype(BF16)) + b_ref[...]


def _mod_vectors(cond, w_mod, b_mod):
    L, D, W = w_mod.shape
    tn = 1024
    return pl.pallas_call(
        _mod_kernel,
        out_shape=jax.ShapeDtypeStruct((L, 8, W), F32),
        grid=(L, W // tn),
        in_specs=[pl.BlockSpec((8, D), lambda l, j: (0, 0)),
                  pl.BlockSpec((None, D, tn), lambda l, j: (l, 0, j)),
                  pl.BlockSpec((None, 1, tn), lambda l, j: (l, 0, j))],
        out_specs=pl.BlockSpec((None, 8, tn), lambda l, j: (l, 0, j)),
        compiler_params=_cparams(("arbitrary", "arbitrary")),
        name="mod_vectors",
    )(cond, w_mod, b_mod.reshape(L, 1, W))


def _norm_mod(x, g, mod_ref, blk, tm, n_ctx, sh, sc):
    D = x.shape[-1]
    y = x * lax.rsqrt(jnp.mean(x * x, axis=-1, keepdims=True) + EPS) * g
    row = blk * tm + lax.broadcasted_iota(jnp.int32, (tm, 1), 0)
    is_ctx = row < n_ctx
    scale = jnp.where(is_ctx, mod_ref[0:1, sc * D:(sc + 1) * D], mod_ref[1:2, sc * D:(sc + 1) * D])
    shift = jnp.where(is_ctx, mod_ref[0:1, sh * D:(sh + 1) * D], mod_ref[1:2, sh * D:(sh + 1) * D])
    return y * (1.0 + scale) + shift


def _inproj_kernel(h_ref, mod_ref, g_ref, w_ref, p_ref, ab_ref, u_scr, *, tm, n_ctx, ab_off):
    i = pl.program_id(1)
    j = pl.program_id(2)

    @pl.when(j == 0)
    def _():
        u_scr[...] = _norm_mod(h_ref[...], g_ref[...], mod_ref, i, tm, n_ctx, 0, 1).astype(BF16)

    r = _dot(u_scr[...], w_ref[...])
    p_ref[...] = r.astype(BF16)

    @pl.when(j == pl.num_programs(2) - 1)
    def _():
        ab_ref[...] = r[:, ab_off:ab_off + LANES]


def _in_projection(h, modtab, g_norm, w_perm, n_ctx, tm):
    B, T, D = h.shape
    ncol = IN_W_PAD // IN_TN
    kern = functools.partial(_inproj_kernel, tm=tm, n_ctx=n_ctx, ab_off=COL_AB - (ncol - 1) * IN_TN)
    return pl.pallas_call(
        kern,
        out_shape=(jax.ShapeDtypeStruct((B, T, IN_W_PAD), BF16),
                   jax.ShapeDtypeStruct((B, T, LANES), F32)),
        grid=(B, T // tm, ncol),
        in_specs=[pl.BlockSpec((None, tm, D), lambda b, i, j: (b, i, 0)),
                  pl.BlockSpec((None, 2, 6 * D), lambda b, i, j: (b, 0, 0)),
                  pl.BlockSpec((1, D), lambda b, i, j: (0, 0)),
                  pl.BlockSpec((D, IN_TN), lambda b, i, j: (0, j))],
        out_specs=(pl.BlockSpec((None, tm, IN_TN), lambda b, i, j: (b, i, j)),
                   pl.BlockSpec((None, tm, LANES), lambda b, i, j: (b, i, 0))),
        scratch_shapes=[pltpu.VMEM((tm, D), BF16)],
        compiler_params=_cparams(("arbitrary", "arbitrary", "arbitrary")),
        name="in_projection",
    )(h, modtab, g_norm.reshape(1, D), w_perm)


def _rope(x, cos, sin_a, sin_b):
    return x * cos + pltpu.roll(x, LANES - 32, 1) * sin_a + pltpu.roll(x, 32, 1) * sin_b


def _attn_prep_kernel(aq_ref, ak_ref, cq_ref, ck_ref, cos_ref, sa_ref, sb_ref, gq_ref, gk_ref, m_ref,
                      qa_ref, ka_ref, qd_ref, kd_ref):
    cos, sa, sb = cos_ref[...], sa_ref[...], sb_ref[...]
    m64 = m_ref[...]
    half = lax.broadcasted_iota(jnp.int32, cos.shape, 1) // HEAD_DIM
    scale = HEAD_DIM ** -0.5

    def head_norm(x, g):
        ms = _dot3_l(x * x, m64) * (1.0 / HEAD_DIM)
        return x * lax.rsqrt(ms + EPS) * g

    for s in range(A_HEADS // 2):
        x = aq_ref[:, s * LANES:(s + 1) * LANES].astype(F32)
        y = _rope(head_norm(x, gq_ref[...]), cos, sa, sb) * scale
        y_sw = pltpu.roll(y, HEAD_DIM, 1)
        kvh = (2 * s) // A_GROUP
        for hh in range(2):
            src = y if hh == kvh else y_sw
            qa_ref[2 * s + hh] = jnp.where(half == kvh, src, 0.0).astype(BF16)
    xk = ak_ref[...].astype(F32)
    ka_ref[...] = _rope(head_norm(xk, gk_ref[...]), cos, sa, sb).astype(BF16)
    for h in range(DF_HEADS):
        x = cq_ref[:, h * LANES:(h + 1) * LANES].astype(F32)
        y = _rope(x, cos, sa, sb) * scale
        qd_ref[2 * h] = jnp.where(half == 0, y, 0.0).astype(BF16)
        qd_ref[2 * h + 1] = jnp.where(half == 1, y, 0.0).astype(BF16)
        xk = ck_ref[:, h * LANES:(h + 1) * LANES].astype(F32)
        kd_ref[:, h * LANES:(h + 1) * LANES] = _rope(xk, cos, sa, sb).astype(BF16)


def _attn_prep(P, tabs, g_q, g_k):
    B, T, _ = P.shape
    tm = 256
    cos2, sin_a, sin_b = tabs
    m64 = jnp.asarray(np.kron(np.eye(2), np.ones((HEAD_DIM, HEAD_DIM))), BF16)
    gq2 = jnp.tile(g_q, 2).reshape(1, LANES)
    gk2 = jnp.tile(g_k, 2).reshape(1, LANES)
    tab_spec = pl.BlockSpec((tm, LANES), lambda b, i: (i, 0))
    vec_spec = pl.BlockSpec((1, LANES), lambda b, i: (0, 0))
    return pl.pallas_call(
        _attn_prep_kernel,
        out_shape=(jax.ShapeDtypeStruct((B, A_HEADS, T, LANES), BF16),
                   jax.ShapeDtypeStruct((B, T, LANES), BF16),
                   jax.ShapeDtypeStruct((B, 2 * DF_HEADS, T, LANES), BF16),
                   jax.ShapeDtypeStruct((B, T, 512), BF16)),
        grid=(B, T // tm),
        in_specs=[pl.BlockSpec((None, tm, 512), lambda b, i: (b, i, COL_AQ // 512)),
                  pl.BlockSpec((None, tm, LANES), lambda b, i: (b, i, COL_AK // LANES)),
                  pl.BlockSpec((None, tm, 512), lambda b, i: (b, i, COL_CQ // 512)),
                  pl.BlockSpec((None, tm, 512), lambda b, i: (b, i, COL_CK // 512)),
                  tab_spec, tab_spec, tab_spec, vec_spec, vec_spec,
                  pl.BlockSpec((LANES, LANES), lambda b, i: (0, 0))],
        out_specs=(pl.BlockSpec((None, A_HEADS, tm, LANES), lambda b, i: (b, 0, i, 0)),
                   pl.BlockSpec((None, tm, LANES), lambda b, i: (b, i, 0)),
                   pl.BlockSpec((None, 2 * DF_HEADS, tm, LANES), lambda b, i: (b, 0, i, 0)),
                   pl.BlockSpec((None, tm, 512), lambda b, i: (b, i, 0))),
        compiler_params=_cparams(("arbitrary", "arbitrary")),
        name="attn_prep",
    )(P, P, P, P, cos2, sin_a, sin_b, gq2, gk2, m64)


def _flash_kernel(q_ref, k_ref, v_ref, *rest, mode, n_sub, tq, tk, n_ctx, n_lat_chunks, lambda_init):
    if mode == "gqa":
        o_ref, m_scr, l_scr, acc_scr = rest
    else:
        lam_ref, g_ref, o_ref, m_scr, l_scr, acc_scr = rest
    i = pl.program_id(2)
    R = n_sub * tq
    q = q_ref[...].reshape(R, LANES)

    m_scr[...] = jnp.full((R, 1), -jnp.inf, F32)
    l_scr[...] = jnp.zeros((R, 1), F32)
    acc_scr[...] = jnp.zeros((R, LANES), F32)

    def chunk(start, size):
        k = k_ref[pl.ds(start, size), :]
        v = v_ref[pl.ds(start, size), :]
        s = _dot_nt(q, k)
        m_old = m_scr[...]
        m_new = jnp.maximum(m_old, jnp.max(s, axis=-1, keepdims=True))
        alpha = jnp.exp(m_old - m_new)
        p = jnp.exp(s - m_new)
        l_scr[...] = alpha * l_scr[...] + jnp.sum(p, axis=-1, keepdims=True)
        acc_scr[...] = alpha * acc_scr[...] + _dot(p.astype(BF16), v)
        m_scr[...] = m_new

    chunk(0, n_ctx)

    @pl.when(i > 0)
    def _():
        def body(c, carry):
            chunk(pl.multiple_of(n_ctx + c * tk, math.gcd(n_ctx, tk)), tk)
            return carry
        lax.fori_loop(0, n_lat_chunks, body, 0)

    o = acc_scr[...] / l_scr[...]
    if mode == "gqa":
        kvh = pl.program_id(1)
        lane = lax.broadcasted_iota(jnp.int32, (tq, LANES), 1)
        for pair in range(n_sub // 2):
            a = o[(2 * pair) * tq:(2 * pair + 1) * tq]
            b = o[(2 * pair + 1) * tq:(2 * pair + 2) * tq]
            a_lo = jnp.where(kvh == 0, a, pltpu.roll(a, HEAD_DIM, 1))
            b_hi = jnp.where(kvh == 0, pltpu.roll(b, HEAD_DIM, 1), b)
            o_ref[:, pair * LANES:(pair + 1) * LANES] = jnp.where(lane < HEAD_DIM, a_lo, b_hi).astype(o_ref.dtype)
    else:
        lq = lam_ref[...]
        lam = (jnp.exp(jnp.sum(lq[0:1] * lq[1:2], axis=-1, keepdims=True))
               - jnp.exp(jnp.sum(lq[2:3] * lq[3:4], axis=-1, keepdims=True)) + lambda_init)
        d = o[:tq] - lam * o[tq:]
        y = d * lax.rsqrt(jnp.mean(d * d, axis=-1, keepdims=True) + EPS) * g_ref[...]
        o_ref[...] = (y * (1.0 - lambda_init)).astype(o_ref.dtype)


def _flash(mode, q, k, v, v_col0, n_ctx, extra=(), lambda_init=0.0):
    B, NQ, T, _ = q.shape
    tq = 256
    S = T - n_ctx
    tk = 1024 if S % 1024 == 0 else 256
    n_sub = A_GROUP if mode == "gqa" else 2
    n_heads = NQ // n_sub
    R = n_sub * tq
    kern = functools.partial(_flash_kernel, mode=mode, n_sub=n_sub, tq=tq, tk=tk, n_ctx=n_ctx,
                             n_lat_chunks=S // tk, lambda_init=lambda_init)
    in_specs = [pl.BlockSpec((None, n_sub, tq, LANES), lambda b, h, i: (b, h, i, 0))]
    if mode == "gqa":
        in_specs += [pl.BlockSpec((None, T, LANES), lambda b, h, i: (b, 0, 0)),
                     pl.BlockSpec((None, T, LANES), lambda b, h, i: (b, 0, v_col0))]
        out_spec = pl.BlockSpec((None, tq, 2 * LANES), lambda b, h, i: (b, i, h))
    else:
        in_specs += [pl.BlockSpec((None, T, LANES), lambda b, h, i: (b, 0, h)),
                     pl.BlockSpec((None, T, LANES), lambda b, h, i: (b, 0, v_col0 + h)),
                     pl.BlockSpec((4, HEAD_DIM), lambda b, h, i: (0, 0)),
                     pl.BlockSpec((1, LANES), lambda b, h, i: (0, 0))]
        out_spec = pl.BlockSpec((None, tq, LANES), lambda b, h, i: (b, i, h))
    return pl.pallas_call(
        kern,
        out_shape=jax.ShapeDtypeStruct((B, T, BR_W), BF16),
        grid=(B, n_heads, T // tq),
        in_specs=in_specs,
        out_specs=out_spec,
        scratch_shapes=[pltpu.VMEM((R, 1), F32), pltpu.VMEM((R, 1), F32), pltpu.VMEM((R, LANES), F32)],
        compiler_params=_cparams(("arbitrary", "arbitrary", "arbitrary")),
        name="flash_" + mode,
    )(q, k, v, *extra)


def _dn_prep_kernel(q_ref, k_ref, v_ref, ql_ref, kl_ref, vl_ref, qr_ref, kr_ref, vr_ref, ab_ref,
                    wc_ref, alog_ref, dtb_ref, mh_ref, eg_ref, eb_ref, tm_ref, ones_ref,
                    qo_ref, ko_ref, kb_ref, vb_ref, kbg_ref, qdec_ref, kdec_ref, dec_ref, egl_ref,
                    *, tm):
    i = pl.program_id(1)
    nblk = pl.num_programs(1)
    W = DN_HEADS * DN_DK
    row = lax.broadcasted_iota(jnp.int32, (tm, W), 0)
    col = lax.broadcasted_iota(jnp.int32, (tm, W), 1)
    left_ok = (i >= 2).astype(F32)
    right_ok = jnp.logical_and(i >= 1, i < nblk - 1).astype(F32)

    def conv_silu(x_ref, xl_ref, xr_ref, c0):
        x = x_ref[...].astype(F32)
        prev_row = xl_ref[7:8, :].astype(F32) * left_ok
        next_row = xr_ref[0:1, :].astype(F32) * right_ok
        x_prev = jnp.where(row == 0, prev_row, pltpu.roll(x, 1, 0))
        x_next = jnp.where(row == tm - 1, next_row, pltpu.roll(x, tm - 1, 0))
        y = (x_prev * wc_ref[0:1, c0:c0 + W] + x * wc_ref[1:2, c0:c0 + W]
             + x_next * wc_ref[2:3, c0:c0 + W])
        return _silu(y)

    def l2norm(x):
        ss = _dot3_l(x * x, mh_ref[...])
        return x * lax.rsqrt(ss + EPS)

    q = l2norm(conv_silu(q_ref, ql_ref, qr_ref, 0)) * (DN_DK ** -0.5)
    k = l2norm(conv_silu(k_ref, kl_ref, kr_ref, W))
    v = conv_silu(v_ref, vl_ref, vr_ref, 2 * W)

    ab = ab_ref[...]
    z = ab + dtb_ref[...]
    softplus = jnp.maximum(z, 0.0) + jnp.log(1.0 + jnp.exp(-jnp.abs(z)))
    g_all = -jnp.exp(alog_ref[...]) * softplus
    beta_all = jax.nn.sigmoid(ab)

    t_in = row & (DN_CHUNK - 1)
    s_in = col & (DN_CHUNK - 1)

    def put(ref, d, val):
        for h in range(DN_HEADS):
            piece = val[:, h * DN_DK:(h + 1) * DN_DK]
            if d is None:
                ref[h] = piece.astype(ref.dtype)
            else:
                ref[d, h] = piece.astype(ref.dtype)

    put(qo_ref, None, q)
    put(ko_ref, None, k)
    for d in range(2):
        g_e = _dot3_l(g_all, eg_ref[d])
        beta_e = _dot3_l(beta_all, eb_ref[d])
        gc = _dot3_r(tm_ref[d], g_e)
        g_tot = _dot3_r(ones_ref[...], g_e)
        strict = (t_in > s_in) if d == 0 else (t_in < s_in)
        causal = (t_in >= s_in) if d == 0 else (t_in <= s_in)
        diff = _dot3_r(tm_ref[d], jnp.where(strict, g_e, 0.0))
        decay = jnp.where(causal, jnp.exp(diff), 0.0)
        e_g = jnp.exp(gc)
        kb = k * beta_e
        put(kb_ref, d, kb)
        put(vb_ref, d, v * beta_e)
        put(kbg_ref, d, kb * e_g)
        put(qdec_ref, d, q * e_g)
        put(kdec_ref, d, k * jnp.exp(g_tot - gc))
        put(dec_ref, d, decay)
        e_tot = jnp.exp(g_tot)
        for h in range(DN_HEADS):
            for c in range(tm // DN_CHUNK):
                egl_ref[d, h, c:c + 1, :] = e_tot[c * DN_CHUNK:c * DN_CHUNK + 1, h * DN_DK:(h + 1) * DN_DK]


def _dn_consts():
    n = DN_BLOCK
    t = np.arange(n)
    same = (t[:, None] // DN_CHUNK) == (t[None, :] // DN_CHUNK)
    pre = same & (t[None, :] <= t[:, None])
    suf = same & (t[None, :] >= t[:, None])
    tmat = np.stack([pre, suf]).astype(np.float32)
    ones = same.astype(np.float32)
    W = DN_HEADS * DN_DK
    eg = np.zeros((2, LANES, W), np.float32)
    eb = np.zeros((2, LANES, W), np.float32)
    for d in range(2):
        for h in range(DN_HEADS):
            eg[d, d * DN_HEADS + h, h * DN_DK:(h + 1) * DN_DK] = 1.0
            eb[d, 2 * DN_HEADS + d * DN_HEADS + h, h * DN_DK:(h + 1) * DN_DK] = 1.0
    mh = np.kron(np.eye(DN_HEADS), np.ones((DN_DK, DN_DK))).astype(np.float32)
    return tuple(jnp.asarray(a, BF16) for a in (mh, eg, eb, tmat, ones))


def _dn_prep(P, AB, w_conv, a_log, dt_bias):
    B, T, _ = P.shape
    tm = DN_BLOCK
    nblk = T // tm
    W = DN_HEADS * DN_DK
    mh, eg, eb, tmat, ones = _dn_consts()
    alog_row = jnp.zeros((1, LANES), F32).at[0, :2 * DN_HEADS].set(a_log.reshape(-1))
    dtb_row = jnp.zeros((1, LANES), F32).at[0, :2 * DN_HEADS].set(dt_bias.reshape(-1))
    r8 = tm // 8

    def main(c0):
        return pl.BlockSpec((None, tm, W), lambda b, i: (b, i, c0 // W))

    def left(c0):
        return pl.BlockSpec((None, 8, W), lambda b, i: (b, jnp.maximum(i * r8 - 1, 0), c0 // W))

    def right(c0):
        return pl.BlockSpec((None, 8, W), lambda b, i: (b, jnp.minimum((i + 1) * r8, nblk * r8 - 1), c0 // W))

    def full(a):
        nd = a.ndim
        return pl.BlockSpec(a.shape, lambda b, i: (0,) * nd)

    hm = pl.BlockSpec((None, DN_HEADS, tm, DN_DK), lambda b, i: (b, 0, i, 0))
    hm2 = pl.BlockSpec((2, None, DN_HEADS, tm, DN_DK), lambda b, i: (0, b, 0, i, 0))
    sh_hm = jax.ShapeDtypeStruct((B, DN_HEADS, T, DN_DK), BF16)
    sh_hm2 = jax.ShapeDtypeStruct((2, B, DN_HEADS, T, DN_DK), BF16)
    return pl.pallas_call(
        functools.partial(_dn_prep_kernel, tm=tm),
        out_shape=(sh_hm, sh_hm, sh_hm2, sh_hm2, sh_hm2, sh_hm2, sh_hm2, sh_hm2,
                   jax.ShapeDtypeStruct((2, B, nblk, DN_HEADS, tm // DN_CHUNK, DN_DK), F32)),
        grid=(B, nblk),
        in_specs=[main(COL_BQ), main(COL_BK), main(COL_BV),
                  left(COL_BQ), left(COL_BK), left(COL_BV),
                  right(COL_BQ), right(COL_BK), right(COL_BV),
                  pl.BlockSpec((None, tm, LANES), lambda b, i: (b, i, 0)),
                  full(w_conv), full(alog_row), full(dtb_row), full(mh), full(eg), full(eb), full(tmat),
                  full(ones)],
        out_specs=(hm, hm, hm2, hm2, hm2, hm2, hm2, hm2,
                   pl.BlockSpec((2, None, None, DN_HEADS, tm // DN_CHUNK, DN_DK),
                                lambda b, i: (0, b, i, 0, 0, 0))),
        compiler_params=_cparams(("arbitrary", "arbitrary")),
        name="dn_prep",
    )(P, P, P, P, P, P, P, P, P, AB, w_conv, alog_row, dtb_row, mh, eg, eb, tmat, ones)


def _dn_scan_kernel(q_ref, k_ref, kb_ref, vb_ref, kbg_ref, qdec_ref, kdec_ref, dec_ref, egl_ref,
                    o_ref, s_scr, *, reverse, heads_per_iter):
    t = pl.program_id(1)
    n_chunks = DN_BLOCK // DN_CHUNK
    C = DN_CHUNK

    @pl.when(t == 0)
    def _():
        s_scr[...] = jnp.zeros_like(s_scr)

    r = lax.broadcasted_iota(jnp.int32, (C, C), 0)
    c = lax.broadcasted_iota(jnp.int32, (C, C), 1)
    strict = (r < c) if reverse else (r > c)
    eye = (r == c).astype(F32)
    order = list(range(n_chunks))[::-1] if reverse else list(range(n_chunks))

    def one_head(h):
        pre = {}
        for ci in order:
            rows = pl.ds(ci * C, C)
            k_c = k_ref[h, rows, :]
            dec = dec_ref[h, rows, :].astype(F32)
            low = jnp.where(strict, _dot_nt(kb_ref[h, rows, :], k_c) * dec, 0.0)
            qk = _dot_nt(q_ref[h, rows, :], k_c) * dec
            p = eye - low
            mpow = low
            for _ in range(5):
                mb = mpow.astype(BF16)
                mpow = _dot(mb, mb)
                p = p + _dot(p.astype(BF16), mpow.astype(BF16))
            pb = p.astype(BF16)
            pre[ci] = (_dot(pb, vb_ref[h, rows, :]), _dot(pb, kbg_ref[h, rows, :]), qk)
        s = s_scr[h]
        for ci in order:
            rows = pl.ds(ci * C, C)
            u, w, qk = pre[ci]
            sb = s.astype(BF16)
            v_new = u - _dot(w.astype(BF16), sb)
            vnb = v_new.astype(BF16)
            o_ref[h, rows, :] = _dot(qdec_ref[h, rows, :], sb) + _dot(qk.astype(BF16), vnb)
            s = s * egl_ref[h, ci:ci + 1, :] + _dot_tn(kdec_ref[h, rows, :], vnb)
        s_scr[h] = s

    def body(it, carry):
        for j in range(heads_per_iter):
            one_head(it * heads_per_iter + j)
        return carry

    lax.fori_loop(0, DN_HEADS // heads_per_iter, body, 0)


def _dn_scan(prep, direction, heads_per_iter=4):
    q, k, kb, vb, kbg, qdec, kdec, dec, egl = prep
    B, H, T, dk = q.shape
    tm = DN_BLOCK
    nblk = T // tm
    reverse = direction == 1

    def blk(t):
        if not reverse:
            return t
        return jnp.where(t == 0, 0, nblk - t)

    hm = pl.BlockSpec((None, H, tm, dk), lambda b, t: (b, 0, blk(t), 0))
    hm2 = pl.BlockSpec((None, None, H, tm, dk), lambda b, t: (direction, b, 0, blk(t), 0))
    return pl.pallas_call(
        functools.partial(_dn_scan_kernel, reverse=reverse, heads_per_iter=heads_per_iter),
        out_shape=jax.ShapeDtypeStruct((B, H, T, dk), F32),
        grid=(B, nblk),
        in_specs=[hm, hm, hm2, hm2, hm2, hm2, hm2, hm2,
                  pl.BlockSpec((None, None, None, H, tm // DN_CHUNK, dk),
                               lambda b, t: (direction, b, blk(t), 0, 0, 0))],
        out_specs=hm,
        scratch_shapes=[pltpu.VMEM((H, dk, dk), F32)],
        compiler_params=_cparams(("arbitrary", "arbitrary")),
        name="dn_scan_bwd" if reverse else "dn_scan_fwd",
    )(q, k, kb, vb, kbg, qdec, kdec, dec, egl)


def _dn_post_kernel(of_ref, ob_ref, z_ref, g_ref, o_ref):
    z = z_ref[...].astype(F32)
    for h in range(DN_HEADS):
        o = of_ref[h] + ob_ref[h]
        y = o * lax.rsqrt(jnp.mean(o * o, axis=-1, keepdims=True) + EPS) * g_ref[...]
        zz = z[:, h * DN_DK:(h + 1) * DN_DK]
        o_ref[:, h * DN_DK:(h + 1) * DN_DK] = (y * _silu(zz)).astype(o_ref.dtype)


def _dn_post(o_f, o_b, P, g_out):
    B, H, T, dk = o_f.shape
    tm = DN_BLOCK
    hm = pl.BlockSpec((None, H, tm, dk), lambda b, i: (b, 0, i, 0))
    return pl.pallas_call(
        _dn_post_kernel,
        out_shape=jax.ShapeDtypeStruct((B, T, BR_W), F32),
        grid=(B, T // tm),
        in_specs=[hm, hm,
                  pl.BlockSpec((None, tm, BR_W), lambda b, i: (b, i, COL_BZ // BR_W)),
                  pl.BlockSpec((1, dk), lambda b, i: (0, 0))],
        out_specs=pl.BlockSpec((None, tm, BR_W), lambda b, i: (b, i, 0)),
        compiler_params=_cparams(("arbitrary", "arbitrary")),
        name="dn_post",
    )(o_f, o_b, P, g_out.reshape(1, dk))


def _merge_kernel(a_ref, b_ref, c_ref, ga_ref, gb_ref, gc_ref, h_ref, mod_ref, g2_ref, wb_ref, wo_ref,
                  h_out_ref, v_out_ref, *, tm, n_ctx):
    i = pl.program_id(1)
    D = h_ref.shape[-1]
    m = jax.nn.sigmoid(ga_ref[...].astype(F32)) * _dot(a_ref[...], wb_ref[0])
    m = m + jax.nn.sigmoid(gb_ref[...].astype(F32)) * _dot(b_ref[...].astype(BF16), wb_ref[1])
    m = m + jax.nn.sigmoid(gc_ref[...].astype(F32)) * _dot(c_ref[...], wb_ref[2])
    y = _dot(m.astype(BF16), wo_ref[...])
    row = i * tm + lax.broadcasted_iota(jnp.int32, (tm, 1), 0)
    gate = jnp.where(row < n_ctx, mod_ref[0:1, 2 * D:3 * D], mod_ref[1:2, 2 * D:3 * D])
    h_new = h_ref[...] + gate * y
    h_out_ref[...] = h_new
    v_out_ref[...] = _norm_mod(h_new, g2_ref[...], mod_ref, i, tm, n_ctx, 3, 4).astype(BF16)


def _merge(br_a, br_b, br_c, P, h, modtab, g_norm2, w_branch, w_out, n_ctx, tm):
    B, T, D = h.shape
    br = pl.BlockSpec((None, tm, BR_W), lambda b, i: (b, i, 0))

    def gate(j):
        return pl.BlockSpec((None, tm, D), lambda b, i: (b, i, COL_GATE // D + j))

    tok = pl.BlockSpec((None, tm, D), lambda b, i: (b, i, 0))
    return pl.pallas_call(
        functools.partial(_merge_kernel, tm=tm, n_ctx=n_ctx),
        out_shape=(jax.ShapeDtypeStruct((B, T, D), F32), jax.ShapeDtypeStruct((B, T, D), BF16)),
        grid=(B, T // tm),
        in_specs=[br, br, br, gate(0), gate(1), gate(2), tok,
                  pl.BlockSpec((None, 2, 6 * D), lambda b, i: (b, 0, 0)),
                  pl.BlockSpec((1, D), lambda b, i: (0, 0)),
                  pl.BlockSpec((3, BR_W, D), lambda b, i: (0, 0, 0)),
                  pl.BlockSpec((D, D), lambda b, i: (0, 0))],
        out_specs=(tok, tok),
        compiler_params=_cparams(("arbitrary", "arbitrary")),
        name="merge",
    )(br_a, br_b, br_c, P, P, P, h, modtab, g_norm2.reshape(1, D), w_branch, w_out)


def _router_kernel(v_ref, wr_ref, br_ref, c_ref):
    tn = v_ref.shape[0]
    G = N_GROUPS
    scores = jax.nn.sigmoid(_dot_nt(wr_ref[...], v_ref[...]))
    sel = scores + br_ref[...]
    slabs = [sel[j * G:(j + 1) * G] for j in range(GROUP_SIZE)]
    m1, m2 = slabs[0], jnp.full((G, tn), -jnp.inf, F32)
    for j in range(1, GROUP_SIZE):
        m2 = jnp.maximum(m2, jnp.minimum(m1, slabs[j]))
        m1 = jnp.maximum(m1, slabs[j])
    gs = m1 + m2
    gidx = lax.broadcasted_iota(jnp.int32, (G, tn), 0)
    grank = jnp.zeros((G, tn), jnp.int32)
    for g in range(G):
        rowv = gs[g:g + 1]
        beats = jnp.where(rowv > gs, 1, jnp.where(jnp.logical_and(rowv == gs, g < gidx), 1, 0))
        grank = grank + beats
    gmask = grank < TOPK_GROUPS
    masked = jnp.concatenate([jnp.where(gmask, s, -jnp.inf) for s in slabs], axis=0)
    pos = lax.broadcasted_iota(jnp.int32, (N_EXPERTS, tn), 0)
    orig = (pos % G) * GROUP_SIZE + pos // G
    rank = jnp.zeros((N_EXPERTS, tn), jnp.int32)
    for p in range(N_EXPERTS):
        o_p = (p % G) * GROUP_SIZE + p // G
        rowv = masked[p:p + 1]
        beats = jnp.where(rowv > masked, 1, jnp.where(jnp.logical_and(rowv == masked, o_p < orig), 1, 0))
        rank = rank + beats
    chosen = rank < TOP_K
    w = jnp.where(chosen, scores, 0.0)
    denom = jnp.sum(w, axis=0, keepdims=True) + 1e-20
    c_ref[...] = w / denom * ROUTED_SCALE


def _router(v_flat, w_router, b_router):
    N, D = v_flat.shape
    tn = 1280 if N % 1280 == 0 else 256
    perm = np.array([(p % N_GROUPS) * GROUP_SIZE + p // N_GROUPS for p in range(N_EXPERTS)])
    wr_t = w_router.T[perm].astype(BF16)
    br = b_router[perm].reshape(N_EXPERTS, 1)
    c_pos = pl.pallas_call(
        _router_kernel,
        out_shape=jax.ShapeDtypeStruct((N_EXPERTS, N), F32),
        grid=(N // tn,),
        in_specs=[pl.BlockSpec((tn, D), lambda i: (i, 0)),
                  pl.BlockSpec((N_EXPERTS, D), lambda i: (0, 0)),
                  pl.BlockSpec((N_EXPERTS, 1), lambda i: (0, 0))],
        out_specs=pl.BlockSpec((N_EXPERTS, tn), lambda i: (0, i)),
        compiler_params=_cparams(("arbitrary",)),
        name="router",
    )(v_flat, wr_t, br)
    inv = np.argsort(perm)
    return c_pos[inv]


MOE_EPS = 4


def _moe_kernel(x_ref, c_ref, wg_ref, wu_ref, wd_ref, sg_ref, su_ref, sd_ref, h_ref, mod_ref,
                o_ref, acc_ref, *, tm, n_ctx):
    i = pl.program_id(1)
    e = pl.program_id(2)
    x = x_ref[...]
    D = x.shape[-1]

    @pl.when(e == 0)
    def _():
        hs = _silu(_dot(x, sg_ref[...])) * _dot(x, su_ref[...])
        acc_ref[...] = _dot(hs.astype(BF16), sd_ref[...])

    c = c_ref[...]
    parts = []
    for j in range(MOE_EPS):
        hj = _silu(_dot(x, wg_ref[j])) * _dot(x, wu_ref[j]) * c[:, j:j + 1]
        parts.append(hj.astype(BF16))
    hcat = jnp.concatenate(parts, axis=-1)
    acc_ref[...] += _dot(hcat, wd_ref[...])

    @pl.when(e == pl.num_programs(2) - 1)
    def _():
        row = i * tm + lax.broadcasted_iota(jnp.int32, (tm, 1), 0)
        gate = jnp.where(row < n_ctx, mod_ref[0:1, 5 * D:6 * D], mod_ref[1:2, 5 * D:6 * D])
        o_ref[...] = h_ref[...] + gate * acc_ref[...]


def _moe(v, c_grp, h, modtab, wg, wu, wd, sg, su, sd, n_ctx, tm):
    B, T, D = v.shape
    F = MOE_FF
    ngrp = N_EXPERTS // MOE_EPS
    tok = pl.BlockSpec((None, tm, D), lambda b, i, e: (b, i, 0))
    return pl.pallas_call(
        functools.partial(_moe_kernel, tm=tm, n_ctx=n_ctx),
        out_shape=jax.ShapeDtypeStruct((B, T, D), F32),
        grid=(B, T // tm, ngrp),
        in_specs=[tok,
                  pl.BlockSpec((None, None, tm, MOE_EPS), lambda b, i, e: (e, b, i, 0)),
                  pl.BlockSpec((MOE_EPS, D, F), lambda b, i, e: (e, 0, 0)),
                  pl.BlockSpec((MOE_EPS, D, F), lambda b, i, e: (e, 0, 0)),
                  pl.BlockSpec((MOE_EPS * F, D), lambda b, i, e: (e, 0)),
                  pl.BlockSpec((D, F), lambda b, i, e: (0, 0)),
                  pl.BlockSpec((D, F), lambda b, i, e: (0, 0)),
                  pl.BlockSpec((F, D), lambda b, i, e: (0, 0)),
                  tok,
                  pl.BlockSpec((None, 2, 6 * D), lambda b, i, e: (b, 0, 0))],
        out_specs=tok,
        scratch_shapes=[pltpu.VMEM((tm, D), F32)],
        compiler_params=_cparams(("arbitrary", "arbitrary", "arbitrary")),
        name="moe",
    )(v, c_grp, wg, wu, wd, sg, su, sd, h, modtab)


def _final_kernel(h_ref, g_ref, o_ref):
    x = h_ref[...]
    o_ref[...] = x * lax.rsqrt(jnp.mean(x * x, axis=-1, keepdims=True) + EPS) * g_ref[...]


def _final_norm(h, g_final, n_ctx):
    B, T, D = h.shape
    tm = n_ctx
    S = T - n_ctx
    return pl.pallas_call(
        _final_kernel,
        out_shape=jax.ShapeDtypeStruct((B, S, D), F32),
        grid=(B, S // tm),
        in_specs=[pl.BlockSpec((None, tm, D), lambda b, i: (b, i + 1, 0)),
                  pl.BlockSpec((1, D), lambda b, i: (0, 0))],
        out_specs=pl.BlockSpec((None, tm, D), lambda b, i: (b, i, 0)),
        compiler_params=_cparams(("arbitrary", "arbitrary")),
        name="final_norm",
    )(h, g_final.reshape(1, D))


def _rope_tables(S, n_ctx):
    rows = S // GRID_W
    row = jnp.repeat(jnp.arange(rows, dtype=F32), GRID_W)
    col = (jnp.arange(S) % GRID_W).astype(F32)
    axis_dim = HEAD_DIM // 2
    inv = 1.0 / (ROPE_THETA ** (jnp.arange(0, axis_dim, 2, dtype=F32) / axis_dim))
    ang = jnp.concatenate([row[:, None] * inv, col[:, None] * inv], axis=-1)
    ang = jnp.concatenate([ang, ang], axis=-1)
    cos = jnp.concatenate([jnp.ones((n_ctx, HEAD_DIM), F32), jnp.cos(ang)], axis=0)
    sin = jnp.concatenate([jnp.zeros((n_ctx, HEAD_DIM), F32), jnp.sin(ang)], axis=0)
    first = (jnp.arange(HEAD_DIM) < HEAD_DIM // 2)[None, :]
    sin_a = jnp.where(first, -sin, 0.0)
    sin_b = jnp.where(first, 0.0, sin)
    return tuple(jnp.tile(t, (1, 2)) for t in (cos, sin_a, sin_b))


def _permute_w_in(w):
    splits = np.cumsum([512, 128, 128, 1536, 512, 16, 16, 512, 512, 512, 3072])[:-1].tolist()
    aq, ak, av, bqkv, bz, ba, bb, cq, ck, cv, gate = jnp.split(w, splits, axis=-1)
    pad = jnp.zeros((w.shape[0], IN_W_PAD - COL_AB - 32), w.dtype)
    out = jnp.concatenate([aq, cq, ck, cv, bz, bqkv, gate, ak, av, ba, bb, pad], axis=-1)
    return out.astype(BF16)


def kernel(x, c, ctx, c_ctx, w_mod, b_mod, g_norm1, g_norm2, w_in, g_qnorm, g_knorm, w_conv, a_log, dt_bias, g_dn_out, lam_qk, g_subln, w_branch, w_out, w_router, b_router, w_e_gate, w_e_up, w_e_down, w_s_gate, w_s_up, w_s_down, g_final):
    B, S, D = x.shape
    n_ctx = ctx.shape[1]
    T = n_ctx + S
    L = w_mod.shape[0]
    N = B * T
    assert n_ctx == DN_BLOCK and S % 1024 == 0 and D == D_MODEL
    tm_tok = 1280 if T % 1280 == 0 else 256
    tm_merge = 640 if T % 640 == 0 else 256

    cond = jnp.zeros((8, D), F32).at[0].set(c_ctx).at[1:1 + B].set(c)
    mod = _mod_vectors(cond, w_mod, b_mod)
    tabs = _rope_tables(S, n_ctx)
    h = jnp.concatenate([ctx, x], axis=1)

    for l in range(L):
        lambda_init = 0.8 - 0.6 * math.exp(-0.3 * l)
        modtab = jnp.stack([jnp.broadcast_to(mod[l, 0], (B, 6 * D)), mod[l, 1:1 + B]], axis=1)
        P, AB = _in_projection(h, modtab, g_norm1[l], _permute_w_in(w_in[l]), n_ctx, tm_tok)

        qa, ka, qd, kd = _attn_prep(P, tabs, g_qnorm[l], g_knorm[l])
        br_a = _flash("gqa", qa, ka, P, COL_AV // LANES, n_ctx)
        br_c = _flash("diff", qd, kd, P, COL_CV // LANES, n_ctx,
                      extra=(lam_qk[l], g_subln[l].reshape(1, LANES)), lambda_init=lambda_init)

        prep = _dn_prep(P, AB, w_conv[l], a_log[l], dt_bias[l])
        o_f = _dn_scan(prep, 0)
        o_b = _dn_scan(prep, 1)
        br_b = _dn_post(o_f, o_b, P, g_dn_out[l])

        h, v = _merge(br_a, br_b, br_c, P, h, modtab, g_norm2[l], w_branch[l].astype(BF16),
                      w_out[l].astype(BF16), n_ctx, tm_merge)

        c_exp = _router(v.reshape(N, D), w_router[l], b_router[l])
        c_grp = c_exp.reshape(N_EXPERTS // MOE_EPS, MOE_EPS, B, T).transpose(0, 2, 3, 1)
        h = _moe(v, c_grp, h, modtab,
                 w_e_gate[l].astype(BF16), w_e_up[l].astype(BF16),
                 w_e_down[l].astype(BF16).reshape(N_EXPERTS * MOE_FF, D),
                 w_s_gate[l].astype(BF16), w_s_up[l].astype(BF16), w_s_down[l].astype(BF16),
                 n_ctx, tm_merge)

    return _final_norm(h, g_final, n_ctx)
```

```python
import functools
import math

import numpy as np
import jax
import jax.numpy as jnp
from jax import lax
from jax.experimental import pallas as pl
from jax.experimental.pallas import tpu as pltpu

F32 = jnp.float32
BF16 = jnp.bfloat16

D_MODEL = 1024
GRID_W = 64
EPS = 1e-6
ROPE_THETA = 10000.0
HEAD_DIM = 64
LANES = 128
A_HEADS = 8
A_KV_HEADS = 2
A_GROUP = A_HEADS // A_KV_HEADS
DN_HEADS = 8
DN_DK = 64
DN_CHUNK = 64
DN_BLOCK = 256
DF_HEADS = 4
N_EXPERTS = 64
TOP_K = 6
N_GROUPS = 8
TOPK_GROUPS = 4
GROUP_SIZE = N_EXPERTS // N_GROUPS
MOE_FF = 256
ROUTED_SCALE = 2.5
BR_W = 512

COL_AQ, COL_CQ, COL_CK, COL_CV, COL_BZ = 0, 512, 1024, 1536, 2048
COL_BQ, COL_BK, COL_BV = 2560, 3072, 3584
COL_GATE = 4096
COL_AK, COL_AV, COL_AB = 7168, 7296, 7424
IN_W_PAD = 7680
IN_TN = 1280

VMEM_LIMIT = 56 * 1024 * 1024


def _cparams(sem):
    return pltpu.CompilerParams(dimension_semantics=sem, vmem_limit_bytes=VMEM_LIMIT)


def _dot(a, b):
    return jnp.dot(a, b, preferred_element_type=F32)


def _dot_nt(a, b):
    return lax.dot_general(a, b, (((1,), (1,)), ((), ())), preferred_element_type=F32)


def _dot_tn(a, b):
    return lax.dot_general(a, b, (((0,), (0,)), ((), ())), preferred_element_type=F32)


def _split3(x):
    hi = x.astype(BF16)
    r = x - hi.astype(F32)
    mid = r.astype(BF16)
    lo = (r - mid.astype(F32)).astype(BF16)
    return hi, mid, lo


def _dot3_l(x, m):
    hi, mid, lo = _split3(x)
    return _dot(hi, m) + _dot(mid, m) + _dot(lo, m)


def _dot3_r(m, x):
    hi, mid, lo = _split3(x)
    return _dot(m, hi) + _dot(m, mid) + _dot(m, lo)


def _silu(x):
    return x * jax.nn.sigmoid(x)


def _mod_kernel(cond_ref, w_ref, b_ref, o_ref):
    a = _silu(cond_ref[...]).astype(BF16)
    o_ref[...] = _dot(a, w_ref[...].astype(BF16)) + b_ref[...]


def _mod_vectors(cond, w_mod, b_mod):
    L, D, W = w_mod.shape
    tn = 1024
    return pl.pallas_call(
        _mod_kernel,
        out_shape=jax.ShapeDtypeStruct((L, 8, W), F32),
        grid=(L, W // tn),
        in_specs=[pl.BlockSpec((8, D), lambda l, j: (0, 0)),
                  pl.BlockSpec((None, D, tn), lambda l, j: (l, 0, j)),
                  pl.BlockSpec((None, 1, tn), lambda l, j: (l, 0, j))],
        out_specs=pl.BlockSpec((None, 8, tn), lambda l, j: (l, 0, j)),
        compiler_params=_cparams(("arbitrary", "arbitrary")),
        name="mod_vectors",
    )(cond, w_mod, b_mod.reshape(L, 1, W))


def _is_ctx_row(blk, tm, n_lat):
    return blk * tm + lax.broadcasted_iota(jnp.int32, (tm, 1), 0) >= n_lat


def _norm_mod(x, g, mod_ref, blk, tm, n_lat, sh, sc):
    D = x.shape[-1]
    y = x * lax.rsqrt(jnp.mean(x * x, axis=-1, keepdims=True) + EPS) * g
    is_ctx = _is_ctx_row(blk, tm, n_lat)
    scale = jnp.where(is_ctx, mod_ref[0:1, sc * D:(sc + 1) * D], mod_ref[1:2, sc * D:(sc + 1) * D])
    shift = jnp.where(is_ctx, mod_ref[0:1, sh * D:(sh + 1) * D], mod_ref[1:2, sh * D:(sh + 1) * D])
    return y * (1.0 + scale) + shift


def _inproj_kernel(h_ref, mod_ref, g_ref, w_ref, p_ref, ab_ref, u_scr, *, tm, n_lat, ab_off):
    i = pl.program_id(1)
    j = pl.program_id(2)

    @pl.when(j == 0)
    def _():
        u_scr[...] = _norm_mod(h_ref[...], g_ref[...], mod_ref, i, tm, n_lat, 0, 1).astype(BF16)

    r = _dot(u_scr[...], w_ref[...])
    p_ref[...] = r.astype(BF16)

    @pl.when(j == pl.num_programs(2) - 1)
    def _():
        ab_ref[...] = r[:, ab_off:ab_off + LANES]


def _in_projection(h, modtab, g_norm, w_perm, n_lat, tm):
    B, T, D = h.shape
    ncol = IN_W_PAD // IN_TN
    kern = functools.partial(_inproj_kernel, tm=tm, n_lat=n_lat, ab_off=COL_AB - (ncol - 1) * IN_TN)
    return pl.pallas_call(
        kern,
        out_shape=(jax.ShapeDtypeStruct((B, T, IN_W_PAD), BF16),
                   jax.ShapeDtypeStruct((B, T, LANES), F32)),
        grid=(B, T // tm, ncol),
        in_specs=[pl.BlockSpec((None, tm, D), lambda b, i, j: (b, i, 0)),
                  pl.BlockSpec((None, 2, 6 * D), lambda b, i, j: (b, 0, 0)),
                  pl.BlockSpec((1, D), lambda b, i, j: (0, 0)),
                  pl.BlockSpec((D, IN_TN), lambda b, i, j: (0, j))],
        out_specs=(pl.BlockSpec((None, tm, IN_TN), lambda b, i, j: (b, i, j)),
                   pl.BlockSpec((None, tm, LANES), lambda b, i, j: (b, i, 0))),
        scratch_shapes=[pltpu.VMEM((tm, D), BF16)],
        compiler_params=_cparams(("arbitrary", "arbitrary", "arbitrary")),
        name="in_projection",
    )(h, modtab, g_norm.reshape(1, D), w_perm)


def _rope(x, cos, sin_a, sin_b):
    return x * cos + pltpu.roll(x, LANES - 32, 1) * sin_a + pltpu.roll(x, 32, 1) * sin_b


def _attn_prep_kernel(aq_ref, ak_ref, cq_ref, ck_ref, cos_ref, sa_ref, sb_ref, gq_ref, gk_ref, m_ref,
                      qa_ref, ka_ref, qd_ref, kd_ref):
    cos, sa, sb = cos_ref[...], sa_ref[...], sb_ref[...]
    m64 = m_ref[...]
    half = lax.broadcasted_iota(jnp.int32, cos.shape, 1) // HEAD_DIM
    scale = HEAD_DIM ** -0.5 * math.log2(math.e)

    def head_norm(x, g):
        ms = _dot3_l(x * x, m64) * (1.0 / HEAD_DIM)
        return x * lax.rsqrt(ms + EPS) * g

    for s in range(A_HEADS // 2):
        x = aq_ref[:, s * LANES:(s + 1) * LANES].astype(F32)
        y = _rope(head_norm(x, gq_ref[...]), cos, sa, sb) * scale
        y_sw = pltpu.roll(y, HEAD_DIM, 1)
        kvh = (2 * s) // A_GROUP
        for hh in range(2):
            src = y if hh == kvh else y_sw
            qa_ref[2 * s + hh] = jnp.where(half == kvh, src, 0.0).astype(BF16)
    xk = ak_ref[...].astype(F32)
    ka_ref[...] = _rope(head_norm(xk, gk_ref[...]), cos, sa, sb).astype(BF16)
    for h in range(DF_HEADS):
        x = cq_ref[:, h * LANES:(h + 1) * LANES].astype(F32)
        y = _rope(x, cos, sa, sb) * scale
        qd_ref[2 * h] = jnp.where(half == 0, y, 0.0).astype(BF16)
        qd_ref[2 * h + 1] = jnp.where(half == 1, y, 0.0).astype(BF16)
        xk = ck_ref[:, h * LANES:(h + 1) * LANES].astype(F32)
        kd_ref[:, h * LANES:(h + 1) * LANES] = _rope(xk, cos, sa, sb).astype(BF16)


def _attn_prep(P, tabs, g_q, g_k):
    B, T, _ = P.shape
    tm = 256
    cos2, sin_a, sin_b = tabs
    m64 = jnp.asarray(np.kron(np.eye(2), np.ones((HEAD_DIM, HEAD_DIM))), BF16)
    gq2 = jnp.tile(g_q, 2).reshape(1, LANES)
    gk2 = jnp.tile(g_k, 2).reshape(1, LANES)
    tab_spec = pl.BlockSpec((tm, LANES), lambda b, i: (i, 0))
    vec_spec = pl.BlockSpec((1, LANES), lambda b, i: (0, 0))
    return pl.pallas_call(
        _attn_prep_kernel,
        out_shape=(jax.ShapeDtypeStruct((B, A_HEADS, T, LANES), BF16),
                   jax.ShapeDtypeStruct((B, T, LANES), BF16),
                   jax.ShapeDtypeStruct((B, 2 * DF_HEADS, T, LANES), BF16),
                   jax.ShapeDtypeStruct((B, T, 512), BF16)),
        grid=(B, T // tm),
        in_specs=[pl.BlockSpec((None, tm, 512), lambda b, i: (b, i, COL_AQ // 512)),
                  pl.BlockSpec((None, tm, LANES), lambda b, i: (b, i, COL_AK // LANES)),
                  pl.BlockSpec((None, tm, 512), lambda b, i: (b, i, COL_CQ // 512)),
                  pl.BlockSpec((None, tm, 512), lambda b, i: (b, i, COL_CK // 512)),
                  tab_spec, tab_spec, tab_spec, vec_spec, vec_spec,
                  pl.BlockSpec((LANES, LANES), lambda b, i: (0, 0))],
        out_specs=(pl.BlockSpec((None, A_HEADS, tm, LANES), lambda b, i: (b, 0, i, 0)),
                   pl.BlockSpec((None, tm, LANES), lambda b, i: (b, i, 0)),
                   pl.BlockSpec((None, 2 * DF_HEADS, tm, LANES), lambda b, i: (b, 0, i, 0)),
                   pl.BlockSpec((None, tm, 512), lambda b, i: (b, i, 0))),
        compiler_params=_cparams(("arbitrary", "arbitrary")),
        name="attn_prep",
    )(P, P, P, P, cos2, sin_a, sin_b, gq2, gk2, m64)


FLASH_TQ = 256


def _flash_kernel(*refs, mode, half, tk, n_lat, n_ctx, n_lat_chunks, ctx_only, lambda_init):
    refs = list(refs)
    q_ref, k_ref = refs[0], refs[1]
    pos = 2
    vt_lat_ref = None
    if not ctx_only:
        vt_lat_ref = refs[pos]
        pos += 1
    vt_ctx_ref = refs[pos]
    pos += 1
    if mode == "diff":
        lam_ref, g_ref = refs[pos], refs[pos + 1]
        pos += 2
    o_ref, sb_scr, mb_scr = refs[pos:pos + 3]
    m_scr, l_scr, acc_scr = refs[pos + 3:pos + 5], refs[pos + 5:pos + 7], refs[pos + 7:pos + 9]
    q = q_ref[...].reshape(2 * half, LANES)
    q_half = (q[:half], q[half:])

    for j in range(2):
        m_scr[j][...] = jnp.full((1, half), -jnp.inf, F32)
        l_scr[j][...] = jnp.zeros((1, half), F32)
        acc_scr[j][...] = jnp.zeros((LANES, half), F32)

    def scores(k, j):
        s = _dot_nt(k, q_half[j])
        return s, jnp.max(s, axis=0, keepdims=True)

    def update(j, s, m_cur, vt):
        m_old = m_scr[j][...]
        m_new = jnp.maximum(m_old, m_cur)
        alpha = jnp.exp2(m_old - m_new)
        p = jnp.exp2(s - m_new)
        l_scr[j][...] = alpha * l_scr[j][...] + jnp.sum(p, axis=0, keepdims=True)
        acc_scr[j][...] = alpha * acc_scr[j][...] + _dot(vt, p.astype(BF16))
        m_scr[j][...] = m_new

    k_ctx = k_ref[pl.ds(n_lat, n_ctx), :]
    for j in range(2):
        s, m_cur = scores(k_ctx, j)
        update(j, s, m_cur, vt_ctx_ref[...])

    if not ctx_only:
        def k_chunk(c):
            return k_ref[pl.ds(pl.multiple_of(c * tk, tk), tk), :]

        def stage(c, first):
            k = k_chunk(c)
            s_a, m_a = scores(k, 0)
            if not first:
                update(1, sb_scr[...], mb_scr[...], vt_lat_ref[c - 1])
            s_b, m_b = scores(k, 1)
            sb_scr[...] = s_b
            mb_scr[...] = m_b
            update(0, s_a, m_a, vt_lat_ref[c])

        stage(0, True)

        def body(c, carry):
            stage(c, False)
            return carry
        lax.fori_loop(1, n_lat_chunks, body, 0)
        update(1, sb_scr[...], mb_scr[...], vt_lat_ref[n_lat_chunks - 1])

    if mode == "gqa":
        kvh = pl.program_id(1)
        per = half // FLASH_TQ
        for j in range(2):
            o = acc_scr[j][pl.ds(pl.multiple_of(kvh * HEAD_DIM, HEAD_DIM), HEAD_DIM), :] / l_scr[j][...]
            for g in range(per):
                o_ref[(j * per + g) * HEAD_DIM:(j * per + g + 1) * HEAD_DIM, :] = (
                    o[:, g * FLASH_TQ:(g + 1) * FLASH_TQ].astype(o_ref.dtype))
    else:
        lq = lam_ref[...]
        lam = (jnp.exp(jnp.sum(lq[0:1] * lq[1:2], axis=-1, keepdims=True))
               - jnp.exp(jnp.sum(lq[2:3] * lq[3:4], axis=-1, keepdims=True)) + lambda_init)
        d = acc_scr[0][...] / l_scr[0][...] - lam * (acc_scr[1][...] / l_scr[1][...])
        y = d * lax.rsqrt(jnp.mean(d * d, axis=0, keepdims=True) + EPS) * g_ref[...]
        o_ref[...] = (y * (1.0 - lambda_init)).astype(o_ref.dtype)


def _flash(mode, q, k, vt_lat, vt_ctx, n_ctx, ctx_only, extra=(), lambda_init=0.0):
    B, NQ, T, _ = q.shape
    S = T - n_ctx
    tk = vt_lat.shape[-1]
    n_chunks = vt_lat.shape[-3]
    n_maps = A_GROUP if mode == "gqa" else 2
    n_heads = NQ // n_maps
    if ctx_only:
        tq, n_q, q0, rows = n_ctx, 1, S // n_ctx, n_ctx
    else:
        tq = FLASH_TQ if mode == "gqa" else 2 * FLASH_TQ
        n_q, q0, rows = S // tq, 0, S
    half = n_maps * tq // 2
    kern = functools.partial(_flash_kernel, mode=mode, half=half, tk=tk, n_lat=S, n_ctx=n_ctx,
                             n_lat_chunks=n_chunks, ctx_only=ctx_only, lambda_init=lambda_init)
    in_specs = [pl.BlockSpec((None, n_maps, tq, LANES), lambda b, h, i: (b, h, i + q0, 0))]
    args = [q, k]
    if mode == "gqa":
        in_specs.append(pl.BlockSpec((None, T, LANES), lambda b, h, i: (b, 0, 0)))
        if not ctx_only:
            in_specs.append(pl.BlockSpec((None, n_chunks, LANES, tk), lambda b, h, i: (b, 0, 0, 0)))
            args.append(vt_lat)
        in_specs.append(pl.BlockSpec((None, LANES, n_ctx), lambda b, h, i: (b, 0, 0)))
        out_spec = pl.BlockSpec((None, A_GROUP * HEAD_DIM, tq), lambda b, h, i: (b, h, i))
    else:
        in_specs.append(pl.BlockSpec((None, T, LANES), lambda b, h, i: (b, 0, h)))
        if not ctx_only:
            in_specs.append(pl.BlockSpec((None, None, n_chunks, LANES, tk), lambda b, h, i: (b, h, 0, 0, 0)))
            args.append(vt_lat)
        in_specs.append(pl.BlockSpec((None, None, LANES, n_ctx), lambda b, h, i: (b, h, 0, 0)))
        out_spec = pl.BlockSpec((None, LANES, tq), lambda b, h, i: (b, h, i))
    args.append(vt_ctx)
    if mode == "diff":
        in_specs += [pl.BlockSpec((4, HEAD_DIM), lambda b, h, i: (0, 0)),
                     pl.BlockSpec((LANES, 1), lambda b, h, i: (0, 0))]
    return pl.pallas_call(
        kern,
        out_shape=jax.ShapeDtypeStruct((B, BR_W, rows), BF16),
        grid=(B, n_heads, n_q),
        in_specs=in_specs,
        out_specs=out_spec,
        scratch_shapes=([pltpu.VMEM((tk, half), F32)] + [pltpu.VMEM((1, half), F32)] * 5
                        + [pltpu.VMEM((LANES, half), F32)] * 2),
        compiler_params=_cparams(("arbitrary", "arbitrary", "arbitrary")),
        name="flash_" + mode + ("_ctx" if ctx_only else ""),
    )(*args, *extra)


def _dn_prep_kernel(q_ref, k_ref, v_ref, ql_ref, kl_ref, vl_ref, qr_ref, kr_ref, vr_ref, ab_ref,
                    wc_ref, alog_ref, dtb_ref, mh_ref, eg_ref, eb_ref, tm_ref, ones_ref,
                    qo_ref, ko_ref, kb_ref, vb_ref, kbg_ref, qdec_ref, kdec_ref, dec_ref, egl_ref,
                    *, tm):
    i = pl.program_id(1)
    nblk = pl.num_programs(1)
    W = DN_HEADS * DN_DK
    row = lax.broadcasted_iota(jnp.int32, (tm, W), 0)
    col = lax.broadcasted_iota(jnp.int32, (tm, W), 1)
    left_ok = jnp.logical_and(i >= 1, i < nblk - 1).astype(F32)
    right_ok = (i < nblk - 2).astype(F32)

    def conv_silu(x_ref, xl_ref, xr_ref, c0):
        x = x_ref[...].astype(F32)
        prev_row = xl_ref[7:8, :].astype(F32) * left_ok
        next_row = xr_ref[0:1, :].astype(F32) * right_ok
        x_prev = jnp.where(row == 0, prev_row, pltpu.roll(x, 1, 0))
        x_next = jnp.where(row == tm - 1, next_row, pltpu.roll(x, tm - 1, 0))
        y = (x_prev * wc_ref[0:1, c0:c0 + W] + x * wc_ref[1:2, c0:c0 + W]
             + x_next * wc_ref[2:3, c0:c0 + W])
        return _silu(y)

    def l2norm(x):
        ss = _dot3_l(x * x, mh_ref[...])
        return x * lax.rsqrt(ss + EPS)

    q = l2norm(conv_silu(q_ref, ql_ref, qr_ref, 0)) * (DN_DK ** -0.5)
    k = l2norm(conv_silu(k_ref, kl_ref, kr_ref, W))
    v = conv_silu(v_ref, vl_ref, vr_ref, 2 * W)

    ab = ab_ref[...]
    z = ab + dtb_ref[...]
    softplus = jnp.maximum(z, 0.0) + jnp.log(1.0 + jnp.exp(-jnp.abs(z)))
    g_all = -jnp.exp(alog_ref[...]) * softplus
    beta_all = jax.nn.sigmoid(ab)

    t_in = row & (DN_CHUNK - 1)
    s_in = col & (DN_CHUNK - 1)

    def put(ref, d, val):
        for h in range(DN_HEADS):
            piece = val[:, h * DN_DK:(h + 1) * DN_DK]
            if d is None:
                ref[h] = piece.astype(ref.dtype)
            else:
                ref[d, h] = piece.astype(ref.dtype)

    put(qo_ref, None, q)
    put(ko_ref, None, k)
    for d in range(2):
        g_e = _dot3_l(g_all, eg_ref[d])
        beta_e = _dot3_l(beta_all, eb_ref[d])
        gc = _dot3_r(tm_ref[d], g_e)
        g_tot = _dot3_r(ones_ref[...], g_e)
        strict = (t_in > s_in) if d == 0 else (t_in < s_in)
        causal = (t_in >= s_in) if d == 0 else (t_in <= s_in)
        diff = _dot3_r(tm_ref[d], jnp.where(strict, g_e, 0.0))
        decay = jnp.where(causal, jnp.exp(diff), 0.0)
        e_g = jnp.exp(gc)
        kb = k * beta_e
        put(kb_ref, d, kb)
        put(vb_ref, d, v * beta_e)
        put(kbg_ref, d, kb * e_g)
        put(qdec_ref, d, q * e_g)
        put(kdec_ref, d, k * jnp.exp(g_tot - gc))
        put(dec_ref, d, decay)
        e_tot = jnp.exp(g_tot)
        for h in range(DN_HEADS):
            for c in range(tm // DN_CHUNK):
                egl_ref[d, h, c:c + 1, :] = e_tot[c * DN_CHUNK:c * DN_CHUNK + 1, h * DN_DK:(h + 1) * DN_DK]


def _dn_consts():
    n = DN_BLOCK
    t = np.arange(n)
    same = (t[:, None] // DN_CHUNK) == (t[None, :] // DN_CHUNK)
    pre = same & (t[None, :] <= t[:, None])
    suf = same & (t[None, :] >= t[:, None])
    tmat = np.stack([pre, suf]).astype(np.float32)
    ones = same.astype(np.float32)
    W = DN_HEADS * DN_DK
    eg = np.zeros((2, LANES, W), np.float32)
    eb = np.zeros((2, LANES, W), np.float32)
    for d in range(2):
        for h in range(DN_HEADS):
            eg[d, d * DN_HEADS + h, h * DN_DK:(h + 1) * DN_DK] = 1.0
            eb[d, 2 * DN_HEADS + d * DN_HEADS + h, h * DN_DK:(h + 1) * DN_DK] = 1.0
    mh = np.kron(np.eye(DN_HEADS), np.ones((DN_DK, DN_DK))).astype(np.float32)
    return tuple(jnp.asarray(a, BF16) for a in (mh, eg, eb, tmat, ones))


def _dn_prep(P, AB, w_conv, a_log, dt_bias):
    B, T, _ = P.shape
    tm = DN_BLOCK
    nblk = T // tm
    W = DN_HEADS * DN_DK
    mh, eg, eb, tmat, ones = _dn_consts()
    alog_row = jnp.zeros((1, LANES), F32).at[0, :2 * DN_HEADS].set(a_log.reshape(-1))
    dtb_row = jnp.zeros((1, LANES), F32).at[0, :2 * DN_HEADS].set(dt_bias.reshape(-1))
    r8 = tm // 8

    def main(c0):
        return pl.BlockSpec((None, tm, W), lambda b, i: (b, i, c0 // W))

    def left(c0):
        return pl.BlockSpec((None, 8, W), lambda b, i: (b, jnp.maximum(i * r8 - 1, 0), c0 // W))

    def right(c0):
        return pl.BlockSpec((None, 8, W), lambda b, i: (b, jnp.minimum((i + 1) * r8, nblk * r8 - 1), c0 // W))

    def full(a):
        nd = a.ndim
        return pl.BlockSpec(a.shape, lambda b, i: (0,) * nd)

    hm = pl.BlockSpec((None, DN_HEADS, tm, DN_DK), lambda b, i: (b, 0, i, 0))
    hm2 = pl.BlockSpec((2, None, DN_HEADS, tm, DN_DK), lambda b, i: (0, b, 0, i, 0))
    sh_hm = jax.ShapeDtypeStruct((B, DN_HEADS, T, DN_DK), BF16)
    sh_hm2 = jax.ShapeDtypeStruct((2, B, DN_HEADS, T, DN_DK), BF16)
    return pl.pallas_call(
        functools.partial(_dn_prep_kernel, tm=tm),
        out_shape=(sh_hm, sh_hm, sh_hm2, sh_hm2, sh_hm2, sh_hm2, sh_hm2, sh_hm2,
                   jax.ShapeDtypeStruct((2, B, nblk, DN_HEADS, tm // DN_CHUNK, DN_DK), F32)),
        grid=(B, nblk),
        in_specs=[main(COL_BQ), main(COL_BK), main(COL_BV),
                  left(COL_BQ), left(COL_BK), left(COL_BV),
                  right(COL_BQ), right(COL_BK), right(COL_BV),
                  pl.BlockSpec((None, tm, LANES), lambda b, i: (b, i, 0)),
                  full(w_conv), full(alog_row), full(dtb_row), full(mh), full(eg), full(eb), full(tmat),
                  full(ones)],
        out_specs=(hm, hm, hm2, hm2, hm2, hm2, hm2, hm2,
                   pl.BlockSpec((2, None, None, DN_HEADS, tm // DN_CHUNK, DN_DK),
                                lambda b, i: (0, b, i, 0, 0, 0))),
        compiler_params=_cparams(("arbitrary", "arbitrary")),
        name="dn_prep",
    )(P, P, P, P, P, P, P, P, P, AB, w_conv, alog_row, dtb_row, mh, eg, eb, tmat, ones)


def _dn_scan_kernel(q_ref, k_ref, kb_ref, vb_ref, kbg_ref, qdec_ref, kdec_ref, dec_ref, egl_ref,
                    o_ref, s_scr, *, reverse, heads_per_iter):
    t = pl.program_id(1)
    n_chunks = DN_BLOCK // DN_CHUNK
    C = DN_CHUNK

    @pl.when(t == 0)
    def _():
        s_scr[...] = jnp.zeros_like(s_scr)

    r = lax.broadcasted_iota(jnp.int32, (C, C), 0)
    c = lax.broadcasted_iota(jnp.int32, (C, C), 1)
    strict = (r < c) if reverse else (r > c)
    eye = (r == c).astype(F32)
    order = list(range(n_chunks))[::-1] if reverse else list(range(n_chunks))

    def one_head(h):
        pre = {}
        for ci in order:
            rows = pl.ds(ci * C, C)
            k_c = k_ref[h, rows, :]
            dec = dec_ref[h, rows, :].astype(F32)
            low = jnp.where(strict, _dot_nt(kb_ref[h, rows, :], k_c) * dec, 0.0)
            qk = _dot_nt(q_ref[h, rows, :], k_c) * dec
            p = eye - low
            mpow = low
            for _ in range(5):
                mb = mpow.astype(BF16)
                mpow = _dot(mb, mb)
                p = p + _dot(p.astype(BF16), mpow.astype(BF16))
            pb = p.astype(BF16)
            pre[ci] = (_dot(pb, vb_ref[h, rows, :]), _dot(pb, kbg_ref[h, rows, :]), qk)
        s = s_scr[h]
        for ci in order:
            rows = pl.ds(ci * C, C)
            u, w, qk = pre[ci]
            sb = s.astype(BF16)
            v_new = u - _dot(w.astype(BF16), sb)
            vnb = v_new.astype(BF16)
            o_ref[h, rows, :] = _dot(qdec_ref[h, rows, :], sb) + _dot(qk.astype(BF16), vnb)
            s = s * egl_ref[h, ci:ci + 1, :] + _dot_tn(kdec_ref[h, rows, :], vnb)
        s_scr[h] = s

    def body(it, carry):
        for j in range(heads_per_iter):
            one_head(it * heads_per_iter + j)
        return carry

    lax.fori_loop(0, DN_HEADS // heads_per_iter, body, 0)


def _dn_scan(prep, direction, heads_per_iter=4):
    q, k, kb, vb, kbg, qdec, kdec, dec, egl = prep
    B, H, T, dk = q.shape
    tm = DN_BLOCK
    nblk = T // tm
    reverse = direction == 1

    def blk(t):
        return jnp.where(t == 0, nblk - 1, nblk - 1 - t if reverse else t - 1)

    hm = pl.BlockSpec((None, H, tm, dk), lambda b, t: (b, 0, blk(t), 0))
    hm2 = pl.BlockSpec((None, None, H, tm, dk), lambda b, t: (direction, b, 0, blk(t), 0))
    return pl.pallas_call(
        functools.partial(_dn_scan_kernel, reverse=reverse, heads_per_iter=heads_per_iter),
        out_shape=jax.ShapeDtypeStruct((B, H, T, dk), F32),
        grid=(B, nblk),
        in_specs=[hm, hm, hm2, hm2, hm2, hm2, hm2, hm2,
                  pl.BlockSpec((None, None, None, H, tm // DN_CHUNK, dk),
                               lambda b, t: (direction, b, blk(t), 0, 0, 0))],
        out_specs=hm,
        scratch_shapes=[pltpu.VMEM((H, dk, dk), F32)],
        compiler_params=_cparams(("arbitrary", "arbitrary")),
        name="dn_scan_bwd" if reverse else "dn_scan_fwd",
    )(q, k, kb, vb, kbg, qdec, kdec, dec, egl)


def _dn_post_kernel(of_ref, ob_ref, z_ref, g_ref, o_ref):
    z = z_ref[...].astype(F32)
    for h in range(DN_HEADS):
        o = of_ref[h] + ob_ref[h]
        y = o * lax.rsqrt(jnp.mean(o * o, axis=-1, keepdims=True) + EPS) * g_ref[...]
        zz = z[:, h * DN_DK:(h + 1) * DN_DK]
        o_ref[:, h * DN_DK:(h + 1) * DN_DK] = (y * _silu(zz)).astype(o_ref.dtype)


def _dn_post(o_f, o_b, P, g_out):
    B, H, T, dk = o_f.shape
    tm = DN_BLOCK
    hm = pl.BlockSpec((None, H, tm, dk), lambda b, i: (b, 0, i, 0))
    return pl.pallas_call(
        _dn_post_kernel,
        out_shape=jax.ShapeDtypeStruct((B, T, BR_W), F32),
        grid=(B, T // tm),
        in_specs=[hm, hm,
                  pl.BlockSpec((None, tm, BR_W), lambda b, i: (b, i, COL_BZ // BR_W)),
                  pl.BlockSpec((1, dk), lambda b, i: (0, 0))],
        out_specs=pl.BlockSpec((None, tm, BR_W), lambda b, i: (b, i, 0)),
        compiler_params=_cparams(("arbitrary", "arbitrary")),
        name="dn_post",
    )(o_f, o_b, P, g_out.reshape(1, dk))


def _merge_kernel(a_ref, b_ref, c_ref, ga_ref, gb_ref, gc_ref, h_ref, mod_ref, g2_ref, wb_ref, wo_ref,
                  h_out_ref, v_out_ref, *, tm, n_lat):
    i = pl.program_id(1)
    D = h_ref.shape[-1]
    m = jax.nn.sigmoid(ga_ref[...].astype(F32)) * _dot(a_ref[...], wb_ref[0])
    m = m + jax.nn.sigmoid(gb_ref[...].astype(F32)) * _dot(b_ref[...].astype(BF16), wb_ref[1])
    m = m + jax.nn.sigmoid(gc_ref[...].astype(F32)) * _dot(c_ref[...], wb_ref[2])
    y = _dot(m.astype(BF16), wo_ref[...])
    gate = jnp.where(_is_ctx_row(i, tm, n_lat), mod_ref[0:1, 2 * D:3 * D], mod_ref[1:2, 2 * D:3 * D])
    h_new = h_ref[...] + gate * y
    h_out_ref[...] = h_new
    v_out_ref[...] = _norm_mod(h_new, g2_ref[...], mod_ref, i, tm, n_lat, 3, 4).astype(BF16)


def _merge(br_a, br_b, br_c, P, h, modtab, g_norm2, w_branch, w_out, n_lat, tm):
    B, T, D = h.shape
    br = pl.BlockSpec((None, tm, BR_W), lambda b, i: (b, i, 0))

    def gate(j):
        return pl.BlockSpec((None, tm, D), lambda b, i: (b, i, COL_GATE // D + j))

    tok = pl.BlockSpec((None, tm, D), lambda b, i: (b, i, 0))
    return pl.pallas_call(
        functools.partial(_merge_kernel, tm=tm, n_lat=n_lat),
        out_shape=(jax.ShapeDtypeStruct((B, T, D), F32), jax.ShapeDtypeStruct((B, T, D), BF16)),
        grid=(B, T // tm),
        in_specs=[br, br, br, gate(0), gate(1), gate(2), tok,
                  pl.BlockSpec((None, 2, 6 * D), lambda b, i: (b, 0, 0)),
                  pl.BlockSpec((1, D), lambda b, i: (0, 0)),
                  pl.BlockSpec((3, BR_W, D), lambda b, i: (0, 0, 0)),
                  pl.BlockSpec((D, D), lambda b, i: (0, 0))],
        out_specs=(tok, tok),
        compiler_params=_cparams(("arbitrary", "arbitrary")),
        name="merge",
    )(br_a, br_b, br_c, P, P, P, h, modtab, g_norm2.reshape(1, D), w_branch, w_out)


def _router_kernel(v_ref, wr_ref, br_ref, c_ref):
    tn = v_ref.shape[0]
    G = N_GROUPS
    scores = jax.nn.sigmoid(_dot_nt(wr_ref[...], v_ref[...]))
    sel = scores + br_ref[...]
    slabs = [sel[j * G:(j + 1) * G] for j in range(GROUP_SIZE)]
    m1, m2 = slabs[0], jnp.full((G, tn), -jnp.inf, F32)
    for j in range(1, GROUP_SIZE):
        m2 = jnp.maximum(m2, jnp.minimum(m1, slabs[j]))
        m1 = jnp.maximum(m1, slabs[j])
    gs = m1 + m2
    gidx = lax.broadcasted_iota(jnp.int32, (G, tn), 0)
    grank = jnp.zeros((G, tn), jnp.int32)
    for g in range(G):
        rowv = gs[g:g + 1]
        beats = jnp.where(rowv > gs, 1, jnp.where(jnp.logical_and(rowv == gs, g < gidx), 1, 0))
        grank = grank + beats
    gmask = grank < TOPK_GROUPS
    masked = jnp.concatenate([jnp.where(gmask, s, -jnp.inf) for s in slabs], axis=0)
    pos = lax.broadcasted_iota(jnp.int32, (N_EXPERTS, tn), 0)
    orig = (pos % G) * GROUP_SIZE + pos // G
    rank = jnp.zeros((N_EXPERTS, tn), jnp.int32)
    for p in range(N_EXPERTS):
        o_p = (p % G) * GROUP_SIZE + p // G
        rowv = masked[p:p + 1]
        beats = jnp.where(rowv > masked, 1, jnp.where(jnp.logical_and(rowv == masked, o_p < orig), 1, 0))
        rank = rank + beats
    chosen = rank < TOP_K
    w = jnp.where(chosen, scores, 0.0)
    denom = jnp.sum(w, axis=0, keepdims=True) + 1e-20
    c_ref[...] = w / denom * ROUTED_SCALE


def _router(v_flat, w_router, b_router):
    N, D = v_flat.shape
    tn = 1280 if N % 1280 == 0 else 256
    perm = np.array([(p % N_GROUPS) * GROUP_SIZE + p // N_GROUPS for p in range(N_EXPERTS)])
    wr_t = w_router.T[perm].astype(BF16)
    br = b_router[perm].reshape(N_EXPERTS, 1)
    c_pos = pl.pallas_call(
        _router_kernel,
        out_shape=jax.ShapeDtypeStruct((N_EXPERTS, N), F32),
        grid=(N // tn,),
        in_specs=[pl.BlockSpec((tn, D), lambda i: (i, 0)),
                  pl.BlockSpec((N_EXPERTS, D), lambda i: (0, 0)),
                  pl.BlockSpec((N_EXPERTS, 1), lambda i: (0, 0))],
        out_specs=pl.BlockSpec((N_EXPERTS, tn), lambda i: (0, i)),
        compiler_params=_cparams(("arbitrary",)),
        name="router",
    )(v_flat, wr_t, br)
    inv = np.argsort(perm)
    return c_pos[inv]


MOE_EPS = 4


def _moe_kernel(x_ref, c_ref, wg_ref, wu_ref, wd_ref, sg_ref, su_ref, sd_ref, h_ref, mod_ref,
                o_ref, acc_ref, *, tm, n_lat):
    i = pl.program_id(1)
    e = pl.program_id(2)
    x = x_ref[...]
    D = x.shape[-1]

    @pl.when(e == 0)
    def _():
        hs = _silu(_dot(x, sg_ref[...])) * _dot(x, su_ref[...])
        acc_ref[...] = _dot(hs.astype(BF16), sd_ref[...])

    c = c_ref[...]
    parts = []
    for j in range(MOE_EPS):
        hj = _silu(_dot(x, wg_ref[j])) * _dot(x, wu_ref[j]) * c[:, j:j + 1]
        parts.append(hj.astype(BF16))
    hcat = jnp.concatenate(parts, axis=-1)
    acc_ref[...] += _dot(hcat, wd_ref[...])

    @pl.when(e == pl.num_programs(2) - 1)
    def _():
        gate = jnp.where(_is_ctx_row(i, tm, n_lat), mod_ref[0:1, 5 * D:6 * D], mod_ref[1:2, 5 * D:6 * D])
        o_ref[...] = h_ref[...] + gate * acc_ref[...]


def _moe(v, c_grp, h, modtab, wg, wu, wd, sg, su, sd, n_lat, tm):
    B, T, D = v.shape
    F = MOE_FF
    ngrp = N_EXPERTS // MOE_EPS
    tok = pl.BlockSpec((None, tm, D), lambda b, i, e: (b, i, 0))
    return pl.pallas_call(
        functools.partial(_moe_kernel, tm=tm, n_lat=n_lat),
        out_shape=jax.ShapeDtypeStruct((B, T, D), F32),
        grid=(B, T // tm, ngrp),
        in_specs=[tok,
                  pl.BlockSpec((None, None, tm, MOE_EPS), lambda b, i, e: (e, b, i, 0)),
                  pl.BlockSpec((MOE_EPS, D, F), lambda b, i, e: (e, 0, 0)),
                  pl.BlockSpec((MOE_EPS, D, F), lambda b, i, e: (e, 0, 0)),
                  pl.BlockSpec((MOE_EPS * F, D), lambda b, i, e: (e, 0)),
                  pl.BlockSpec((D, F), lambda b, i, e: (0, 0)),
                  pl.BlockSpec((D, F), lambda b, i, e: (0, 0)),
                  pl.BlockSpec((F, D), lambda b, i, e: (0, 0)),
                  tok,
                  pl.BlockSpec((None, 2, 6 * D), lambda b, i, e: (b, 0, 0))],
        out_specs=tok,
        scratch_shapes=[pltpu.VMEM((tm, D), F32)],
        compiler_params=_cparams(("arbitrary", "arbitrary", "arbitrary")),
        name="moe",
    )(v, c_grp, wg, wu, wd, sg, su, sd, h, modtab)


def _final_kernel(h_ref, g_ref, o_ref):
    x = h_ref[...]
    o_ref[...] = x * lax.rsqrt(jnp.mean(x * x, axis=-1, keepdims=True) + EPS) * g_ref[...]


def _final_norm(h, g_final, S):
    B, T, D = h.shape
    tm = 1024
    return pl.pallas_call(
        _final_kernel,
        out_shape=jax.ShapeDtypeStruct((B, S, D), F32),
        grid=(B, S // tm),
        in_specs=[pl.BlockSpec((None, tm, D), lambda b, i: (b, i, 0)),
                  pl.BlockSpec((1, D), lambda b, i: (0, 0))],
        out_specs=pl.BlockSpec((None, tm, D), lambda b, i: (b, i, 0)),
        compiler_params=_cparams(("arbitrary", "arbitrary")),
        name="final_norm",
    )(h, g_final.reshape(1, D))


def _rope_tables(S, n_ctx):
    rows = S // GRID_W
    row = jnp.repeat(jnp.arange(rows, dtype=F32), GRID_W)
    col = (jnp.arange(S) % GRID_W).astype(F32)
    axis_dim = HEAD_DIM // 2
    inv = 1.0 / (ROPE_THETA ** (jnp.arange(0, axis_dim, 2, dtype=F32) / axis_dim))
    ang = jnp.concatenate([row[:, None] * inv, col[:, None] * inv], axis=-1)
    ang = jnp.concatenate([ang, ang], axis=-1)
    cos = jnp.concatenate([jnp.cos(ang), jnp.ones((n_ctx, HEAD_DIM), F32)], axis=0)
    sin = jnp.concatenate([jnp.sin(ang), jnp.zeros((n_ctx, HEAD_DIM), F32)], axis=0)
    first = (jnp.arange(HEAD_DIM) < HEAD_DIM // 2)[None, :]
    sin_a = jnp.where(first, -sin, 0.0)
    sin_b = jnp.where(first, 0.0, sin)
    return tuple(jnp.tile(t, (1, 2)) for t in (cos, sin_a, sin_b))


def _permute_w_in(w):
    splits = np.cumsum([512, 128, 128, 1536, 512, 16, 16, 512, 512, 512, 3072])[:-1].tolist()
    aq, ak, av, bqkv, bz, ba, bb, cq, ck, cv, gate = jnp.split(w, splits, axis=-1)
    pad = jnp.zeros((w.shape[0], IN_W_PAD - COL_AB - 32), w.dtype)
    out = jnp.concatenate([aq, cq, ck, cv, bz, bqkv, gate, ak, av, ba, bb, pad], axis=-1)
    return out.astype(BF16)


def kernel(x, c, ctx, c_ctx, w_mod, b_mod, g_norm1, g_norm2, w_in, g_qnorm, g_knorm, w_conv, a_log, dt_bias, g_dn_out, lam_qk, g_subln, w_branch, w_out, w_router, b_router, w_e_gate, w_e_up, w_e_down, w_s_gate, w_s_up, w_s_down, g_final):
    B, S, D = x.shape
    n_ctx = ctx.shape[1]
    T = n_ctx + S
    L = w_mod.shape[0]
    N = B * T
    assert n_ctx == DN_BLOCK and S % 1024 == 0 and D == D_MODEL
    tm_tok = 1280 if T % 1280 == 0 else 256
    tm_merge = 640 if T % 640 == 0 else 256

    cond = jnp.zeros((8, D), F32).at[0].set(c_ctx).at[1:1 + B].set(c)
    mod = _mod_vectors(cond, w_mod, b_mod)
    tabs = _rope_tables(S, n_ctx)
    h = jnp.concatenate([x, ctx], axis=1)
    tk = 1024
    nch = S // tk

    for l in range(L):
        lambda_init = 0.8 - 0.6 * math.exp(-0.3 * l)
        modtab = jnp.stack([jnp.broadcast_to(mod[l, 0], (B, 6 * D)), mod[l, 1:1 + B]], axis=1)
        P, AB = _in_projection(h, modtab, g_norm1[l], _permute_w_in(w_in[l]), S, tm_tok)

        qa, ka, qd, kd = _attn_prep(P, tabs, g_qnorm[l], g_knorm[l])
        va = P[:, :, COL_AV:COL_AV + LANES]
        vta_lat = va[:, :S].reshape(B, nch, tk, LANES).transpose(0, 1, 3, 2)
        vta_ctx = va[:, S:].transpose(0, 2, 1)
        vd = P[:, :, COL_CV:COL_CV + BR_W]
        vtd_lat = vd[:, :S].reshape(B, nch, tk, DF_HEADS, LANES).transpose(0, 3, 1, 4, 2)
        vtd_ctx = vd[:, S:].reshape(B, n_ctx, DF_HEADS, LANES).transpose(0, 2, 3, 1)
        dx = (lam_qk[l], g_subln[l].reshape(LANES, 1))
        br_a = jnp.concatenate([_flash("gqa", qa, ka, vta_lat, vta_ctx, n_ctx, False),
                                _flash("gqa", qa, ka, vta_lat, vta_ctx, n_ctx, True)], axis=-1).transpose(0, 2, 1)
        br_c = jnp.concatenate([_flash("diff", qd, kd, vtd_lat, vtd_ctx, n_ctx, False, dx, lambda_init),
                                _flash("diff", qd, kd, vtd_lat, vtd_ctx, n_ctx, True, dx, lambda_init)],
                               axis=-1).transpose(0, 2, 1)

        prep = _dn_prep(P, AB, w_conv[l], a_log[l], dt_bias[l])
        o_f = _dn_scan(prep, 0)
        o_b = _dn_scan(prep, 1)
        br_b = _dn_post(o_f, o_b, P, g_dn_out[l])

        h, v = _merge(br_a, br_b, br_c, P, h, modtab, g_norm2[l], w_branch[l].astype(BF16),
                      w_out[l].astype(BF16), S, tm_merge)

        c_exp = _router(v.reshape(N, D), w_router[l], b_router[l])
        c_grp = c_exp.reshape(N_EXPERTS // MOE_EPS, MOE_EPS, B, T).transpose(0, 2, 3, 1)
        h = _moe(v, c_grp, h, modtab,
                 w_e_gate[l].astype(BF16), w_e_up[l].astype(BF16),
                 w_e_down[l].astype(BF16).reshape(N_EXPERTS * MOE_FF, D),
                 w_s_gate[l].astype(BF16), w_s_up[l].astype(BF16), w_s_down[l].astype(BF16),
                 S, tm_merge)

    return _final_norm(h, g_final, S)
```

```python
import functools
import math

import numpy as np
import jax
import jax.numpy as jnp
from jax import lax
from jax.experimental import pallas as pl
from jax.experimental.pallas import tpu as pltpu

F32 = jnp.float32
BF16 = jnp.bfloat16

D_MODEL = 1024
GRID_W = 64
EPS = 1e-6
ROPE_THETA = 10000.0
HEAD_DIM = 64
LANES = 128
A_HEADS = 8
A_KV_HEADS = 2
A_GROUP = A_HEADS // A_KV_HEADS
DN_HEADS = 8
DN_DK = 64
DN_CHUNK = 64
DN_BLOCK = 256
DF_HEADS = 4
N_EXPERTS = 64
TOP_K = 6
N_GROUPS = 8
TOPK_GROUPS = 4
GROUP_SIZE = N_EXPERTS // N_GROUPS
MOE_FF = 256
ROUTED_SCALE = 2.5
BR_W = 512

COL_AQ, COL_CQ, COL_CK, COL_CV, COL_BZ = 0, 512, 1024, 1536, 2048
COL_BQ, COL_BK, COL_BV = 2560, 3072, 3584
COL_GATE = 4096
COL_AK, COL_AV, COL_AB = 7168, 7296, 7424
IN_W_PAD = 7680
IN_TN = 1280

VMEM_LIMIT = 56 * 1024 * 1024


def _cparams(sem):
    return pltpu.CompilerParams(dimension_semantics=sem, vmem_limit_bytes=VMEM_LIMIT)


def _dot(a, b):
    return jnp.dot(a, b, preferred_element_type=F32)


def _dot_nt(a, b):
    return lax.dot_general(a, b, (((1,), (1,)), ((), ())), preferred_element_type=F32)


def _dot_tn(a, b):
    return lax.dot_general(a, b, (((0,), (0,)), ((), ())), preferred_element_type=F32)


def _split3(x):
    hi = x.astype(BF16)
    r = x - hi.astype(F32)
    mid = r.astype(BF16)
    lo = (r - mid.astype(F32)).astype(BF16)
    return hi, mid, lo


def _dot3_l(x, m):
    hi, mid, lo = _split3(x)
    return _dot(hi, m) + _dot(mid, m) + _dot(lo, m)


def _dot3_r(m, x):
    hi, mid, lo = _split3(x)
    return _dot(m, hi) + _dot(m, mid) + _dot(m, lo)


def _silu(x):
    return x * jax.nn.sigmoid(x)


def _mod_kernel(cond_ref, w_ref, b_ref, o_ref):
    a = _silu(cond_ref[...]).astype(BF16)
    o_ref[...] = _dot(a, w_ref[...].astype(BF16)) + b_ref[...]


def _mod_vectors(cond, w_mod, b_mod):
    L, D, W = w_mod.shape
    tn = 1024
    return pl.pallas_call(
        _mod_kernel,
        out_shape=jax.ShapeDtypeStruct((L, 8, W), F32),
        grid=(L, W // tn),
        in_specs=[pl.BlockSpec((8, D), lambda l, j: (0, 0)),
                  pl.BlockSpec((None, D, tn), lambda l, j: (l, 0, j)),
                  pl.BlockSpec((None, 1, tn), lambda l, j: (l, 0, j))],
        out_specs=pl.BlockSpec((None, 8, tn), lambda l, j: (l, 0, j)),
        compiler_params=_cparams(("arbitrary", "arbitrary")),
        name="mod_vectors",
    )(cond, w_mod, b_mod.reshape(L, 1, W))


def _is_ctx_row(blk, tm, n_lat):
    return blk * tm + lax.broadcasted_iota(jnp.int32, (tm, 1), 0) >= n_lat


def _norm_mod(x, g, mod_ref, blk, tm, n_lat, sh, sc):
    D = x.shape[-1]
    y = x * lax.rsqrt(jnp.mean(x * x, axis=-1, keepdims=True) + EPS) * g
    is_ctx = _is_ctx_row(blk, tm, n_lat)
    scale = jnp.where(is_ctx, mod_ref[0:1, sc * D:(sc + 1) * D], mod_ref[1:2, sc * D:(sc + 1) * D])
    shift = jnp.where(is_ctx, mod_ref[0:1, sh * D:(sh + 1) * D], mod_ref[1:2, sh * D:(sh + 1) * D])
    return y * (1.0 + scale) + shift


def _inproj_kernel(h_ref, mod_ref, g_ref, w_ref, p_ref, ab_ref, u_scr, *, tm, n_lat, ab_off):
    i = pl.program_id(1)
    j = pl.program_id(2)

    @pl.when(j == 0)
    def _():
        u_scr[...] = _norm_mod(h_ref[...], g_ref[...], mod_ref, i, tm, n_lat, 0, 1).astype(BF16)

    r = _dot(u_scr[...], w_ref[...])
    p_ref[...] = r.astype(BF16)

    @pl.when(j == pl.num_programs(2) - 1)
    def _():
        ab_ref[...] = r[:, ab_off:ab_off + LANES]


def _in_projection(h, modtab, g_norm, w_perm, n_lat, tm):
    B, T, D = h.shape
    ncol = IN_W_PAD // IN_TN
    kern = functools.partial(_inproj_kernel, tm=tm, n_lat=n_lat, ab_off=COL_AB - (ncol - 1) * IN_TN)
    return pl.pallas_call(
        kern,
        out_shape=(jax.ShapeDtypeStruct((B, T, IN_W_PAD), BF16),
                   jax.ShapeDtypeStruct((B, T, LANES), F32)),
        grid=(B, T // tm, ncol),
        in_specs=[pl.BlockSpec((None, tm, D), lambda b, i, j: (b, i, 0)),
                  pl.BlockSpec((None, 2, 6 * D), lambda b, i, j: (b, 0, 0)),
                  pl.BlockSpec((1, D), lambda b, i, j: (0, 0)),
                  pl.BlockSpec((D, IN_TN), lambda b, i, j: (0, j))],
        out_specs=(pl.BlockSpec((None, tm, IN_TN), lambda b, i, j: (b, i, j)),
                   pl.BlockSpec((None, tm, LANES), lambda b, i, j: (b, i, 0))),
        scratch_shapes=[pltpu.VMEM((tm, D), BF16)],
        compiler_params=_cparams(("arbitrary", "arbitrary", "arbitrary")),
        name="in_projection",
    )(h, modtab, g_norm.reshape(1, D), w_perm)


def _rope(x, cos, sin_a, sin_b):
    return x * cos + pltpu.roll(x, LANES - 32, 1) * sin_a + pltpu.roll(x, 32, 1) * sin_b


def _attn_prep_kernel(aq_ref, ak_ref, cq_ref, ck_ref, cos_ref, sa_ref, sb_ref, gq_ref, gk_ref, m_ref,
                      qa_ref, ka_ref, qd_ref, kd_ref):
    cos, sa, sb = cos_ref[...], sa_ref[...], sb_ref[...]
    m64 = m_ref[...]
    half = lax.broadcasted_iota(jnp.int32, cos.shape, 1) // HEAD_DIM
    scale = HEAD_DIM ** -0.5 * math.log2(math.e)

    def head_norm(x, g):
        ms = _dot3_l(x * x, m64) * (1.0 / HEAD_DIM)
        return x * lax.rsqrt(ms + EPS) * g

    for s in range(A_HEADS // 2):
        x = aq_ref[:, s * LANES:(s + 1) * LANES].astype(F32)
        y = _rope(head_norm(x, gq_ref[...]), cos, sa, sb) * scale
        y_sw = pltpu.roll(y, HEAD_DIM, 1)
        kvh = (2 * s) // A_GROUP
        for hh in range(2):
            src = y if hh == kvh else y_sw
            qa_ref[2 * s + hh] = jnp.where(half == kvh, src, 0.0).astype(BF16)
    xk = ak_ref[...].astype(F32)
    ka_ref[...] = _rope(head_norm(xk, gk_ref[...]), cos, sa, sb).astype(BF16)
    for h in range(DF_HEADS):
        x = cq_ref[:, h * LANES:(h + 1) * LANES].astype(F32)
        y = _rope(x, cos, sa, sb) * scale
        qd_ref[2 * h] = jnp.where(half == 0, y, 0.0).astype(BF16)
        qd_ref[2 * h + 1] = jnp.where(half == 1, y, 0.0).astype(BF16)
        xk = ck_ref[:, h * LANES:(h + 1) * LANES].astype(F32)
        kd_ref[:, h * LANES:(h + 1) * LANES] = _rope(xk, cos, sa, sb).astype(BF16)


def _attn_prep(P, tabs, g_q, g_k):
    B, T, _ = P.shape
    tm = 256
    cos2, sin_a, sin_b = tabs
    m64 = jnp.asarray(np.kron(np.eye(2), np.ones((HEAD_DIM, HEAD_DIM))), BF16)
    gq2 = jnp.tile(g_q, 2).reshape(1, LANES)
    gk2 = jnp.tile(g_k, 2).reshape(1, LANES)
    tab_spec = pl.BlockSpec((tm, LANES), lambda b, i: (i, 0))
    vec_spec = pl.BlockSpec((1, LANES), lambda b, i: (0, 0))
    return pl.pallas_call(
        _attn_prep_kernel,
        out_shape=(jax.ShapeDtypeStruct((B, A_HEADS, T, LANES), BF16),
                   jax.ShapeDtypeStruct((B, T, LANES), BF16),
                   jax.ShapeDtypeStruct((B, 2 * DF_HEADS, T, LANES), BF16),
                   jax.ShapeDtypeStruct((B, T, 512), BF16)),
        grid=(B, T // tm),
        in_specs=[pl.BlockSpec((None, tm, 512), lambda b, i: (b, i, COL_AQ // 512)),
                  pl.BlockSpec((None, tm, LANES), lambda b, i: (b, i, COL_AK // LANES)),
                  pl.BlockSpec((None, tm, 512), lambda b, i: (b, i, COL_CQ // 512)),
                  pl.BlockSpec((None, tm, 512), lambda b, i: (b, i, COL_CK // 512)),
                  tab_spec, tab_spec, tab_spec, vec_spec, vec_spec,
                  pl.BlockSpec((LANES, LANES), lambda b, i: (0, 0))],
        out_specs=(pl.BlockSpec((None, A_HEADS, tm, LANES), lambda b, i: (b, 0, i, 0)),
                   pl.BlockSpec((None, tm, LANES), lambda b, i: (b, i, 0)),
                   pl.BlockSpec((None, 2 * DF_HEADS, tm, LANES), lambda b, i: (b, 0, i, 0)),
                   pl.BlockSpec((None, tm, 512), lambda b, i: (b, i, 0))),
        compiler_params=_cparams(("arbitrary", "arbitrary")),
        name="attn_prep",
    )(P, P, P, P, cos2, sin_a, sin_b, gq2, gk2, m64)


FLASH_TQ = 256


def _flash_kernel(*refs, mode, half, tk, n_lat, n_ctx, n_lat_chunks, ctx_only, lambda_init):
    refs = list(refs)
    q_ref, k_ref = refs[0], refs[1]
    pos = 2
    vt_lat_ref = None
    if not ctx_only:
        vt_lat_ref = refs[pos]
        pos += 1
    vt_ctx_ref = refs[pos]
    pos += 1
    if mode == "diff":
        lam_ref, g_ref = refs[pos], refs[pos + 1]
        pos += 2
    o_ref, sb_scr, mb_scr = refs[pos:pos + 3]
    m_scr, l_scr, acc_scr = refs[pos + 3:pos + 5], refs[pos + 5:pos + 7], refs[pos + 7:pos + 9]
    q = q_ref[...].reshape(2 * half, LANES)
    q_half = (q[:half], q[half:])

    for j in range(2):
        m_scr[j][...] = jnp.full((1, half), -jnp.inf, F32)
        l_scr[j][...] = jnp.zeros((1, half), F32)
        acc_scr[j][...] = jnp.zeros((LANES, half), F32)

    def scores(k, j):
        s = _dot_nt(k, q_half[j])
        return s, jnp.max(s, axis=0, keepdims=True)

    def update(j, s, m_cur, vt):
        m_old = m_scr[j][...]
        m_new = jnp.maximum(m_old, m_cur)
        alpha = jnp.exp2(m_old - m_new)
        p = jnp.exp2(s - m_new)
        l_scr[j][...] = alpha * l_scr[j][...] + jnp.sum(p, axis=0, keepdims=True)
        acc_scr[j][...] = alpha * acc_scr[j][...] + _dot(vt, p.astype(BF16))
        m_scr[j][...] = m_new

    k_ctx = k_ref[pl.ds(n_lat, n_ctx), :]
    for j in range(2):
        s, m_cur = scores(k_ctx, j)
        update(j, s, m_cur, vt_ctx_ref[...])

    if not ctx_only:
        def k_chunk(c):
            return k_ref[pl.ds(pl.multiple_of(c * tk, tk), tk), :]

        def stage(c, first):
            k = k_chunk(c)
            s_a, m_a = scores(k, 0)
            if not first:
                update(1, sb_scr[...], mb_scr[...], vt_lat_ref[c - 1])
            s_b, m_b = scores(k, 1)
            sb_scr[...] = s_b
            mb_scr[...] = m_b
            update(0, s_a, m_a, vt_lat_ref[c])

        stage(0, True)

        def body(c, carry):
            stage(c, False)
            return carry
        lax.fori_loop(1, n_lat_chunks, body, 0)
        update(1, sb_scr[...], mb_scr[...], vt_lat_ref[n_lat_chunks - 1])

    if mode == "gqa":
        kvh = pl.program_id(1)
        per = half // FLASH_TQ
        for j in range(2):
            o = acc_scr[j][pl.ds(pl.multiple_of(kvh * HEAD_DIM, HEAD_DIM), HEAD_DIM), :] / l_scr[j][...]
            for g in range(per):
                o_ref[(j * per + g) * HEAD_DIM:(j * per + g + 1) * HEAD_DIM, :] = (
                    o[:, g * FLASH_TQ:(g + 1) * FLASH_TQ].astype(o_ref.dtype))
    else:
        lq = lam_ref[...]
        lam = (jnp.exp(jnp.sum(lq[0:1] * lq[1:2], axis=-1, keepdims=True))
               - jnp.exp(jnp.sum(lq[2:3] * lq[3:4], axis=-1, keepdims=True)) + lambda_init)
        d = acc_scr[0][...] / l_scr[0][...] - lam * (acc_scr[1][...] / l_scr[1][...])
        y = d * lax.rsqrt(jnp.mean(d * d, axis=0, keepdims=True) + EPS) * g_ref[...]
        o_ref[...] = (y * (1.0 - lambda_init)).astype(o_ref.dtype)


def _flash(mode, q, k, vt_lat, vt_ctx, n_ctx, ctx_only, extra=(), lambda_init=0.0):
    B, NQ, T, _ = q.shape
    S = T - n_ctx
    tk = vt_lat.shape[-1]
    n_chunks = vt_lat.shape[-3]
    n_maps = A_GROUP if mode == "gqa" else 2
    n_heads = NQ // n_maps
    if ctx_only:
        tq, n_q, q0, rows = n_ctx, 1, S // n_ctx, n_ctx
    else:
        tq = FLASH_TQ if mode == "gqa" else 2 * FLASH_TQ
        n_q, q0, rows = S // tq, 0, S
    half = n_maps * tq // 2
    kern = functools.partial(_flash_kernel, mode=mode, half=half, tk=tk, n_lat=S, n_ctx=n_ctx,
                             n_lat_chunks=n_chunks, ctx_only=ctx_only, lambda_init=lambda_init)
    in_specs = [pl.BlockSpec((None, n_maps, tq, LANES), lambda b, h, i: (b, h, i + q0, 0))]
    args = [q, k]
    if mode == "gqa":
        in_specs.append(pl.BlockSpec((None, T, LANES), lambda b, h, i: (b, 0, 0)))
        if not ctx_only:
            in_specs.append(pl.BlockSpec((None, n_chunks, LANES, tk), lambda b, h, i: (b, 0, 0, 0)))
            args.append(vt_lat)
        in_specs.append(pl.BlockSpec((None, LANES, n_ctx), lambda b, h, i: (b, 0, 0)))
        out_spec = pl.BlockSpec((None, A_GROUP * HEAD_DIM, tq), lambda b, h, i: (b, h, i))
    else:
        in_specs.append(pl.BlockSpec((None, T, LANES), lambda b, h, i: (b, 0, h)))
        if not ctx_only:
            in_specs.append(pl.BlockSpec((None, None, n_chunks, LANES, tk), lambda b, h, i: (b, h, 0, 0, 0)))
            args.append(vt_lat)
        in_specs.append(pl.BlockSpec((None, None, LANES, n_ctx), lambda b, h, i: (b, h, 0, 0)))
        out_spec = pl.BlockSpec((None, LANES, tq), lambda b, h, i: (b, h, i))
    args.append(vt_ctx)
    if mode == "diff":
        in_specs += [pl.BlockSpec((4, HEAD_DIM), lambda b, h, i: (0, 0)),
                     pl.BlockSpec((LANES, 1), lambda b, h, i: (0, 0))]
    return pl.pallas_call(
        kern,
        out_shape=jax.ShapeDtypeStruct((B, BR_W, rows), BF16),
        grid=(B, n_heads, n_q),
        in_specs=in_specs,
        out_specs=out_spec,
        scratch_shapes=([pltpu.VMEM((tk, half), F32)] + [pltpu.VMEM((1, half), F32)] * 5
                        + [pltpu.VMEM((LANES, half), F32)] * 2),
        compiler_params=_cparams(("arbitrary", "arbitrary", "arbitrary")),
        name="flash_" + mode + ("_ctx" if ctx_only else ""),
    )(*args, *extra)


def _dn_prep_kernel(q_ref, k_ref, v_ref, ql_ref, kl_ref, vl_ref, qr_ref, kr_ref, vr_ref, ab_ref,
                    wc_ref, alog_ref, dtb_ref, mh_ref, eg_ref, eb_ref, tm_ref, ones_ref,
                    qo_ref, ko_ref, kb_ref, vbk_ref, qdec_ref, kdecbd_ref, decbd_ref, egl_ref,
                    *, tm):
    i = pl.program_id(1)
    nblk = pl.num_programs(1)
    W = DN_HEADS * DN_DK
    row = lax.broadcasted_iota(jnp.int32, (tm, W), 0)
    col = lax.broadcasted_iota(jnp.int32, (tm, W), 1)
    left_ok = jnp.logical_and(i >= 1, i < nblk - 1).astype(F32)
    right_ok = (i < nblk - 2).astype(F32)

    def conv_silu(x_ref, xl_ref, xr_ref, c0):
        x = x_ref[...].astype(F32)
        prev_row = xl_ref[7:8, :].astype(F32) * left_ok
        next_row = xr_ref[0:1, :].astype(F32) * right_ok
        x_prev = jnp.where(row == 0, prev_row, pltpu.roll(x, 1, 0))
        x_next = jnp.where(row == tm - 1, next_row, pltpu.roll(x, tm - 1, 0))
        y = (x_prev * wc_ref[0:1, c0:c0 + W] + x * wc_ref[1:2, c0:c0 + W]
             + x_next * wc_ref[2:3, c0:c0 + W])
        return _silu(y)

    def l2norm(x):
        ss = _dot3_l(x * x, mh_ref[...])
        return x * lax.rsqrt(ss + EPS)

    q = l2norm(conv_silu(q_ref, ql_ref, qr_ref, 0)) * (DN_DK ** -0.5)
    k = l2norm(conv_silu(k_ref, kl_ref, kr_ref, W))
    v = conv_silu(v_ref, vl_ref, vr_ref, 2 * W)

    ab = ab_ref[...]
    z = ab + dtb_ref[...]
    softplus = jnp.maximum(z, 0.0) + jnp.log(1.0 + jnp.exp(-jnp.abs(z)))
    g_all = -jnp.exp(alog_ref[...]) * softplus
    beta_all = jax.nn.sigmoid(ab)

    t_in = row & (DN_CHUNK - 1)
    s_in = col & (DN_CHUNK - 1)

    def put(ref, d, val):
        for h in range(DN_HEADS):
            piece = val[:, h * DN_DK:(h + 1) * DN_DK]
            if d is None:
                ref[h] = piece.astype(ref.dtype)
            else:
                ref[d, h] = piece.astype(ref.dtype)

    lo = lax.broadcasted_iota(jnp.int32, (tm, LANES), 1) < DN_DK
    n_chunks = tm // DN_CHUNK
    bd_shape = (tm, n_chunks * DN_CHUNK)
    same_chunk = (lax.broadcasted_iota(jnp.int32, bd_shape, 0) // DN_CHUNK
                  == lax.broadcasted_iota(jnp.int32, bd_shape, 1) // DN_CHUNK)

    def pair(a, b, h):
        sa = a[:, (h // 2) * LANES:(h // 2 + 1) * LANES]
        sb = b[:, (h // 2) * LANES:(h // 2 + 1) * LANES]
        if h % 2 == 0:
            return jnp.where(lo, sa, pltpu.roll(sb, DN_DK, 1))
        return jnp.where(lo, pltpu.roll(sa, DN_DK, 1), sb)

    def put_block_diag(ref, d, val):
        for h in range(DN_HEADS):
            x = pair(val, val, h)
            ref[d, h] = jnp.where(same_chunk, jnp.concatenate([x] * (n_chunks // 2), axis=1), 0.0).astype(ref.dtype)

    put(qo_ref, None, q)
    put(ko_ref, None, k)
    for d in range(2):
        g_e = _dot3_l(g_all, eg_ref[d])
        beta_e = _dot3_l(beta_all, eb_ref[d])
        gc = _dot3_r(tm_ref[d], g_e)
        g_tot = _dot3_r(ones_ref[...], g_e)
        strict = (t_in > s_in) if d == 0 else (t_in < s_in)
        causal = (t_in >= s_in) if d == 0 else (t_in <= s_in)
        diff = _dot3_r(tm_ref[d], jnp.where(strict, g_e, 0.0))
        decay = jnp.where(causal, jnp.exp(diff), 0.0)
        e_g = jnp.exp(gc)
        kb = k * beta_e
        put(kb_ref, d, kb)
        vb, kbg = v * beta_e, kb * e_g
        for h in range(DN_HEADS):
            vbk_ref[d, h] = pair(vb, kbg, h).astype(vbk_ref.dtype)
        put(qdec_ref, d, q * e_g)
        put_block_diag(kdecbd_ref, d, k * jnp.exp(g_tot - gc))
        put_block_diag(decbd_ref, d, decay)
        e_tot = jnp.exp(g_tot)
        for h in range(DN_HEADS):
            for c in range(tm // DN_CHUNK):
                egl_ref[d, h, c:c + 1, :] = e_tot[c * DN_CHUNK:c * DN_CHUNK + 1, h * DN_DK:(h + 1) * DN_DK]


def _dn_consts():
    n = DN_BLOCK
    t = np.arange(n)
    same = (t[:, None] // DN_CHUNK) == (t[None, :] // DN_CHUNK)
    pre = same & (t[None, :] <= t[:, None])
    suf = same & (t[None, :] >= t[:, None])
    tmat = np.stack([pre, suf]).astype(np.float32)
    ones = same.astype(np.float32)
    W = DN_HEADS * DN_DK
    eg = np.zeros((2, LANES, W), np.float32)
    eb = np.zeros((2, LANES, W), np.float32)
    for d in range(2):
        for h in range(DN_HEADS):
            eg[d, d * DN_HEADS + h, h * DN_DK:(h + 1) * DN_DK] = 1.0
            eb[d, 2 * DN_HEADS + d * DN_HEADS + h, h * DN_DK:(h + 1) * DN_DK] = 1.0
    mh = np.kron(np.eye(DN_HEADS), np.ones((DN_DK, DN_DK))).astype(np.float32)
    return tuple(jnp.asarray(a, BF16) for a in (mh, eg, eb, tmat, ones))


def _dn_prep(P, AB, w_conv, a_log, dt_bias):
    B, T, _ = P.shape
    tm = DN_BLOCK
    nblk = T // tm
    W = DN_HEADS * DN_DK
    mh, eg, eb, tmat, ones = _dn_consts()
    alog_row = jnp.zeros((1, LANES), F32).at[0, :2 * DN_HEADS].set(a_log.reshape(-1))
    dtb_row = jnp.zeros((1, LANES), F32).at[0, :2 * DN_HEADS].set(dt_bias.reshape(-1))
    r8 = tm // 8

    def main(c0):
        return pl.BlockSpec((None, tm, W), lambda b, i: (b, i, c0 // W))

    def left(c0):
        return pl.BlockSpec((None, 8, W), lambda b, i: (b, jnp.maximum(i * r8 - 1, 0), c0 // W))

    def right(c0):
        return pl.BlockSpec((None, 8, W), lambda b, i: (b, jnp.minimum((i + 1) * r8, nblk * r8 - 1), c0 // W))

    def full(a):
        nd = a.ndim
        return pl.BlockSpec(a.shape, lambda b, i: (0,) * nd)

    hm = pl.BlockSpec((None, DN_HEADS, tm, DN_DK), lambda b, i: (b, 0, i, 0))

    def hm2(width):
        return pl.BlockSpec((2, None, DN_HEADS, tm, width), lambda b, i: (0, b, 0, i, 0))

    def sh_hm2(width):
        return jax.ShapeDtypeStruct((2, B, DN_HEADS, T, width), BF16)

    sh_hm = jax.ShapeDtypeStruct((B, DN_HEADS, T, DN_DK), BF16)
    return pl.pallas_call(
        functools.partial(_dn_prep_kernel, tm=tm),
        out_shape=(sh_hm, sh_hm, sh_hm2(DN_DK), sh_hm2(2 * DN_DK), sh_hm2(DN_DK), sh_hm2(tm), sh_hm2(tm),
                   jax.ShapeDtypeStruct((2, B, nblk, DN_HEADS, tm // DN_CHUNK, DN_DK), F32)),
        grid=(B, nblk),
        in_specs=[main(COL_BQ), main(COL_BK), main(COL_BV),
                  left(COL_BQ), left(COL_BK), left(COL_BV),
                  right(COL_BQ), right(COL_BK), right(COL_BV),
                  pl.BlockSpec((None, tm, LANES), lambda b, i: (b, i, 0)),
                  full(w_conv), full(alog_row), full(dtb_row), full(mh), full(eg), full(eb), full(tmat),
                  full(ones)],
        out_specs=(hm, hm, hm2(DN_DK), hm2(2 * DN_DK), hm2(DN_DK), hm2(tm), hm2(tm),
                   pl.BlockSpec((2, None, None, DN_HEADS, tm // DN_CHUNK, DN_DK),
                                lambda b, i: (0, b, i, 0, 0, 0))),
        compiler_params=_cparams(("arbitrary", "arbitrary")),
        name="dn_prep",
    )(P, P, P, P, P, P, P, P, P, AB, w_conv, alog_row, dtb_row, mh, eg, eb, tmat, ones)


def _dn_chunk_kernel(q_ref, k_ref, kb_ref, vbk_ref, qdec_ref, kdecbd_ref, decbd_ref,
                     qeff_ref, o0_ref, mk_ref, n_ref):
    n = DN_BLOCK
    r = lax.broadcasted_iota(jnp.int32, (n, n), 0)
    c = lax.broadcasted_iota(jnp.int32, (n, n), 1)
    off_diag = r != c
    eye = (r == c).astype(F32)
    heads = range(DN_HEADS)
    low, qk = [], []
    for h in heads:
        dec = decbd_ref[h].astype(F32)
        k = k_ref[h]
        low.append(jnp.where(off_diag, _dot_nt(kb_ref[h], k) * dec, 0.0))
        qk.append((_dot_nt(q_ref[h], k) * dec).astype(BF16))
    p = [eye - m for m in low]
    mpow = low
    for _ in range(5):
        mpow = [_dot(m.astype(BF16), m.astype(BF16)) for m in mpow]
        p = [a + _dot(a.astype(BF16), m.astype(BF16)) for a, m in zip(p, mpow)]
    uw = [_dot(p[h].astype(BF16), vbk_ref[h]).astype(BF16) for h in heads]
    qkuw = [_dot(qk[h], uw[h]) for h in heads]
    for h in heads:
        o0_ref[h] = qkuw[h][:, :DN_DK]
        qeff_ref[h] = (qdec_ref[h].astype(F32) - qkuw[h][:, DN_DK:]).astype(qeff_ref.dtype)
        kt = _dot_tn(kdecbd_ref[h], uw[h])
        n_ref[h] = kt[:, :DN_DK]
        mk_ref[h] = kt[:, DN_DK:].astype(mk_ref.dtype)


def _dn_chunk(prep):
    q, k, kb, vbk, qdec, kdecbd, decbd, _ = prep
    B, H, T, dk = q.shape
    tm = DN_BLOCK
    hm = pl.BlockSpec((None, H, tm, dk), lambda d, b, i: (b, 0, i, 0))

    def hm2(width):
        return pl.BlockSpec((None, None, H, tm, width), lambda d, b, i: (d, b, 0, i, 0))

    def sh(dtype):
        return jax.ShapeDtypeStruct((2, B, H, T, dk), dtype)

    return pl.pallas_call(
        _dn_chunk_kernel,
        out_shape=(sh(BF16), sh(F32), sh(BF16), sh(F32)),
        grid=(2, B, T // tm),
        in_specs=[hm, hm, hm2(dk), hm2(2 * dk), hm2(dk), hm2(tm), hm2(tm)],
        out_specs=(hm2(dk), hm2(dk), hm2(dk), hm2(dk)),
        compiler_params=_cparams(("arbitrary", "arbitrary", "arbitrary")),
        name="dn_chunk",
    )(q, k, kb, vbk, qdec, kdecbd, decbd)


def _dn_scan_kernel(*refs, n_batch):
    ins = (refs[0:5], refs[5:10])
    o_refs = refs[10:12]
    s_scr = refs[12]
    t = pl.program_id(0)
    n_chunks = DN_BLOCK // DN_CHUNK
    C = DN_CHUNK

    @pl.when(t == 0)
    def _():
        s_scr[...] = jnp.zeros_like(s_scr)

    for step in range(n_chunks):
        for d in range(2):
            ci = step if d == 0 else n_chunks - 1 - step
            rows = slice(ci * C, (ci + 1) * C)
            qeff_ref, o0_ref, mk_ref, n_ref, egl_ref = ins[d]
            for b in range(n_batch):
                for h in range(DN_HEADS):
                    s = s_scr[d, b, h]
                    sb = s.astype(BF16)
                    o_refs[d][b, h, rows, :] = _dot(qeff_ref[b, h, rows, :], sb) + o0_ref[b, h, rows, :]
                    s_scr[d, b, h] = (s * egl_ref[b, h, ci:ci + 1, :] - _dot(mk_ref[b, h, rows, :], sb)
                                      + n_ref[b, h, rows, :])


def _dn_scan(maps, egl):
    qeff, o0, mk, nn = maps
    _, B, H, T, dk = qeff.shape
    tm = DN_BLOCK
    nblk = T // tm

    def blk(d, t):
        return jnp.where(t == 0, nblk - 1, nblk - 1 - t if d == 1 else t - 1)

    in_specs, args = [], []
    for d in range(2):
        for a in (qeff, o0, mk, nn):
            in_specs.append(pl.BlockSpec((None, B, H, tm, dk), lambda t, d=d: (d, 0, 0, blk(d, t), 0)))
            args.append(a)
        in_specs.append(pl.BlockSpec((None, B, None, H, tm // DN_CHUNK, dk),
                                     lambda t, d=d: (d, 0, blk(d, t), 0, 0, 0)))
        args.append(egl)
    out_specs = tuple(pl.BlockSpec((B, H, tm, dk), lambda t, d=d: (0, 0, blk(d, t), 0)) for d in range(2))
    return pl.pallas_call(
        functools.partial(_dn_scan_kernel, n_batch=B),
        out_shape=(jax.ShapeDtypeStruct((B, H, T, dk), F32),) * 2,
        grid=(nblk,),
        in_specs=in_specs,
        out_specs=out_specs,
        scratch_shapes=[pltpu.VMEM((2, B, H, dk, dk), F32)],
        compiler_params=_cparams(("arbitrary",)),
        name="dn_scan",
    )(*args)


def _dn_post_kernel(of_ref, ob_ref, z_ref, g_ref, o_ref):
    z = z_ref[...].astype(F32)
    for h in range(DN_HEADS):
        o = of_ref[h] + ob_ref[h]
        y = o * lax.rsqrt(jnp.mean(o * o, axis=-1, keepdims=True) + EPS) * g_ref[...]
        zz = z[:, h * DN_DK:(h + 1) * DN_DK]
        o_ref[:, h * DN_DK:(h + 1) * DN_DK] = (y * _silu(zz)).astype(o_ref.dtype)


def _dn_post(o_f, o_b, P, g_out):
    B, H, T, dk = o_f.shape
    tm = DN_BLOCK
    hm = pl.BlockSpec((None, H, tm, dk), lambda b, i: (b, 0, i, 0))
    return pl.pallas_call(
        _dn_post_kernel,
        out_shape=jax.ShapeDtypeStruct((B, T, BR_W), F32),
        grid=(B, T // tm),
        in_specs=[hm, hm,
                  pl.BlockSpec((None, tm, BR_W), lambda b, i: (b, i, COL_BZ // BR_W)),
                  pl.BlockSpec((1, dk), lambda b, i: (0, 0))],
        out_specs=pl.BlockSpec((None, tm, BR_W), lambda b, i: (b, i, 0)),
        compiler_params=_cparams(("arbitrary", "arbitrary")),
        name="dn_post",
    )(o_f, o_b, P, g_out.reshape(1, dk))


def _merge_kernel(a_ref, b_ref, c_ref, ga_ref, gb_ref, gc_ref, h_ref, mod_ref, g2_ref, wb_ref, wo_ref,
                  h_out_ref, v_out_ref, *, tm, n_lat):
    i = pl.program_id(1)
    D = h_ref.shape[-1]
    m = jax.nn.sigmoid(ga_ref[...].astype(F32)) * _dot(a_ref[...], wb_ref[0])
    m = m + jax.nn.sigmoid(gb_ref[...].astype(F32)) * _dot(b_ref[...].astype(BF16), wb_ref[1])
    m = m + jax.nn.sigmoid(gc_ref[...].astype(F32)) * _dot(c_ref[...], wb_ref[2])
    y = _dot(m.astype(BF16), wo_ref[...])
    gate = jnp.where(_is_ctx_row(i, tm, n_lat), mod_ref[0:1, 2 * D:3 * D], mod_ref[1:2, 2 * D:3 * D])
    h_new = h_ref[...] + gate * y
    h_out_ref[...] = h_new
    v_out_ref[...] = _norm_mod(h_new, g2_ref[...], mod_ref, i, tm, n_lat, 3, 4).astype(BF16)


def _merge(br_a, br_b, br_c, P, h, modtab, g_norm2, w_branch, w_out, n_lat, tm):
    B, T, D = h.shape
    br = pl.BlockSpec((None, tm, BR_W), lambda b, i: (b, i, 0))

    def gate(j):
        return pl.BlockSpec((None, tm, D), lambda b, i: (b, i, COL_GATE // D + j))

    tok = pl.BlockSpec((None, tm, D), lambda b, i: (b, i, 0))
    return pl.pallas_call(
        functools.partial(_merge_kernel, tm=tm, n_lat=n_lat),
        out_shape=(jax.ShapeDtypeStruct((B, T, D), F32), jax.ShapeDtypeStruct((B, T, D), BF16)),
        grid=(B, T // tm),
        in_specs=[br, br, br, gate(0), gate(1), gate(2), tok,
                  pl.BlockSpec((None, 2, 6 * D), lambda b, i: (b, 0, 0)),
                  pl.BlockSpec((1, D), lambda b, i: (0, 0)),
                  pl.BlockSpec((3, BR_W, D), lambda b, i: (0, 0, 0)),
                  pl.BlockSpec((D, D), lambda b, i: (0, 0))],
        out_specs=(tok, tok),
        compiler_params=_cparams(("arbitrary", "arbitrary")),
        name="merge",
    )(br_a, br_b, br_c, P, P, P, h, modtab, g_norm2.reshape(1, D), w_branch, w_out)


def _router_kernel(v_ref, wr_ref, br_ref, c_ref):
    tn = v_ref.shape[0]
    G = N_GROUPS
    scores = jax.nn.sigmoid(_dot_nt(wr_ref[...], v_ref[...]))
    sel = scores + br_ref[...]
    slabs = [sel[j * G:(j + 1) * G] for j in range(GROUP_SIZE)]
    m1, m2 = slabs[0], jnp.full((G, tn), -jnp.inf, F32)
    for j in range(1, GROUP_SIZE):
        m2 = jnp.maximum(m2, jnp.minimum(m1, slabs[j]))
        m1 = jnp.maximum(m1, slabs[j])
    gs = m1 + m2
    gidx = lax.broadcasted_iota(jnp.int32, (G, tn), 0)
    grank = jnp.zeros((G, tn), jnp.int32)
    for g in range(G):
        rowv = gs[g:g + 1]
        beats = jnp.where(rowv > gs, 1, jnp.where(jnp.logical_and(rowv == gs, g < gidx), 1, 0))
        grank = grank + beats
    gmask = grank < TOPK_GROUPS
    masked = jnp.concatenate([jnp.where(gmask, s, -jnp.inf) for s in slabs], axis=0)
    pos = lax.broadcasted_iota(jnp.int32, (N_EXPERTS, tn), 0)
    orig = (pos % G) * GROUP_SIZE + pos // G
    rank = jnp.zeros((N_EXPERTS, tn), jnp.int32)
    for p in range(N_EXPERTS):
        o_p = (p % G) * GROUP_SIZE + p // G
        rowv = masked[p:p + 1]
        beats = jnp.where(rowv > masked, 1, jnp.where(jnp.logical_and(rowv == masked, o_p < orig), 1, 0))
        rank = rank + beats
    chosen = rank < TOP_K
    w = jnp.where(chosen, scores, 0.0)
    denom = jnp.sum(w, axis=0, keepdims=True) + 1e-20
    c_ref[...] = w / denom * ROUTED_SCALE


def _router(v_flat, w_router, b_router):
    N, D = v_flat.shape
    tn = 1280 if N % 1280 == 0 else 256
    perm = np.array([(p % N_GROUPS) * GROUP_SIZE + p // N_GROUPS for p in range(N_EXPERTS)])
    wr_t = w_router.T[perm].astype(BF16)
    br = b_router[perm].reshape(N_EXPERTS, 1)
    c_pos = pl.pallas_call(
        _router_kernel,
        out_shape=jax.ShapeDtypeStruct((N_EXPERTS, N), F32),
        grid=(N // tn,),
        in_specs=[pl.BlockSpec((tn, D), lambda i: (i, 0)),
                  pl.BlockSpec((N_EXPERTS, D), lambda i: (0, 0)),
                  pl.BlockSpec((N_EXPERTS, 1), lambda i: (0, 0))],
        out_specs=pl.BlockSpec((N_EXPERTS, tn), lambda i: (0, i)),
        compiler_params=_cparams(("arbitrary",)),
        name="router",
    )(v_flat, wr_t, br)
    inv = np.argsort(perm)
    return c_pos[inv]


MOE_EPS = 4


def _moe_kernel(x_ref, c_ref, wg_ref, wu_ref, wd_ref, sg_ref, su_ref, sd_ref, h_ref, mod_ref,
                o_ref, acc_ref, *, tm, n_lat):
    i = pl.program_id(1)
    e = pl.program_id(2)
    x = x_ref[...]
    D = x.shape[-1]

    @pl.when(e == 0)
    def _():
        hs = _silu(_dot(x, sg_ref[...])) * _dot(x, su_ref[...])
        acc_ref[...] = _dot(hs.astype(BF16), sd_ref[...])

    c = c_ref[...]
    parts = []
    for j in range(MOE_EPS):
        hj = _silu(_dot(x, wg_ref[j])) * _dot(x, wu_ref[j]) * c[:, j:j + 1]
        parts.append(hj.astype(BF16))
    hcat = jnp.concatenate(parts, axis=-1)
    acc_ref[...] += _dot(hcat, wd_ref[...])

    @pl.when(e == pl.num_programs(2) - 1)
    def _():
        gate = jnp.where(_is_ctx_row(i, tm, n_lat), mod_ref[0:1, 5 * D:6 * D], mod_ref[1:2, 5 * D:6 * D])
        o_ref[...] = h_ref[...] + gate * acc_ref[...]


def _moe(v, c_grp, h, modtab, wg, wu, wd, sg, su, sd, n_lat, tm):
    B, T, D = v.shape
    F = MOE_FF
    ngrp = N_EXPERTS // MOE_EPS
    tok = pl.BlockSpec((None, tm, D), lambda b, i, e: (b, i, 0))
    return pl.pallas_call(
        functools.partial(_moe_kernel, tm=tm, n_lat=n_lat),
        out_shape=jax.ShapeDtypeStruct((B, T, D), F32),
        grid=(B, T // tm, ngrp),
        in_specs=[tok,
                  pl.BlockSpec((None, None, tm, MOE_EPS), lambda b, i, e: (e, b, i, 0)),
                  pl.BlockSpec((MOE_EPS, D, F), lambda b, i, e: (e, 0, 0)),
                  pl.BlockSpec((MOE_EPS, D, F), lambda b, i, e: (e, 0, 0)),
                  pl.BlockSpec((MOE_EPS * F, D), lambda b, i, e: (e, 0)),
                  pl.BlockSpec((D, F), lambda b, i, e: (0, 0)),
                  pl.BlockSpec((D, F), lambda b, i, e: (0, 0)),
                  pl.BlockSpec((F, D), lambda b, i, e: (0, 0)),
                  tok,
                  pl.BlockSpec((None, 2, 6 * D), lambda b, i, e: (b, 0, 0))],
        out_specs=tok,
        scratch_shapes=[pltpu.VMEM((tm, D), F32)],
        compiler_params=_cparams(("arbitrary", "arbitrary", "arbitrary")),
        name="moe",
    )(v, c_grp, wg, wu, wd, sg, su, sd, h, modtab)


def _final_kernel(h_ref, g_ref, o_ref):
    x = h_ref[...]
    o_ref[...] = x * lax.rsqrt(jnp.mean(x * x, axis=-1, keepdims=True) + EPS) * g_ref[...]


def _final_norm(h, g_final, S):
    B, T, D = h.shape
    tm = 1024
    return pl.pallas_call(
        _final_kernel,
        out_shape=jax.ShapeDtypeStruct((B, S, D), F32),
        grid=(B, S // tm),
        in_specs=[pl.BlockSpec((None, tm, D), lambda b, i: (b, i, 0)),
                  pl.BlockSpec((1, D), lambda b, i: (0, 0))],
        out_specs=pl.BlockSpec((None, tm, D), lambda b, i: (b, i, 0)),
        compiler_params=_cparams(("arbitrary", "arbitrary")),
        name="final_norm",
    )(h, g_final.reshape(1, D))


def _rope_tables(S, n_ctx):
    rows = S // GRID_W
    row = jnp.repeat(jnp.arange(rows, dtype=F32), GRID_W)
    col = (jnp.arange(S) % GRID_W).astype(F32)
    axis_dim = HEAD_DIM // 2
    inv = 1.0 / (ROPE_THETA ** (jnp.arange(0, axis_dim, 2, dtype=F32) / axis_dim))
    ang = jnp.concatenate([row[:, None] * inv, col[:, None] * inv], axis=-1)
    ang = jnp.concatenate([ang, ang], axis=-1)
    cos = jnp.concatenate([jnp.cos(ang), jnp.ones((n_ctx, HEAD_DIM), F32)], axis=0)
    sin = jnp.concatenate([jnp.sin(ang), jnp.zeros((n_ctx, HEAD_DIM), F32)], axis=0)
    first = (jnp.arange(HEAD_DIM) < HEAD_DIM // 2)[None, :]
    sin_a = jnp.where(first, -sin, 0.0)
    sin_b = jnp.where(first, 0.0, sin)
    return tuple(jnp.tile(t, (1, 2)) for t in (cos, sin_a, sin_b))


def _permute_w_in(w):
    splits = np.cumsum([512, 128, 128, 1536, 512, 16, 16, 512, 512, 512, 3072])[:-1].tolist()
    aq, ak, av, bqkv, bz, ba, bb, cq, ck, cv, gate = jnp.split(w, splits, axis=-1)
    pad = jnp.zeros((w.shape[0], IN_W_PAD - COL_AB - 32), w.dtype)
    out = jnp.concatenate([aq, cq, ck, cv, bz, bqkv, gate, ak, av, ba, bb, pad], axis=-1)
    return out.astype(BF16)


def kernel(x, c, ctx, c_ctx, w_mod, b_mod, g_norm1, g_norm2, w_in, g_qnorm, g_knorm, w_conv, a_log, dt_bias, g_dn_out, lam_qk, g_subln, w_branch, w_out, w_router, b_router, w_e_gate, w_e_up, w_e_down, w_s_gate, w_s_up, w_s_down, g_final):
    B, S, D = x.shape
    n_ctx = ctx.shape[1]
    T = n_ctx + S
    L = w_mod.shape[0]
    N = B * T
    assert n_ctx == DN_BLOCK and S % 1024 == 0 and D == D_MODEL
    tm_tok = 1280 if T % 1280 == 0 else 256
    tm_merge = 640 if T % 640 == 0 else 256

    cond = jnp.zeros((8, D), F32).at[0].set(c_ctx).at[1:1 + B].set(c)
    mod = _mod_vectors(cond, w_mod, b_mod)
    tabs = _rope_tables(S, n_ctx)
    h = jnp.concatenate([x, ctx], axis=1)
    tk = 1024
    nch = S // tk

    for l in range(L):
        lambda_init = 0.8 - 0.6 * math.exp(-0.3 * l)
        modtab = jnp.stack([jnp.broadcast_to(mod[l, 0], (B, 6 * D)), mod[l, 1:1 + B]], axis=1)
        P, AB = _in_projection(h, modtab, g_norm1[l], _permute_w_in(w_in[l]), S, tm_tok)

        qa, ka, qd, kd = _attn_prep(P, tabs, g_qnorm[l], g_knorm[l])
        va = P[:, :, COL_AV:COL_AV + LANES]
        vta_lat = va[:, :S].reshape(B, nch, tk, LANES).transpose(0, 1, 3, 2)
        vta_ctx = va[:, S:].transpose(0, 2, 1)
        vd = P[:, :, COL_CV:COL_CV + BR_W]
        vtd_lat = vd[:, :S].reshape(B, nch, tk, DF_HEADS, LANES).transpose(0, 3, 1, 4, 2)
        vtd_ctx = vd[:, S:].reshape(B, n_ctx, DF_HEADS, LANES).transpose(0, 2, 3, 1)
        dx = (lam_qk[l], g_subln[l].reshape(LANES, 1))
        br_a = jnp.concatenate([_flash("gqa", qa, ka, vta_lat, vta_ctx, n_ctx, False),
                                _flash("gqa", qa, ka, vta_lat, vta_ctx, n_ctx, True)], axis=-1).transpose(0, 2, 1)
        br_c = jnp.concatenate([_flash("diff", qd, kd, vtd_lat, vtd_ctx, n_ctx, False, dx, lambda_init),
                                _flash("diff", qd, kd, vtd_lat, vtd_ctx, n_ctx, True, dx, lambda_init)],
                               axis=-1).transpose(0, 2, 1)

        prep = _dn_prep(P, AB, w_conv[l], a_log[l], dt_bias[l])
        o_f, o_b = _dn_scan(_dn_chunk(prep), prep[-1])
        br_b = _dn_post(o_f, o_b, P, g_dn_out[l])

        h, v = _merge(br_a, br_b, br_c, P, h, modtab, g_norm2[l], w_branch[l].astype(BF16),
                      w_out[l].astype(BF16), S, tm_merge)

        c_exp = _router(v.reshape(N, D), w_router[l], b_router[l])
        c_grp = c_exp.reshape(N_EXPERTS // MOE_EPS, MOE_EPS, B, T).transpose(0, 2, 3, 1)
        h = _moe(v, c_grp, h, modtab,
                 w_e_gate[l].astype(BF16), w_e_up[l].astype(BF16),
                 w_e_down[l].astype(BF16).reshape(N_EXPERTS * MOE_FF, D),
                 w_s_gate[l].astype(BF16), w_s_up[l].astype(BF16), w_s_down[l].astype(BF16),
                 S, tm_merge)

    return _final_norm(h, g_final, S)
```

```python
import functools
import math

import numpy as np
import jax
import jax.numpy as jnp
from jax import lax
from jax.experimental import pallas as pl
from jax.experimental.pallas import tpu as pltpu

F32 = jnp.float32
BF16 = jnp.bfloat16

D_MODEL = 1024
GRID_W = 64
EPS = 1e-6
ROPE_THETA = 10000.0
HEAD_DIM = 64
LANES = 128
A_HEADS = 8
A_KV_HEADS = 2
A_GROUP = A_HEADS // A_KV_HEADS
DN_HEADS = 8
DN_DK = 64
DN_CHUNK = 64
DN_BLOCK = 256
DF_HEADS = 4
N_EXPERTS = 64
TOP_K = 6
N_GROUPS = 8
TOPK_GROUPS = 4
GROUP_SIZE = N_EXPERTS // N_GROUPS
MOE_FF = 256
ROUTED_SCALE = 2.5
BR_W = 512

COL_AQ, COL_CQ, COL_CK, COL_CV, COL_BZ = 0, 512, 1024, 1536, 2048
COL_BQ, COL_BK, COL_BV = 2560, 3072, 3584
COL_GATE = 4096
COL_AK, COL_AV, COL_AB = 7168, 7296, 7424
IN_W_PAD = 7680
IN_TN = 1280

VMEM_LIMIT = 56 * 1024 * 1024


def _cparams(sem):
    return pltpu.CompilerParams(dimension_semantics=sem, vmem_limit_bytes=VMEM_LIMIT)


def _dot(a, b):
    return jnp.dot(a, b, preferred_element_type=F32)


def _dot_nt(a, b):
    return lax.dot_general(a, b, (((1,), (1,)), ((), ())), preferred_element_type=F32)


def _dot_tn(a, b):
    return lax.dot_general(a, b, (((0,), (0,)), ((), ())), preferred_element_type=F32)


def _split3(x):
    hi = x.astype(BF16)
    r = x - hi.astype(F32)
    mid = r.astype(BF16)
    lo = (r - mid.astype(F32)).astype(BF16)
    return hi, mid, lo


def _dot3_l(x, m):
    hi, mid, lo = _split3(x)
    return _dot(hi, m) + _dot(mid, m) + _dot(lo, m)


def _dot3_r(m, x):
    hi, mid, lo = _split3(x)
    return _dot(m, hi) + _dot(m, mid) + _dot(m, lo)


def _silu(x):
    return x * jax.nn.sigmoid(x)


def _mod_kernel(cond_ref, w_ref, b_ref, o_ref):
    a = _silu(cond_ref[...]).astype(BF16)
    o_ref[...] = _dot(a, w_ref[...].astype(BF16)) + b_ref[...]


def _mod_vectors(cond, w_mod, b_mod):
    L, D, W = w_mod.shape
    tn = 1024
    return pl.pallas_call(
        _mod_kernel,
        out_shape=jax.ShapeDtypeStruct((L, 8, W), F32),
        grid=(L, W // tn),
        in_specs=[pl.BlockSpec((8, D), lambda l, j: (0, 0)),
                  pl.BlockSpec((None, D, tn), lambda l, j: (l, 0, j)),
                  pl.BlockSpec((None, 1, tn), lambda l, j: (l, 0, j))],
        out_specs=pl.BlockSpec((None, 8, tn), lambda l, j: (l, 0, j)),
        compiler_params=_cparams(("arbitrary", "arbitrary")),
        name="mod_vectors",
    )(cond, w_mod, b_mod.reshape(L, 1, W))


def _is_ctx_row(blk, tm, n_lat):
    return blk * tm + lax.broadcasted_iota(jnp.int32, (tm, 1), 0) >= n_lat


def _norm_mod(x, g, mod_ref, blk, tm, n_lat, sh, sc):
    D = x.shape[-1]
    y = x * lax.rsqrt(jnp.mean(x * x, axis=-1, keepdims=True) + EPS) * g
    is_ctx = _is_ctx_row(blk, tm, n_lat)
    scale = jnp.where(is_ctx, mod_ref[0:1, sc * D:(sc + 1) * D], mod_ref[1:2, sc * D:(sc + 1) * D])
    shift = jnp.where(is_ctx, mod_ref[0:1, sh * D:(sh + 1) * D], mod_ref[1:2, sh * D:(sh + 1) * D])
    return y * (1.0 + scale) + shift


def _inproj_kernel(h_ref, mod_ref, g_ref, w_ref, p_ref, ab_ref, u_scr, *, tm, n_lat, ab_off):
    i = pl.program_id(1)
    j = pl.program_id(2)

    @pl.when(j == 0)
    def _():
        u_scr[...] = _norm_mod(h_ref[...], g_ref[...], mod_ref, i, tm, n_lat, 0, 1).astype(BF16)

    r = _dot(u_scr[...], w_ref[...])
    p_ref[...] = r.astype(BF16)

    @pl.when(j == pl.num_programs(2) - 1)
    def _():
        ab_ref[...] = r[:, ab_off:ab_off + LANES]


def _in_projection(h, modtab, g_norm, w_perm, n_lat, tm):
    B, T, D = h.shape
    ncol = IN_W_PAD // IN_TN
    kern = functools.partial(_inproj_kernel, tm=tm, n_lat=n_lat, ab_off=COL_AB - (ncol - 1) * IN_TN)
    return pl.pallas_call(
        kern,
        out_shape=(jax.ShapeDtypeStruct((B, T, IN_W_PAD), BF16),
                   jax.ShapeDtypeStruct((B, T, LANES), F32)),
        grid=(B, T // tm, ncol),
        in_specs=[pl.BlockSpec((None, tm, D), lambda b, i, j: (b, i, 0)),
                  pl.BlockSpec((None, 2, 6 * D), lambda b, i, j: (b, 0, 0)),
                  pl.BlockSpec((1, D), lambda b, i, j: (0, 0)),
                  pl.BlockSpec((D, IN_TN), lambda b, i, j: (0, j))],
        out_specs=(pl.BlockSpec((None, tm, IN_TN), lambda b, i, j: (b, i, j)),
                   pl.BlockSpec((None, tm, LANES), lambda b, i, j: (b, i, 0))),
        scratch_shapes=[pltpu.VMEM((tm, D), BF16)],
        compiler_params=_cparams(("arbitrary", "arbitrary", "arbitrary")),
        name="in_projection",
    )(h, modtab, g_norm.reshape(1, D), w_perm)


def _rope(x, cos, sin_a, sin_b):
    return x * cos + pltpu.roll(x, LANES - 32, 1) * sin_a + pltpu.roll(x, 32, 1) * sin_b


def _attn_prep_kernel(aq_ref, ak_ref, av_ref, cq_ref, ck_ref, cv_ref, cos_ref, sa_ref, sb_ref, gq_ref, gk_ref,
                      m_ref, qa_ref, ka_ref, vta_ref, qd_ref, kd_ref, vtd_ref):
    cos, sa, sb = cos_ref[...], sa_ref[...], sb_ref[...]
    vta_ref[...] = av_ref[...].astype(F32).T.astype(BF16)
    for h in range(DF_HEADS):
        vtd_ref[h] = cv_ref[:, h * LANES:(h + 1) * LANES].astype(F32).T.astype(BF16)
    m64 = m_ref[...]
    half = lax.broadcasted_iota(jnp.int32, cos.shape, 1) // HEAD_DIM
    scale = HEAD_DIM ** -0.5 * math.log2(math.e)

    def head_norm(x, g):
        ms = _dot3_l(x * x, m64) * (1.0 / HEAD_DIM)
        return x * lax.rsqrt(ms + EPS) * g

    for s in range(A_HEADS // 2):
        x = aq_ref[:, s * LANES:(s + 1) * LANES].astype(F32)
        y = _rope(head_norm(x, gq_ref[...]), cos, sa, sb) * scale
        y_sw = pltpu.roll(y, HEAD_DIM, 1)
        kvh = (2 * s) // A_GROUP
        for hh in range(2):
            src = y if hh == kvh else y_sw
            qa_ref[2 * s + hh] = jnp.where(half == kvh, src, 0.0).astype(BF16)
    xk = ak_ref[...].astype(F32)
    ka_ref[...] = _rope(head_norm(xk, gk_ref[...]), cos, sa, sb).astype(BF16)
    for h in range(DF_HEADS):
        x = cq_ref[:, h * LANES:(h + 1) * LANES].astype(F32)
        y = _rope(x, cos, sa, sb) * scale
        qd_ref[2 * h] = jnp.where(half == 0, y, 0.0).astype(BF16)
        qd_ref[2 * h + 1] = jnp.where(half == 1, y, 0.0).astype(BF16)
        xk = ck_ref[:, h * LANES:(h + 1) * LANES].astype(F32)
        kd_ref[:, h * LANES:(h + 1) * LANES] = _rope(xk, cos, sa, sb).astype(BF16)


def _attn_prep(P, tabs, g_q, g_k, tk):
    B, T, _ = P.shape
    tm = 256
    per = tk // tm
    cos2, sin_a, sin_b = tabs
    m64 = jnp.asarray(np.kron(np.eye(2), np.ones((HEAD_DIM, HEAD_DIM))), BF16)
    gq2 = jnp.tile(g_q, 2).reshape(1, LANES)
    gk2 = jnp.tile(g_k, 2).reshape(1, LANES)
    tab_spec = pl.BlockSpec((tm, LANES), lambda b, i: (i, 0))
    vec_spec = pl.BlockSpec((1, LANES), lambda b, i: (0, 0))
    return pl.pallas_call(
        _attn_prep_kernel,
        out_shape=(jax.ShapeDtypeStruct((B, A_HEADS, T, LANES), BF16),
                   jax.ShapeDtypeStruct((B, T, LANES), BF16),
                   jax.ShapeDtypeStruct((B, T // tk, LANES, tk), BF16),
                   jax.ShapeDtypeStruct((B, 2 * DF_HEADS, T, LANES), BF16),
                   jax.ShapeDtypeStruct((B, T, 512), BF16),
                   jax.ShapeDtypeStruct((B, DF_HEADS, T // tk, LANES, tk), BF16)),
        grid=(B, T // tm),
        in_specs=[pl.BlockSpec((None, tm, 512), lambda b, i: (b, i, COL_AQ // 512)),
                  pl.BlockSpec((None, tm, LANES), lambda b, i: (b, i, COL_AK // LANES)),
                  pl.BlockSpec((None, tm, LANES), lambda b, i: (b, i, COL_AV // LANES)),
                  pl.BlockSpec((None, tm, 512), lambda b, i: (b, i, COL_CQ // 512)),
                  pl.BlockSpec((None, tm, 512), lambda b, i: (b, i, COL_CK // 512)),
                  pl.BlockSpec((None, tm, 512), lambda b, i: (b, i, COL_CV // 512)),
                  tab_spec, tab_spec, tab_spec, vec_spec, vec_spec,
                  pl.BlockSpec((LANES, LANES), lambda b, i: (0, 0))],
        out_specs=(pl.BlockSpec((None, A_HEADS, tm, LANES), lambda b, i: (b, 0, i, 0)),
                   pl.BlockSpec((None, tm, LANES), lambda b, i: (b, i, 0)),
                   pl.BlockSpec((None, None, LANES, tm), lambda b, i: (b, i // per, 0, i % per)),
                   pl.BlockSpec((None, 2 * DF_HEADS, tm, LANES), lambda b, i: (b, 0, i, 0)),
                   pl.BlockSpec((None, tm, 512), lambda b, i: (b, i, 0)),
                   pl.BlockSpec((None, DF_HEADS, None, LANES, tm), lambda b, i: (b, 0, i // per, 0, i % per))),
        compiler_params=_cparams(("arbitrary", "arbitrary")),
        name="attn_prep",
    )(P, P, P, P, P, P, cos2, sin_a, sin_b, gq2, gk2, m64)


FLASH_TQ = 256
FLASH_PIECE = 256
FLASH_MAX_OVERSHOOT = 60.0


def _flash_kernel(*refs, mode, cols, tk, n_ctx, n_chunks, ctx_only, lambda_init):
    refs = list(refs)
    q_ref, k_ref, vt_ref = refs[0:3]
    pos = 3
    if mode == "diff":
        lam_ref, g_ref = refs[pos], refs[pos + 1]
        pos += 2
    o_ref, m_scr, l_scr, over_scr, acc_scr = refs[pos:pos + 5]
    q = q_ref[...].reshape(cols, LANES)

    def reset():
        m_scr[...] = jnp.full((1, cols), -jnp.inf, F32)
        l_scr[...] = jnp.zeros((1, cols), F32)
        acc_scr[...] = jnp.zeros((LANES, cols), F32)

    def exact(k, vt):
        s = _dot_nt(k, q)
        m_old = m_scr[...]
        m_new = jnp.maximum(m_old, jnp.max(s, axis=0, keepdims=True))
        alpha = jnp.exp2(m_old - m_new)
        p = jnp.exp2(s - m_new)
        l_scr[...] = alpha * l_scr[...] + jnp.sum(p, axis=0, keepdims=True)
        acc_scr[...] = alpha * acc_scr[...] + _dot(vt, p.astype(BF16))
        m_scr[...] = m_new

    def lagged(k, vt):
        ref = m_scr[...]
        s = _dot_nt(k, q)
        p = jnp.exp2(s - ref)
        m_cur = jnp.max(s, axis=0, keepdims=True)
        m_new = jnp.maximum(ref, m_cur)
        alpha = jnp.exp2(ref - m_new)
        l_scr[...] = (l_scr[...] + jnp.sum(p, axis=0, keepdims=True)) * alpha
        acc_scr[...] = (acc_scr[...] + _dot(vt, p.astype(BF16))) * alpha
        over_scr[...] = jnp.maximum(over_scr[...], m_cur - ref)
        m_scr[...] = m_new

    def k_chunk(c):
        return k_ref[pl.ds(pl.multiple_of(c * tk, tk), tk), :]

    reset()
    if ctx_only:
        exact(k_ref[...], vt_ref[:, tk - n_ctx:])
    else:
        over_scr[...] = jnp.zeros((1, cols), F32)
        exact(k_ref[0:FLASH_PIECE, :], vt_ref[0, :, 0:FLASH_PIECE])
        if tk > FLASH_PIECE:
            lagged(k_ref[FLASH_PIECE:tk, :], vt_ref[0, :, FLASH_PIECE:tk])

        def lagged_pair(c):
            ref0 = m_scr[...]
            s0 = _dot_nt(k_chunk(c), q)
            m0 = jnp.max(s0, axis=0, keepdims=True)
            ref1 = jnp.maximum(ref0, m0)
            s1 = _dot_nt(k_chunk(c + 1), q)
            m1 = jnp.max(s1, axis=0, keepdims=True)
            ref2 = jnp.maximum(ref1, m1)
            p0 = jnp.exp2(s0 - ref0)
            p1 = jnp.exp2(s1 - ref1)
            a0 = jnp.exp2(ref0 - ref1)
            a1 = jnp.exp2(ref1 - ref2)
            l_scr[...] = ((l_scr[...] + jnp.sum(p0, axis=0, keepdims=True)) * a0
                          + jnp.sum(p1, axis=0, keepdims=True)) * a1
            acc_scr[...] = ((acc_scr[...] + _dot(vt_ref[c], p0.astype(BF16))) * a0
                            + _dot(vt_ref[c + 1], p1.astype(BF16))) * a1
            over_scr[...] = jnp.maximum(over_scr[...], jnp.maximum(m0 - ref0, m1 - ref1))
            m_scr[...] = ref2

        def fast_body(it, carry):
            lagged_pair(1 + 2 * it)
            return carry
        lax.fori_loop(0, (n_chunks - 1) // 2, fast_body, 0)

        @pl.when(jnp.max(over_scr[...]) > FLASH_MAX_OVERSHOOT)
        def _():
            reset()

            def exact_body(c, carry):
                exact(k_chunk(c), vt_ref[c])
                return carry
            lax.fori_loop(0, n_chunks, exact_body, 0)

    if mode == "gqa":
        kvh = pl.program_id(1)
        o = acc_scr[pl.ds(pl.multiple_of(kvh * HEAD_DIM, HEAD_DIM), HEAD_DIM), :] / l_scr[...]
        pieces = [o[:, g * FLASH_TQ:(g + 1) * FLASH_TQ] for g in range(cols // FLASH_TQ)]
        o_ref[...] = jnp.concatenate(pieces, axis=0).T.astype(o_ref.dtype)
    else:
        lq = lam_ref[...]
        lam = (jnp.exp(jnp.sum(lq[0:1] * lq[1:2], axis=-1, keepdims=True))
               - jnp.exp(jnp.sum(lq[2:3] * lq[3:4], axis=-1, keepdims=True)) + lambda_init)
        o = acc_scr[...] / l_scr[...]
        d = o[:, :cols // 2] - lam * o[:, cols // 2:]
        y = d * lax.rsqrt(jnp.mean(d * d, axis=0, keepdims=True) + EPS) * g_ref[...]
        o_ref[...] = (y * (1.0 - lambda_init)).T.astype(o_ref.dtype)


def _flash(mode, q, k, vt, n_ctx, ctx_only, extra=(), lambda_init=0.0):
    B, NQ, T, _ = q.shape
    S = T - n_ctx
    tk = vt.shape[-1]
    n_chunks = vt.shape[-3]
    assert n_chunks % 2 == 1
    n_maps = A_GROUP if mode == "gqa" else 2
    n_heads = NQ // n_maps
    if ctx_only:
        tq, n_q, q0, rows = n_ctx, 1, S // n_ctx, n_ctx
        k_rows, k0 = n_ctx, S // n_ctx
        vt_blk, vt0 = None, n_chunks - 1
    else:
        tq = FLASH_TQ if mode == "gqa" else 2 * FLASH_TQ
        n_q, q0, rows = S // tq, 0, S
        k_rows, k0 = T, 0
        vt_blk, vt0 = n_chunks, 0
    cols = n_maps * tq
    kern = functools.partial(_flash_kernel, mode=mode, cols=cols, tk=tk, n_ctx=n_ctx,
                             n_chunks=n_chunks, ctx_only=ctx_only, lambda_init=lambda_init)
    in_specs = [pl.BlockSpec((None, n_maps, tq, LANES), lambda b, h, i: (b, h, i + q0, 0))]
    if mode == "gqa":
        in_specs += [pl.BlockSpec((None, k_rows, LANES), lambda b, h, i: (b, k0, 0)),
                     pl.BlockSpec((None, vt_blk, LANES, tk), lambda b, h, i: (b, vt0, 0, 0))]
        out_spec = pl.BlockSpec((None, tq, A_GROUP * HEAD_DIM), lambda b, h, i: (b, i, h))
    else:
        in_specs += [pl.BlockSpec((None, k_rows, LANES), lambda b, h, i: (b, k0, h)),
                     pl.BlockSpec((None, None, vt_blk, LANES, tk), lambda b, h, i: (b, h, vt0, 0, 0)),
                     pl.BlockSpec((4, HEAD_DIM), lambda b, h, i: (0, 0)),
                     pl.BlockSpec((LANES, 1), lambda b, h, i: (0, 0))]
        out_spec = pl.BlockSpec((None, tq, LANES), lambda b, h, i: (b, i, h))
    return pl.pallas_call(
        kern,
        out_shape=jax.ShapeDtypeStruct((B, rows, BR_W), BF16),
        grid=(B, n_heads, n_q),
        in_specs=in_specs,
        out_specs=out_spec,
        scratch_shapes=[pltpu.VMEM((1, cols), F32)] * 3 + [pltpu.VMEM((LANES, cols), F32)],
        compiler_params=_cparams(("arbitrary", "arbitrary", "arbitrary")),
        name="flash_" + mode + ("_ctx" if ctx_only else ""),
    )(q, k, vt, *extra)


def _dn_prep_kernel(q_ref, k_ref, v_ref, ql_ref, kl_ref, vl_ref, qr_ref, kr_ref, vr_ref, ab_ref,
                    wc_ref, alog_ref, dtb_ref, mh_ref, eg_ref, eb_ref, tm_ref, ones_ref,
                    qo_ref, ko_ref, kb_ref, vbk_ref, qdec_ref, kdecbd_ref, decbd_ref, egl_ref,
                    *, tm):
    i = pl.program_id(1)
    nblk = pl.num_programs(1)
    W = DN_HEADS * DN_DK
    row = lax.broadcasted_iota(jnp.int32, (tm, W), 0)
    col = lax.broadcasted_iota(jnp.int32, (tm, W), 1)
    left_ok = jnp.logical_and(i >= 1, i < nblk - 1).astype(F32)
    right_ok = (i < nblk - 2).astype(F32)

    def conv_silu(x_ref, xl_ref, xr_ref, c0):
        x = x_ref[...].astype(F32)
        prev_row = xl_ref[7:8, :].astype(F32) * left_ok
        next_row = xr_ref[0:1, :].astype(F32) * right_ok
        x_prev = jnp.where(row == 0, prev_row, pltpu.roll(x, 1, 0))
        x_next = jnp.where(row == tm - 1, next_row, pltpu.roll(x, tm - 1, 0))
        y = (x_prev * wc_ref[0:1, c0:c0 + W] + x * wc_ref[1:2, c0:c0 + W]
             + x_next * wc_ref[2:3, c0:c0 + W])
        return _silu(y)

    def l2norm(x):
        ss = _dot3_l(x * x, mh_ref[...])
        return x * lax.rsqrt(ss + EPS)

    q = l2norm(conv_silu(q_ref, ql_ref, qr_ref, 0)) * (DN_DK ** -0.5)
    k = l2norm(conv_silu(k_ref, kl_ref, kr_ref, W))
    v = conv_silu(v_ref, vl_ref, vr_ref, 2 * W)

    ab = ab_ref[...]
    z = ab + dtb_ref[...]
    softplus = jnp.maximum(z, 0.0) + jnp.log(1.0 + jnp.exp(-jnp.abs(z)))
    g_all = -jnp.exp(alog_ref[...]) * softplus
    beta_all = jax.nn.sigmoid(ab)

    t_in = row & (DN_CHUNK - 1)
    s_in = col & (DN_CHUNK - 1)

    def put(ref, d, val):
        for h in range(DN_HEADS):
            piece = val[:, h * DN_DK:(h + 1) * DN_DK]
            if d is None:
                ref[h] = piece.astype(ref.dtype)
            else:
                ref[d, h] = piece.astype(ref.dtype)

    lo = lax.broadcasted_iota(jnp.int32, (tm, LANES), 1) < DN_DK
    n_chunks = tm // DN_CHUNK
    bd_shape = (tm, n_chunks * DN_CHUNK)
    same_chunk = (lax.broadcasted_iota(jnp.int32, bd_shape, 0) // DN_CHUNK
                  == lax.broadcasted_iota(jnp.int32, bd_shape, 1) // DN_CHUNK)

    def pair(a, b, h):
        sa = a[:, (h // 2) * LANES:(h // 2 + 1) * LANES]
        sb = b[:, (h // 2) * LANES:(h // 2 + 1) * LANES]
        if h % 2 == 0:
            return jnp.where(lo, sa, pltpu.roll(sb, DN_DK, 1))
        return jnp.where(lo, pltpu.roll(sa, DN_DK, 1), sb)

    def put_block_diag(ref, d, val):
        for h in range(DN_HEADS):
            x = pair(val, val, h)
            ref[d, h] = jnp.where(same_chunk, jnp.concatenate([x] * (n_chunks // 2), axis=1), 0.0).astype(ref.dtype)

    put(qo_ref, None, q)
    put(ko_ref, None, k)
    for d in range(2):
        g_e = _dot3_l(g_all, eg_ref[d])
        beta_e = _dot3_l(beta_all, eb_ref[d])
        gc = _dot3_r(tm_ref[d], g_e)
        g_tot = _dot3_r(ones_ref[...], g_e)
        strict = (t_in > s_in) if d == 0 else (t_in < s_in)
        causal = (t_in >= s_in) if d == 0 else (t_in <= s_in)
        diff = _dot3_r(tm_ref[d], jnp.where(strict, g_e, 0.0))
        decay = jnp.where(causal, jnp.exp(diff), 0.0)
        e_g = jnp.exp(gc)
        kb = k * beta_e
        put(kb_ref, d, kb)
        vb, kbg = v * beta_e, kb * e_g
        for h in range(DN_HEADS):
            vbk_ref[d, h] = pair(vb, kbg, h).astype(vbk_ref.dtype)
        put(qdec_ref, d, q * e_g)
        put_block_diag(kdecbd_ref, d, k * jnp.exp(g_tot - gc))
        put_block_diag(decbd_ref, d, decay)
        e_tot = jnp.exp(g_tot)
        for h in range(DN_HEADS):
            for c in range(tm // DN_CHUNK):
                egl_ref[d, h, c:c + 1, :] = e_tot[c * DN_CHUNK:c * DN_CHUNK + 1, h * DN_DK:(h + 1) * DN_DK]


def _dn_consts():
    n = DN_BLOCK
    t = np.arange(n)
    same = (t[:, None] // DN_CHUNK) == (t[None, :] // DN_CHUNK)
    pre = same & (t[None, :] <= t[:, None])
    suf = same & (t[None, :] >= t[:, None])
    tmat = np.stack([pre, suf]).astype(np.float32)
    ones = same.astype(np.float32)
    W = DN_HEADS * DN_DK
    eg = np.zeros((2, LANES, W), np.float32)
    eb = np.zeros((2, LANES, W), np.float32)
    for d in range(2):
        for h in range(DN_HEADS):
            eg[d, d * DN_HEADS + h, h * DN_DK:(h + 1) * DN_DK] = 1.0
            eb[d, 2 * DN_HEADS + d * DN_HEADS + h, h * DN_DK:(h + 1) * DN_DK] = 1.0
    mh = np.kron(np.eye(DN_HEADS), np.ones((DN_DK, DN_DK))).astype(np.float32)
    return tuple(jnp.asarray(a, BF16) for a in (mh, eg, eb, tmat, ones))


def _dn_prep(P, AB, w_conv, a_log, dt_bias):
    B, T, _ = P.shape
    tm = DN_BLOCK
    nblk = T // tm
    W = DN_HEADS * DN_DK
    mh, eg, eb, tmat, ones = _dn_consts()
    alog_row = jnp.zeros((1, LANES), F32).at[0, :2 * DN_HEADS].set(a_log.reshape(-1))
    dtb_row = jnp.zeros((1, LANES), F32).at[0, :2 * DN_HEADS].set(dt_bias.reshape(-1))
    r8 = tm // 8

    def main(c0):
        return pl.BlockSpec((None, tm, W), lambda b, i: (b, i, c0 // W))

    def left(c0):
        return pl.BlockSpec((None, 8, W), lambda b, i: (b, jnp.maximum(i * r8 - 1, 0), c0 // W))

    def right(c0):
        return pl.BlockSpec((None, 8, W), lambda b, i: (b, jnp.minimum((i + 1) * r8, nblk * r8 - 1), c0 // W))

    def full(a):
        nd = a.ndim
        return pl.BlockSpec(a.shape, lambda b, i: (0,) * nd)

    hm = pl.BlockSpec((None, DN_HEADS, tm, DN_DK), lambda b, i: (b, 0, i, 0))

    def hm2(width):
        return pl.BlockSpec((2, None, DN_HEADS, tm, width), lambda b, i: (0, b, 0, i, 0))

    def sh_hm2(width):
        return jax.ShapeDtypeStruct((2, B, DN_HEADS, T, width), BF16)

    sh_hm = jax.ShapeDtypeStruct((B, DN_HEADS, T, DN_DK), BF16)
    return pl.pallas_call(
        functools.partial(_dn_prep_kernel, tm=tm),
        out_shape=(sh_hm, sh_hm, sh_hm2(DN_DK), sh_hm2(2 * DN_DK), sh_hm2(DN_DK), sh_hm2(tm), sh_hm2(tm),
                   jax.ShapeDtypeStruct((2, B, nblk, DN_HEADS, tm // DN_CHUNK, DN_DK), F32)),
        grid=(B, nblk),
        in_specs=[main(COL_BQ), main(COL_BK), main(COL_BV),
                  left(COL_BQ), left(COL_BK), left(COL_BV),
                  right(COL_BQ), right(COL_BK), right(COL_BV),
                  pl.BlockSpec((None, tm, LANES), lambda b, i: (b, i, 0)),
                  full(w_conv), full(alog_row), full(dtb_row), full(mh), full(eg), full(eb), full(tmat),
                  full(ones)],
        out_specs=(hm, hm, hm2(DN_DK), hm2(2 * DN_DK), hm2(DN_DK), hm2(tm), hm2(tm),
                   pl.BlockSpec((2, None, None, DN_HEADS, tm // DN_CHUNK, DN_DK),
                                lambda b, i: (0, b, i, 0, 0, 0))),
        compiler_params=_cparams(("arbitrary", "arbitrary")),
        name="dn_prep",
    )(P, P, P, P, P, P, P, P, P, AB, w_conv, alog_row, dtb_row, mh, eg, eb, tmat, ones)


def _dn_chunk_kernel(q_ref, k_ref, kb_ref, vbk_ref, qdec_ref, kdecbd_ref, decbd_ref,
                     qeff_ref, o0_ref, mk_ref, n_ref):
    n = DN_BLOCK
    r = lax.broadcasted_iota(jnp.int32, (n, n), 0)
    c = lax.broadcasted_iota(jnp.int32, (n, n), 1)
    off_diag = r != c
    eye = (r == c).astype(F32)
    heads = range(DN_HEADS)
    low, qk = [], []
    for h in heads:
        dec = decbd_ref[h].astype(F32)
        k = k_ref[h]
        low.append(jnp.where(off_diag, _dot_nt(kb_ref[h], k) * dec, 0.0))
        qk.append((_dot_nt(q_ref[h], k) * dec).astype(BF16))
    p = [eye - m for m in low]
    mpow = low
    for _ in range(5):
        mpow = [_dot(m.astype(BF16), m.astype(BF16)) for m in mpow]
        p = [a + _dot(a.astype(BF16), m.astype(BF16)) for a, m in zip(p, mpow)]
    uw = [_dot(p[h].astype(BF16), vbk_ref[h]).astype(BF16) for h in heads]
    qkuw = [_dot(qk[h], uw[h]) for h in heads]
    for h in heads:
        o0_ref[h] = qkuw[h][:, :DN_DK]
        qeff_ref[h] = (qdec_ref[h].astype(F32) - qkuw[h][:, DN_DK:]).astype(qeff_ref.dtype)
        kt = _dot_tn(kdecbd_ref[h], uw[h])
        n_ref[h] = kt[:, :DN_DK]
        mk_ref[h] = kt[:, DN_DK:].astype(mk_ref.dtype)


def _dn_chunk(prep):
    q, k, kb, vbk, qdec, kdecbd, decbd, _ = prep
    B, H, T, dk = q.shape
    tm = DN_BLOCK
    hm = pl.BlockSpec((None, H, tm, dk), lambda d, b, i: (b, 0, i, 0))

    def hm2(width):
        return pl.BlockSpec((None, None, H, tm, width), lambda d, b, i: (d, b, 0, i, 0))

    def sh(dtype):
        return jax.ShapeDtypeStruct((2, B, H, T, dk), dtype)

    return pl.pallas_call(
        _dn_chunk_kernel,
        out_shape=(sh(BF16), sh(F32), sh(BF16), sh(F32)),
        grid=(2, B, T // tm),
        in_specs=[hm, hm, hm2(dk), hm2(2 * dk), hm2(dk), hm2(tm), hm2(tm)],
        out_specs=(hm2(dk), hm2(dk), hm2(dk), hm2(dk)),
        compiler_params=_cparams(("arbitrary", "arbitrary", "arbitrary")),
        name="dn_chunk",
    )(q, k, kb, vbk, qdec, kdecbd, decbd)


def _dn_scan_kernel(*refs, n_batch):
    ins = (refs[0:5], refs[5:10])
    o_refs = refs[10:12]
    s_scr = refs[12]
    t = pl.program_id(0)
    n_chunks = DN_BLOCK // DN_CHUNK
    C = DN_CHUNK

    @pl.when(t == 0)
    def _():
        s_scr[...] = jnp.zeros_like(s_scr)

    for step in range(n_chunks):
        for d in range(2):
            ci = step if d == 0 else n_chunks - 1 - step
            rows = slice(ci * C, (ci + 1) * C)
            qeff_ref, o0_ref, mk_ref, n_ref, egl_ref = ins[d]
            for b in range(n_batch):
                for h in range(DN_HEADS):
                    s = s_scr[d, b, h]
                    sb = s.astype(BF16)
                    o_refs[d][b, h, rows, :] = _dot(qeff_ref[b, h, rows, :], sb) + o0_ref[b, h, rows, :]
                    s_scr[d, b, h] = (s * egl_ref[b, h, ci:ci + 1, :] - _dot(mk_ref[b, h, rows, :], sb)
                                      + n_ref[b, h, rows, :])


def _dn_scan(maps, egl):
    qeff, o0, mk, nn = maps
    _, B, H, T, dk = qeff.shape
    tm = DN_BLOCK
    nblk = T // tm

    def blk(d, t):
        return jnp.where(t == 0, nblk - 1, nblk - 1 - t if d == 1 else t - 1)

    in_specs, args = [], []
    for d in range(2):
        for a in (qeff, o0, mk, nn):
            in_specs.append(pl.BlockSpec((None, B, H, tm, dk), lambda t, d=d: (d, 0, 0, blk(d, t), 0)))
            args.append(a)
        in_specs.append(pl.BlockSpec((None, B, None, H, tm // DN_CHUNK, dk),
                                     lambda t, d=d: (d, 0, blk(d, t), 0, 0, 0)))
        args.append(egl)
    out_specs = tuple(pl.BlockSpec((B, H, tm, dk), lambda t, d=d: (0, 0, blk(d, t), 0)) for d in range(2))
    return pl.pallas_call(
        functools.partial(_dn_scan_kernel, n_batch=B),
        out_shape=(jax.ShapeDtypeStruct((B, H, T, dk), F32),) * 2,
        grid=(nblk,),
        in_specs=in_specs,
        out_specs=out_specs,
        scratch_shapes=[pltpu.VMEM((2, B, H, dk, dk), F32)],
        compiler_params=_cparams(("arbitrary",)),
        name="dn_scan",
    )(*args)


def _dn_post_kernel(of_ref, ob_ref, z_ref, g_ref, o_ref):
    z = z_ref[...].astype(F32)
    for h in range(DN_HEADS):
        o = of_ref[h] + ob_ref[h]
        y = o * lax.rsqrt(jnp.mean(o * o, axis=-1, keepdims=True) + EPS) * g_ref[...]
        zz = z[:, h * DN_DK:(h + 1) * DN_DK]
        o_ref[:, h * DN_DK:(h + 1) * DN_DK] = (y * _silu(zz)).astype(o_ref.dtype)


def _dn_post(o_f, o_b, P, g_out):
    B, H, T, dk = o_f.shape
    tm = DN_BLOCK
    hm = pl.BlockSpec((None, H, tm, dk), lambda b, i: (b, 0, i, 0))
    return pl.pallas_call(
        _dn_post_kernel,
        out_shape=jax.ShapeDtypeStruct((B, T, BR_W), F32),
        grid=(B, T // tm),
        in_specs=[hm, hm,
                  pl.BlockSpec((None, tm, BR_W), lambda b, i: (b, i, COL_BZ // BR_W)),
                  pl.BlockSpec((1, dk), lambda b, i: (0, 0))],
        out_specs=pl.BlockSpec((None, tm, BR_W), lambda b, i: (b, i, 0)),
        compiler_params=_cparams(("arbitrary", "arbitrary")),
        name="dn_post",
    )(o_f, o_b, P, g_out.reshape(1, dk))


def _merge_kernel(a_ref, b_ref, c_ref, ga_ref, gb_ref, gc_ref, h_ref, mod_ref, g2_ref, wb_ref, wo_ref,
                  h_out_ref, v_out_ref, *, tm, n_lat):
    i = pl.program_id(1)
    D = h_ref.shape[-1]
    m = jax.nn.sigmoid(ga_ref[...].astype(F32)) * _dot(a_ref[...], wb_ref[0])
    m = m + jax.nn.sigmoid(gb_ref[...].astype(F32)) * _dot(b_ref[...].astype(BF16), wb_ref[1])
    m = m + jax.nn.sigmoid(gc_ref[...].astype(F32)) * _dot(c_ref[...], wb_ref[2])
    y = _dot(m.astype(BF16), wo_ref[...])
    gate = jnp.where(_is_ctx_row(i, tm, n_lat), mod_ref[0:1, 2 * D:3 * D], mod_ref[1:2, 2 * D:3 * D])
    h_new = h_ref[...] + gate * y
    h_out_ref[...] = h_new
    v_out_ref[...] = _norm_mod(h_new, g2_ref[...], mod_ref, i, tm, n_lat, 3, 4).astype(BF16)


def _merge(br_a, br_b, br_c, P, h, modtab, g_norm2, w_branch, w_out, n_lat, tm):
    B, T, D = h.shape
    br = pl.BlockSpec((None, tm, BR_W), lambda b, i: (b, i, 0))

    def gate(j):
        return pl.BlockSpec((None, tm, D), lambda b, i: (b, i, COL_GATE // D + j))

    tok = pl.BlockSpec((None, tm, D), lambda b, i: (b, i, 0))
    return pl.pallas_call(
        functools.partial(_merge_kernel, tm=tm, n_lat=n_lat),
        out_shape=(jax.ShapeDtypeStruct((B, T, D), F32), jax.ShapeDtypeStruct((B, T, D), BF16)),
        grid=(B, T // tm),
        in_specs=[br, br, br, gate(0), gate(1), gate(2), tok,
                  pl.BlockSpec((None, 2, 6 * D), lambda b, i: (b, 0, 0)),
                  pl.BlockSpec((1, D), lambda b, i: (0, 0)),
                  pl.BlockSpec((3, BR_W, D), lambda b, i: (0, 0, 0)),
                  pl.BlockSpec((D, D), lambda b, i: (0, 0))],
        out_specs=(tok, tok),
        compiler_params=_cparams(("arbitrary", "arbitrary")),
        name="merge",
    )(br_a, br_b, br_c, P, P, P, h, modtab, g_norm2.reshape(1, D), w_branch, w_out)


def _router_kernel(v_ref, wr_ref, br_ref, c_ref):
    tn = v_ref.shape[0]
    G = N_GROUPS
    scores = jax.nn.sigmoid(_dot_nt(wr_ref[...], v_ref[...]))
    sel = scores + br_ref[...]
    slabs = [sel[j * G:(j + 1) * G] for j in range(GROUP_SIZE)]
    m1, m2 = slabs[0], jnp.full((G, tn), -jnp.inf, F32)
    for j in range(1, GROUP_SIZE):
        m2 = jnp.maximum(m2, jnp.minimum(m1, slabs[j]))
        m1 = jnp.maximum(m1, slabs[j])
    gs = m1 + m2
    gidx = lax.broadcasted_iota(jnp.int32, (G, tn), 0)
    grank = jnp.zeros((G, tn), jnp.int32)
    for g in range(G):
        rowv = gs[g:g + 1]
        beats = jnp.where(rowv > gs, 1, jnp.where(jnp.logical_and(rowv == gs, g < gidx), 1, 0))
        grank = grank + beats
    gmask = grank < TOPK_GROUPS
    masked = jnp.concatenate([jnp.where(gmask, s, -jnp.inf) for s in slabs], axis=0)
    pos = lax.broadcasted_iota(jnp.int32, (N_EXPERTS, tn), 0)
    orig = (pos % G) * GROUP_SIZE + pos // G
    rank = jnp.zeros((N_EXPERTS, tn), jnp.int32)
    for p in range(N_EXPERTS):
        o_p = (p % G) * GROUP_SIZE + p // G
        rowv = masked[p:p + 1]
        beats = jnp.where(rowv > masked, 1, jnp.where(jnp.logical_and(rowv == masked, o_p < orig), 1, 0))
        rank = rank + beats
    chosen = rank < TOP_K
    w = jnp.where(chosen, scores, 0.0)
    denom = jnp.sum(w, axis=0, keepdims=True) + 1e-20
    c_ref[...] = w / denom * ROUTED_SCALE


def _router(v_flat, w_router, b_router):
    N, D = v_flat.shape
    tn = 1280 if N % 1280 == 0 else 256
    perm = np.array([(p % N_GROUPS) * GROUP_SIZE + p // N_GROUPS for p in range(N_EXPERTS)])
    wr_t = w_router.T[perm].astype(BF16)
    br = b_router[perm].reshape(N_EXPERTS, 1)
    c_pos = pl.pallas_call(
        _router_kernel,
        out_shape=jax.ShapeDtypeStruct((N_EXPERTS, N), F32),
        grid=(N // tn,),
        in_specs=[pl.BlockSpec((tn, D), lambda i: (i, 0)),
                  pl.BlockSpec((N_EXPERTS, D), lambda i: (0, 0)),
                  pl.BlockSpec((N_EXPERTS, 1), lambda i: (0, 0))],
        out_specs=pl.BlockSpec((N_EXPERTS, tn), lambda i: (0, i)),
        compiler_params=_cparams(("arbitrary",)),
        name="router",
    )(v_flat, wr_t, br)
    inv = np.argsort(perm)
    return c_pos[inv]


MOE_EPS = 4


def _moe_kernel(x_ref, c_ref, wg_ref, wu_ref, wd_ref, sg_ref, su_ref, sd_ref, h_ref, mod_ref,
                o_ref, acc_ref, *, tm, n_lat):
    i = pl.program_id(1)
    e = pl.program_id(2)
    x = x_ref[...]
    D = x.shape[-1]

    @pl.when(e == 0)
    def _():
        hs = _silu(_dot(x, sg_ref[...])) * _dot(x, su_ref[...])
        acc_ref[...] = _dot(hs.astype(BF16), sd_ref[...])

    c = c_ref[...]
    parts = []
    for j in range(MOE_EPS):
        hj = _silu(_dot(x, wg_ref[j])) * _dot(x, wu_ref[j]) * c[:, j:j + 1]
        parts.append(hj.astype(BF16))
    hcat = jnp.concatenate(parts, axis=-1)
    acc_ref[...] += _dot(hcat, wd_ref[...])

    @pl.when(e == pl.num_programs(2) - 1)
    def _():
        gate = jnp.where(_is_ctx_row(i, tm, n_lat), mod_ref[0:1, 5 * D:6 * D], mod_ref[1:2, 5 * D:6 * D])
        o_ref[...] = h_ref[...] + gate * acc_ref[...]


def _moe(v, c_grp, h, modtab, wg, wu, wd, sg, su, sd, n_lat, tm):
    B, T, D = v.shape
    F = MOE_FF
    ngrp = N_EXPERTS // MOE_EPS
    tok = pl.BlockSpec((None, tm, D), lambda b, i, e: (b, i, 0))
    return pl.pallas_call(
        functools.partial(_moe_kernel, tm=tm, n_lat=n_lat),
        out_shape=jax.ShapeDtypeStruct((B, T, D), F32),
        grid=(B, T // tm, ngrp),
        in_specs=[tok,
                  pl.BlockSpec((None, None, tm, MOE_EPS), lambda b, i, e: (e, b, i, 0)),
                  pl.BlockSpec((MOE_EPS, D, F), lambda b, i, e: (e, 0, 0)),
                  pl.BlockSpec((MOE_EPS, D, F), lambda b, i, e: (e, 0, 0)),
                  pl.BlockSpec((MOE_EPS * F, D), lambda b, i, e: (e, 0)),
                  pl.BlockSpec((D, F), lambda b, i, e: (0, 0)),
                  pl.BlockSpec((D, F), lambda b, i, e: (0, 0)),
                  pl.BlockSpec((F, D), lambda b, i, e: (0, 0)),
                  tok,
                  pl.BlockSpec((None, 2, 6 * D), lambda b, i, e: (b, 0, 0))],
        out_specs=tok,
        scratch_shapes=[pltpu.VMEM((tm, D), F32)],
        compiler_params=_cparams(("arbitrary", "arbitrary", "arbitrary")),
        name="moe",
    )(v, c_grp, wg, wu, wd, sg, su, sd, h, modtab)


def _final_kernel(h_ref, g_ref, o_ref):
    x = h_ref[...]
    o_ref[...] = x * lax.rsqrt(jnp.mean(x * x, axis=-1, keepdims=True) + EPS) * g_ref[...]


def _final_norm(h, g_final, S):
    B, T, D = h.shape
    tm = 1024
    return pl.pallas_call(
        _final_kernel,
        out_shape=jax.ShapeDtypeStruct((B, S, D), F32),
        grid=(B, S // tm),
        in_specs=[pl.BlockSpec((None, tm, D), lambda b, i: (b, i, 0)),
                  pl.BlockSpec((1, D), lambda b, i: (0, 0))],
        out_specs=pl.BlockSpec((None, tm, D), lambda b, i: (b, i, 0)),
        compiler_params=_cparams(("arbitrary", "arbitrary")),
        name="final_norm",
    )(h, g_final.reshape(1, D))


def _rope_tables(S, n_ctx):
    rows = S // GRID_W
    row = jnp.repeat(jnp.arange(rows, dtype=F32), GRID_W)
    col = (jnp.arange(S) % GRID_W).astype(F32)
    axis_dim = HEAD_DIM // 2
    inv = 1.0 / (ROPE_THETA ** (jnp.arange(0, axis_dim, 2, dtype=F32) / axis_dim))
    ang = jnp.concatenate([row[:, None] * inv, col[:, None] * inv], axis=-1)
    ang = jnp.concatenate([ang, ang], axis=-1)
    cos = jnp.concatenate([jnp.cos(ang), jnp.ones((n_ctx, HEAD_DIM), F32)], axis=0)
    sin = jnp.concatenate([jnp.sin(ang), jnp.zeros((n_ctx, HEAD_DIM), F32)], axis=0)
    first = (jnp.arange(HEAD_DIM) < HEAD_DIM // 2)[None, :]
    sin_a = jnp.where(first, -sin, 0.0)
    sin_b = jnp.where(first, 0.0, sin)
    return tuple(jnp.tile(t, (1, 2)) for t in (cos, sin_a, sin_b))


def _permute_w_in(w):
    splits = np.cumsum([512, 128, 128, 1536, 512, 16, 16, 512, 512, 512, 3072])[:-1].tolist()
    aq, ak, av, bqkv, bz, ba, bb, cq, ck, cv, gate = jnp.split(w, splits, axis=-1)
    pad = jnp.zeros((w.shape[0], IN_W_PAD - COL_AB - 32), w.dtype)
    out = jnp.concatenate([aq, cq, ck, cv, bz, bqkv, gate, ak, av, ba, bb, pad], axis=-1)
    return out.astype(BF16)


def kernel(x, c, ctx, c_ctx, w_mod, b_mod, g_norm1, g_norm2, w_in, g_qnorm, g_knorm, w_conv, a_log, dt_bias, g_dn_out, lam_qk, g_subln, w_branch, w_out, w_router, b_router, w_e_gate, w_e_up, w_e_down, w_s_gate, w_s_up, w_s_down, g_final):
    B, S, D = x.shape
    n_ctx = ctx.shape[1]
    T = n_ctx + S
    L = w_mod.shape[0]
    N = B * T
    assert n_ctx == DN_BLOCK and S % 1024 == 0 and D == D_MODEL
    tm_tok = 1280 if T % 1280 == 0 else 256
    tm_merge = 640 if T % 640 == 0 else 256

    cond = jnp.zeros((8, D), F32).at[0].set(c_ctx).at[1:1 + B].set(c)
    mod = _mod_vectors(cond, w_mod, b_mod)
    tabs = _rope_tables(S, n_ctx)
    h = jnp.concatenate([x, ctx], axis=1)
    tk = 1280 if T % 1280 == 0 else 256

    for l in range(L):
        lambda_init = 0.8 - 0.6 * math.exp(-0.3 * l)
        modtab = jnp.stack([jnp.broadcast_to(mod[l, 0], (B, 6 * D)), mod[l, 1:1 + B]], axis=1)
        P, AB = _in_projection(h, modtab, g_norm1[l], _permute_w_in(w_in[l]), S, tm_tok)

        qa, ka, vta, qd, kd, vtd = _attn_prep(P, tabs, g_qnorm[l], g_knorm[l], tk)
        dx = (lam_qk[l], g_subln[l].reshape(LANES, 1))
        br_a = jnp.concatenate([_flash("gqa", qa, ka, vta, n_ctx, False),
                                _flash("gqa", qa, ka, vta, n_ctx, True)], axis=1)
        br_c = jnp.concatenate([_flash("diff", qd, kd, vtd, n_ctx, False, dx, lambda_init),
                                _flash("diff", qd, kd, vtd, n_ctx, True, dx, lambda_init)], axis=1)

        prep = _dn_prep(P, AB, w_conv[l], a_log[l], dt_bias[l])
        o_f, o_b = _dn_scan(_dn_chunk(prep), prep[-1])
        br_b = _dn_post(o_f, o_b, P, g_dn_out[l])

        h, v = _merge(br_a, br_b, br_c, P, h, modtab, g_norm2[l], w_branch[l].astype(BF16),
                      w_out[l].astype(BF16), S, tm_merge)

        c_exp = _router(v.reshape(N, D), w_router[l], b_router[l])
        c_grp = c_exp.reshape(N_EXPERTS // MOE_EPS, MOE_EPS, B, T).transpose(0, 2, 3, 1)
        h = _moe(v, c_grp, h, modtab,
                 w_e_gate[l].astype(BF16), w_e_up[l].astype(BF16),
                 w_e_down[l].astype(BF16).reshape(N_EXPERTS * MOE_FF, D),
                 w_s_gate[l].astype(BF16), w_s_up[l].astype(BF16), w_s_down[l].astype(BF16),
                 S, tm_merge)

    return _final_norm(h, g_final, S)
```

```python
import functools
import math

import numpy as np
import jax
import jax.numpy as jnp
from jax import lax
from jax.experimental import pallas as pl
from jax.experimental.pallas import tpu as pltpu

F32 = jnp.float32
BF16 = jnp.bfloat16

D_MODEL = 1024
GRID_W = 64
EPS = 1e-6
ROPE_THETA = 10000.0
HEAD_DIM = 64
LANES = 128
A_HEADS = 8
A_KV_HEADS = 2
A_GROUP = A_HEADS // A_KV_HEADS
DN_HEADS = 8
DN_DK = 64
DN_CHUNK = 64
DN_BLOCK = 256
DF_HEADS = 4
N_EXPERTS = 64
TOP_K = 6
N_GROUPS = 8
TOPK_GROUPS = 4
GROUP_SIZE = N_EXPERTS // N_GROUPS
MOE_FF = 256
ROUTED_SCALE = 2.5
BR_W = 512

COL_AQ, COL_CQ, COL_CK, COL_CV, COL_BZ = 0, 512, 1024, 1536, 2048
COL_BQ, COL_BK, COL_BV = 2560, 3072, 3584
COL_GATE = 4096
COL_AK, COL_AV, COL_AB = 7168, 7296, 7424
IN_W_PAD = 7680
IN_TN = 1280

VMEM_LIMIT = 56 * 1024 * 1024


def _cparams(sem):
    return pltpu.CompilerParams(dimension_semantics=sem, vmem_limit_bytes=VMEM_LIMIT)


def _dot(a, b):
    return jnp.dot(a, b, preferred_element_type=F32)


def _dot_nt(a, b):
    return lax.dot_general(a, b, (((1,), (1,)), ((), ())), preferred_element_type=F32)


def _dot_tn(a, b):
    return lax.dot_general(a, b, (((0,), (0,)), ((), ())), preferred_element_type=F32)


def _split3(x):
    hi = x.astype(BF16)
    r = x - hi.astype(F32)
    mid = r.astype(BF16)
    lo = (r - mid.astype(F32)).astype(BF16)
    return hi, mid, lo


def _dot3_l(x, m):
    hi, mid, lo = _split3(x)
    return _dot(hi, m) + _dot(mid, m) + _dot(lo, m)


def _dot3_r(m, x):
    hi, mid, lo = _split3(x)
    return _dot(m, hi) + _dot(m, mid) + _dot(m, lo)


def _silu(x):
    return x * jax.nn.sigmoid(x)


def _mod_kernel(cond_ref, w_ref, b_ref, o_ref):
    a = _silu(cond_ref[...]).astype(BF16)
    o_ref[...] = _dot(a, w_ref[...].astype(BF16)) + b_ref[...]


def _mod_vectors(cond, w_mod, b_mod):
    L, D, W = w_mod.shape
    tn = 1024
    return pl.pallas_call(
        _mod_kernel,
        out_shape=jax.ShapeDtypeStruct((L, 8, W), F32),
        grid=(L, W // tn),
        in_specs=[pl.BlockSpec((8, D), lambda l, j: (0, 0)),
                  pl.BlockSpec((None, D, tn), lambda l, j: (l, 0, j)),
                  pl.BlockSpec((None, 1, tn), lambda l, j: (l, 0, j))],
        out_specs=pl.BlockSpec((None, 8, tn), lambda l, j: (l, 0, j)),
        compiler_params=_cparams(("arbitrary", "arbitrary")),
        name="mod_vectors",
    )(cond, w_mod, b_mod.reshape(L, 1, W))


def _is_ctx_row(blk, tm, n_lat):
    return blk * tm + lax.broadcasted_iota(jnp.int32, (tm, 1), 0) >= n_lat


def _norm_mod(x, g, mod_ref, blk, tm, n_lat, sh, sc):
    D = x.shape[-1]
    y = x * lax.rsqrt(jnp.mean(x * x, axis=-1, keepdims=True) + EPS) * g
    is_ctx = _is_ctx_row(blk, tm, n_lat)
    scale = jnp.where(is_ctx, mod_ref[0:1, sc * D:(sc + 1) * D], mod_ref[1:2, sc * D:(sc + 1) * D])
    shift = jnp.where(is_ctx, mod_ref[0:1, sh * D:(sh + 1) * D], mod_ref[1:2, sh * D:(sh + 1) * D])
    return y * (1.0 + scale) + shift


def _inproj_kernel(h_ref, mod_ref, g_ref, w_ref, p_ref, ab_ref, u_scr, *, tm, n_lat, ab_off):
    i = pl.program_id(1)
    j = pl.program_id(2)

    @pl.when(j == 0)
    def _():
        u_scr[...] = _norm_mod(h_ref[...], g_ref[...], mod_ref, i, tm, n_lat, 0, 1).astype(BF16)

    r = _dot(u_scr[...], w_ref[...])
    p_ref[...] = r.astype(BF16)

    @pl.when(j == pl.num_programs(2) - 1)
    def _():
        ab_ref[...] = r[:, ab_off:ab_off + LANES]


def _in_projection(h, modtab, g_norm, w_perm, n_lat, tm):
    B, T, D = h.shape
    ncol = IN_W_PAD // IN_TN
    kern = functools.partial(_inproj_kernel, tm=tm, n_lat=n_lat, ab_off=COL_AB - (ncol - 1) * IN_TN)
    return pl.pallas_call(
        kern,
        out_shape=(jax.ShapeDtypeStruct((B, T, IN_W_PAD), BF16),
                   jax.ShapeDtypeStruct((B, T, LANES), F32)),
        grid=(B, T // tm, ncol),
        in_specs=[pl.BlockSpec((None, tm, D), lambda b, i, j: (b, i, 0)),
                  pl.BlockSpec((None, 2, 6 * D), lambda b, i, j: (b, 0, 0)),
                  pl.BlockSpec((1, D), lambda b, i, j: (0, 0)),
                  pl.BlockSpec((D, IN_TN), lambda b, i, j: (0, j))],
        out_specs=(pl.BlockSpec((None, tm, IN_TN), lambda b, i, j: (b, i, j)),
                   pl.BlockSpec((None, tm, LANES), lambda b, i, j: (b, i, 0))),
        scratch_shapes=[pltpu.VMEM((tm, D), BF16)],
        compiler_params=_cparams(("arbitrary", "arbitrary", "arbitrary")),
        name="in_projection",
    )(h, modtab, g_norm.reshape(1, D), w_perm)


def _rope(x, cos, sin_a, sin_b):
    return x * cos + pltpu.roll(x, LANES - 32, 1) * sin_a + pltpu.roll(x, 32, 1) * sin_b


def _attn_prep_kernel(aq_ref, ak_ref, av_ref, cq_ref, ck_ref, cv_ref, cos_ref, sa_ref, sb_ref, gq_ref, gk_ref,
                      m_ref, qa_ref, ka_ref, vta_ref, qd_ref, kd_ref, vtd_ref):
    cos, sa, sb = cos_ref[...], sa_ref[...], sb_ref[...]
    vta_ref[...] = av_ref[...].astype(F32).T.astype(BF16)
    for h in range(DF_HEADS):
        vtd_ref[h] = cv_ref[:, h * LANES:(h + 1) * LANES].astype(F32).T.astype(BF16)
    m64 = m_ref[...]
    half = lax.broadcasted_iota(jnp.int32, cos.shape, 1) // HEAD_DIM
    scale = HEAD_DIM ** -0.5 * math.log2(math.e)

    def head_norm(x, g):
        ms = _dot3_l(x * x, m64) * (1.0 / HEAD_DIM)
        return x * lax.rsqrt(ms + EPS) * g

    for s in range(A_HEADS // 2):
        x = aq_ref[:, s * LANES:(s + 1) * LANES].astype(F32)
        y = _rope(head_norm(x, gq_ref[...]), cos, sa, sb) * scale
        y_sw = pltpu.roll(y, HEAD_DIM, 1)
        kvh = (2 * s) // A_GROUP
        for hh in range(2):
            src = y if hh == kvh else y_sw
            qa_ref[2 * s + hh] = jnp.where(half == kvh, src, 0.0).astype(BF16)
    xk = ak_ref[...].astype(F32)
    ka_ref[...] = _rope(head_norm(xk, gk_ref[...]), cos, sa, sb).astype(BF16)
    for h in range(DF_HEADS):
        x = cq_ref[:, h * LANES:(h + 1) * LANES].astype(F32)
        y = _rope(x, cos, sa, sb) * scale
        qd_ref[2 * h] = jnp.where(half == 0, y, 0.0).astype(BF16)
        qd_ref[2 * h + 1] = jnp.where(half == 1, y, 0.0).astype(BF16)
        xk = ck_ref[:, h * LANES:(h + 1) * LANES].astype(F32)
        kd_ref[:, h * LANES:(h + 1) * LANES] = _rope(xk, cos, sa, sb).astype(BF16)


def _attn_prep(P, tabs, g_q, g_k, tk):
    B, T, _ = P.shape
    tm = 256
    per = tk // tm
    cos2, sin_a, sin_b = tabs
    m64 = jnp.asarray(np.kron(np.eye(2), np.ones((HEAD_DIM, HEAD_DIM))), BF16)
    gq2 = jnp.tile(g_q, 2).reshape(1, LANES)
    gk2 = jnp.tile(g_k, 2).reshape(1, LANES)
    tab_spec = pl.BlockSpec((tm, LANES), lambda b, i: (i, 0))
    vec_spec = pl.BlockSpec((1, LANES), lambda b, i: (0, 0))
    return pl.pallas_call(
        _attn_prep_kernel,
        out_shape=(jax.ShapeDtypeStruct((B, A_HEADS, T, LANES), BF16),
                   jax.ShapeDtypeStruct((B, T, LANES), BF16),
                   jax.ShapeDtypeStruct((B, T // tk, LANES, tk), BF16),
                   jax.ShapeDtypeStruct((B, 2 * DF_HEADS, T, LANES), BF16),
                   jax.ShapeDtypeStruct((B, T, 512), BF16),
                   jax.ShapeDtypeStruct((B, DF_HEADS, T // tk, LANES, tk), BF16)),
        grid=(B, T // tm),
        in_specs=[pl.BlockSpec((None, tm, 512), lambda b, i: (b, i, COL_AQ // 512)),
                  pl.BlockSpec((None, tm, LANES), lambda b, i: (b, i, COL_AK // LANES)),
                  pl.BlockSpec((None, tm, LANES), lambda b, i: (b, i, COL_AV // LANES)),
                  pl.BlockSpec((None, tm, 512), lambda b, i: (b, i, COL_CQ // 512)),
                  pl.BlockSpec((None, tm, 512), lambda b, i: (b, i, COL_CK // 512)),
                  pl.BlockSpec((None, tm, 512), lambda b, i: (b, i, COL_CV // 512)),
                  tab_spec, tab_spec, tab_spec, vec_spec, vec_spec,
                  pl.BlockSpec((LANES, LANES), lambda b, i: (0, 0))],
        out_specs=(pl.BlockSpec((None, A_HEADS, tm, LANES), lambda b, i: (b, 0, i, 0)),
                   pl.BlockSpec((None, tm, LANES), lambda b, i: (b, i, 0)),
                   pl.BlockSpec((None, None, LANES, tm), lambda b, i: (b, i // per, 0, i % per)),
                   pl.BlockSpec((None, 2 * DF_HEADS, tm, LANES), lambda b, i: (b, 0, i, 0)),
                   pl.BlockSpec((None, tm, 512), lambda b, i: (b, i, 0)),
                   pl.BlockSpec((None, DF_HEADS, None, LANES, tm), lambda b, i: (b, 0, i // per, 0, i % per))),
        compiler_params=_cparams(("arbitrary", "arbitrary")),
        name="attn_prep",
    )(P, P, P, P, P, P, cos2, sin_a, sin_b, gq2, gk2, m64)


FLASH_TQ = 256
FLASH_PIECE = 256
FLASH_MAX_OVERSHOOT = 60.0


def _flash_kernel(*refs, mode, cols, tk, n_ctx, n_chunks, ctx_only, lambda_init):
    refs = list(refs)
    q_ref, k_ref, vt_ref = refs[0:3]
    pos = 3
    if mode == "diff":
        lam_ref, g_ref = refs[pos], refs[pos + 1]
        pos += 2
    o_ref, m_scr, l_scr, over_scr, acc_scr = refs[pos:pos + 5]
    q = q_ref[...].reshape(cols, LANES)

    def reset():
        m_scr[...] = jnp.full((1, cols), -jnp.inf, F32)
        l_scr[...] = jnp.zeros((1, cols), F32)
        acc_scr[...] = jnp.zeros((LANES, cols), F32)

    def exact(k, vt):
        s = _dot_nt(k, q)
        m_old = m_scr[...]
        m_new = jnp.maximum(m_old, jnp.max(s, axis=0, keepdims=True))
        alpha = jnp.exp2(m_old - m_new)
        p = jnp.exp2(s - m_new)
        l_scr[...] = alpha * l_scr[...] + jnp.sum(p, axis=0, keepdims=True)
        acc_scr[...] = alpha * acc_scr[...] + _dot(vt, p.astype(BF16))
        m_scr[...] = m_new

    def lagged(k, vt):
        ref = m_scr[...]
        s = _dot_nt(k, q)
        p = jnp.exp2(s - ref)
        m_cur = jnp.max(s, axis=0, keepdims=True)
        m_new = jnp.maximum(ref, m_cur)
        alpha = jnp.exp2(ref - m_new)
        l_scr[...] = (l_scr[...] + jnp.sum(p, axis=0, keepdims=True)) * alpha
        acc_scr[...] = (acc_scr[...] + _dot(vt, p.astype(BF16))) * alpha
        over_scr[...] = jnp.maximum(over_scr[...], m_cur - ref)
        m_scr[...] = m_new

    def k_chunk(c):
        return k_ref[pl.ds(pl.multiple_of(c * tk, tk), tk), :]

    reset()
    if ctx_only:
        exact(k_ref[...], vt_ref[:, tk - n_ctx:])
    else:
        over_scr[...] = jnp.zeros((1, cols), F32)
        exact(k_ref[0:FLASH_PIECE, :], vt_ref[0, :, 0:FLASH_PIECE])
        if tk > FLASH_PIECE:
            lagged(k_ref[FLASH_PIECE:tk, :], vt_ref[0, :, FLASH_PIECE:tk])

        def lagged_pair(c):
            ref0 = m_scr[...]
            s0 = _dot_nt(k_chunk(c), q)
            m0 = jnp.max(s0, axis=0, keepdims=True)
            ref1 = jnp.maximum(ref0, m0)
            s1 = _dot_nt(k_chunk(c + 1), q)
            m1 = jnp.max(s1, axis=0, keepdims=True)
            ref2 = jnp.maximum(ref1, m1)
            p0 = jnp.exp2(s0 - ref0)
            p1 = jnp.exp2(s1 - ref1)
            a0 = jnp.exp2(ref0 - ref1)
            a1 = jnp.exp2(ref1 - ref2)
            l_scr[...] = ((l_scr[...] + jnp.sum(p0, axis=0, keepdims=True)) * a0
                          + jnp.sum(p1, axis=0, keepdims=True)) * a1
            acc_scr[...] = ((acc_scr[...] + _dot(vt_ref[c], p0.astype(BF16))) * a0
                            + _dot(vt_ref[c + 1], p1.astype(BF16))) * a1
            over_scr[...] = jnp.maximum(over_scr[...], jnp.maximum(m0 - ref0, m1 - ref1))
            m_scr[...] = ref2

        def fast_body(it, carry):
            lagged_pair(1 + 2 * it)
            return carry
        lax.fori_loop(0, (n_chunks - 1) // 2, fast_body, 0)

        @pl.when(jnp.max(over_scr[...]) > FLASH_MAX_OVERSHOOT)
        def _():
            reset()

            def exact_body(c, carry):
                exact(k_chunk(c), vt_ref[c])
                return carry
            lax.fori_loop(0, n_chunks, exact_body, 0)

    if mode == "gqa":
        kvh = pl.program_id(1)
        o = acc_scr[pl.ds(pl.multiple_of(kvh * HEAD_DIM, HEAD_DIM), HEAD_DIM), :] / l_scr[...]
        pieces = [o[:, g * FLASH_TQ:(g + 1) * FLASH_TQ] for g in range(cols // FLASH_TQ)]
        o_ref[...] = jnp.concatenate(pieces, axis=0).T.astype(o_ref.dtype)
    else:
        lq = lam_ref[...]
        lam = (jnp.exp(jnp.sum(lq[0:1] * lq[1:2], axis=-1, keepdims=True))
               - jnp.exp(jnp.sum(lq[2:3] * lq[3:4], axis=-1, keepdims=True)) + lambda_init)
        o = acc_scr[...] / l_scr[...]
        d = o[:, :cols // 2] - lam * o[:, cols // 2:]
        y = d * lax.rsqrt(jnp.mean(d * d, axis=0, keepdims=True) + EPS) * g_ref[...]
        o_ref[...] = (y * (1.0 - lambda_init)).T.astype(o_ref.dtype)


def _flash(mode, q, k, vt, n_ctx, ctx_only, extra=(), lambda_init=0.0):
    B, NQ, T, _ = q.shape
    S = T - n_ctx
    tk = vt.shape[-1]
    n_chunks = vt.shape[-3]
    assert n_chunks % 2 == 1
    n_maps = A_GROUP if mode == "gqa" else 2
    n_heads = NQ // n_maps
    if ctx_only:
        tq, n_q, q0, rows = n_ctx, 1, S // n_ctx, n_ctx
        k_rows, k0 = n_ctx, S // n_ctx
        vt_blk, vt0 = None, n_chunks - 1
    else:
        tq = FLASH_TQ if mode == "gqa" else 2 * FLASH_TQ
        n_q, q0, rows = S // tq, 0, S
        k_rows, k0 = T, 0
        vt_blk, vt0 = n_chunks, 0
    cols = n_maps * tq
    kern = functools.partial(_flash_kernel, mode=mode, cols=cols, tk=tk, n_ctx=n_ctx,
                             n_chunks=n_chunks, ctx_only=ctx_only, lambda_init=lambda_init)
    in_specs = [pl.BlockSpec((None, n_maps, tq, LANES), lambda b, h, i: (b, h, i + q0, 0))]
    if mode == "gqa":
        in_specs += [pl.BlockSpec((None, k_rows, LANES), lambda b, h, i: (b, k0, 0)),
                     pl.BlockSpec((None, vt_blk, LANES, tk), lambda b, h, i: (b, vt0, 0, 0))]
        out_spec = pl.BlockSpec((None, tq, A_GROUP * HEAD_DIM), lambda b, h, i: (b, i, h))
    else:
        in_specs += [pl.BlockSpec((None, k_rows, LANES), lambda b, h, i: (b, k0, h)),
                     pl.BlockSpec((None, None, vt_blk, LANES, tk), lambda b, h, i: (b, h, vt0, 0, 0)),
                     pl.BlockSpec((4, HEAD_DIM), lambda b, h, i: (0, 0)),
                     pl.BlockSpec((LANES, 1), lambda b, h, i: (0, 0))]
        out_spec = pl.BlockSpec((None, tq, LANES), lambda b, h, i: (b, i, h))
    return pl.pallas_call(
        kern,
        out_shape=jax.ShapeDtypeStruct((B, rows, BR_W), BF16),
        grid=(B, n_heads, n_q),
        in_specs=in_specs,
        out_specs=out_spec,
        scratch_shapes=[pltpu.VMEM((1, cols), F32)] * 3 + [pltpu.VMEM((LANES, cols), F32)],
        compiler_params=_cparams(("arbitrary", "arbitrary", "arbitrary")),
        name="flash_" + mode + ("_ctx" if ctx_only else ""),
    )(q, k, vt, *extra)


def _dn_prep_kernel(q_ref, k_ref, v_ref, ql_ref, kl_ref, vl_ref, qr_ref, kr_ref, vr_ref, ab_ref,
                    wc_ref, alog_ref, dtb_ref, mh_ref, eg_ref, eb_ref, tm_ref, ones_ref,
                    qo_ref, ko_ref, kb_ref, vbk_ref, qdec_ref, kdecbd_ref, decbd_ref, egl_ref,
                    *, tm):
    i = pl.program_id(1)
    nblk = pl.num_programs(1)
    W = DN_HEADS * DN_DK
    row = lax.broadcasted_iota(jnp.int32, (tm, W), 0)
    col = lax.broadcasted_iota(jnp.int32, (tm, W), 1)
    left_ok = jnp.logical_and(i >= 1, i < nblk - 1).astype(F32)
    right_ok = (i < nblk - 2).astype(F32)

    def conv_silu(x_ref, xl_ref, xr_ref, c0):
        x = x_ref[...].astype(F32)
        prev_row = xl_ref[7:8, :].astype(F32) * left_ok
        next_row = xr_ref[0:1, :].astype(F32) * right_ok
        x_prev = jnp.where(row == 0, prev_row, pltpu.roll(x, 1, 0))
        x_next = jnp.where(row == tm - 1, next_row, pltpu.roll(x, tm - 1, 0))
        y = (x_prev * wc_ref[0:1, c0:c0 + W] + x * wc_ref[1:2, c0:c0 + W]
             + x_next * wc_ref[2:3, c0:c0 + W])
        return _silu(y)

    def l2norm(x):
        ss = _dot3_l(x * x, mh_ref[...])
        return x * lax.rsqrt(ss + EPS)

    q = l2norm(conv_silu(q_ref, ql_ref, qr_ref, 0)) * (DN_DK ** -0.5)
    k = l2norm(conv_silu(k_ref, kl_ref, kr_ref, W))
    v = conv_silu(v_ref, vl_ref, vr_ref, 2 * W)

    ab = ab_ref[...]
    z = ab + dtb_ref[...]
    softplus = jnp.maximum(z, 0.0) + jnp.log(1.0 + jnp.exp(-jnp.abs(z)))
    g_all = -jnp.exp(alog_ref[...]) * softplus
    beta_all = jax.nn.sigmoid(ab)

    t_in = row & (DN_CHUNK - 1)
    s_in = col & (DN_CHUNK - 1)

    def put(ref, d, val):
        for h in range(DN_HEADS):
            piece = val[:, h * DN_DK:(h + 1) * DN_DK]
            if d is None:
                ref[h] = piece.astype(ref.dtype)
            else:
                ref[d, h] = piece.astype(ref.dtype)

    lo = lax.broadcasted_iota(jnp.int32, (tm, LANES), 1) < DN_DK
    n_chunks = tm // DN_CHUNK
    bd_shape = (tm, n_chunks * DN_CHUNK)
    same_chunk = (lax.broadcasted_iota(jnp.int32, bd_shape, 0) // DN_CHUNK
                  == lax.broadcasted_iota(jnp.int32, bd_shape, 1) // DN_CHUNK)

    def pair(a, b, h):
        sa = a[:, (h // 2) * LANES:(h // 2 + 1) * LANES]
        sb = b[:, (h // 2) * LANES:(h // 2 + 1) * LANES]
        if h % 2 == 0:
            return jnp.where(lo, sa, pltpu.roll(sb, DN_DK, 1))
        return jnp.where(lo, pltpu.roll(sa, DN_DK, 1), sb)

    def put_block_diag(ref, d, val):
        for h in range(DN_HEADS):
            x = pair(val, val, h)
            ref[d, h] = jnp.where(same_chunk, jnp.concatenate([x] * (n_chunks // 2), axis=1), 0.0).astype(ref.dtype)

    put(qo_ref, None, q)
    put(ko_ref, None, k)
    for d in range(2):
        g_e = _dot3_l(g_all, eg_ref[d])
        beta_e = _dot3_l(beta_all, eb_ref[d])
        gc = _dot3_r(tm_ref[d], g_e)
        g_tot = _dot3_r(ones_ref[...], g_e)
        strict = (t_in > s_in) if d == 0 else (t_in < s_in)
        causal = (t_in >= s_in) if d == 0 else (t_in <= s_in)
        diff = _dot3_r(tm_ref[d], jnp.where(strict, g_e, 0.0))
        decay = jnp.where(causal, jnp.exp(diff), 0.0)
        e_g = jnp.exp(gc)
        kb = k * beta_e
        put(kb_ref, d, kb)
        vb, kbg = v * beta_e, kb * e_g
        for h in range(DN_HEADS):
            vbk_ref[d, h] = pair(vb, kbg, h).astype(vbk_ref.dtype)
        put(qdec_ref, d, q * e_g)
        put_block_diag(kdecbd_ref, d, k * jnp.exp(g_tot - gc))
        put_block_diag(decbd_ref, d, decay)
        e_tot = jnp.exp(g_tot)
        for h in range(DN_HEADS):
            for c in range(tm // DN_CHUNK):
                egl_ref[d, h, c:c + 1, :] = e_tot[c * DN_CHUNK:c * DN_CHUNK + 1, h * DN_DK:(h + 1) * DN_DK]


def _dn_consts():
    n = DN_BLOCK
    t = np.arange(n)
    same = (t[:, None] // DN_CHUNK) == (t[None, :] // DN_CHUNK)
    pre = same & (t[None, :] <= t[:, None])
    suf = same & (t[None, :] >= t[:, None])
    tmat = np.stack([pre, suf]).astype(np.float32)
    ones = same.astype(np.float32)
    W = DN_HEADS * DN_DK
    eg = np.zeros((2, LANES, W), np.float32)
    eb = np.zeros((2, LANES, W), np.float32)
    for d in range(2):
        for h in range(DN_HEADS):
            eg[d, d * DN_HEADS + h, h * DN_DK:(h + 1) * DN_DK] = 1.0
            eb[d, 2 * DN_HEADS + d * DN_HEADS + h, h * DN_DK:(h + 1) * DN_DK] = 1.0
    mh = np.kron(np.eye(DN_HEADS), np.ones((DN_DK, DN_DK))).astype(np.float32)
    return tuple(jnp.asarray(a, BF16) for a in (mh, eg, eb, tmat, ones))


def _dn_prep(P, AB, w_conv, a_log, dt_bias):
    B, T, _ = P.shape
    tm = DN_BLOCK
    nblk = T // tm
    W = DN_HEADS * DN_DK
    mh, eg, eb, tmat, ones = _dn_consts()
    alog_row = jnp.zeros((1, LANES), F32).at[0, :2 * DN_HEADS].set(a_log.reshape(-1))
    dtb_row = jnp.zeros((1, LANES), F32).at[0, :2 * DN_HEADS].set(dt_bias.reshape(-1))
    r8 = tm // 8

    def main(c0):
        return pl.BlockSpec((None, tm, W), lambda b, i: (b, i, c0 // W))

    def left(c0):
        return pl.BlockSpec((None, 8, W), lambda b, i: (b, jnp.maximum(i * r8 - 1, 0), c0 // W))

    def right(c0):
        return pl.BlockSpec((None, 8, W), lambda b, i: (b, jnp.minimum((i + 1) * r8, nblk * r8 - 1), c0 // W))

    def full(a):
        nd = a.ndim
        return pl.BlockSpec(a.shape, lambda b, i: (0,) * nd)

    hm = pl.BlockSpec((None, DN_HEADS, tm, DN_DK), lambda b, i: (b, 0, i, 0))

    def hm2(width):
        return pl.BlockSpec((2, None, DN_HEADS, tm, width), lambda b, i: (0, b, 0, i, 0))

    def sh_hm2(width):
        return jax.ShapeDtypeStruct((2, B, DN_HEADS, T, width), BF16)

    sh_hm = jax.ShapeDtypeStruct((B, DN_HEADS, T, DN_DK), BF16)
    return pl.pallas_call(
        functools.partial(_dn_prep_kernel, tm=tm),
        out_shape=(sh_hm, sh_hm, sh_hm2(DN_DK), sh_hm2(2 * DN_DK), sh_hm2(DN_DK), sh_hm2(tm), sh_hm2(tm),
                   jax.ShapeDtypeStruct((2, B, nblk, DN_HEADS, tm // DN_CHUNK, DN_DK), F32)),
        grid=(B, nblk),
        in_specs=[main(COL_BQ), main(COL_BK), main(COL_BV),
                  left(COL_BQ), left(COL_BK), left(COL_BV),
                  right(COL_BQ), right(COL_BK), right(COL_BV),
                  pl.BlockSpec((None, tm, LANES), lambda b, i: (b, i, 0)),
                  full(w_conv), full(alog_row), full(dtb_row), full(mh), full(eg), full(eb), full(tmat),
                  full(ones)],
        out_specs=(hm, hm, hm2(DN_DK), hm2(2 * DN_DK), hm2(DN_DK), hm2(tm), hm2(tm),
                   pl.BlockSpec((2, None, None, DN_HEADS, tm // DN_CHUNK, DN_DK),
                                lambda b, i: (0, b, i, 0, 0, 0))),
        compiler_params=_cparams(("arbitrary", "arbitrary")),
        name="dn_prep",
    )(P, P, P, P, P, P, P, P, P, AB, w_conv, alog_row, dtb_row, mh, eg, eb, tmat, ones)


def _dn_chunk_kernel(q_ref, k_ref, kb_ref, vbk_ref, qdec_ref, kdecbd_ref, decbd_ref,
                     qeff_ref, o0_ref, mk_ref, n_ref):
    n = DN_BLOCK
    r = lax.broadcasted_iota(jnp.int32, (n, n), 0)
    c = lax.broadcasted_iota(jnp.int32, (n, n), 1)
    off_diag = r != c
    eye = (r == c).astype(F32)
    heads = range(DN_HEADS)
    low, qk = [], []
    for h in heads:
        dec = decbd_ref[h].astype(F32)
        k = k_ref[h]
        low.append(jnp.where(off_diag, _dot_nt(kb_ref[h], k) * dec, 0.0))
        qk.append((_dot_nt(q_ref[h], k) * dec).astype(BF16))
    p = [eye - m for m in low]
    mpow = low
    for _ in range(5):
        mpow = [_dot(m.astype(BF16), m.astype(BF16)) for m in mpow]
        p = [a + _dot(a.astype(BF16), m.astype(BF16)) for a, m in zip(p, mpow)]
    uw = [_dot(p[h].astype(BF16), vbk_ref[h]).astype(BF16) for h in heads]
    qkuw = [_dot(qk[h], uw[h]) for h in heads]
    for h in heads:
        o0_ref[h] = qkuw[h][:, :DN_DK]
        qeff_ref[h] = (qdec_ref[h].astype(F32) - qkuw[h][:, DN_DK:]).astype(qeff_ref.dtype)
        kt = _dot_tn(kdecbd_ref[h], uw[h])
        n_ref[h] = kt[:, :DN_DK]
        mk_ref[h] = kt[:, DN_DK:].astype(mk_ref.dtype)


def _dn_chunk(prep):
    q, k, kb, vbk, qdec, kdecbd, decbd, _ = prep
    B, H, T, dk = q.shape
    tm = DN_BLOCK
    hm = pl.BlockSpec((None, H, tm, dk), lambda d, b, i: (b, 0, i, 0))

    def hm2(width):
        return pl.BlockSpec((None, None, H, tm, width), lambda d, b, i: (d, b, 0, i, 0))

    def sh(dtype):
        return jax.ShapeDtypeStruct((2, B, H, T, dk), dtype)

    return pl.pallas_call(
        _dn_chunk_kernel,
        out_shape=(sh(BF16), sh(F32), sh(BF16), sh(F32)),
        grid=(2, B, T // tm),
        in_specs=[hm, hm, hm2(dk), hm2(2 * dk), hm2(dk), hm2(tm), hm2(tm)],
        out_specs=(hm2(dk), hm2(dk), hm2(dk), hm2(dk)),
        compiler_params=_cparams(("arbitrary", "arbitrary", "arbitrary")),
        name="dn_chunk",
    )(q, k, kb, vbk, qdec, kdecbd, decbd)


def _dn_scan_kernel(*refs, n_batch):
    ins = (refs[0:5], refs[5:10])
    o_refs = refs[10:12]
    s_scr = refs[12]
    t = pl.program_id(0)
    n_chunks = DN_BLOCK // DN_CHUNK
    C = DN_CHUNK

    @pl.when(t == 0)
    def _():
        s_scr[...] = jnp.zeros_like(s_scr)

    for step in range(n_chunks):
        for d in range(2):
            ci = step if d == 0 else n_chunks - 1 - step
            rows = slice(ci * C, (ci + 1) * C)
            qeff_ref, o0_ref, mk_ref, n_ref, egl_ref = ins[d]
            for b in range(n_batch):
                for h in range(DN_HEADS):
                    s = s_scr[d, b, h]
                    sb = s.astype(BF16)
                    o_refs[d][b, h, rows, :] = _dot(qeff_ref[b, h, rows, :], sb) + o0_ref[b, h, rows, :]
                    s_scr[d, b, h] = (s * egl_ref[b, h, ci:ci + 1, :] - _dot(mk_ref[b, h, rows, :], sb)
                                      + n_ref[b, h, rows, :])


def _dn_scan(maps, egl):
    qeff, o0, mk, nn = maps
    _, B, H, T, dk = qeff.shape
    tm = DN_BLOCK
    nblk = T // tm

    def blk(d, t):
        return jnp.where(t == 0, nblk - 1, nblk - 1 - t if d == 1 else t - 1)

    in_specs, args = [], []
    for d in range(2):
        for a in (qeff, o0, mk, nn):
            in_specs.append(pl.BlockSpec((None, B, H, tm, dk), lambda t, d=d: (d, 0, 0, blk(d, t), 0)))
            args.append(a)
        in_specs.append(pl.BlockSpec((None, B, None, H, tm // DN_CHUNK, dk),
                                     lambda t, d=d: (d, 0, blk(d, t), 0, 0, 0)))
        args.append(egl)
    out_specs = tuple(pl.BlockSpec((B, H, tm, dk), lambda t, d=d: (0, 0, blk(d, t), 0)) for d in range(2))
    return pl.pallas_call(
        functools.partial(_dn_scan_kernel, n_batch=B),
        out_shape=(jax.ShapeDtypeStruct((B, H, T, dk), F32),) * 2,
        grid=(nblk,),
        in_specs=in_specs,
        out_specs=out_specs,
        scratch_shapes=[pltpu.VMEM((2, B, H, dk, dk), F32)],
        compiler_params=_cparams(("arbitrary",)),
        name="dn_scan",
    )(*args)


def _dn_post_kernel(of_ref, ob_ref, z_ref, g_ref, o_ref):
    z = z_ref[...].astype(F32)
    for h in range(DN_HEADS):
        o = of_ref[h] + ob_ref[h]
        y = o * lax.rsqrt(jnp.mean(o * o, axis=-1, keepdims=True) + EPS) * g_ref[...]
        zz = z[:, h * DN_DK:(h + 1) * DN_DK]
        o_ref[:, h * DN_DK:(h + 1) * DN_DK] = (y * _silu(zz)).astype(o_ref.dtype)


def _dn_post(o_f, o_b, P, g_out):
    B, H, T, dk = o_f.shape
    tm = DN_BLOCK
    hm = pl.BlockSpec((None, H, tm, dk), lambda b, i: (b, 0, i, 0))
    return pl.pallas_call(
        _dn_post_kernel,
        out_shape=jax.ShapeDtypeStruct((B, T, BR_W), F32),
        grid=(B, T // tm),
        in_specs=[hm, hm,
                  pl.BlockSpec((None, tm, BR_W), lambda b, i: (b, i, COL_BZ // BR_W)),
                  pl.BlockSpec((1, dk), lambda b, i: (0, 0))],
        out_specs=pl.BlockSpec((None, tm, BR_W), lambda b, i: (b, i, 0)),
        compiler_params=_cparams(("arbitrary", "arbitrary")),
        name="dn_post",
    )(o_f, o_b, P, g_out.reshape(1, dk))


def _merge_kernel(a_ref, b_ref, c_ref, ga_ref, gb_ref, gc_ref, h_ref, mod_ref, g2_ref, wb_ref, wo_ref,
                  h_out_ref, v_out_ref, *, tm, n_lat):
    i = pl.program_id(1)
    D = h_ref.shape[-1]
    m = jax.nn.sigmoid(ga_ref[...].astype(F32)) * _dot(a_ref[...], wb_ref[0])
    m = m + jax.nn.sigmoid(gb_ref[...].astype(F32)) * _dot(b_ref[...].astype(BF16), wb_ref[1])
    m = m + jax.nn.sigmoid(gc_ref[...].astype(F32)) * _dot(c_ref[...], wb_ref[2])
    y = _dot(m.astype(BF16), wo_ref[...])
    gate = jnp.where(_is_ctx_row(i, tm, n_lat), mod_ref[0:1, 2 * D:3 * D], mod_ref[1:2, 2 * D:3 * D])
    h_new = h_ref[...] + gate * y
    h_out_ref[...] = h_new
    v_out_ref[...] = _norm_mod(h_new, g2_ref[...], mod_ref, i, tm, n_lat, 3, 4).astype(BF16)


def _merge(br_a, br_b, br_c, P, h, modtab, g_norm2, w_branch, w_out, n_lat, tm):
    B, T, D = h.shape
    br = pl.BlockSpec((None, tm, BR_W), lambda b, i: (b, i, 0))

    def gate(j):
        return pl.BlockSpec((None, tm, D), lambda b, i: (b, i, COL_GATE // D + j))

    tok = pl.BlockSpec((None, tm, D), lambda b, i: (b, i, 0))
    return pl.pallas_call(
        functools.partial(_merge_kernel, tm=tm, n_lat=n_lat),
        out_shape=(jax.ShapeDtypeStruct((B, T, D), F32), jax.ShapeDtypeStruct((B, T, D), BF16)),
        grid=(B, T // tm),
        in_specs=[br, br, br, gate(0), gate(1), gate(2), tok,
                  pl.BlockSpec((None, 2, 6 * D), lambda b, i: (b, 0, 0)),
                  pl.BlockSpec((1, D), lambda b, i: (0, 0)),
                  pl.BlockSpec((3, BR_W, D), lambda b, i: (0, 0, 0)),
                  pl.BlockSpec((D, D), lambda b, i: (0, 0))],
        out_specs=(tok, tok),
        compiler_params=_cparams(("arbitrary", "arbitrary")),
        name="merge",
    )(br_a, br_b, br_c, P, P, P, h, modtab, g_norm2.reshape(1, D), w_branch, w_out)


def _router_kernel(v_ref, wr_ref, br_ref, perm_ref, c_ref):
    tn = v_ref.shape[0]
    G = N_GROUPS
    scores = jax.nn.sigmoid(_dot_nt(wr_ref[...], v_ref[...]))
    sel = scores + br_ref[...]
    slabs = [sel[j * G:(j + 1) * G] for j in range(GROUP_SIZE)]
    m1, m2 = slabs[0], jnp.full((G, tn), -jnp.inf, F32)
    for j in range(1, GROUP_SIZE):
        m2 = jnp.maximum(m2, jnp.minimum(m1, slabs[j]))
        m1 = jnp.maximum(m1, slabs[j])
    gs = m1 + m2
    gidx = lax.broadcasted_iota(jnp.int32, (G, tn), 0)
    grank = jnp.zeros((G, tn), jnp.int32)
    for g in range(G):
        rowv = gs[g:g + 1]
        beats = jnp.where(rowv > gs, 1, jnp.where(jnp.logical_and(rowv == gs, g < gidx), 1, 0))
        grank = grank + beats
    gmask = grank < TOPK_GROUPS
    masked = jnp.concatenate([jnp.where(gmask, s, -jnp.inf) for s in slabs], axis=0)
    pos = lax.broadcasted_iota(jnp.int32, (N_EXPERTS, tn), 0)
    orig = (pos % G) * GROUP_SIZE + pos // G
    rank = jnp.zeros((N_EXPERTS, tn), jnp.int32)
    for p in range(N_EXPERTS):
        o_p = (p % G) * GROUP_SIZE + p // G
        rowv = masked[p:p + 1]
        beats = jnp.where(rowv > masked, 1, jnp.where(jnp.logical_and(rowv == masked, o_p < orig), 1, 0))
        rank = rank + beats
    chosen = rank < TOP_K
    w = jnp.where(chosen, scores, 0.0)
    denom = jnp.sum(w, axis=0, keepdims=True) + 1e-20
    c = w / denom * ROUTED_SCALE
    c_t = sum(_dot_tn(part, perm_ref[...]) for part in _split3(c))
    for g in range(N_EXPERTS // MOE_EPS):
        c_ref[g] = c_t[:, g * MOE_EPS:(g + 1) * MOE_EPS]


def _router(v_flat, w_router, b_router):
    N, D = v_flat.shape
    tn = 1280 if N % 1280 == 0 else 256
    perm = np.array([(p % N_GROUPS) * GROUP_SIZE + p // N_GROUPS for p in range(N_EXPERTS)])
    wr_t = w_router.T[perm].astype(BF16)
    br = b_router[perm].reshape(N_EXPERTS, 1)
    unperm = jnp.asarray(np.eye(N_EXPERTS)[perm], BF16)
    ngrp = N_EXPERTS // MOE_EPS
    return pl.pallas_call(
        _router_kernel,
        out_shape=jax.ShapeDtypeStruct((ngrp, N, MOE_EPS), F32),
        grid=(N // tn,),
        in_specs=[pl.BlockSpec((tn, D), lambda i: (i, 0)),
                  pl.BlockSpec((N_EXPERTS, D), lambda i: (0, 0)),
                  pl.BlockSpec((N_EXPERTS, 1), lambda i: (0, 0)),
                  pl.BlockSpec((N_EXPERTS, N_EXPERTS), lambda i: (0, 0))],
        out_specs=pl.BlockSpec((ngrp, tn, MOE_EPS), lambda i: (0, i, 0)),
        compiler_params=_cparams(("arbitrary",)),
        name="router",
    )(v_flat, wr_t, br, unperm)


MOE_EPS = 4


def _moe_kernel(x_ref, c_ref, wgu_ref, wd_ref, sgu_ref, sd_ref, h_ref, mod_ref,
                o_ref, acc_ref, *, tm, n_lat):
    i = pl.program_id(1)
    e = pl.program_id(2)
    x = x_ref[...]
    D = x.shape[-1]
    F = MOE_FF

    @pl.when(e == 0)
    def _():
        gu = _dot(x, sgu_ref[...])
        hs = _silu(gu[:, :F]) * gu[:, F:]
        acc_ref[...] = _dot(hs.astype(BF16), sd_ref[...])

    c = c_ref[...]
    gu = _dot(x, wgu_ref[...])
    parts = []
    for j in range(MOE_EPS):
        hj = _silu(gu[:, 2 * j * F:(2 * j + 1) * F]) * gu[:, (2 * j + 1) * F:(2 * j + 2) * F] * c[:, j:j + 1]
        parts.append(hj.astype(BF16))
    hcat = jnp.concatenate(parts, axis=-1)
    acc_ref[...] += _dot(hcat, wd_ref[...])

    @pl.when(e == pl.num_programs(2) - 1)
    def _():
        gate = jnp.where(_is_ctx_row(i, tm, n_lat), mod_ref[0:1, 5 * D:6 * D], mod_ref[1:2, 5 * D:6 * D])
        o_ref[...] = h_ref[...] + gate * acc_ref[...]


def _moe_weights(w_gate, w_up, w_down):
    E, D, F = w_gate.shape
    gu = jnp.concatenate([w_gate, w_up], axis=-1).astype(BF16)
    gu = gu.reshape(E // MOE_EPS, MOE_EPS, D, 2 * F).transpose(0, 2, 1, 3).reshape(E // MOE_EPS, D, MOE_EPS * 2 * F)
    return gu, w_down.astype(BF16).reshape(E * F, D)


def _moe(v, c_grp, h, modtab, wgu, wd, sgu, sd, n_lat, tm):
    B, T, D = v.shape
    F = MOE_FF
    ngrp = N_EXPERTS // MOE_EPS
    tok = pl.BlockSpec((None, tm, D), lambda b, i, e: (b, i, 0))
    return pl.pallas_call(
        functools.partial(_moe_kernel, tm=tm, n_lat=n_lat),
        out_shape=jax.ShapeDtypeStruct((B, T, D), F32),
        grid=(B, T // tm, ngrp),
        in_specs=[tok,
                  pl.BlockSpec((None, None, tm, MOE_EPS), lambda b, i, e: (e, b, i, 0)),
                  pl.BlockSpec((None, D, MOE_EPS * 2 * F), lambda b, i, e: (e, 0, 0)),
                  pl.BlockSpec((MOE_EPS * F, D), lambda b, i, e: (e, 0)),
                  pl.BlockSpec((D, 2 * F), lambda b, i, e: (0, 0)),
                  pl.BlockSpec((F, D), lambda b, i, e: (0, 0)),
                  tok,
                  pl.BlockSpec((None, 2, 6 * D), lambda b, i, e: (b, 0, 0))],
        out_specs=tok,
        scratch_shapes=[pltpu.VMEM((tm, D), F32)],
        compiler_params=_cparams(("arbitrary", "arbitrary", "arbitrary")),
        name="moe",
    )(v, c_grp, wgu, wd, sgu, sd, h, modtab)


def _final_kernel(h_ref, g_ref, o_ref):
    x = h_ref[...]
    o_ref[...] = x * lax.rsqrt(jnp.mean(x * x, axis=-1, keepdims=True) + EPS) * g_ref[...]


def _final_norm(h, g_final, S):
    B, T, D = h.shape
    tm = 1024
    return pl.pallas_call(
        _final_kernel,
        out_shape=jax.ShapeDtypeStruct((B, S, D), F32),
        grid=(B, S // tm),
        in_specs=[pl.BlockSpec((None, tm, D), lambda b, i: (b, i, 0)),
                  pl.BlockSpec((1, D), lambda b, i: (0, 0))],
        out_specs=pl.BlockSpec((None, tm, D), lambda b, i: (b, i, 0)),
        compiler_params=_cparams(("arbitrary", "arbitrary")),
        name="final_norm",
    )(h, g_final.reshape(1, D))


def _rope_tables(S, n_ctx):
    rows = S // GRID_W
    row = jnp.repeat(jnp.arange(rows, dtype=F32), GRID_W)
    col = (jnp.arange(S) % GRID_W).astype(F32)
    axis_dim = HEAD_DIM // 2
    inv = 1.0 / (ROPE_THETA ** (jnp.arange(0, axis_dim, 2, dtype=F32) / axis_dim))
    ang = jnp.concatenate([row[:, None] * inv, col[:, None] * inv], axis=-1)
    ang = jnp.concatenate([ang, ang], axis=-1)
    cos = jnp.concatenate([jnp.cos(ang), jnp.ones((n_ctx, HEAD_DIM), F32)], axis=0)
    sin = jnp.concatenate([jnp.sin(ang), jnp.zeros((n_ctx, HEAD_DIM), F32)], axis=0)
    first = (jnp.arange(HEAD_DIM) < HEAD_DIM // 2)[None, :]
    sin_a = jnp.where(first, -sin, 0.0)
    sin_b = jnp.where(first, 0.0, sin)
    return tuple(jnp.tile(t, (1, 2)) for t in (cos, sin_a, sin_b))


def _permute_w_in(w):
    splits = np.cumsum([512, 128, 128, 1536, 512, 16, 16, 512, 512, 512, 3072])[:-1].tolist()
    aq, ak, av, bqkv, bz, ba, bb, cq, ck, cv, gate = jnp.split(w, splits, axis=-1)
    pad = jnp.zeros((w.shape[0], IN_W_PAD - COL_AB - 32), w.dtype)
    out = jnp.concatenate([aq, cq, ck, cv, bz, bqkv, gate, ak, av, ba, bb, pad], axis=-1)
    return out.astype(BF16)


def kernel(x, c, ctx, c_ctx, w_mod, b_mod, g_norm1, g_norm2, w_in, g_qnorm, g_knorm, w_conv, a_log, dt_bias, g_dn_out, lam_qk, g_subln, w_branch, w_out, w_router, b_router, w_e_gate, w_e_up, w_e_down, w_s_gate, w_s_up, w_s_down, g_final):
    B, S, D = x.shape
    n_ctx = ctx.shape[1]
    T = n_ctx + S
    L = w_mod.shape[0]
    N = B * T
    assert n_ctx == DN_BLOCK and S % 1024 == 0 and D == D_MODEL
    tm_tok = 1280 if T % 1280 == 0 else 256
    tm_merge = 640 if T % 640 == 0 else 256

    cond = jnp.zeros((8, D), F32).at[0].set(c_ctx).at[1:1 + B].set(c)
    mod = _mod_vectors(cond, w_mod, b_mod)
    tabs = _rope_tables(S, n_ctx)
    h = jnp.concatenate([x, ctx], axis=1)
    tk = 1280 if T % 1280 == 0 else 256

    for l in range(L):
        lambda_init = 0.8 - 0.6 * math.exp(-0.3 * l)
        modtab = jnp.stack([jnp.broadcast_to(mod[l, 0], (B, 6 * D)), mod[l, 1:1 + B]], axis=1)
        P, AB = _in_projection(h, modtab, g_norm1[l], _permute_w_in(w_in[l]), S, tm_tok)

        qa, ka, vta, qd, kd, vtd = _attn_prep(P, tabs, g_qnorm[l], g_knorm[l], tk)
        dx = (lam_qk[l], g_subln[l].reshape(LANES, 1))
        br_a = jnp.concatenate([_flash("gqa", qa, ka, vta, n_ctx, False),
                                _flash("gqa", qa, ka, vta, n_ctx, True)], axis=1)
        br_c = jnp.concatenate([_flash("diff", qd, kd, vtd, n_ctx, False, dx, lambda_init),
                                _flash("diff", qd, kd, vtd, n_ctx, True, dx, lambda_init)], axis=1)

        prep = _dn_prep(P, AB, w_conv[l], a_log[l], dt_bias[l])
        o_f, o_b = _dn_scan(_dn_chunk(prep), prep[-1])
        br_b = _dn_post(o_f, o_b, P, g_dn_out[l])

        h, v = _merge(br_a, br_b, br_c, P, h, modtab, g_norm2[l], w_branch[l].astype(BF16),
                      w_out[l].astype(BF16), S, tm_merge)

        c_grp = _router(v.reshape(N, D), w_router[l], b_router[l]).reshape(-1, B, T, MOE_EPS)
        wgu, wd = _moe_weights(w_e_gate[l], w_e_up[l], w_e_down[l])
        sgu = jnp.concatenate([w_s_gate[l], w_s_up[l]], axis=-1).astype(BF16)
        h = _moe(v, c_grp, h, modtab, wgu, wd, sgu, w_s_down[l].astype(BF16), S, tm_merge)

    return _final_norm(h, g_final, S)
```

```python
import functools
import math

import numpy as np
import jax
import jax.numpy as jnp
from jax import lax
from jax.experimental import pallas as pl
from jax.experimental.pallas import tpu as pltpu

F32 = jnp.float32
BF16 = jnp.bfloat16

D_MODEL = 1024
GRID_W = 64
EPS = 1e-6
ROPE_THETA = 10000.0
HEAD_DIM = 64
LANES = 128
A_HEADS = 8
A_KV_HEADS = 2
A_GROUP = A_HEADS // A_KV_HEADS
DN_HEADS = 8
DN_DK = 64
DN_CHUNK = 64
DN_BLOCK = 256
DF_HEADS = 4
N_EXPERTS = 64
TOP_K = 6
N_GROUPS = 8
TOPK_GROUPS = 4
GROUP_SIZE = N_EXPERTS // N_GROUPS
MOE_FF = 256
ROUTED_SCALE = 2.5
BR_W = 512

COL_AQ, COL_CQ, COL_CK, COL_CV, COL_BZ = 0, 512, 1024, 1536, 2048
COL_BQ, COL_BK, COL_BV = 2560, 3072, 3584
COL_GATE = 4096
COL_AK, COL_AV, COL_AB = 7168, 7296, 7424
IN_W_PAD = 7680
IN_TN = 1280

VMEM_LIMIT = 56 * 1024 * 1024


def _cparams(sem):
    return pltpu.CompilerParams(dimension_semantics=sem, vmem_limit_bytes=VMEM_LIMIT)


def _dot(a, b):
    return jnp.dot(a, b, preferred_element_type=F32)


def _dot_nt(a, b):
    return lax.dot_general(a, b, (((1,), (1,)), ((), ())), preferred_element_type=F32)


def _dot_tn(a, b):
    return lax.dot_general(a, b, (((0,), (0,)), ((), ())), preferred_element_type=F32)


def _split3(x):
    hi = x.astype(BF16)
    r = x - hi.astype(F32)
    mid = r.astype(BF16)
    lo = (r - mid.astype(F32)).astype(BF16)
    return hi, mid, lo


def _dot3_l(x, m):
    hi, mid, lo = _split3(x)
    return _dot(hi, m) + _dot(mid, m) + _dot(lo, m)


def _dot3_r(m, x):
    hi, mid, lo = _split3(x)
    return _dot(m, hi) + _dot(m, mid) + _dot(m, lo)


def _silu(x):
    return x * jax.nn.sigmoid(x)


def _mod_kernel(cond_ref, w_ref, b_ref, o_ref):
    a = _silu(cond_ref[...]).astype(BF16)
    o_ref[...] = _dot(a, w_ref[...].astype(BF16)) + b_ref[...]


def _mod_vectors(cond, w_mod, b_mod):
    L, D, W = w_mod.shape
    tn = 1024
    return pl.pallas_call(
        _mod_kernel,
        out_shape=jax.ShapeDtypeStruct((L, 8, W), F32),
        grid=(L, W // tn),
        in_specs=[pl.BlockSpec((8, D), lambda l, j: (0, 0)),
                  pl.BlockSpec((None, D, tn), lambda l, j: (l, 0, j)),
                  pl.BlockSpec((None, 1, tn), lambda l, j: (l, 0, j))],
        out_specs=pl.BlockSpec((None, 8, tn), lambda l, j: (l, 0, j)),
        compiler_params=_cparams(("arbitrary", "arbitrary")),
        name="mod_vectors",
    )(cond, w_mod, b_mod.reshape(L, 1, W))


def _is_ctx_row(blk, tm, n_lat):
    return blk * tm + lax.broadcasted_iota(jnp.int32, (tm, 1), 0) >= n_lat


def _norm_mod(x, g, mod_ref, blk, tm, n_lat, sh, sc):
    D = x.shape[-1]
    y = x * lax.rsqrt(jnp.mean(x * x, axis=-1, keepdims=True) + EPS) * g
    is_ctx = _is_ctx_row(blk, tm, n_lat)
    scale = jnp.where(is_ctx, mod_ref[0:1, sc * D:(sc + 1) * D], mod_ref[1:2, sc * D:(sc + 1) * D])
    shift = jnp.where(is_ctx, mod_ref[0:1, sh * D:(sh + 1) * D], mod_ref[1:2, sh * D:(sh + 1) * D])
    return y * (1.0 + scale) + shift


def _inproj_kernel(h_ref, mod_ref, g_ref, w_ref, p_ref, ab_ref, u_scr, *, tm, n_lat, ab_off):
    i = pl.program_id(1)
    j = pl.program_id(2)

    @pl.when(j == 0)
    def _():
        u_scr[...] = _norm_mod(h_ref[...], g_ref[...], mod_ref, i, tm, n_lat, 0, 1).astype(BF16)

    r = _dot(u_scr[...], w_ref[...])
    p_ref[...] = r.astype(BF16)

    @pl.when(j == pl.num_programs(2) - 1)
    def _():
        ab_ref[...] = r[:, ab_off:ab_off + LANES]


def _in_projection(h, modtab, g_norm, w_perm, n_lat, tm):
    B, T, D = h.shape
    ncol = IN_W_PAD // IN_TN
    kern = functools.partial(_inproj_kernel, tm=tm, n_lat=n_lat, ab_off=COL_AB - (ncol - 1) * IN_TN)
    return pl.pallas_call(
        kern,
        out_shape=(jax.ShapeDtypeStruct((B, T, IN_W_PAD), BF16),
                   jax.ShapeDtypeStruct((B, T, LANES), F32)),
        grid=(B, T // tm, ncol),
        in_specs=[pl.BlockSpec((None, tm, D), lambda b, i, j: (b, i, 0)),
                  pl.BlockSpec((None, 2, 6 * D), lambda b, i, j: (b, 0, 0)),
                  pl.BlockSpec((1, D), lambda b, i, j: (0, 0)),
                  pl.BlockSpec((D, IN_TN), lambda b, i, j: (0, j))],
        out_specs=(pl.BlockSpec((None, tm, IN_TN), lambda b, i, j: (b, i, j)),
                   pl.BlockSpec((None, tm, LANES), lambda b, i, j: (b, i, 0))),
        scratch_shapes=[pltpu.VMEM((tm, D), BF16)],
        compiler_params=_cparams(("arbitrary", "arbitrary", "arbitrary")),
        name="in_projection",
    )(h, modtab, g_norm.reshape(1, D), w_perm)


def _rope(x, cos, sin_a, sin_b):
    return x * cos + pltpu.roll(x, LANES - 32, 1) * sin_a + pltpu.roll(x, 32, 1) * sin_b


def _attn_prep_kernel(aq_ref, ak_ref, av_ref, cq_ref, ck_ref, cv_ref, cos_ref, sa_ref, sb_ref, gq_ref, gk_ref,
                      m_ref, qa_ref, ka_ref, vta_ref, qd_ref, kd_ref, vtd_ref):
    cos, sa, sb = cos_ref[...], sa_ref[...], sb_ref[...]
    vta_ref[...] = av_ref[...].astype(F32).T.astype(BF16)
    for h in range(DF_HEADS):
        vtd_ref[h] = cv_ref[:, h * LANES:(h + 1) * LANES].astype(F32).T.astype(BF16)
    m64 = m_ref[...]
    half = lax.broadcasted_iota(jnp.int32, cos.shape, 1) // HEAD_DIM
    scale = HEAD_DIM ** -0.5 * math.log2(math.e)

    def head_norm(x, g):
        ms = _dot3_l(x * x, m64) * (1.0 / HEAD_DIM)
        return x * lax.rsqrt(ms + EPS) * g

    for s in range(A_HEADS // 2):
        x = aq_ref[:, s * LANES:(s + 1) * LANES].astype(F32)
        y = _rope(head_norm(x, gq_ref[...]), cos, sa, sb) * scale
        y_sw = pltpu.roll(y, HEAD_DIM, 1)
        kvh = (2 * s) // A_GROUP
        for hh in range(2):
            src = y if hh == kvh else y_sw
            qa_ref[2 * s + hh] = jnp.where(half == kvh, src, 0.0).astype(BF16)
    xk = ak_ref[...].astype(F32)
    ka_ref[...] = _rope(head_norm(xk, gk_ref[...]), cos, sa, sb).astype(BF16)
    for h in range(DF_HEADS):
        x = cq_ref[:, h * LANES:(h + 1) * LANES].astype(F32)
        y = _rope(x, cos, sa, sb) * scale
        qd_ref[2 * h] = jnp.where(half == 0, y, 0.0).astype(BF16)
        qd_ref[2 * h + 1] = jnp.where(half == 1, y, 0.0).astype(BF16)
        xk = ck_ref[:, h * LANES:(h + 1) * LANES].astype(F32)
        kd_ref[:, h * LANES:(h + 1) * LANES] = _rope(xk, cos, sa, sb).astype(BF16)


def _attn_prep(P, tabs, g_q, g_k, tk):
    B, T, _ = P.shape
    tm = 256
    per = tk // tm
    cos2, sin_a, sin_b = tabs
    m64 = jnp.asarray(np.kron(np.eye(2), np.ones((HEAD_DIM, HEAD_DIM))), BF16)
    gq2 = jnp.tile(g_q, 2).reshape(1, LANES)
    gk2 = jnp.tile(g_k, 2).reshape(1, LANES)
    tab_spec = pl.BlockSpec((tm, LANES), lambda b, i: (i, 0))
    vec_spec = pl.BlockSpec((1, LANES), lambda b, i: (0, 0))
    return pl.pallas_call(
        _attn_prep_kernel,
        out_shape=(jax.ShapeDtypeStruct((B, A_HEADS, T, LANES), BF16),
                   jax.ShapeDtypeStruct((B, T, LANES), BF16),
                   jax.ShapeDtypeStruct((B, T // tk, LANES, tk), BF16),
                   jax.ShapeDtypeStruct((B, 2 * DF_HEADS, T, LANES), BF16),
                   jax.ShapeDtypeStruct((B, T, 512), BF16),
                   jax.ShapeDtypeStruct((B, DF_HEADS, T // tk, LANES, tk), BF16)),
        grid=(B, T // tm),
        in_specs=[pl.BlockSpec((None, tm, 512), lambda b, i: (b, i, COL_AQ // 512)),
                  pl.BlockSpec((None, tm, LANES), lambda b, i: (b, i, COL_AK // LANES)),
                  pl.BlockSpec((None, tm, LANES), lambda b, i: (b, i, COL_AV // LANES)),
                  pl.BlockSpec((None, tm, 512), lambda b, i: (b, i, COL_CQ // 512)),
                  pl.BlockSpec((None, tm, 512), lambda b, i: (b, i, COL_CK // 512)),
                  pl.BlockSpec((None, tm, 512), lambda b, i: (b, i, COL_CV // 512)),
                  tab_spec, tab_spec, tab_spec, vec_spec, vec_spec,
                  pl.BlockSpec((LANES, LANES), lambda b, i: (0, 0))],
        out_specs=(pl.BlockSpec((None, A_HEADS, tm, LANES), lambda b, i: (b, 0, i, 0)),
                   pl.BlockSpec((None, tm, LANES), lambda b, i: (b, i, 0)),
                   pl.BlockSpec((None, None, LANES, tm), lambda b, i: (b, i // per, 0, i % per)),
                   pl.BlockSpec((None, 2 * DF_HEADS, tm, LANES), lambda b, i: (b, 0, i, 0)),
                   pl.BlockSpec((None, tm, 512), lambda b, i: (b, i, 0)),
                   pl.BlockSpec((None, DF_HEADS, None, LANES, tm), lambda b, i: (b, 0, i // per, 0, i % per))),
        compiler_params=_cparams(("arbitrary", "arbitrary")),
        name="attn_prep",
    )(P, P, P, P, P, P, cos2, sin_a, sin_b, gq2, gk2, m64)


FLASH_TQ = 256
FLASH_PIECE = 256
FLASH_MAX_OVERSHOOT = 60.0
FLASH_GROUP = 2


def _flash_kernel(*refs, mode, cols, tk, n_ctx, n_chunks, ctx_only, lambda_init):
    refs = list(refs)
    q_ref, k_ref, vt_ref = refs[0:3]
    pos = 3
    if mode == "diff":
        lam_ref, g_ref = refs[pos], refs[pos + 1]
        pos += 2
    o_ref, m_scr, l_scr, over_scr, acc_scr = refs[pos:pos + 5]
    q = q_ref[...].reshape(cols, LANES)

    def reset():
        m_scr[...] = jnp.full((1, cols), -jnp.inf, F32)
        l_scr[...] = jnp.zeros((1, cols), F32)
        acc_scr[...] = jnp.zeros((LANES, cols), F32)

    def exact(k, vt):
        s = _dot_nt(k, q)
        m_old = m_scr[...]
        m_new = jnp.maximum(m_old, jnp.max(s, axis=0, keepdims=True))
        alpha = jnp.exp2(m_old - m_new)
        p = jnp.exp2(s - m_new)
        l_scr[...] = alpha * l_scr[...] + jnp.sum(p, axis=0, keepdims=True)
        acc_scr[...] = alpha * acc_scr[...] + _dot(vt, p.astype(BF16))
        m_scr[...] = m_new

    def lagged(k, vt):
        ref = m_scr[...]
        s = _dot_nt(k, q)
        p = jnp.exp2(s - ref)
        m_cur = jnp.max(s, axis=0, keepdims=True)
        m_new = jnp.maximum(ref, m_cur)
        alpha = jnp.exp2(ref - m_new)
        l_scr[...] = (l_scr[...] + jnp.sum(p, axis=0, keepdims=True)) * alpha
        acc_scr[...] = (acc_scr[...] + _dot(vt, p.astype(BF16))) * alpha
        over_scr[...] = jnp.maximum(over_scr[...], m_cur - ref)
        m_scr[...] = m_new

    def k_chunk(c):
        return k_ref[pl.ds(pl.multiple_of(c * tk, tk), tk), :]

    reset()
    if ctx_only:
        exact(k_ref[...], vt_ref[:, tk - n_ctx:])
    else:
        over_scr[...] = jnp.zeros((1, cols), F32)
        exact(k_ref[0:FLASH_PIECE, :], vt_ref[0, :, 0:FLASH_PIECE])
        if tk > FLASH_PIECE:
            lagged(k_ref[FLASH_PIECE:tk, :], vt_ref[0, :, FLASH_PIECE:tk])

        def lagged_group(c0):
            ref = m_scr[...]
            l, acc, over = l_scr[...], acc_scr[...], over_scr[...]
            pending = None
            for u in range(FLASH_GROUP + 1):
                if u < FLASH_GROUP:
                    s = _dot_nt(k_chunk(c0 + u), q)
                    m_cur = jnp.max(s, axis=0, keepdims=True)
                    over = jnp.maximum(over, m_cur - ref)
                    ref_next = jnp.maximum(ref, m_cur)
                if pending is not None:
                    s_p, ref_p, ref_after, c_p = pending
                    p = jnp.exp2(s_p - ref_p)
                    alpha = jnp.exp2(ref_p - ref_after)
                    l = (l + jnp.sum(p, axis=0, keepdims=True)) * alpha
                    acc = (acc + _dot(vt_ref[c_p], p.astype(BF16))) * alpha
                if u < FLASH_GROUP:
                    pending = (s, ref, ref_next, c0 + u)
                    ref = ref_next
            l_scr[...] = l
            acc_scr[...] = acc
            over_scr[...] = over
            m_scr[...] = ref

        def fast_body(it, carry):
            lagged_group(1 + FLASH_GROUP * it)
            return carry
        lax.fori_loop(0, (n_chunks - 1) // FLASH_GROUP, fast_body, 0)

        @pl.when(jnp.max(over_scr[...]) > FLASH_MAX_OVERSHOOT)
        def _():
            reset()

            def exact_body(c, carry):
                exact(k_chunk(c), vt_ref[c])
                return carry
            lax.fori_loop(0, n_chunks, exact_body, 0)

    if mode == "gqa":
        kvh = pl.program_id(1)
        o = acc_scr[pl.ds(pl.multiple_of(kvh * HEAD_DIM, HEAD_DIM), HEAD_DIM), :] / l_scr[...]
        tq = cols // A_GROUP
        pieces = [o[:, g * tq:(g + 1) * tq] for g in range(A_GROUP)]
        o_ref[...] = jnp.concatenate(pieces, axis=0).T.astype(o_ref.dtype)
    else:
        lq = lam_ref[...]
        lam = (jnp.exp(jnp.sum(lq[0:1] * lq[1:2], axis=-1, keepdims=True))
               - jnp.exp(jnp.sum(lq[2:3] * lq[3:4], axis=-1, keepdims=True)) + lambda_init)
        o = acc_scr[...] / l_scr[...]
        d = o[:, :cols // 2] - lam * o[:, cols // 2:]
        y = d * lax.rsqrt(jnp.mean(d * d, axis=0, keepdims=True) + EPS) * g_ref[...]
        o_ref[...] = (y * (1.0 - lambda_init)).T.astype(o_ref.dtype)


def _flash(mode, q, k, vt, n_ctx, ctx_only, extra=(), lambda_init=0.0):
    B, NQ, T, _ = q.shape
    S = T - n_ctx
    tk = vt.shape[-1]
    n_chunks = vt.shape[-3]
    assert (n_chunks - 1) % FLASH_GROUP == 0
    n_maps = A_GROUP if mode == "gqa" else 2
    n_heads = NQ // n_maps
    if ctx_only:
        tq, n_q, q0, rows = n_ctx, 1, S // n_ctx, n_ctx
        k_rows, k0 = n_ctx, S // n_ctx
        vt_blk, vt0 = None, n_chunks - 1
    else:
        tq = FLASH_TQ if mode == "gqa" else 2 * FLASH_TQ
        n_q, q0, rows = S // tq, 0, S
        k_rows, k0 = T, 0
        vt_blk, vt0 = n_chunks, 0
    cols = n_maps * tq
    kern = functools.partial(_flash_kernel, mode=mode, cols=cols, tk=tk, n_ctx=n_ctx,
                             n_chunks=n_chunks, ctx_only=ctx_only, lambda_init=lambda_init)
    in_specs = [pl.BlockSpec((None, n_maps, tq, LANES), lambda b, h, i: (b, h, i + q0, 0))]
    if mode == "gqa":
        in_specs += [pl.BlockSpec((None, k_rows, LANES), lambda b, h, i: (b, k0, 0)),
                     pl.BlockSpec((None, vt_blk, LANES, tk), lambda b, h, i: (b, vt0, 0, 0))]
        out_spec = pl.BlockSpec((None, tq, A_GROUP * HEAD_DIM), lambda b, h, i: (b, i, h))
    else:
        in_specs += [pl.BlockSpec((None, k_rows, LANES), lambda b, h, i: (b, k0, h)),
                     pl.BlockSpec((None, None, vt_blk, LANES, tk), lambda b, h, i: (b, h, vt0, 0, 0)),
                     pl.BlockSpec((4, HEAD_DIM), lambda b, h, i: (0, 0)),
                     pl.BlockSpec((LANES, 1), lambda b, h, i: (0, 0))]
        out_spec = pl.BlockSpec((None, tq, LANES), lambda b, h, i: (b, i, h))
    return pl.pallas_call(
        kern,
        out_shape=jax.ShapeDtypeStruct((B, rows, BR_W), BF16),
        grid=(B, n_heads, n_q),
        in_specs=in_specs,
        out_specs=out_spec,
        scratch_shapes=[pltpu.VMEM((1, cols), F32)] * 3 + [pltpu.VMEM((LANES, cols), F32)],
        compiler_params=_cparams(("arbitrary", "arbitrary", "arbitrary")),
        name="flash_" + mode + ("_ctx" if ctx_only else ""),
    )(q, k, vt, *extra)


def _dn_prep_kernel(q_ref, k_ref, v_ref, ql_ref, kl_ref, vl_ref, qr_ref, kr_ref, vr_ref, ab_ref,
                    wc_ref, alog_ref, dtb_ref, mh_ref, eg_ref, eb_ref, tm_ref, ones_ref,
                    qo_ref, ko_ref, kb_ref, vbk_ref, qdec_ref, kdecbd_ref, decbd_ref, egl_ref,
                    *, tm):
    i = pl.program_id(1)
    nblk = pl.num_programs(1)
    W = DN_HEADS * DN_DK
    row = lax.broadcasted_iota(jnp.int32, (tm, W), 0)
    col = lax.broadcasted_iota(jnp.int32, (tm, W), 1)
    left_ok = jnp.logical_and(i >= 1, i < nblk - 1).astype(F32)
    right_ok = (i < nblk - 2).astype(F32)

    def conv_silu(x_ref, xl_ref, xr_ref, c0):
        x = x_ref[...].astype(F32)
        prev_row = xl_ref[7:8, :].astype(F32) * left_ok
        next_row = xr_ref[0:1, :].astype(F32) * right_ok
        x_prev = jnp.where(row == 0, prev_row, pltpu.roll(x, 1, 0))
        x_next = jnp.where(row == tm - 1, next_row, pltpu.roll(x, tm - 1, 0))
        y = (x_prev * wc_ref[0:1, c0:c0 + W] + x * wc_ref[1:2, c0:c0 + W]
             + x_next * wc_ref[2:3, c0:c0 + W])
        return _silu(y)

    def l2norm(x):
        ss = _dot3_l(x * x, mh_ref[...])
        return x * lax.rsqrt(ss + EPS)

    q = l2norm(conv_silu(q_ref, ql_ref, qr_ref, 0)) * (DN_DK ** -0.5)
    k = l2norm(conv_silu(k_ref, kl_ref, kr_ref, W))
    v = conv_silu(v_ref, vl_ref, vr_ref, 2 * W)

    ab = ab_ref[...]
    z = ab + dtb_ref[...]
    softplus = jnp.maximum(z, 0.0) + jnp.log(1.0 + jnp.exp(-jnp.abs(z)))
    g_all = -jnp.exp(alog_ref[...]) * softplus
    beta_all = jax.nn.sigmoid(ab)

    t_in = row & (DN_CHUNK - 1)
    s_in = col & (DN_CHUNK - 1)

    def put(ref, d, val):
        for h in range(DN_HEADS):
            piece = val[:, h * DN_DK:(h + 1) * DN_DK]
            if d is None:
                ref[h] = piece.astype(ref.dtype)
            else:
                ref[d, h] = piece.astype(ref.dtype)

    lo = lax.broadcasted_iota(jnp.int32, (tm, LANES), 1) < DN_DK
    n_chunks = tm // DN_CHUNK
    bd_shape = (tm, n_chunks * DN_CHUNK)
    same_chunk = (lax.broadcasted_iota(jnp.int32, bd_shape, 0) // DN_CHUNK
                  == lax.broadcasted_iota(jnp.int32, bd_shape, 1) // DN_CHUNK)

    def pair(a, b, h):
        sa = a[:, (h // 2) * LANES:(h // 2 + 1) * LANES]
        sb = b[:, (h // 2) * LANES:(h // 2 + 1) * LANES]
        if h % 2 == 0:
            return jnp.where(lo, sa, pltpu.roll(sb, DN_DK, 1))
        return jnp.where(lo, pltpu.roll(sa, DN_DK, 1), sb)

    def put_block_diag(ref, d, val):
        for h in range(DN_HEADS):
            x = pair(val, val, h)
            ref[d, h] = jnp.where(same_chunk, jnp.concatenate([x] * (n_chunks // 2), axis=1), 0.0).astype(ref.dtype)

    put(qo_ref, None, q)
    put(ko_ref, None, k)
    for d in range(2):
        g_e = _dot3_l(g_all, eg_ref[d])
        beta_e = _dot3_l(beta_all, eb_ref[d])
        gc = _dot3_r(tm_ref[d], g_e)
        g_tot = _dot3_r(ones_ref[...], g_e)
        strict = (t_in > s_in) if d == 0 else (t_in < s_in)
        causal = (t_in >= s_in) if d == 0 else (t_in <= s_in)
        diff = _dot3_r(tm_ref[d], jnp.where(strict, g_e, 0.0))
        decay = jnp.where(causal, jnp.exp(diff), 0.0)
        e_g = jnp.exp(gc)
        kb = k * beta_e
        put(kb_ref, d, kb)
        vb, kbg = v * beta_e, kb * e_g
        for h in range(DN_HEADS):
            vbk_ref[d, h] = pair(vb, kbg, h).astype(vbk_ref.dtype)
        put(qdec_ref, d, q * e_g)
        put_block_diag(kdecbd_ref, d, k * jnp.exp(g_tot - gc))
        put_block_diag(decbd_ref, d, decay)
        e_tot = jnp.exp(g_tot)
        for h in range(DN_HEADS):
            for c in range(tm // DN_CHUNK):
                egl_ref[d, h, c:c + 1, :] = e_tot[c * DN_CHUNK:c * DN_CHUNK + 1, h * DN_DK:(h + 1) * DN_DK]


def _dn_consts():
    n = DN_BLOCK
    t = np.arange(n)
    same = (t[:, None] // DN_CHUNK) == (t[None, :] // DN_CHUNK)
    pre = same & (t[None, :] <= t[:, None])
    suf = same & (t[None, :] >= t[:, None])
    tmat = np.stack([pre, suf]).astype(np.float32)
    ones = same.astype(np.float32)
    W = DN_HEADS * DN_DK
    eg = np.zeros((2, LANES, W), np.float32)
    eb = np.zeros((2, LANES, W), np.float32)
    for d in range(2):
        for h in range(DN_HEADS):
            eg[d, d * DN_HEADS + h, h * DN_DK:(h + 1) * DN_DK] = 1.0
            eb[d, 2 * DN_HEADS + d * DN_HEADS + h, h * DN_DK:(h + 1) * DN_DK] = 1.0
    mh = np.kron(np.eye(DN_HEADS), np.ones((DN_DK, DN_DK))).astype(np.float32)
    return tuple(jnp.asarray(a, BF16) for a in (mh, eg, eb, tmat, ones))


def _dn_prep(P, AB, w_conv, a_log, dt_bias):
    B, T, _ = P.shape
    tm = DN_BLOCK
    nblk = T // tm
    W = DN_HEADS * DN_DK
    mh, eg, eb, tmat, ones = _dn_consts()
    alog_row = jnp.zeros((1, LANES), F32).at[0, :2 * DN_HEADS].set(a_log.reshape(-1))
    dtb_row = jnp.zeros((1, LANES), F32).at[0, :2 * DN_HEADS].set(dt_bias.reshape(-1))
    r8 = tm // 8

    def main(c0):
        return pl.BlockSpec((None, tm, W), lambda b, i: (b, i, c0 // W))

    def left(c0):
        return pl.BlockSpec((None, 8, W), lambda b, i: (b, jnp.maximum(i * r8 - 1, 0), c0 // W))

    def right(c0):
        return pl.BlockSpec((None, 8, W), lambda b, i: (b, jnp.minimum((i + 1) * r8, nblk * r8 - 1), c0 // W))

    def full(a):
        nd = a.ndim
        return pl.BlockSpec(a.shape, lambda b, i: (0,) * nd)

    hm = pl.BlockSpec((None, DN_HEADS, tm, DN_DK), lambda b, i: (b, 0, i, 0))

    def hm2(width):
        return pl.BlockSpec((2, None, DN_HEADS, tm, width), lambda b, i: (0, b, 0, i, 0))

    def sh_hm2(width):
        return jax.ShapeDtypeStruct((2, B, DN_HEADS, T, width), BF16)

    sh_hm = jax.ShapeDtypeStruct((B, DN_HEADS, T, DN_DK), BF16)
    return pl.pallas_call(
        functools.partial(_dn_prep_kernel, tm=tm),
        out_shape=(sh_hm, sh_hm, sh_hm2(DN_DK), sh_hm2(2 * DN_DK), sh_hm2(DN_DK), sh_hm2(tm), sh_hm2(tm),
                   jax.ShapeDtypeStruct((2, B, nblk, DN_HEADS, tm // DN_CHUNK, DN_DK), F32)),
        grid=(B, nblk),
        in_specs=[main(COL_BQ), main(COL_BK), main(COL_BV),
                  left(COL_BQ), left(COL_BK), left(COL_BV),
                  right(COL_BQ), right(COL_BK), right(COL_BV),
                  pl.BlockSpec((None, tm, LANES), lambda b, i: (b, i, 0)),
                  full(w_conv), full(alog_row), full(dtb_row), full(mh), full(eg), full(eb), full(tmat),
                  full(ones)],
        out_specs=(hm, hm, hm2(DN_DK), hm2(2 * DN_DK), hm2(DN_DK), hm2(tm), hm2(tm),
                   pl.BlockSpec((2, None, None, DN_HEADS, tm // DN_CHUNK, DN_DK),
                                lambda b, i: (0, b, i, 0, 0, 0))),
        compiler_params=_cparams(("arbitrary", "arbitrary")),
        name="dn_prep",
    )(P, P, P, P, P, P, P, P, P, AB, w_conv, alog_row, dtb_row, mh, eg, eb, tmat, ones)


def _dn_chunk_kernel(q_ref, k_ref, kb_ref, vbk_ref, qdec_ref, kdecbd_ref, decbd_ref,
                     qeff_ref, o0_ref, mk_ref, n_ref):
    n, C = DN_BLOCK, DN_CHUNK
    nc = n // C
    r = lax.broadcasted_iota(jnp.int32, (n, n), 0)
    c = lax.broadcasted_iota(jnp.int32, (n, n), 1)
    off_diag = r != c
    same_chunk = (r // C) == (c // C)
    rp = lax.broadcasted_iota(jnp.int32, (C, n), 0)
    cp = lax.broadcasted_iota(jnp.int32, (C, n), 1)
    eye_packed = (rp == cp % C).astype(F32)
    own_values = (lax.broadcasted_iota(jnp.int32, (n, 2 * nc * C), 0) // C
                  == lax.broadcasted_iota(jnp.int32, (n, 2 * nc * C), 1) // (2 * C))
    heads = range(DN_HEADS)

    def block_diag(xp):
        xb = xp.astype(BF16)
        return jnp.where(same_chunk, jnp.concatenate([xb] * nc, axis=0), jnp.zeros((), BF16))

    low, qk = [], []
    for h in heads:
        dec = decbd_ref[h].astype(F32)
        k = k_ref[h]
        a = jnp.where(off_diag, _dot_nt(kb_ref[h], k) * dec, 0.0)
        low.append(sum(a[i * C:(i + 1) * C] for i in range(nc)))
        qk.append((_dot_nt(q_ref[h], k) * dec).astype(BF16))
    p = [eye_packed - m for m in low]
    mpow = low
    for _ in range(5):
        mpow = [_dot(m.astype(BF16), block_diag(m)) for m in mpow]
        p = [a + _dot(a.astype(BF16), block_diag(m)) for a, m in zip(p, mpow)]
    uw = []
    for h in heads:
        vals = jnp.where(own_values, jnp.concatenate([vbk_ref[h]] * nc, axis=1), jnp.zeros((), BF16))
        packed = _dot(p[h].astype(BF16), vals)
        uw.append(jnp.concatenate([packed[:, i * 2 * C:(i + 1) * 2 * C] for i in range(nc)],
                                  axis=0).astype(BF16))
    qkuw = [_dot(qk[h], uw[h]) for h in heads]
    for h in heads:
        o0_ref[h] = qkuw[h][:, :DN_DK]
        qeff_ref[h] = (qdec_ref[h].astype(F32) - qkuw[h][:, DN_DK:]).astype(qeff_ref.dtype)
        kt = _dot_tn(kdecbd_ref[h], uw[h])
        n_ref[h] = kt[:, :DN_DK]
        mk_ref[h] = kt[:, DN_DK:].astype(mk_ref.dtype)


def _dn_chunk(prep):
    q, k, kb, vbk, qdec, kdecbd, decbd, _ = prep
    B, H, T, dk = q.shape
    tm = DN_BLOCK
    hm = pl.BlockSpec((None, H, tm, dk), lambda d, b, i: (b, 0, i, 0))

    def hm2(width):
        return pl.BlockSpec((None, None, H, tm, width), lambda d, b, i: (d, b, 0, i, 0))

    def sh(dtype):
        return jax.ShapeDtypeStruct((2, B, H, T, dk), dtype)

    return pl.pallas_call(
        _dn_chunk_kernel,
        out_shape=(sh(BF16), sh(F32), sh(BF16), sh(F32)),
        grid=(2, B, T // tm),
        in_specs=[hm, hm, hm2(dk), hm2(2 * dk), hm2(dk), hm2(tm), hm2(tm)],
        out_specs=(hm2(dk), hm2(dk), hm2(dk), hm2(dk)),
        compiler_params=_cparams(("arbitrary", "arbitrary", "arbitrary")),
        name="dn_chunk",
    )(q, k, kb, vbk, qdec, kdecbd, decbd)


def _dn_scan_kernel(*refs, n_batch):
    ins = (refs[0:5], refs[5:10])
    o_refs = refs[10:12]
    s_scr = refs[12]
    t = pl.program_id(0)
    n_chunks = DN_BLOCK // DN_CHUNK
    C = DN_CHUNK

    @pl.when(t == 0)
    def _():
        s_scr[...] = jnp.zeros_like(s_scr)

    for step in range(n_chunks):
        for d in range(2):
            ci = step if d == 0 else n_chunks - 1 - step
            rows = slice(ci * C, (ci + 1) * C)
            qeff_ref, o0_ref, mk_ref, n_ref, egl_ref = ins[d]
            for b in range(n_batch):
                for h in range(DN_HEADS):
                    s = s_scr[d, b, h]
                    sb = s.astype(BF16)
                    o_refs[d][b, h, rows, :] = _dot(qeff_ref[b, h, rows, :], sb) + o0_ref[b, h, rows, :]
                    s_scr[d, b, h] = (s * egl_ref[b, h, ci:ci + 1, :] - _dot(mk_ref[b, h, rows, :], sb)
                                      + n_ref[b, h, rows, :])


def _dn_scan(maps, egl):
    qeff, o0, mk, nn = maps
    _, B, H, T, dk = qeff.shape
    tm = DN_BLOCK
    nblk = T // tm

    def blk(d, t):
        return jnp.where(t == 0, nblk - 1, nblk - 1 - t if d == 1 else t - 1)

    in_specs, args = [], []
    for d in range(2):
        for a in (qeff, o0, mk, nn):
            in_specs.append(pl.BlockSpec((None, B, H, tm, dk), lambda t, d=d: (d, 0, 0, blk(d, t), 0)))
            args.append(a)
        in_specs.append(pl.BlockSpec((None, B, None, H, tm // DN_CHUNK, dk),
                                     lambda t, d=d: (d, 0, blk(d, t), 0, 0, 0)))
        args.append(egl)
    out_specs = tuple(pl.BlockSpec((B, H, tm, dk), lambda t, d=d: (0, 0, blk(d, t), 0)) for d in range(2))
    return pl.pallas_call(
        functools.partial(_dn_scan_kernel, n_batch=B),
        out_shape=(jax.ShapeDtypeStruct((B, H, T, dk), F32),) * 2,
        grid=(nblk,),
        in_specs=in_specs,
        out_specs=out_specs,
        scratch_shapes=[pltpu.VMEM((2, B, H, dk, dk), F32)],
        compiler_params=_cparams(("arbitrary",)),
        name="dn_scan",
    )(*args)


def _dn_post_kernel(of_ref, ob_ref, z_ref, g_ref, o_ref):
    z = z_ref[...].astype(F32)
    for h in range(DN_HEADS):
        o = of_ref[h] + ob_ref[h]
        y = o * lax.rsqrt(jnp.mean(o * o, axis=-1, keepdims=True) + EPS) * g_ref[...]
        zz = z[:, h * DN_DK:(h + 1) * DN_DK]
        o_ref[:, h * DN_DK:(h + 1) * DN_DK] = (y * _silu(zz)).astype(o_ref.dtype)


def _dn_post(o_f, o_b, P, g_out):
    B, H, T, dk = o_f.shape
    tm = DN_BLOCK
    hm = pl.BlockSpec((None, H, tm, dk), lambda b, i: (b, 0, i, 0))
    return pl.pallas_call(
        _dn_post_kernel,
        out_shape=jax.ShapeDtypeStruct((B, T, BR_W), F32),
        grid=(B, T // tm),
        in_specs=[hm, hm,
                  pl.BlockSpec((None, tm, BR_W), lambda b, i: (b, i, COL_BZ // BR_W)),
                  pl.BlockSpec((1, dk), lambda b, i: (0, 0))],
        out_specs=pl.BlockSpec((None, tm, BR_W), lambda b, i: (b, i, 0)),
        compiler_params=_cparams(("arbitrary", "arbitrary")),
        name="dn_post",
    )(o_f, o_b, P, g_out.reshape(1, dk))


def _merge_kernel(a_ref, b_ref, c_ref, ga_ref, gb_ref, gc_ref, h_ref, mod_ref, g2_ref, wb_ref, wo_ref,
                  h_out_ref, v_out_ref, *, tm, n_lat):
    i = pl.program_id(1)
    D = h_ref.shape[-1]
    m = jax.nn.sigmoid(ga_ref[...].astype(F32)) * _dot(a_ref[...], wb_ref[0])
    m = m + jax.nn.sigmoid(gb_ref[...].astype(F32)) * _dot(b_ref[...].astype(BF16), wb_ref[1])
    m = m + jax.nn.sigmoid(gc_ref[...].astype(F32)) * _dot(c_ref[...], wb_ref[2])
    y = _dot(m.astype(BF16), wo_ref[...])
    gate = jnp.where(_is_ctx_row(i, tm, n_lat), mod_ref[0:1, 2 * D:3 * D], mod_ref[1:2, 2 * D:3 * D])
    h_new = h_ref[...] + gate * y
    h_out_ref[...] = h_new
    v_out_ref[...] = _norm_mod(h_new, g2_ref[...], mod_ref, i, tm, n_lat, 3, 4).astype(BF16)


def _merge(br_a, br_b, br_c, P, h, modtab, g_norm2, w_branch, w_out, n_lat, tm):
    B, T, D = h.shape
    br = pl.BlockSpec((None, tm, BR_W), lambda b, i: (b, i, 0))

    def gate(j):
        return pl.BlockSpec((None, tm, D), lambda b, i: (b, i, COL_GATE // D + j))

    tok = pl.BlockSpec((None, tm, D), lambda b, i: (b, i, 0))
    return pl.pallas_call(
        functools.partial(_merge_kernel, tm=tm, n_lat=n_lat),
        out_shape=(jax.ShapeDtypeStruct((B, T, D), F32), jax.ShapeDtypeStruct((B, T, D), BF16)),
        grid=(B, T // tm),
        in_specs=[br, br, br, gate(0), gate(1), gate(2), tok,
                  pl.BlockSpec((None, 2, 6 * D), lambda b, i: (b, 0, 0)),
                  pl.BlockSpec((1, D), lambda b, i: (0, 0)),
                  pl.BlockSpec((3, BR_W, D), lambda b, i: (0, 0, 0)),
                  pl.BlockSpec((D, D), lambda b, i: (0, 0))],
        out_specs=(tok, tok),
        compiler_params=_cparams(("arbitrary", "arbitrary")),
        name="merge",
    )(br_a, br_b, br_c, P, P, P, h, modtab, g_norm2.reshape(1, D), w_branch, w_out)


def _router_kernel(v_ref, wr_ref, br_ref, perm_ref, c_ref):
    tn = v_ref.shape[0]
    G = N_GROUPS
    scores = jax.nn.sigmoid(_dot_nt(wr_ref[...], v_ref[...]))
    sel = scores + br_ref[...]
    slabs = [sel[j * G:(j + 1) * G] for j in range(GROUP_SIZE)]
    m1, m2 = slabs[0], jnp.full((G, tn), -jnp.inf, F32)
    for j in range(1, GROUP_SIZE):
        m2 = jnp.maximum(m2, jnp.minimum(m1, slabs[j]))
        m1 = jnp.maximum(m1, slabs[j])
    gs = m1 + m2
    gidx = lax.broadcasted_iota(jnp.int32, (G, tn), 0)
    grank = jnp.zeros((G, tn), jnp.int32)
    for g in range(G):
        rowv = gs[g:g + 1]
        beats = jnp.where(rowv > gs, 1, jnp.where(jnp.logical_and(rowv == gs, g < gidx), 1, 0))
        grank = grank + beats
    gmask = grank < TOPK_GROUPS
    masked = jnp.concatenate([jnp.where(gmask, s, -jnp.inf) for s in slabs], axis=0)
    pos = lax.broadcasted_iota(jnp.int32, (N_EXPERTS, tn), 0)
    orig = (pos % G) * GROUP_SIZE + pos // G
    rank = jnp.zeros((N_EXPERTS, tn), jnp.int32)
    for p in range(N_EXPERTS):
        o_p = (p % G) * GROUP_SIZE + p // G
        rowv = masked[p:p + 1]
        beats = jnp.where(rowv > masked, 1, jnp.where(jnp.logical_and(rowv == masked, o_p < orig), 1, 0))
        rank = rank + beats
    chosen = rank < TOP_K
    w = jnp.where(chosen, scores, 0.0)
    denom = jnp.sum(w, axis=0, keepdims=True) + 1e-20
    c = w / denom * ROUTED_SCALE
    c_t = sum(_dot_tn(part, perm_ref[...]) for part in _split3(c))
    for g in range(N_EXPERTS // MOE_EPS):
        c_ref[g] = c_t[:, g * MOE_EPS:(g + 1) * MOE_EPS]


def _router(v_flat, w_router, b_router):
    N, D = v_flat.shape
    tn = 1280 if N % 1280 == 0 else 256
    perm = np.array([(p % N_GROUPS) * GROUP_SIZE + p // N_GROUPS for p in range(N_EXPERTS)])
    wr_t = w_router.T[perm].astype(BF16)
    br = b_router[perm].reshape(N_EXPERTS, 1)
    unperm = jnp.asarray(np.eye(N_EXPERTS)[perm], BF16)
    ngrp = N_EXPERTS // MOE_EPS
    return pl.pallas_call(
        _router_kernel,
        out_shape=jax.ShapeDtypeStruct((ngrp, N, MOE_EPS), F32),
        grid=(N // tn,),
        in_specs=[pl.BlockSpec((tn, D), lambda i: (i, 0)),
                  pl.BlockSpec((N_EXPERTS, D), lambda i: (0, 0)),
                  pl.BlockSpec((N_EXPERTS, 1), lambda i: (0, 0)),
                  pl.BlockSpec((N_EXPERTS, N_EXPERTS), lambda i: (0, 0))],
        out_specs=pl.BlockSpec((ngrp, tn, MOE_EPS), lambda i: (0, i, 0)),
        compiler_params=_cparams(("arbitrary",)),
        name="router",
    )(v_flat, wr_t, br, unperm)


MOE_EPS = 4


def _moe_kernel(x_ref, c_ref, wgu_ref, wd_ref, sgu_ref, sd_ref, h_ref, mod_ref,
                o_ref, acc_ref, *, tm, n_lat):
    i = pl.program_id(1)
    e = pl.program_id(2)
    x = x_ref[...]
    D = x.shape[-1]
    F = MOE_FF

    @pl.when(e == 0)
    def _():
        gu = _dot(x, sgu_ref[...])
        hs = _silu(gu[:, :F]) * gu[:, F:]
        acc_ref[...] = _dot(hs.astype(BF16), sd_ref[...])

    c = c_ref[...]
    gu = _dot(x, wgu_ref[...])
    parts = []
    for j in range(MOE_EPS):
        hj = _silu(gu[:, 2 * j * F:(2 * j + 1) * F]) * gu[:, (2 * j + 1) * F:(2 * j + 2) * F] * c[:, j:j + 1]
        parts.append(hj.astype(BF16))
    hcat = jnp.concatenate(parts, axis=-1)
    acc_ref[...] += _dot(hcat, wd_ref[...])

    @pl.when(e == pl.num_programs(2) - 1)
    def _():
        gate = jnp.where(_is_ctx_row(i, tm, n_lat), mod_ref[0:1, 5 * D:6 * D], mod_ref[1:2, 5 * D:6 * D])
        o_ref[...] = h_ref[...] + gate * acc_ref[...]


def _moe_weights(w_gate, w_up, w_down):
    E, D, F = w_gate.shape
    gu = jnp.concatenate([w_gate, w_up], axis=-1).astype(BF16)
    gu = gu.reshape(E // MOE_EPS, MOE_EPS, D, 2 * F).transpose(0, 2, 1, 3).reshape(E // MOE_EPS, D, MOE_EPS * 2 * F)
    return gu, w_down.astype(BF16).reshape(E * F, D)


def _moe(v, c_grp, h, modtab, wgu, wd, sgu, sd, n_lat, tm):
    B, T, D = v.shape
    F = MOE_FF
    ngrp = N_EXPERTS // MOE_EPS
    tok = pl.BlockSpec((None, tm, D), lambda b, i, e: (b, i, 0))
    return pl.pallas_call(
        functools.partial(_moe_kernel, tm=tm, n_lat=n_lat),
        out_shape=jax.ShapeDtypeStruct((B, T, D), F32),
        grid=(B, T // tm, ngrp),
        in_specs=[tok,
                  pl.BlockSpec((None, None, tm, MOE_EPS), lambda b, i, e: (e, b, i, 0)),
                  pl.BlockSpec((None, D, MOE_EPS * 2 * F), lambda b, i, e: (e, 0, 0)),
                  pl.BlockSpec((MOE_EPS * F, D), lambda b, i, e: (e, 0)),
                  pl.BlockSpec((D, 2 * F), lambda b, i, e: (0, 0)),
                  pl.BlockSpec((F, D), lambda b, i, e: (0, 0)),
                  tok,
                  pl.BlockSpec((None, 2, 6 * D), lambda b, i, e: (b, 0, 0))],
        out_specs=tok,
        scratch_shapes=[pltpu.VMEM((tm, D), F32)],
        compiler_params=_cparams(("arbitrary", "arbitrary", "arbitrary")),
        name="moe",
    )(v, c_grp, wgu, wd, sgu, sd, h, modtab)


def _final_kernel(h_ref, g_ref, o_ref):
    x = h_ref[...]
    o_ref[...] = x * lax.rsqrt(jnp.mean(x * x, axis=-1, keepdims=True) + EPS) * g_ref[...]


def _final_norm(h, g_final, S):
    B, T, D = h.shape
    tm = 1024
    return pl.pallas_call(
        _final_kernel,
        out_shape=jax.ShapeDtypeStruct((B, S, D), F32),
        grid=(B, S // tm),
        in_specs=[pl.BlockSpec((None, tm, D), lambda b, i: (b, i, 0)),
                  pl.BlockSpec((1, D), lambda b, i: (0, 0))],
        out_specs=pl.BlockSpec((None, tm, D), lambda b, i: (b, i, 0)),
        compiler_params=_cparams(("arbitrary", "arbitrary")),
        name="final_norm",
    )(h, g_final.reshape(1, D))


def _rope_tables(S, n_ctx):
    rows = S // GRID_W
    row = jnp.repeat(jnp.arange(rows, dtype=F32), GRID_W)
    col = (jnp.arange(S) % GRID_W).astype(F32)
    axis_dim = HEAD_DIM // 2
    inv = 1.0 / (ROPE_THETA ** (jnp.arange(0, axis_dim, 2, dtype=F32) / axis_dim))
    ang = jnp.concatenate([row[:, None] * inv, col[:, None] * inv], axis=-1)
    ang = jnp.concatenate([ang, ang], axis=-1)
    cos = jnp.concatenate([jnp.cos(ang), jnp.ones((n_ctx, HEAD_DIM), F32)], axis=0)
    sin = jnp.concatenate([jnp.sin(ang), jnp.zeros((n_ctx, HEAD_DIM), F32)], axis=0)
    first = (jnp.arange(HEAD_DIM) < HEAD_DIM // 2)[None, :]
    sin_a = jnp.where(first, -sin, 0.0)
    sin_b = jnp.where(first, 0.0, sin)
    return tuple(jnp.tile(t, (1, 2)) for t in (cos, sin_a, sin_b))


def _permute_w_in(w):
    splits = np.cumsum([512, 128, 128, 1536, 512, 16, 16, 512, 512, 512, 3072])[:-1].tolist()
    aq, ak, av, bqkv, bz, ba, bb, cq, ck, cv, gate = jnp.split(w, splits, axis=-1)
    pad = jnp.zeros((w.shape[0], IN_W_PAD - COL_AB - 32), w.dtype)
    out = jnp.concatenate([aq, cq, ck, cv, bz, bqkv, gate, ak, av, ba, bb, pad], axis=-1)
    return out.astype(BF16)


def kernel(x, c, ctx, c_ctx, w_mod, b_mod, g_norm1, g_norm2, w_in, g_qnorm, g_knorm, w_conv, a_log, dt_bias, g_dn_out, lam_qk, g_subln, w_branch, w_out, w_router, b_router, w_e_gate, w_e_up, w_e_down, w_s_gate, w_s_up, w_s_down, g_final):
    B, S, D = x.shape
    n_ctx = ctx.shape[1]
    T = n_ctx + S
    L = w_mod.shape[0]
    N = B * T
    assert n_ctx == DN_BLOCK and S % 1024 == 0 and D == D_MODEL
    tm_tok = 1280 if T % 1280 == 0 else 256
    tm_merge = 640 if T % 640 == 0 else 256

    cond = jnp.zeros((8, D), F32).at[0].set(c_ctx).at[1:1 + B].set(c)
    mod = _mod_vectors(cond, w_mod, b_mod)
    tabs = _rope_tables(S, n_ctx)
    h = jnp.concatenate([x, ctx], axis=1)
    tk = 1280 if T % 1280 == 0 else 256

    for l in range(L):
        lambda_init = 0.8 - 0.6 * math.exp(-0.3 * l)
        modtab = jnp.stack([jnp.broadcast_to(mod[l, 0], (B, 6 * D)), mod[l, 1:1 + B]], axis=1)
        P, AB = _in_projection(h, modtab, g_norm1[l], _permute_w_in(w_in[l]), S, tm_tok)

        qa, ka, vta, qd, kd, vtd = _attn_prep(P, tabs, g_qnorm[l], g_knorm[l], tk)
        dx = (lam_qk[l], g_subln[l].reshape(LANES, 1))
        br_a = jnp.concatenate([_flash("gqa", qa, ka, vta, n_ctx, False),
                                _flash("gqa", qa, ka, vta, n_ctx, True)], axis=1)
        br_c = jnp.concatenate([_flash("diff", qd, kd, vtd, n_ctx, False, dx, lambda_init),
                                _flash("diff", qd, kd, vtd, n_ctx, True, dx, lambda_init)], axis=1)

        prep = _dn_prep(P, AB, w_conv[l], a_log[l], dt_bias[l])
        o_f, o_b = _dn_scan(_dn_chunk(prep), prep[-1])
        br_b = _dn_post(o_f, o_b, P, g_dn_out[l])

        h, v = _merge(br_a, br_b, br_c, P, h, modtab, g_norm2[l], w_branch[l].astype(BF16),
                      w_out[l].astype(BF16), S, tm_merge)

        c_grp = _router(v.reshape(N, D), w_router[l], b_router[l]).reshape(-1, B, T, MOE_EPS)
        wgu, wd = _moe_weights(w_e_gate[l], w_e_up[l], w_e_down[l])
        sgu = jnp.concatenate([w_s_gate[l], w_s_up[l]], axis=-1).astype(BF16)
        h = _moe(v, c_grp, h, modtab, wgu, wd, sgu, w_s_down[l].astype(BF16), S, tm_merge)

    return _final_norm(h, g_final, S)
```

```python
import functools
import math

import numpy as np
import jax
import jax.numpy as jnp
from jax import lax
from jax.experimental import pallas as pl
from jax.experimental.pallas import tpu as pltpu

F32 = jnp.float32
BF16 = jnp.bfloat16

D_MODEL = 1024
GRID_W = 64
EPS = 1e-6
ROPE_THETA = 10000.0
HEAD_DIM = 64
LANES = 128
A_HEADS = 8
A_KV_HEADS = 2
A_GROUP = A_HEADS // A_KV_HEADS
DN_HEADS = 8
DN_DK = 64
DN_CHUNK = 64
DN_BLOCK = 256
DF_HEADS = 4
N_EXPERTS = 64
TOP_K = 6
N_GROUPS = 8
TOPK_GROUPS = 4
GROUP_SIZE = N_EXPERTS // N_GROUPS
MOE_FF = 256
ROUTED_SCALE = 2.5
BR_W = 512

COL_AQ, COL_CQ, COL_CK, COL_CV, COL_BZ = 0, 512, 1024, 1536, 2048
COL_BQ, COL_BK, COL_BV = 2560, 3072, 3584
COL_GATE = 4096
COL_AK, COL_AV, COL_AB = 7168, 7296, 7424
IN_W_PAD = 7680
IN_TN = 1280

VMEM_LIMIT = 56 * 1024 * 1024


def _cparams(sem):
    return pltpu.CompilerParams(dimension_semantics=sem, vmem_limit_bytes=VMEM_LIMIT)


def _dot(a, b):
    return jnp.dot(a, b, preferred_element_type=F32)


def _dot_nt(a, b):
    return lax.dot_general(a, b, (((1,), (1,)), ((), ())), preferred_element_type=F32)


def _dot_tn(a, b):
    return lax.dot_general(a, b, (((0,), (0,)), ((), ())), preferred_element_type=F32)


def _split3(x):
    hi = x.astype(BF16)
    r = x - hi.astype(F32)
    mid = r.astype(BF16)
    lo = (r - mid.astype(F32)).astype(BF16)
    return hi, mid, lo


def _dot3_l(x, m):
    hi, mid, lo = _split3(x)
    return _dot(hi, m) + _dot(mid, m) + _dot(lo, m)


def _dot3_r(m, x):
    hi, mid, lo = _split3(x)
    return _dot(m, hi) + _dot(m, mid) + _dot(m, lo)


def _silu(x):
    return x * jax.nn.sigmoid(x)


def _mod_kernel(cond_ref, w_ref, b_ref, o_ref):
    a = _silu(cond_ref[...]).astype(BF16)
    o_ref[...] = _dot(a, w_ref[...].astype(BF16)) + b_ref[...]


def _mod_vectors(cond, w_mod, b_mod):
    L, D, W = w_mod.shape
    tn = 1024
    return pl.pallas_call(
        _mod_kernel,
        out_shape=jax.ShapeDtypeStruct((L, 8, W), F32),
        grid=(L, W // tn),
        in_specs=[pl.BlockSpec((8, D), lambda l, j: (0, 0)),
                  pl.BlockSpec((None, D, tn), lambda l, j: (l, 0, j)),
                  pl.BlockSpec((None, 1, tn), lambda l, j: (l, 0, j))],
        out_specs=pl.BlockSpec((None, 8, tn), lambda l, j: (l, 0, j)),
        compiler_params=_cparams(("arbitrary", "arbitrary")),
        name="mod_vectors",
    )(cond, w_mod, b_mod.reshape(L, 1, W))


def _is_ctx_row(blk, tm, n_lat):
    return blk * tm + lax.broadcasted_iota(jnp.int32, (tm, 1), 0) >= n_lat


def _norm_mod(x, g, mod_ref, blk, tm, n_lat, sh, sc):
    D = x.shape[-1]
    y = x * lax.rsqrt(jnp.mean(x * x, axis=-1, keepdims=True) + EPS) * g
    is_ctx = _is_ctx_row(blk, tm, n_lat)
    scale = jnp.where(is_ctx, mod_ref[0:1, sc * D:(sc + 1) * D], mod_ref[1:2, sc * D:(sc + 1) * D])
    shift = jnp.where(is_ctx, mod_ref[0:1, sh * D:(sh + 1) * D], mod_ref[1:2, sh * D:(sh + 1) * D])
    return y * (1.0 + scale) + shift


def _inproj_kernel(h_ref, mod_ref, g_ref, w_ref, p_ref, ab_ref, u_scr, *, tm, n_lat, ab_off):
    i = pl.program_id(1)
    j = pl.program_id(2)

    @pl.when(j == 0)
    def _():
        u_scr[...] = _norm_mod(h_ref[...], g_ref[...], mod_ref, i, tm, n_lat, 0, 1).astype(BF16)

    r = _dot(u_scr[...], w_ref[...])
    p_ref[...] = r.astype(BF16)

    @pl.when(j == pl.num_programs(2) - 1)
    def _():
        ab_ref[...] = r[:, ab_off:ab_off + LANES]


def _in_projection(h, modtab, g_norm, w_perm, n_lat, tm):
    B, T, D = h.shape
    ncol = IN_W_PAD // IN_TN
    kern = functools.partial(_inproj_kernel, tm=tm, n_lat=n_lat, ab_off=COL_AB - (ncol - 1) * IN_TN)
    return pl.pallas_call(
        kern,
        out_shape=(jax.ShapeDtypeStruct((B, T, IN_W_PAD), BF16),
                   jax.ShapeDtypeStruct((B, T, LANES), F32)),
        grid=(B, T // tm, ncol),
        in_specs=[pl.BlockSpec((None, tm, D), lambda b, i, j: (b, i, 0)),
                  pl.BlockSpec((None, 2, 6 * D), lambda b, i, j: (b, 0, 0)),
                  pl.BlockSpec((1, D), lambda b, i, j: (0, 0)),
                  pl.BlockSpec((D, IN_TN), lambda b, i, j: (0, j))],
        out_specs=(pl.BlockSpec((None, tm, IN_TN), lambda b, i, j: (b, i, j)),
                   pl.BlockSpec((None, tm, LANES), lambda b, i, j: (b, i, 0))),
        scratch_shapes=[pltpu.VMEM((tm, D), BF16)],
        compiler_params=_cparams(("arbitrary", "arbitrary", "arbitrary")),
        name="in_projection",
    )(h, modtab, g_norm.reshape(1, D), w_perm)


def _rope(x, cos, sin_a, sin_b):
    return x * cos + pltpu.roll(x, LANES - 32, 1) * sin_a + pltpu.roll(x, 32, 1) * sin_b


def _attn_prep_kernel(aq_ref, ak_ref, av_ref, cq_ref, ck_ref, cv_ref, cos_ref, sa_ref, sb_ref, gq_ref, gk_ref,
                      m_ref, qa_ref, ka_ref, vta_ref, qd_ref, kd_ref, vtd_ref):
    cos, sa, sb = cos_ref[...], sa_ref[...], sb_ref[...]
    vta_ref[...] = av_ref[...].astype(F32).T.astype(BF16)
    for h in range(DF_HEADS):
        vtd_ref[h] = cv_ref[:, h * LANES:(h + 1) * LANES].astype(F32).T.astype(BF16)
    m64 = m_ref[...]
    half = lax.broadcasted_iota(jnp.int32, cos.shape, 1) // HEAD_DIM
    scale = HEAD_DIM ** -0.5 * math.log2(math.e)

    def head_norm(x, g):
        ms = _dot3_l(x * x, m64) * (1.0 / HEAD_DIM)
        return x * lax.rsqrt(ms + EPS) * g

    for s in range(A_HEADS // 2):
        x = aq_ref[:, s * LANES:(s + 1) * LANES].astype(F32)
        y = _rope(head_norm(x, gq_ref[...]), cos, sa, sb) * scale
        y_sw = pltpu.roll(y, HEAD_DIM, 1)
        kvh = (2 * s) // A_GROUP
        for hh in range(2):
            src = y if hh == kvh else y_sw
            qa_ref[2 * s + hh] = jnp.where(half == kvh, src, 0.0).astype(BF16)
    xk = ak_ref[...].astype(F32)
    ka_ref[...] = _rope(head_norm(xk, gk_ref[...]), cos, sa, sb).astype(BF16)
    for h in range(DF_HEADS):
        x = cq_ref[:, h * LANES:(h + 1) * LANES].astype(F32)
        y = _rope(x, cos, sa, sb) * scale
        qd_ref[2 * h] = jnp.where(half == 0, y, 0.0).astype(BF16)
        qd_ref[2 * h + 1] = jnp.where(half == 1, y, 0.0).astype(BF16)
        xk = ck_ref[:, h * LANES:(h + 1) * LANES].astype(F32)
        kd_ref[:, h * LANES:(h + 1) * LANES] = _rope(xk, cos, sa, sb).astype(BF16)


def _attn_prep(P, tabs, g_q, g_k, tk):
    B, T, _ = P.shape
    tm = 256
    per = tk // tm
    cos2, sin_a, sin_b = tabs
    m64 = jnp.asarray(np.kron(np.eye(2), np.ones((HEAD_DIM, HEAD_DIM))), BF16)
    gq2 = jnp.tile(g_q, 2).reshape(1, LANES)
    gk2 = jnp.tile(g_k, 2).reshape(1, LANES)
    tab_spec = pl.BlockSpec((tm, LANES), lambda b, i: (i, 0))
    vec_spec = pl.BlockSpec((1, LANES), lambda b, i: (0, 0))
    return pl.pallas_call(
        _attn_prep_kernel,
        out_shape=(jax.ShapeDtypeStruct((B, A_HEADS, T, LANES), BF16),
                   jax.ShapeDtypeStruct((B, T, LANES), BF16),
                   jax.ShapeDtypeStruct((B, T // tk, LANES, tk), BF16),
                   jax.ShapeDtypeStruct((B, 2 * DF_HEADS, T, LANES), BF16),
                   jax.ShapeDtypeStruct((B, T, 512), BF16),
                   jax.ShapeDtypeStruct((B, DF_HEADS, T // tk, LANES, tk), BF16)),
        grid=(B, T // tm),
        in_specs=[pl.BlockSpec((None, tm, 512), lambda b, i: (b, i, COL_AQ // 512)),
                  pl.BlockSpec((None, tm, LANES), lambda b, i: (b, i, COL_AK // LANES)),
                  pl.BlockSpec((None, tm, LANES), lambda b, i: (b, i, COL_AV // LANES)),
                  pl.BlockSpec((None, tm, 512), lambda b, i: (b, i, COL_CQ // 512)),
                  pl.BlockSpec((None, tm, 512), lambda b, i: (b, i, COL_CK // 512)),
                  pl.BlockSpec((None, tm, 512), lambda b, i: (b, i, COL_CV // 512)),
                  tab_spec, tab_spec, tab_spec, vec_spec, vec_spec,
                  pl.BlockSpec((LANES, LANES), lambda b, i: (0, 0))],
        out_specs=(pl.BlockSpec((None, A_HEADS, tm, LANES), lambda b, i: (b, 0, i, 0)),
                   pl.BlockSpec((None, tm, LANES), lambda b, i: (b, i, 0)),
                   pl.BlockSpec((None, None, LANES, tm), lambda b, i: (b, i // per, 0, i % per)),
                   pl.BlockSpec((None, 2 * DF_HEADS, tm, LANES), lambda b, i: (b, 0, i, 0)),
                   pl.BlockSpec((None, tm, 512), lambda b, i: (b, i, 0)),
                   pl.BlockSpec((None, DF_HEADS, None, LANES, tm), lambda b, i: (b, 0, i // per, 0, i % per))),
        compiler_params=_cparams(("arbitrary", "arbitrary")),
        name="attn_prep",
    )(P, P, P, P, P, P, cos2, sin_a, sin_b, gq2, gk2, m64)


FLASH_TQ = 512
FLASH_PIECE = 256
FLASH_MAX_OVERSHOOT = 60.0
FLASH_GROUP = 2


def _flash_kernel(*refs, mode, cols, tk, n_ctx, n_chunks, ctx_only, lambda_init):
    refs = list(refs)
    q_ref, k_ref, vt_ref = refs[0:3]
    pos = 3
    if mode == "diff":
        lam_ref, g_ref = refs[pos], refs[pos + 1]
        pos += 2
    o_ref, m_scr, l_scr, over_scr, acc_scr = refs[pos:pos + 5]
    q = q_ref[...].reshape(cols, LANES)

    def reset():
        m_scr[...] = jnp.full((1, cols), -jnp.inf, F32)
        l_scr[...] = jnp.zeros((1, cols), F32)
        acc_scr[...] = jnp.zeros((LANES, cols), F32)

    def exact(k, vt):
        s = _dot_nt(k, q)
        m_old = m_scr[...]
        m_new = jnp.maximum(m_old, jnp.max(s, axis=0, keepdims=True))
        alpha = jnp.exp2(m_old - m_new)
        p = jnp.exp2(s - m_new)
        l_scr[...] = alpha * l_scr[...] + jnp.sum(p, axis=0, keepdims=True)
        acc_scr[...] = alpha * acc_scr[...] + _dot(vt, p.astype(BF16))
        m_scr[...] = m_new

    def lagged(k, vt):
        ref = m_scr[...]
        s = _dot_nt(k, q)
        p = jnp.exp2(s - ref)
        m_cur = jnp.max(s, axis=0, keepdims=True)
        m_new = jnp.maximum(ref, m_cur)
        alpha = jnp.exp2(ref - m_new)
        l_scr[...] = (l_scr[...] + jnp.sum(p, axis=0, keepdims=True)) * alpha
        acc_scr[...] = (acc_scr[...] + _dot(vt, p.astype(BF16))) * alpha
        over_scr[...] = jnp.maximum(over_scr[...], m_cur - ref)
        m_scr[...] = m_new

    def k_chunk(c):
        return k_ref[pl.ds(pl.multiple_of(c * tk, tk), tk), :]

    reset()
    if ctx_only:
        exact(k_ref[...], vt_ref[:, tk - n_ctx:])
    else:
        over_scr[...] = jnp.zeros((1, cols), F32)
        exact(k_ref[0:FLASH_PIECE, :], vt_ref[0, :, 0:FLASH_PIECE])
        if tk > FLASH_PIECE:
            lagged(k_ref[FLASH_PIECE:tk, :], vt_ref[0, :, FLASH_PIECE:tk])

        def lagged_group(c0):
            ref = m_scr[...]
            l, acc, over = l_scr[...], acc_scr[...], over_scr[...]
            pending = None
            for u in range(FLASH_GROUP + 1):
                if u < FLASH_GROUP:
                    s = _dot_nt(k_chunk(c0 + u), q)
                    m_cur = jnp.max(s, axis=0, keepdims=True)
                    over = jnp.maximum(over, m_cur - ref)
                    ref_next = jnp.maximum(ref, m_cur)
                if pending is not None:
                    s_p, ref_p, ref_after, c_p = pending
                    p = jnp.exp2(s_p - ref_p)
                    alpha = jnp.exp2(ref_p - ref_after)
                    l = (l + jnp.sum(p, axis=0, keepdims=True)) * alpha
                    acc = (acc + _dot(vt_ref[c_p], p.astype(BF16))) * alpha
                if u < FLASH_GROUP:
                    pending = (s, ref, ref_next, c0 + u)
                    ref = ref_next
            l_scr[...] = l
            acc_scr[...] = acc
            over_scr[...] = over
            m_scr[...] = ref

        def fast_body(it, carry):
            lagged_group(1 + FLASH_GROUP * it)
            return carry
        lax.fori_loop(0, (n_chunks - 1) // FLASH_GROUP, fast_body, 0)

        @pl.when(jnp.max(over_scr[...]) > FLASH_MAX_OVERSHOOT)
        def _():
            reset()

            def exact_body(c, carry):
                exact(k_chunk(c), vt_ref[c])
                return carry
            lax.fori_loop(0, n_chunks, exact_body, 0)

    if mode == "gqa":
        kvh = pl.program_id(1)
        o = acc_scr[pl.ds(pl.multiple_of(kvh * HEAD_DIM, HEAD_DIM), HEAD_DIM), :] / l_scr[...]
        tq = cols // A_GROUP
        pieces = [o[:, g * tq:(g + 1) * tq] for g in range(A_GROUP)]
        o_ref[...] = jnp.concatenate(pieces, axis=0).T.astype(o_ref.dtype)
    else:
        lq = lam_ref[...]
        lam = (jnp.exp(jnp.sum(lq[0:1] * lq[1:2], axis=-1, keepdims=True))
               - jnp.exp(jnp.sum(lq[2:3] * lq[3:4], axis=-1, keepdims=True)) + lambda_init)
        o = acc_scr[...] / l_scr[...]
        d = o[:, :cols // 2] - lam * o[:, cols // 2:]
        y = d * lax.rsqrt(jnp.mean(d * d, axis=0, keepdims=True) + EPS) * g_ref[...]
        o_ref[...] = (y * (1.0 - lambda_init)).T.astype(o_ref.dtype)


def _flash(mode, q, k, vt, n_ctx, ctx_only, extra=(), lambda_init=0.0):
    B, NQ, T, _ = q.shape
    S = T - n_ctx
    tk = vt.shape[-1]
    n_chunks = vt.shape[-3]
    assert (n_chunks - 1) % FLASH_GROUP == 0
    n_maps = A_GROUP if mode == "gqa" else 2
    n_heads = NQ // n_maps
    if ctx_only:
        tq, n_q, q0, rows = n_ctx, 1, S // n_ctx, n_ctx
        k_rows, k0 = n_ctx, S // n_ctx
        vt_blk, vt0 = None, n_chunks - 1
    else:
        tq = FLASH_TQ if mode == "gqa" else 2 * FLASH_TQ
        n_q, q0, rows = S // tq, 0, S
        k_rows, k0 = T, 0
        vt_blk, vt0 = n_chunks, 0
    cols = n_maps * tq
    kern = functools.partial(_flash_kernel, mode=mode, cols=cols, tk=tk, n_ctx=n_ctx,
                             n_chunks=n_chunks, ctx_only=ctx_only, lambda_init=lambda_init)
    in_specs = [pl.BlockSpec((None, n_maps, tq, LANES), lambda b, h, i: (b, h, i + q0, 0))]
    if mode == "gqa":
        in_specs += [pl.BlockSpec((None, k_rows, LANES), lambda b, h, i: (b, k0, 0)),
                     pl.BlockSpec((None, vt_blk, LANES, tk), lambda b, h, i: (b, vt0, 0, 0))]
        out_spec = pl.BlockSpec((None, tq, A_GROUP * HEAD_DIM), lambda b, h, i: (b, i, h))
    else:
        in_specs += [pl.BlockSpec((None, k_rows, LANES), lambda b, h, i: (b, k0, h)),
                     pl.BlockSpec((None, None, vt_blk, LANES, tk), lambda b, h, i: (b, h, vt0, 0, 0)),
                     pl.BlockSpec((4, HEAD_DIM), lambda b, h, i: (0, 0)),
                     pl.BlockSpec((LANES, 1), lambda b, h, i: (0, 0))]
        out_spec = pl.BlockSpec((None, tq, LANES), lambda b, h, i: (b, i, h))
    return pl.pallas_call(
        kern,
        out_shape=jax.ShapeDtypeStruct((B, rows, BR_W), BF16),
        grid=(B, n_heads, n_q),
        in_specs=in_specs,
        out_specs=out_spec,
        scratch_shapes=[pltpu.VMEM((1, cols), F32)] * 3 + [pltpu.VMEM((LANES, cols), F32)],
        compiler_params=_cparams(("arbitrary", "arbitrary", "arbitrary")),
        name="flash_" + mode + ("_ctx" if ctx_only else ""),
    )(q, k, vt, *extra)


def _dn_prep_kernel(q_ref, k_ref, v_ref, ql_ref, kl_ref, vl_ref, qr_ref, kr_ref, vr_ref, ab_ref,
                    wc_ref, alog_ref, dtb_ref, mh_ref, eg_ref, eb_ref, tm_ref, ones_ref,
                    qo_ref, ko_ref, kb_ref, vbk_ref, qdec_ref, kdecbd_ref, decbd_ref, egl_ref,
                    *, tm):
    i = pl.program_id(1)
    nblk = pl.num_programs(1)
    W = DN_HEADS * DN_DK
    row = lax.broadcasted_iota(jnp.int32, (tm, W), 0)
    col = lax.broadcasted_iota(jnp.int32, (tm, W), 1)
    left_ok = jnp.logical_and(i >= 1, i < nblk - 1).astype(F32)
    right_ok = (i < nblk - 2).astype(F32)

    def conv_silu(x_ref, xl_ref, xr_ref, c0):
        x = x_ref[...].astype(F32)
        prev_row = xl_ref[7:8, :].astype(F32) * left_ok
        next_row = xr_ref[0:1, :].astype(F32) * right_ok
        x_prev = jnp.where(row == 0, prev_row, pltpu.roll(x, 1, 0))
        x_next = jnp.where(row == tm - 1, next_row, pltpu.roll(x, tm - 1, 0))
        y = (x_prev * wc_ref[0:1, c0:c0 + W] + x * wc_ref[1:2, c0:c0 + W]
             + x_next * wc_ref[2:3, c0:c0 + W])
        return _silu(y)

    def l2norm(x):
        ss = _dot3_l(x * x, mh_ref[...])
        return x * lax.rsqrt(ss + EPS)

    q = l2norm(conv_silu(q_ref, ql_ref, qr_ref, 0)) * (DN_DK ** -0.5)
    k = l2norm(conv_silu(k_ref, kl_ref, kr_ref, W))
    v = conv_silu(v_ref, vl_ref, vr_ref, 2 * W)

    ab = ab_ref[...]
    z = ab + dtb_ref[...]
    softplus = jnp.maximum(z, 0.0) + jnp.log(1.0 + jnp.exp(-jnp.abs(z)))
    g_all = -jnp.exp(alog_ref[...]) * softplus
    beta_all = jax.nn.sigmoid(ab)

    t_in = row & (DN_CHUNK - 1)
    s_in = col & (DN_CHUNK - 1)

    def put(ref, d, val):
        for h in range(DN_HEADS):
            piece = val[:, h * DN_DK:(h + 1) * DN_DK]
            if d is None:
                ref[h] = piece.astype(ref.dtype)
            else:
                ref[d, h] = piece.astype(ref.dtype)

    lo = lax.broadcasted_iota(jnp.int32, (tm, LANES), 1) < DN_DK
    n_chunks = tm // DN_CHUNK
    bd_shape = (tm, n_chunks * DN_CHUNK)
    same_chunk = (lax.broadcasted_iota(jnp.int32, bd_shape, 0) // DN_CHUNK
                  == lax.broadcasted_iota(jnp.int32, bd_shape, 1) // DN_CHUNK)

    def pair(a, b, h):
        sa = a[:, (h // 2) * LANES:(h // 2 + 1) * LANES]
        sb = b[:, (h // 2) * LANES:(h // 2 + 1) * LANES]
        if h % 2 == 0:
            return jnp.where(lo, sa, pltpu.roll(sb, DN_DK, 1))
        return jnp.where(lo, pltpu.roll(sa, DN_DK, 1), sb)

    def put_block_diag(ref, d, val):
        for h in range(DN_HEADS):
            x = pair(val, val, h)
            ref[d, h] = jnp.where(same_chunk, jnp.concatenate([x] * (n_chunks // 2), axis=1), 0.0).astype(ref.dtype)

    put(qo_ref, None, q)
    put(ko_ref, None, k)
    for d in range(2):
        g_e = _dot3_l(g_all, eg_ref[d])
        beta_e = _dot3_l(beta_all, eb_ref[d])
        gc = _dot3_r(tm_ref[d], g_e)
        g_tot = _dot3_r(ones_ref[...], g_e)
        strict = (t_in > s_in) if d == 0 else (t_in < s_in)
        causal = (t_in >= s_in) if d == 0 else (t_in <= s_in)
        diff = _dot3_r(tm_ref[d], jnp.where(strict, g_e, 0.0))
        decay = jnp.where(causal, jnp.exp(diff), 0.0)
        e_g = jnp.exp(gc)
        kb = k * beta_e
        put(kb_ref, d, kb)
        vb, kbg = v * beta_e, kb * e_g
        for h in range(DN_HEADS):
            vbk_ref[d, h] = pair(vb, kbg, h).astype(vbk_ref.dtype)
        put(qdec_ref, d, q * e_g)
        put_block_diag(kdecbd_ref, d, k * jnp.exp(g_tot - gc))
        put_block_diag(decbd_ref, d, decay)
        e_tot = jnp.exp(g_tot)
        for h in range(DN_HEADS):
            for c in range(tm // DN_CHUNK):
                egl_ref[d, h, c:c + 1, :] = e_tot[c * DN_CHUNK:c * DN_CHUNK + 1, h * DN_DK:(h + 1) * DN_DK]


def _dn_consts():
    n = DN_BLOCK
    t = np.arange(n)
    same = (t[:, None] // DN_CHUNK) == (t[None, :] // DN_CHUNK)
    pre = same & (t[None, :] <= t[:, None])
    suf = same & (t[None, :] >= t[:, None])
    tmat = np.stack([pre, suf]).astype(np.float32)
    ones = same.astype(np.float32)
    W = DN_HEADS * DN_DK
    eg = np.zeros((2, LANES, W), np.float32)
    eb = np.zeros((2, LANES, W), np.float32)
    for d in range(2):
        for h in range(DN_HEADS):
            eg[d, d * DN_HEADS + h, h * DN_DK:(h + 1) * DN_DK] = 1.0
            eb[d, 2 * DN_HEADS + d * DN_HEADS + h, h * DN_DK:(h + 1) * DN_DK] = 1.0
    mh = np.kron(np.eye(DN_HEADS), np.ones((DN_DK, DN_DK))).astype(np.float32)
    return tuple(jnp.asarray(a, BF16) for a in (mh, eg, eb, tmat, ones))


def _dn_prep(P, AB, w_conv, a_log, dt_bias):
    B, T, _ = P.shape
    tm = DN_BLOCK
    nblk = T // tm
    W = DN_HEADS * DN_DK
    mh, eg, eb, tmat, ones = _dn_consts()
    alog_row = jnp.zeros((1, LANES), F32).at[0, :2 * DN_HEADS].set(a_log.reshape(-1))
    dtb_row = jnp.zeros((1, LANES), F32).at[0, :2 * DN_HEADS].set(dt_bias.reshape(-1))
    r8 = tm // 8

    def main(c0):
        return pl.BlockSpec((None, tm, W), lambda b, i: (b, i, c0 // W))

    def left(c0):
        return pl.BlockSpec((None, 8, W), lambda b, i: (b, jnp.maximum(i * r8 - 1, 0), c0 // W))

    def right(c0):
        return pl.BlockSpec((None, 8, W), lambda b, i: (b, jnp.minimum((i + 1) * r8, nblk * r8 - 1), c0 // W))

    def full(a):
        nd = a.ndim
        return pl.BlockSpec(a.shape, lambda b, i: (0,) * nd)

    hm = pl.BlockSpec((None, DN_HEADS, tm, DN_DK), lambda b, i: (b, 0, i, 0))

    def hm2(width):
        return pl.BlockSpec((2, None, DN_HEADS, tm, width), lambda b, i: (0, b, 0, i, 0))

    def sh_hm2(width):
        return jax.ShapeDtypeStruct((2, B, DN_HEADS, T, width), BF16)

    sh_hm = jax.ShapeDtypeStruct((B, DN_HEADS, T, DN_DK), BF16)
    return pl.pallas_call(
        functools.partial(_dn_prep_kernel, tm=tm),
        out_shape=(sh_hm, sh_hm, sh_hm2(DN_DK), sh_hm2(2 * DN_DK), sh_hm2(DN_DK), sh_hm2(tm), sh_hm2(tm),
                   jax.ShapeDtypeStruct((2, B, nblk, DN_HEADS, tm // DN_CHUNK, DN_DK), F32)),
        grid=(B, nblk),
        in_specs=[main(COL_BQ), main(COL_BK), main(COL_BV),
                  left(COL_BQ), left(COL_BK), left(COL_BV),
                  right(COL_BQ), right(COL_BK), right(COL_BV),
                  pl.BlockSpec((None, tm, LANES), lambda b, i: (b, i, 0)),
                  full(w_conv), full(alog_row), full(dtb_row), full(mh), full(eg), full(eb), full(tmat),
                  full(ones)],
        out_specs=(hm, hm, hm2(DN_DK), hm2(2 * DN_DK), hm2(DN_DK), hm2(tm), hm2(tm),
                   pl.BlockSpec((2, None, None, DN_HEADS, tm // DN_CHUNK, DN_DK),
                                lambda b, i: (0, b, i, 0, 0, 0))),
        compiler_params=_cparams(("arbitrary", "arbitrary")),
        name="dn_prep",
    )(P, P, P, P, P, P, P, P, P, AB, w_conv, alog_row, dtb_row, mh, eg, eb, tmat, ones)


def _dn_chunk_kernel(q_ref, k_ref, kb_ref, vbk_ref, qdec_ref, kdecbd_ref, decbd_ref,
                     qeff_ref, o0_ref, mk_ref, n_ref):
    n, C = DN_BLOCK, DN_CHUNK
    nc = n // C
    r = lax.broadcasted_iota(jnp.int32, (n, n), 0)
    c = lax.broadcasted_iota(jnp.int32, (n, n), 1)
    off_diag = r != c
    same_chunk = (r // C) == (c // C)
    rp = lax.broadcasted_iota(jnp.int32, (C, n), 0)
    cp = lax.broadcasted_iota(jnp.int32, (C, n), 1)
    eye_packed = (rp == cp % C).astype(F32)
    own_values = (lax.broadcasted_iota(jnp.int32, (n, 2 * nc * C), 0) // C
                  == lax.broadcasted_iota(jnp.int32, (n, 2 * nc * C), 1) // (2 * C))
    heads = range(DN_HEADS)

    def block_diag(xp):
        xb = xp.astype(BF16)
        return jnp.where(same_chunk, jnp.concatenate([xb] * nc, axis=0), jnp.zeros((), BF16))

    low, qk = [], []
    for h in heads:
        dec = decbd_ref[h].astype(F32)
        k = k_ref[h]
        a = jnp.where(off_diag, _dot_nt(kb_ref[h], k) * dec, 0.0)
        low.append(sum(a[i * C:(i + 1) * C] for i in range(nc)))
        qk.append((_dot_nt(q_ref[h], k) * dec).astype(BF16))
    p = [eye_packed - m for m in low]
    mpow = low
    for _ in range(5):
        mpow = [_dot(m.astype(BF16), block_diag(m)) for m in mpow]
        p = [a + _dot(a.astype(BF16), block_diag(m)) for a, m in zip(p, mpow)]
    uw = []
    for h in heads:
        vals = jnp.where(own_values, jnp.concatenate([vbk_ref[h]] * nc, axis=1), jnp.zeros((), BF16))
        packed = _dot(p[h].astype(BF16), vals)
        uw.append(jnp.concatenate([packed[:, i * 2 * C:(i + 1) * 2 * C] for i in range(nc)],
                                  axis=0).astype(BF16))
    qkuw = [_dot(qk[h], uw[h]) for h in heads]
    for h in heads:
        o0_ref[h] = qkuw[h][:, :DN_DK]
        qeff_ref[h] = (qdec_ref[h].astype(F32) - qkuw[h][:, DN_DK:]).astype(qeff_ref.dtype)
        kt = _dot_tn(kdecbd_ref[h], uw[h])
        n_ref[h] = kt[:, :DN_DK]
        mk_ref[h] = kt[:, DN_DK:].astype(mk_ref.dtype)


def _dn_chunk(prep):
    q, k, kb, vbk, qdec, kdecbd, decbd, _ = prep
    B, H, T, dk = q.shape
    tm = DN_BLOCK
    hm = pl.BlockSpec((None, H, tm, dk), lambda d, b, i: (b, 0, i, 0))

    def hm2(width):
        return pl.BlockSpec((None, None, H, tm, width), lambda d, b, i: (d, b, 0, i, 0))

    def sh(dtype):
        return jax.ShapeDtypeStruct((2, B, H, T, dk), dtype)

    return pl.pallas_call(
        _dn_chunk_kernel,
        out_shape=(sh(BF16), sh(F32), sh(BF16), sh(F32)),
        grid=(2, B, T // tm),
        in_specs=[hm, hm, hm2(dk), hm2(2 * dk), hm2(dk), hm2(tm), hm2(tm)],
        out_specs=(hm2(dk), hm2(dk), hm2(dk), hm2(dk)),
        compiler_params=_cparams(("arbitrary", "arbitrary", "arbitrary")),
        name="dn_chunk",
    )(q, k, kb, vbk, qdec, kdecbd, decbd)


def _dn_scan_kernel(*refs, n_batch):
    ins = (refs[0:5], refs[5:10])
    o_refs = refs[10:12]
    s_scr = refs[12]
    t = pl.program_id(0)
    n_chunks = DN_BLOCK // DN_CHUNK
    C = DN_CHUNK

    @pl.when(t == 0)
    def _():
        s_scr[...] = jnp.zeros_like(s_scr)

    for step in range(n_chunks):
        for d in range(2):
            ci = step if d == 0 else n_chunks - 1 - step
            rows = slice(ci * C, (ci + 1) * C)
            qeff_ref, o0_ref, mk_ref, n_ref, egl_ref = ins[d]
            for b in range(n_batch):
                for h in range(DN_HEADS):
                    s = s_scr[d, b, h]
                    sb = s.astype(BF16)
                    o_refs[d][b, h, rows, :] = _dot(qeff_ref[b, h, rows, :], sb) + o0_ref[b, h, rows, :]
                    s_scr[d, b, h] = (s * egl_ref[b, h, ci:ci + 1, :] - _dot(mk_ref[b, h, rows, :], sb)
                                      + n_ref[b, h, rows, :])


def _dn_scan(maps, egl):
    qeff, o0, mk, nn = maps
    _, B, H, T, dk = qeff.shape
    tm = DN_BLOCK
    nblk = T // tm

    def blk(d, t):
        return jnp.where(t == 0, nblk - 1, nblk - 1 - t if d == 1 else t - 1)

    in_specs, args = [], []
    for d in range(2):
        for a in (qeff, o0, mk, nn):
            in_specs.append(pl.BlockSpec((None, B, H, tm, dk), lambda t, d=d: (d, 0, 0, blk(d, t), 0)))
            args.append(a)
        in_specs.append(pl.BlockSpec((None, B, None, H, tm // DN_CHUNK, dk),
                                     lambda t, d=d: (d, 0, blk(d, t), 0, 0, 0)))
        args.append(egl)
    out_specs = tuple(pl.BlockSpec((B, H, tm, dk), lambda t, d=d: (0, 0, blk(d, t), 0)) for d in range(2))
    return pl.pallas_call(
        functools.partial(_dn_scan_kernel, n_batch=B),
        out_shape=(jax.ShapeDtypeStruct((B, H, T, dk), F32),) * 2,
        grid=(nblk,),
        in_specs=in_specs,
        out_specs=out_specs,
        scratch_shapes=[pltpu.VMEM((2, B, H, dk, dk), F32)],
        compiler_params=_cparams(("arbitrary",)),
        name="dn_scan",
    )(*args)


def _dn_post_kernel(of_ref, ob_ref, z_ref, g_ref, o_ref):
    z = z_ref[...].astype(F32)
    for h in range(DN_HEADS):
        o = of_ref[h] + ob_ref[h]
        y = o * lax.rsqrt(jnp.mean(o * o, axis=-1, keepdims=True) + EPS) * g_ref[...]
        zz = z[:, h * DN_DK:(h + 1) * DN_DK]
        o_ref[:, h * DN_DK:(h + 1) * DN_DK] = (y * _silu(zz)).astype(o_ref.dtype)


def _dn_post(o_f, o_b, P, g_out):
    B, H, T, dk = o_f.shape
    tm = DN_BLOCK
    hm = pl.BlockSpec((None, H, tm, dk), lambda b, i: (b, 0, i, 0))
    return pl.pallas_call(
        _dn_post_kernel,
        out_shape=jax.ShapeDtypeStruct((B, T, BR_W), F32),
        grid=(B, T // tm),
        in_specs=[hm, hm,
                  pl.BlockSpec((None, tm, BR_W), lambda b, i: (b, i, COL_BZ // BR_W)),
                  pl.BlockSpec((1, dk), lambda b, i: (0, 0))],
        out_specs=pl.BlockSpec((None, tm, BR_W), lambda b, i: (b, i, 0)),
        compiler_params=_cparams(("arbitrary", "arbitrary")),
        name="dn_post",
    )(o_f, o_b, P, g_out.reshape(1, dk))


def _merge_kernel(a_ref, b_ref, c_ref, ga_ref, gb_ref, gc_ref, h_ref, mod_ref, g2_ref, wb_ref, wo_ref,
                  h_out_ref, v_out_ref, *, tm, n_lat):
    i = pl.program_id(1)
    D = h_ref.shape[-1]
    m = jax.nn.sigmoid(ga_ref[...].astype(F32)) * _dot(a_ref[...], wb_ref[0])
    m = m + jax.nn.sigmoid(gb_ref[...].astype(F32)) * _dot(b_ref[...].astype(BF16), wb_ref[1])
    m = m + jax.nn.sigmoid(gc_ref[...].astype(F32)) * _dot(c_ref[...], wb_ref[2])
    y = _dot(m.astype(BF16), wo_ref[...])
    gate = jnp.where(_is_ctx_row(i, tm, n_lat), mod_ref[0:1, 2 * D:3 * D], mod_ref[1:2, 2 * D:3 * D])
    h_new = h_ref[...] + gate * y
    h_out_ref[...] = h_new
    v_out_ref[...] = _norm_mod(h_new, g2_ref[...], mod_ref, i, tm, n_lat, 3, 4).astype(BF16)


def _merge(br_a, br_b, br_c, P, h, modtab, g_norm2, w_branch, w_out, n_lat, tm):
    B, T, D = h.shape
    br = pl.BlockSpec((None, tm, BR_W), lambda b, i: (b, i, 0))

    def gate(j):
        return pl.BlockSpec((None, tm, D), lambda b, i: (b, i, COL_GATE // D + j))

    tok = pl.BlockSpec((None, tm, D), lambda b, i: (b, i, 0))
    return pl.pallas_call(
        functools.partial(_merge_kernel, tm=tm, n_lat=n_lat),
        out_shape=(jax.ShapeDtypeStruct((B, T, D), F32), jax.ShapeDtypeStruct((B, T, D), BF16)),
        grid=(B, T // tm),
        in_specs=[br, br, br, gate(0), gate(1), gate(2), tok,
                  pl.BlockSpec((None, 2, 6 * D), lambda b, i: (b, 0, 0)),
                  pl.BlockSpec((1, D), lambda b, i: (0, 0)),
                  pl.BlockSpec((3, BR_W, D), lambda b, i: (0, 0, 0)),
                  pl.BlockSpec((D, D), lambda b, i: (0, 0))],
        out_specs=(tok, tok),
        compiler_params=_cparams(("arbitrary", "arbitrary")),
        name="merge",
    )(br_a, br_b, br_c, P, P, P, h, modtab, g_norm2.reshape(1, D), w_branch, w_out)


def _router_kernel(v_ref, wr_ref, br_ref, perm_ref, c_ref):
    tn = v_ref.shape[0]
    G = N_GROUPS
    scores = jax.nn.sigmoid(_dot_nt(wr_ref[...], v_ref[...]))
    sel = scores + br_ref[...]
    slabs = [sel[j * G:(j + 1) * G] for j in range(GROUP_SIZE)]
    m1, m2 = slabs[0], jnp.full((G, tn), -jnp.inf, F32)
    for j in range(1, GROUP_SIZE):
        m2 = jnp.maximum(m2, jnp.minimum(m1, slabs[j]))
        m1 = jnp.maximum(m1, slabs[j])
    gs = m1 + m2
    gidx = lax.broadcasted_iota(jnp.int32, (G, tn), 0)
    grank = jnp.zeros((G, tn), jnp.int32)
    for g in range(G):
        rowv = gs[g:g + 1]
        beats = jnp.where(rowv > gs, 1, jnp.where(jnp.logical_and(rowv == gs, g < gidx), 1, 0))
        grank = grank + beats
    gmask = grank < TOPK_GROUPS
    masked = jnp.concatenate([jnp.where(gmask, s, -jnp.inf) for s in slabs], axis=0)
    pos = lax.broadcasted_iota(jnp.int32, (N_EXPERTS, tn), 0)
    orig = (pos % G) * GROUP_SIZE + pos // G
    rank = jnp.zeros((N_EXPERTS, tn), jnp.int32)
    for p in range(N_EXPERTS):
        o_p = (p % G) * GROUP_SIZE + p // G
        rowv = masked[p:p + 1]
        beats = jnp.where(rowv > masked, 1, jnp.where(jnp.logical_and(rowv == masked, o_p < orig), 1, 0))
        rank = rank + beats
    chosen = rank < TOP_K
    w = jnp.where(chosen, scores, 0.0)
    denom = jnp.sum(w, axis=0, keepdims=True) + 1e-20
    c = w / denom * ROUTED_SCALE
    c_t = sum(_dot_tn(part, perm_ref[...]) for part in _split3(c))
    for g in range(N_EXPERTS // MOE_EPS):
        c_ref[g] = c_t[:, g * MOE_EPS:(g + 1) * MOE_EPS]


def _router(v_flat, w_router, b_router):
    N, D = v_flat.shape
    tn = 1280 if N % 1280 == 0 else 256
    perm = np.array([(p % N_GROUPS) * GROUP_SIZE + p // N_GROUPS for p in range(N_EXPERTS)])
    wr_t = w_router.T[perm].astype(BF16)
    br = b_router[perm].reshape(N_EXPERTS, 1)
    unperm = jnp.asarray(np.eye(N_EXPERTS)[perm], BF16)
    ngrp = N_EXPERTS // MOE_EPS
    return pl.pallas_call(
        _router_kernel,
        out_shape=jax.ShapeDtypeStruct((ngrp, N, MOE_EPS), F32),
        grid=(N // tn,),
        in_specs=[pl.BlockSpec((tn, D), lambda i: (i, 0)),
                  pl.BlockSpec((N_EXPERTS, D), lambda i: (0, 0)),
                  pl.BlockSpec((N_EXPERTS, 1), lambda i: (0, 0)),
                  pl.BlockSpec((N_EXPERTS, N_EXPERTS), lambda i: (0, 0))],
        out_specs=pl.BlockSpec((ngrp, tn, MOE_EPS), lambda i: (0, i, 0)),
        compiler_params=_cparams(("arbitrary",)),
        name="router",
    )(v_flat, wr_t, br, unperm)


MOE_EPS = 4


def _moe_kernel(x_ref, c_ref, wgu_ref, wd_ref, sgu_ref, sd_ref, h_ref, mod_ref,
                o_ref, acc_ref, *, tm, n_lat):
    i = pl.program_id(1)
    e = pl.program_id(2)
    x = x_ref[...]
    D = x.shape[-1]
    F = MOE_FF

    @pl.when(e == 0)
    def _():
        gu = _dot(x, sgu_ref[...])
        hs = _silu(gu[:, :F]) * gu[:, F:]
        acc_ref[...] = _dot(hs.astype(BF16), sd_ref[...])

    c = c_ref[...]
    parts = []
    for j in range(MOE_EPS):
        gu = _dot(x, wgu_ref[j])
        hj = _silu(gu[:, :F]) * gu[:, F:] * c[:, j:j + 1]
        parts.append(hj.astype(BF16))
    hcat = jnp.concatenate(parts, axis=-1)
    acc_ref[...] += _dot(hcat, wd_ref[...])

    @pl.when(e == pl.num_programs(2) - 1)
    def _():
        gate = jnp.where(_is_ctx_row(i, tm, n_lat), mod_ref[0:1, 5 * D:6 * D], mod_ref[1:2, 5 * D:6 * D])
        o_ref[...] = h_ref[...] + gate * acc_ref[...]


def _moe_weights(w_gate, w_up, w_down):
    E, D, F = w_gate.shape
    gu = jnp.concatenate([w_gate, w_up], axis=-1).astype(BF16)
    return gu, w_down.astype(BF16).reshape(E * F, D)


def _moe(v, c_grp, h, modtab, wgu, wd, sgu, sd, n_lat, tm):
    B, T, D = v.shape
    F = MOE_FF
    ngrp = N_EXPERTS // MOE_EPS
    tok = pl.BlockSpec((None, tm, D), lambda b, i, e: (b, i, 0))
    return pl.pallas_call(
        functools.partial(_moe_kernel, tm=tm, n_lat=n_lat),
        out_shape=jax.ShapeDtypeStruct((B, T, D), F32),
        grid=(B, T // tm, ngrp),
        in_specs=[tok,
                  pl.BlockSpec((None, None, tm, MOE_EPS), lambda b, i, e: (e, b, i, 0)),
                  pl.BlockSpec((MOE_EPS, D, 2 * F), lambda b, i, e: (e, 0, 0)),
                  pl.BlockSpec((MOE_EPS * F, D), lambda b, i, e: (e, 0)),
                  pl.BlockSpec((D, 2 * F), lambda b, i, e: (0, 0)),
                  pl.BlockSpec((F, D), lambda b, i, e: (0, 0)),
                  tok,
                  pl.BlockSpec((None, 2, 6 * D), lambda b, i, e: (b, 0, 0))],
        out_specs=tok,
        scratch_shapes=[pltpu.VMEM((tm, D), F32)],
        compiler_params=_cparams(("arbitrary", "arbitrary", "arbitrary")),
        name="moe",
    )(v, c_grp, wgu, wd, sgu, sd, h, modtab)


def _final_kernel(h_ref, g_ref, o_ref):
    x = h_ref[...]
    o_ref[...] = x * lax.rsqrt(jnp.mean(x * x, axis=-1, keepdims=True) + EPS) * g_ref[...]


def _final_norm(h, g_final, S):
    B, T, D = h.shape
    tm = 1024
    return pl.pallas_call(
        _final_kernel,
        out_shape=jax.ShapeDtypeStruct((B, S, D), F32),
        grid=(B, S // tm),
        in_specs=[pl.BlockSpec((None, tm, D), lambda b, i: (b, i, 0)),
                  pl.BlockSpec((1, D), lambda b, i: (0, 0))],
        out_specs=pl.BlockSpec((None, tm, D), lambda b, i: (b, i, 0)),
        compiler_params=_cparams(("arbitrary", "arbitrary")),
        name="final_norm",
    )(h, g_final.reshape(1, D))


def _rope_tables(S, n_ctx):
    rows = S // GRID_W
    row = jnp.repeat(jnp.arange(rows, dtype=F32), GRID_W)
    col = (jnp.arange(S) % GRID_W).astype(F32)
    axis_dim = HEAD_DIM // 2
    inv = 1.0 / (ROPE_THETA ** (jnp.arange(0, axis_dim, 2, dtype=F32) / axis_dim))
    ang = jnp.concatenate([row[:, None] * inv, col[:, None] * inv], axis=-1)
    ang = jnp.concatenate([ang, ang], axis=-1)
    cos = jnp.concatenate([jnp.cos(ang), jnp.ones((n_ctx, HEAD_DIM), F32)], axis=0)
    sin = jnp.concatenate([jnp.sin(ang), jnp.zeros((n_ctx, HEAD_DIM), F32)], axis=0)
    first = (jnp.arange(HEAD_DIM) < HEAD_DIM // 2)[None, :]
    sin_a = jnp.where(first, -sin, 0.0)
    sin_b = jnp.where(first, 0.0, sin)
    return tuple(jnp.tile(t, (1, 2)) for t in (cos, sin_a, sin_b))


def _permute_w_in(w):
    splits = np.cumsum([512, 128, 128, 1536, 512, 16, 16, 512, 512, 512, 3072])[:-1].tolist()
    aq, ak, av, bqkv, bz, ba, bb, cq, ck, cv, gate = jnp.split(w, splits, axis=-1)
    pad = jnp.zeros((w.shape[0], IN_W_PAD - COL_AB - 32), w.dtype)
    out = jnp.concatenate([aq, cq, ck, cv, bz, bqkv, gate, ak, av, ba, bb, pad], axis=-1)
    return out.astype(BF16)


def kernel(x, c, ctx, c_ctx, w_mod, b_mod, g_norm1, g_norm2, w_in, g_qnorm, g_knorm, w_conv, a_log, dt_bias, g_dn_out, lam_qk, g_subln, w_branch, w_out, w_router, b_router, w_e_gate, w_e_up, w_e_down, w_s_gate, w_s_up, w_s_down, g_final):
    B, S, D = x.shape
    n_ctx = ctx.shape[1]
    T = n_ctx + S
    L = w_mod.shape[0]
    N = B * T
    assert n_ctx == DN_BLOCK and S % 1024 == 0 and D == D_MODEL
    tm_tok = 1280 if T % 1280 == 0 else 256
    tm_merge = 640 if T % 640 == 0 else 256

    cond = jnp.zeros((8, D), F32).at[0].set(c_ctx).at[1:1 + B].set(c)
    mod = _mod_vectors(cond, w_mod, b_mod)
    tabs = _rope_tables(S, n_ctx)
    h = jnp.concatenate([x, ctx], axis=1)
    tk = 1280 if T % 1280 == 0 else 256

    for l in range(L):
        lambda_init = 0.8 - 0.6 * math.exp(-0.3 * l)
        modtab = jnp.stack([jnp.broadcast_to(mod[l, 0], (B, 6 * D)), mod[l, 1:1 + B]], axis=1)
        P, AB = _in_projection(h, modtab, g_norm1[l], _permute_w_in(w_in[l]), S, tm_tok)

        qa, ka, vta, qd, kd, vtd = _attn_prep(P, tabs, g_qnorm[l], g_knorm[l], tk)
        dx = (lam_qk[l], g_subln[l].reshape(LANES, 1))
        br_a = jnp.concatenate([_flash("gqa", qa, ka, vta, n_ctx, False),
                                _flash("gqa", qa, ka, vta, n_ctx, True)], axis=1)
        br_c = jnp.concatenate([_flash("diff", qd, kd, vtd, n_ctx, False, dx, lambda_init),
                                _flash("diff", qd, kd, vtd, n_ctx, True, dx, lambda_init)], axis=1)

        prep = _dn_prep(P, AB, w_conv[l], a_log[l], dt_bias[l])
        o_f, o_b = _dn_scan(_dn_chunk(prep), prep[-1])
        br_b = _dn_post(o_f, o_b, P, g_dn_out[l])

        h, v = _merge(br_a, br_b, br_c, P, h, modtab, g_norm2[l], w_branch[l].astype(BF16),
                      w_out[l].astype(BF16), S, tm_merge)

        c_grp = _router(v.reshape(N, D), w_router[l], b_router[l]).reshape(-1, B, T, MOE_EPS)
        wgu, wd = _moe_weights(w_e_gate[l], w_e_up[l], w_e_down[l])
        sgu = jnp.concatenate([w_s_gate[l], w_s_up[l]], axis=-1).astype(BF16)
        h = _moe(v, c_grp, h, modtab, wgu, wd, sgu, w_s_down[l].astype(BF16), S, tm_merge)

    return _final_norm(h, g_final, S)
```

```python
import functools
import math

import numpy as np
import jax
import jax.numpy as jnp
from jax import lax
from jax.experimental import pallas as pl
from jax.experimental.pallas import tpu as pltpu

F32 = jnp.float32
BF16 = jnp.bfloat16

D_MODEL = 1024
GRID_W = 64
EPS = 1e-6
ROPE_THETA = 10000.0
HEAD_DIM = 64
LANES = 128
A_HEADS = 8
A_KV_HEADS = 2
A_GROUP = A_HEADS // A_KV_HEADS
DN_HEADS = 8
DN_DK = 64
DN_CHUNK = 64
DN_BLOCK = 256
DF_HEADS = 4
N_EXPERTS = 64
TOP_K = 6
N_GROUPS = 8
TOPK_GROUPS = 4
GROUP_SIZE = N_EXPERTS // N_GROUPS
MOE_FF = 256
ROUTED_SCALE = 2.5
BR_W = 512

COL_AQ, COL_CQ, COL_CK, COL_CV, COL_BZ = 0, 512, 1024, 1536, 2048
COL_BQ, COL_BK, COL_BV = 2560, 3072, 3584
COL_GATE = 4096
COL_AK, COL_AV, COL_AB = 7168, 7296, 7424
IN_W_PAD = 7680
IN_TN = 1280

VMEM_LIMIT = 56 * 1024 * 1024


def _cparams(sem):
    return pltpu.CompilerParams(dimension_semantics=sem, vmem_limit_bytes=VMEM_LIMIT)


def _dot(a, b):
    return jnp.dot(a, b, preferred_element_type=F32)


def _dot_nt(a, b):
    return lax.dot_general(a, b, (((1,), (1,)), ((), ())), preferred_element_type=F32)


def _dot_tn(a, b):
    return lax.dot_general(a, b, (((0,), (0,)), ((), ())), preferred_element_type=F32)


def _split3(x):
    hi = x.astype(BF16)
    r = x - hi.astype(F32)
    mid = r.astype(BF16)
    lo = (r - mid.astype(F32)).astype(BF16)
    return hi, mid, lo


def _dot3_l(x, m):
    hi, mid, lo = _split3(x)
    return _dot(hi, m) + _dot(mid, m) + _dot(lo, m)


def _dot3_r(m, x):
    hi, mid, lo = _split3(x)
    return _dot(m, hi) + _dot(m, mid) + _dot(m, lo)


def _silu(x):
    return x * jax.nn.sigmoid(x)


def _mod_kernel(cond_ref, w_ref, b_ref, o_ref):
    a = _silu(cond_ref[...]).astype(BF16)
    o_ref[...] = _dot(a, w_ref[...].astype(BF16)) + b_ref[...]


def _mod_vectors(cond, w_mod, b_mod):
    L, D, W = w_mod.shape
    tn = 1024
    return pl.pallas_call(
        _mod_kernel,
        out_shape=jax.ShapeDtypeStruct((L, 8, W), F32),
        grid=(L, W // tn),
        in_specs=[pl.BlockSpec((8, D), lambda l, j: (0, 0)),
                  pl.BlockSpec((None, D, tn), lambda l, j: (l, 0, j)),
                  pl.BlockSpec((None, 1, tn), lambda l, j: (l, 0, j))],
        out_specs=pl.BlockSpec((None, 8, tn), lambda l, j: (l, 0, j)),
        compiler_params=_cparams(("arbitrary", "arbitrary")),
        name="mod_vectors",
    )(cond, w_mod, b_mod.reshape(L, 1, W))


def _is_ctx_row(blk, tm, n_lat):
    return blk * tm + lax.broadcasted_iota(jnp.int32, (tm, 1), 0) >= n_lat


def _norm_mod(x, g, mod_ref, blk, tm, n_lat, sh, sc):
    D = x.shape[-1]
    y = x * lax.rsqrt(jnp.mean(x * x, axis=-1, keepdims=True) + EPS) * g
    is_ctx = _is_ctx_row(blk, tm, n_lat)
    scale = jnp.where(is_ctx, mod_ref[0:1, sc * D:(sc + 1) * D], mod_ref[1:2, sc * D:(sc + 1) * D])
    shift = jnp.where(is_ctx, mod_ref[0:1, sh * D:(sh + 1) * D], mod_ref[1:2, sh * D:(sh + 1) * D])
    return y * (1.0 + scale) + shift


def _inproj_kernel(h_ref, mod_ref, g_ref, w_ref, p_ref, ab_ref, u_scr, *, tm, n_lat, ab_off):
    i = pl.program_id(1)
    j = pl.program_id(2)

    @pl.when(j == 0)
    def _():
        u_scr[...] = _norm_mod(h_ref[...], g_ref[...], mod_ref, i, tm, n_lat, 0, 1).astype(BF16)

    r = _dot(u_scr[...], w_ref[...])
    p_ref[...] = r.astype(BF16)

    @pl.when(j == pl.num_programs(2) - 1)
    def _():
        ab_ref[...] = r[:, ab_off:ab_off + LANES]


def _in_projection(h, modtab, g_norm, w_perm, n_lat, tm):
    B, T, D = h.shape
    ncol = IN_W_PAD // IN_TN
    kern = functools.partial(_inproj_kernel, tm=tm, n_lat=n_lat, ab_off=COL_AB - (ncol - 1) * IN_TN)
    return pl.pallas_call(
        kern,
        out_shape=(jax.ShapeDtypeStruct((B, T, IN_W_PAD), BF16),
                   jax.ShapeDtypeStruct((B, T, LANES), F32)),
        grid=(B, T // tm, ncol),
        in_specs=[pl.BlockSpec((None, tm, D), lambda b, i, j: (b, i, 0)),
                  pl.BlockSpec((None, 2, 6 * D), lambda b, i, j: (b, 0, 0)),
                  pl.BlockSpec((1, D), lambda b, i, j: (0, 0)),
                  pl.BlockSpec((D, IN_TN), lambda b, i, j: (0, j))],
        out_specs=(pl.BlockSpec((None, tm, IN_TN), lambda b, i, j: (b, i, j)),
                   pl.BlockSpec((None, tm, LANES), lambda b, i, j: (b, i, 0))),
        scratch_shapes=[pltpu.VMEM((tm, D), BF16)],
        compiler_params=_cparams(("arbitrary", "arbitrary", "arbitrary")),
        name="in_projection",
    )(h, modtab, g_norm.reshape(1, D), w_perm)


def _rope(x, cos, sin_a, sin_b):
    return x * cos + pltpu.roll(x, LANES - 32, 1) * sin_a + pltpu.roll(x, 32, 1) * sin_b


def _attn_prep_kernel(aq_ref, ak_ref, av_ref, cq_ref, ck_ref, cv_ref, cos_ref, sa_ref, sb_ref, gq_ref, gk_ref,
                      m_ref, qa_ref, ka_ref, vta_ref, qd_ref, kd_ref, vtd_ref):
    cos, sa, sb = cos_ref[...], sa_ref[...], sb_ref[...]
    vta_ref[...] = av_ref[...].astype(F32).T.astype(BF16)
    for h in range(DF_HEADS):
        vtd_ref[h] = cv_ref[:, h * LANES:(h + 1) * LANES].astype(F32).T.astype(BF16)
    m64 = m_ref[...]
    half = lax.broadcasted_iota(jnp.int32, cos.shape, 1) // HEAD_DIM
    scale = HEAD_DIM ** -0.5 * math.log2(math.e)

    def head_norm(x, g):
        ms = _dot3_l(x * x, m64) * (1.0 / HEAD_DIM)
        return x * lax.rsqrt(ms + EPS) * g

    for s in range(A_HEADS // 2):
        x = aq_ref[:, s * LANES:(s + 1) * LANES].astype(F32)
        y = _rope(head_norm(x, gq_ref[...]), cos, sa, sb) * scale
        y_sw = pltpu.roll(y, HEAD_DIM, 1)
        kvh = (2 * s) // A_GROUP
        for hh in range(2):
            src = y if hh == kvh else y_sw
            qa_ref[2 * s + hh] = jnp.where(half == kvh, src, 0.0).astype(BF16)
    xk = ak_ref[...].astype(F32)
    ka_ref[...] = _rope(head_norm(xk, gk_ref[...]), cos, sa, sb).astype(BF16)
    for h in range(DF_HEADS):
        x = cq_ref[:, h * LANES:(h + 1) * LANES].astype(F32)
        y = _rope(x, cos, sa, sb) * scale
        qd_ref[2 * h] = jnp.where(half == 0, y, 0.0).astype(BF16)
        qd_ref[2 * h + 1] = jnp.where(half == 1, y, 0.0).astype(BF16)
        xk = ck_ref[:, h * LANES:(h + 1) * LANES].astype(F32)
        kd_ref[:, h * LANES:(h + 1) * LANES] = _rope(xk, cos, sa, sb).astype(BF16)


def _attn_prep(P, tabs, g_q, g_k, tk):
    B, T, _ = P.shape
    tm = 256
    per = tk // tm
    cos2, sin_a, sin_b = tabs
    m64 = jnp.asarray(np.kron(np.eye(2), np.ones((HEAD_DIM, HEAD_DIM))), BF16)
    gq2 = jnp.tile(g_q, 2).reshape(1, LANES)
    gk2 = jnp.tile(g_k, 2).reshape(1, LANES)
    tab_spec = pl.BlockSpec((tm, LANES), lambda b, i: (i, 0))
    vec_spec = pl.BlockSpec((1, LANES), lambda b, i: (0, 0))
    return pl.pallas_call(
        _attn_prep_kernel,
        out_shape=(jax.ShapeDtypeStruct((B, A_HEADS, T, LANES), BF16),
                   jax.ShapeDtypeStruct((B, T, LANES), BF16),
                   jax.ShapeDtypeStruct((B, T // tk, LANES, tk), BF16),
                   jax.ShapeDtypeStruct((B, 2 * DF_HEADS, T, LANES), BF16),
                   jax.ShapeDtypeStruct((B, T, 512), BF16),
                   jax.ShapeDtypeStruct((B, DF_HEADS, T // tk, LANES, tk), BF16)),
        grid=(B, T // tm),
        in_specs=[pl.BlockSpec((None, tm, 512), lambda b, i: (b, i, COL_AQ // 512)),
                  pl.BlockSpec((None, tm, LANES), lambda b, i: (b, i, COL_AK // LANES)),
                  pl.BlockSpec((None, tm, LANES), lambda b, i: (b, i, COL_AV // LANES)),
                  pl.BlockSpec((None, tm, 512), lambda b, i: (b, i, COL_CQ // 512)),
                  pl.BlockSpec((None, tm, 512), lambda b, i: (b, i, COL_CK // 512)),
                  pl.BlockSpec((None, tm, 512), lambda b, i: (b, i, COL_CV // 512)),
                  tab_spec, tab_spec, tab_spec, vec_spec, vec_spec,
                  pl.BlockSpec((LANES, LANES), lambda b, i: (0, 0))],
        out_specs=(pl.BlockSpec((None, A_HEADS, tm, LANES), lambda b, i: (b, 0, i, 0)),
                   pl.BlockSpec((None, tm, LANES), lambda b, i: (b, i, 0)),
                   pl.BlockSpec((None, None, LANES, tm), lambda b, i: (b, i // per, 0, i % per)),
                   pl.BlockSpec((None, 2 * DF_HEADS, tm, LANES), lambda b, i: (b, 0, i, 0)),
                   pl.BlockSpec((None, tm, 512), lambda b, i: (b, i, 0)),
                   pl.BlockSpec((None, DF_HEADS, None, LANES, tm), lambda b, i: (b, 0, i // per, 0, i % per))),
        compiler_params=_cparams(("arbitrary", "arbitrary")),
        name="attn_prep",
    )(P, P, P, P, P, P, cos2, sin_a, sin_b, gq2, gk2, m64)


FLASH_TQ = 512
FLASH_PIECE = 256
FLASH_MAX_OVERSHOOT = 60.0
FLASH_GROUP = 2


def _flash_kernel(*refs, mode, cols, tk, n_ctx, n_chunks, ctx_only, lambda_init):
    refs = list(refs)
    q_ref, k_ref, vt_ref = refs[0:3]
    pos = 3
    if mode == "diff":
        lam_ref, g_ref = refs[pos], refs[pos + 1]
        pos += 2
    o_ref, m_scr, l_scr, over_scr, acc_scr = refs[pos:pos + 5]
    q = q_ref[...].reshape(cols, LANES)

    def reset():
        m_scr[...] = jnp.full((1, cols), -jnp.inf, F32)
        l_scr[...] = jnp.zeros((1, cols), F32)
        acc_scr[...] = jnp.zeros((LANES, cols), F32)

    def exact(k, vt):
        s = _dot_nt(k, q)
        m_old = m_scr[...]
        m_new = jnp.maximum(m_old, jnp.max(s, axis=0, keepdims=True))
        alpha = jnp.exp2(m_old - m_new)
        p = jnp.exp2(s - m_new)
        l_scr[...] = alpha * l_scr[...] + jnp.sum(p, axis=0, keepdims=True)
        acc_scr[...] = alpha * acc_scr[...] + _dot(vt, p.astype(BF16))
        m_scr[...] = m_new

    def lagged(k, vt):
        ref = m_scr[...]
        s = _dot_nt(k, q)
        p = jnp.exp2(s - ref)
        m_cur = jnp.max(s, axis=0, keepdims=True)
        m_new = jnp.maximum(ref, m_cur)
        alpha = jnp.exp2(ref - m_new)
        l_scr[...] = (l_scr[...] + jnp.sum(p, axis=0, keepdims=True)) * alpha
        acc_scr[...] = (acc_scr[...] + _dot(vt, p.astype(BF16))) * alpha
        over_scr[...] = jnp.maximum(over_scr[...], m_cur - ref)
        m_scr[...] = m_new

    def k_chunk(c):
        return k_ref[pl.ds(pl.multiple_of(c * tk, tk), tk), :]

    reset()
    if ctx_only:
        exact(k_ref[...], vt_ref[:, tk - n_ctx:])
    else:
        over_scr[...] = jnp.zeros((1, cols), F32)
        exact(k_ref[0:FLASH_PIECE, :], vt_ref[0, :, 0:FLASH_PIECE])
        if tk > FLASH_PIECE:
            lagged(k_ref[FLASH_PIECE:tk, :], vt_ref[0, :, FLASH_PIECE:tk])

        def lagged_group(c0):
            ref = m_scr[...]
            l, acc, over = l_scr[...], acc_scr[...], over_scr[...]
            pending = None
            for u in range(FLASH_GROUP + 1):
                if u < FLASH_GROUP:
                    s = _dot_nt(k_chunk(c0 + u), q)
                    m_cur = jnp.max(s, axis=0, keepdims=True)
                    over = jnp.maximum(over, m_cur - ref)
                    ref_next = jnp.maximum(ref, m_cur)
                if pending is not None:
                    s_p, ref_p, ref_after, c_p = pending
                    p = jnp.exp2(s_p - ref_p)
                    alpha = jnp.exp2(ref_p - ref_after)
                    l = (l + jnp.sum(p, axis=0, keepdims=True)) * alpha
                    acc = (acc + _dot(vt_ref[c_p], p.astype(BF16))) * alpha
                if u < FLASH_GROUP:
                    pending = (s, ref, ref_next, c0 + u)
                    ref = ref_next
            l_scr[...] = l
            acc_scr[...] = acc
            over_scr[...] = over
            m_scr[...] = ref

        def fast_body(it, carry):
            lagged_group(1 + FLASH_GROUP * it)
            return carry
        lax.fori_loop(0, (n_chunks - 1) // FLASH_GROUP, fast_body, 0)

        @pl.when(jnp.max(over_scr[...]) > FLASH_MAX_OVERSHOOT)
        def _():
            reset()

            def exact_body(c, carry):
                exact(k_chunk(c), vt_ref[c])
                return carry
            lax.fori_loop(0, n_chunks, exact_body, 0)

    if mode == "gqa":
        kvh = pl.program_id(1)
        o = acc_scr[pl.ds(pl.multiple_of(kvh * HEAD_DIM, HEAD_DIM), HEAD_DIM), :] / l_scr[...]
        tq = cols // A_GROUP
        pieces = [o[:, g * tq:(g + 1) * tq] for g in range(A_GROUP)]
        o_ref[...] = jnp.concatenate(pieces, axis=0).T.astype(o_ref.dtype)
    else:
        lq = lam_ref[...]
        lam = (jnp.exp(jnp.sum(lq[0:1] * lq[1:2], axis=-1, keepdims=True))
               - jnp.exp(jnp.sum(lq[2:3] * lq[3:4], axis=-1, keepdims=True)) + lambda_init)
        o = acc_scr[...] / l_scr[...]
        d = o[:, :cols // 2] - lam * o[:, cols // 2:]
        y = d * lax.rsqrt(jnp.mean(d * d, axis=0, keepdims=True) + EPS) * g_ref[...]
        o_ref[...] = (y * (1.0 - lambda_init)).T.astype(o_ref.dtype)


def _flash(mode, q, k, vt, n_ctx, ctx_only, extra=(), lambda_init=0.0):
    B, NQ, T, _ = q.shape
    S = T - n_ctx
    tk = vt.shape[-1]
    n_chunks = vt.shape[-3]
    assert (n_chunks - 1) % FLASH_GROUP == 0
    n_maps = A_GROUP if mode == "gqa" else 2
    n_heads = NQ // n_maps
    if ctx_only:
        tq, n_q, q0, rows = n_ctx, 1, S // n_ctx, n_ctx
        k_rows, k0 = n_ctx, S // n_ctx
        vt_blk, vt0 = None, n_chunks - 1
    else:
        tq = FLASH_TQ if mode == "gqa" else 2 * FLASH_TQ
        n_q, q0, rows = S // tq, 0, S
        k_rows, k0 = T, 0
        vt_blk, vt0 = n_chunks, 0
    cols = n_maps * tq
    kern = functools.partial(_flash_kernel, mode=mode, cols=cols, tk=tk, n_ctx=n_ctx,
                             n_chunks=n_chunks, ctx_only=ctx_only, lambda_init=lambda_init)
    in_specs = [pl.BlockSpec((None, n_maps, tq, LANES), lambda b, h, i: (b, h, i + q0, 0))]
    if mode == "gqa":
        in_specs += [pl.BlockSpec((None, k_rows, LANES), lambda b, h, i: (b, k0, 0)),
                     pl.BlockSpec((None, vt_blk, LANES, tk), lambda b, h, i: (b, vt0, 0, 0))]
        out_spec = pl.BlockSpec((None, tq, A_GROUP * HEAD_DIM), lambda b, h, i: (b, i, h))
    else:
        in_specs += [pl.BlockSpec((None, k_rows, LANES), lambda b, h, i: (b, k0, h)),
                     pl.BlockSpec((None, None, vt_blk, LANES, tk), lambda b, h, i: (b, h, vt0, 0, 0)),
                     pl.BlockSpec((4, HEAD_DIM), lambda b, h, i: (0, 0)),
                     pl.BlockSpec((LANES, 1), lambda b, h, i: (0, 0))]
        out_spec = pl.BlockSpec((None, tq, LANES), lambda b, h, i: (b, i, h))
    return pl.pallas_call(
        kern,
        out_shape=jax.ShapeDtypeStruct((B, rows, BR_W), BF16),
        grid=(B, n_heads, n_q),
        in_specs=in_specs,
        out_specs=out_spec,
        scratch_shapes=[pltpu.VMEM((1, cols), F32)] * 3 + [pltpu.VMEM((LANES, cols), F32)],
        compiler_params=_cparams(("arbitrary", "arbitrary", "arbitrary")),
        name="flash_" + mode + ("_ctx" if ctx_only else ""),
    )(q, k, vt, *extra)


def _dn_prep_kernel(q_ref, k_ref, v_ref, ql_ref, kl_ref, vl_ref, qr_ref, kr_ref, vr_ref, ab_ref,
                    wc_ref, alog_ref, dtb_ref, mh_ref, eg_ref, eb_ref, tm_ref, ones_ref,
                    qo_ref, ko_ref, kb_ref, vbk_ref, qdec_ref, kdecbd_ref, decbd_ref, egl_ref,
                    *, tm):
    i = pl.program_id(1)
    nblk = pl.num_programs(1)
    W = DN_HEADS * DN_DK
    row = lax.broadcasted_iota(jnp.int32, (tm, W), 0)
    col = lax.broadcasted_iota(jnp.int32, (tm, W), 1)
    left_ok = jnp.logical_and(i >= 1, i < nblk - 1).astype(F32)
    right_ok = (i < nblk - 2).astype(F32)

    def conv_silu(x_ref, xl_ref, xr_ref, c0):
        x = x_ref[...].astype(F32)
        prev_row = xl_ref[7:8, :].astype(F32) * left_ok
        next_row = xr_ref[0:1, :].astype(F32) * right_ok
        x_prev = jnp.where(row == 0, prev_row, pltpu.roll(x, 1, 0))
        x_next = jnp.where(row == tm - 1, next_row, pltpu.roll(x, tm - 1, 0))
        y = (x_prev * wc_ref[0:1, c0:c0 + W] + x * wc_ref[1:2, c0:c0 + W]
             + x_next * wc_ref[2:3, c0:c0 + W])
        return _silu(y)

    def l2norm(x):
        ss = _dot3_l(x * x, mh_ref[...])
        return x * lax.rsqrt(ss + EPS)

    q = l2norm(conv_silu(q_ref, ql_ref, qr_ref, 0)) * (DN_DK ** -0.5)
    k = l2norm(conv_silu(k_ref, kl_ref, kr_ref, W))
    v = conv_silu(v_ref, vl_ref, vr_ref, 2 * W)

    ab = ab_ref[...]
    z = ab + dtb_ref[...]
    softplus = jnp.maximum(z, 0.0) + jnp.log(1.0 + jnp.exp(-jnp.abs(z)))
    g_all = -jnp.exp(alog_ref[...]) * softplus
    beta_all = jax.nn.sigmoid(ab)

    t_in = row & (DN_CHUNK - 1)
    s_in = col & (DN_CHUNK - 1)

    def put(ref, d, val):
        for h in range(DN_HEADS):
            piece = val[:, h * DN_DK:(h + 1) * DN_DK]
            if d is None:
                ref[h] = piece.astype(ref.dtype)
            else:
                ref[d, h] = piece.astype(ref.dtype)

    lo = lax.broadcasted_iota(jnp.int32, (tm, LANES), 1) < DN_DK
    n_chunks = tm // DN_CHUNK
    bd_shape = (tm, n_chunks * DN_CHUNK)
    same_chunk = (lax.broadcasted_iota(jnp.int32, bd_shape, 0) // DN_CHUNK
                  == lax.broadcasted_iota(jnp.int32, bd_shape, 1) // DN_CHUNK)

    def pair(a, b, h):
        sa = a[:, (h // 2) * LANES:(h // 2 + 1) * LANES]
        sb = b[:, (h // 2) * LANES:(h // 2 + 1) * LANES]
        if h % 2 == 0:
            return jnp.where(lo, sa, pltpu.roll(sb, DN_DK, 1))
        return jnp.where(lo, pltpu.roll(sa, DN_DK, 1), sb)

    def put_block_diag(ref, d, val):
        for h in range(DN_HEADS):
            x = pair(val, val, h)
            ref[d, h] = jnp.where(same_chunk, jnp.concatenate([x] * (n_chunks // 2), axis=1), 0.0).astype(ref.dtype)

    put(qo_ref, None, q)
    put(ko_ref, None, k)
    for d in range(2):
        g_e = _dot3_l(g_all, eg_ref[d])
        beta_e = _dot3_l(beta_all, eb_ref[d])
        gc = _dot3_r(tm_ref[d], g_e)
        g_tot = _dot3_r(ones_ref[...], g_e)
        strict = (t_in > s_in) if d == 0 else (t_in < s_in)
        causal = (t_in >= s_in) if d == 0 else (t_in <= s_in)
        diff = _dot3_r(tm_ref[d], jnp.where(strict, g_e, 0.0))
        decay = jnp.where(causal, jnp.exp(diff), 0.0)
        e_g = jnp.exp(gc)
        kb = k * beta_e
        put(kb_ref, d, kb)
        vb, kbg = v * beta_e, kb * e_g
        for h in range(DN_HEADS):
            vbk_ref[d, h] = pair(vb, kbg, h).astype(vbk_ref.dtype)
        put(qdec_ref, d, q * e_g)
        put_block_diag(kdecbd_ref, d, k * jnp.exp(g_tot - gc))
        put_block_diag(decbd_ref, d, decay)
        e_tot = jnp.exp(g_tot)
        for h in range(DN_HEADS):
            for c in range(tm // DN_CHUNK):
                egl_ref[d, h, c:c + 1, :] = e_tot[c * DN_CHUNK:c * DN_CHUNK + 1, h * DN_DK:(h + 1) * DN_DK]


def _dn_consts():
    n = DN_BLOCK
    t = np.arange(n)
    same = (t[:, None] // DN_CHUNK) == (t[None, :] // DN_CHUNK)
    pre = same & (t[None, :] <= t[:, None])
    suf = same & (t[None, :] >= t[:, None])
    tmat = np.stack([pre, suf]).astype(np.float32)
    ones = same.astype(np.float32)
    W = DN_HEADS * DN_DK
    eg = np.zeros((2, LANES, W), np.float32)
    eb = np.zeros((2, LANES, W), np.float32)
    for d in range(2):
        for h in range(DN_HEADS):
            eg[d, d * DN_HEADS + h, h * DN_DK:(h + 1) * DN_DK] = 1.0
            eb[d, 2 * DN_HEADS + d * DN_HEADS + h, h * DN_DK:(h + 1) * DN_DK] = 1.0
    mh = np.kron(np.eye(DN_HEADS), np.ones((DN_DK, DN_DK))).astype(np.float32)
    return tuple(jnp.asarray(a, BF16) for a in (mh, eg, eb, tmat, ones))


def _dn_prep(P, AB, w_conv, a_log, dt_bias):
    B, T, _ = P.shape
    tm = DN_BLOCK
    nblk = T // tm
    W = DN_HEADS * DN_DK
    mh, eg, eb, tmat, ones = _dn_consts()
    alog_row = jnp.zeros((1, LANES), F32).at[0, :2 * DN_HEADS].set(a_log.reshape(-1))
    dtb_row = jnp.zeros((1, LANES), F32).at[0, :2 * DN_HEADS].set(dt_bias.reshape(-1))
    r8 = tm // 8

    def main(c0):
        return pl.BlockSpec((None, tm, W), lambda b, i: (b, i, c0 // W))

    def left(c0):
        return pl.BlockSpec((None, 8, W), lambda b, i: (b, jnp.maximum(i * r8 - 1, 0), c0 // W))

    def right(c0):
        return pl.BlockSpec((None, 8, W), lambda b, i: (b, jnp.minimum((i + 1) * r8, nblk * r8 - 1), c0 // W))

    def full(a):
        nd = a.ndim
        return pl.BlockSpec(a.shape, lambda b, i: (0,) * nd)

    hm = pl.BlockSpec((None, DN_HEADS, tm, DN_DK), lambda b, i: (b, 0, i, 0))

    def hm2(width):
        return pl.BlockSpec((2, None, DN_HEADS, tm, width), lambda b, i: (0, b, 0, i, 0))

    def sh_hm2(width):
        return jax.ShapeDtypeStruct((2, B, DN_HEADS, T, width), BF16)

    sh_hm = jax.ShapeDtypeStruct((B, DN_HEADS, T, DN_DK), BF16)
    return pl.pallas_call(
        functools.partial(_dn_prep_kernel, tm=tm),
        out_shape=(sh_hm, sh_hm, sh_hm2(DN_DK), sh_hm2(2 * DN_DK), sh_hm2(DN_DK), sh_hm2(tm), sh_hm2(tm),
                   jax.ShapeDtypeStruct((2, B, nblk, DN_HEADS, tm // DN_CHUNK, DN_DK), F32)),
        grid=(B, nblk),
        in_specs=[main(COL_BQ), main(COL_BK), main(COL_BV),
                  left(COL_BQ), left(COL_BK), left(COL_BV),
                  right(COL_BQ), right(COL_BK), right(COL_BV),
                  pl.BlockSpec((None, tm, LANES), lambda b, i: (b, i, 0)),
                  full(w_conv), full(alog_row), full(dtb_row), full(mh), full(eg), full(eb), full(tmat),
                  full(ones)],
        out_specs=(hm, hm, hm2(DN_DK), hm2(2 * DN_DK), hm2(DN_DK), hm2(tm), hm2(tm),
                   pl.BlockSpec((2, None, None, DN_HEADS, tm // DN_CHUNK, DN_DK),
                                lambda b, i: (0, b, i, 0, 0, 0))),
        compiler_params=_cparams(("arbitrary", "arbitrary")),
        name="dn_prep",
    )(P, P, P, P, P, P, P, P, P, AB, w_conv, alog_row, dtb_row, mh, eg, eb, tmat, ones)


def _dn_chunk_kernel(q_ref, k_ref, kb_ref, vbk_ref, qdec_ref, kdecbd_ref, decbd_ref,
                     qeff_ref, o0_ref, mk_ref, n_ref):
    n, C = DN_BLOCK, DN_CHUNK
    nc = n // C
    r = lax.broadcasted_iota(jnp.int32, (n, n), 0)
    c = lax.broadcasted_iota(jnp.int32, (n, n), 1)
    off_diag = r != c
    same_chunk = (r // C) == (c // C)
    rp = lax.broadcasted_iota(jnp.int32, (C, n), 0)
    cp = lax.broadcasted_iota(jnp.int32, (C, n), 1)
    eye_packed = (rp == cp % C).astype(F32)
    own_values = (lax.broadcasted_iota(jnp.int32, (n, 2 * nc * C), 0) // C
                  == lax.broadcasted_iota(jnp.int32, (n, 2 * nc * C), 1) // (2 * C))
    heads = range(DN_HEADS)

    def block_diag(xp):
        xb = xp.astype(BF16)
        return jnp.where(same_chunk, jnp.concatenate([xb] * nc, axis=0), jnp.zeros((), BF16))

    low, qk = [], []
    for h in heads:
        dec = decbd_ref[h].astype(F32)
        k = k_ref[h]
        a = jnp.where(off_diag, _dot_nt(kb_ref[h], k) * dec, 0.0)
        low.append(sum(a[i * C:(i + 1) * C] for i in range(nc)))
        qk.append((_dot_nt(q_ref[h], k) * dec).astype(BF16))
    p = [eye_packed - m for m in low]
    mpow = low
    for _ in range(5):
        mpow = [_dot(m.astype(BF16), block_diag(m)) for m in mpow]
        p = [a + _dot(a.astype(BF16), block_diag(m)) for a, m in zip(p, mpow)]
    uw = []
    for h in heads:
        vals = jnp.where(own_values, jnp.concatenate([vbk_ref[h]] * nc, axis=1), jnp.zeros((), BF16))
        packed = _dot(p[h].astype(BF16), vals)
        uw.append(jnp.concatenate([packed[:, i * 2 * C:(i + 1) * 2 * C] for i in range(nc)],
                                  axis=0).astype(BF16))
    qkuw = [_dot(qk[h], uw[h]) for h in heads]
    for h in heads:
        o0_ref[h] = qkuw[h][:, :DN_DK].astype(o0_ref.dtype)
        qeff_ref[h] = (qdec_ref[h].astype(F32) - qkuw[h][:, DN_DK:]).astype(qeff_ref.dtype)
        kt = _dot_tn(kdecbd_ref[h], uw[h])
        n_ref[h] = kt[:, :DN_DK].astype(n_ref.dtype)
        mk_ref[h] = kt[:, DN_DK:].astype(mk_ref.dtype)


def _dn_chunk(prep):
    q, k, kb, vbk, qdec, kdecbd, decbd, _ = prep
    B, H, T, dk = q.shape
    tm = DN_BLOCK
    hm = pl.BlockSpec((None, H, tm, dk), lambda d, b, i: (b, 0, i, 0))

    def hm2(width):
        return pl.BlockSpec((None, None, H, tm, width), lambda d, b, i: (d, b, 0, i, 0))

    def sh(dtype):
        return jax.ShapeDtypeStruct((2, B, H, T, dk), dtype)

    return pl.pallas_call(
        _dn_chunk_kernel,
        out_shape=(sh(BF16), sh(BF16), sh(BF16), sh(BF16)),
        grid=(2, B, T // tm),
        in_specs=[hm, hm, hm2(dk), hm2(2 * dk), hm2(dk), hm2(tm), hm2(tm)],
        out_specs=(hm2(dk), hm2(dk), hm2(dk), hm2(dk)),
        compiler_params=_cparams(("arbitrary", "arbitrary", "arbitrary")),
        name="dn_chunk",
    )(q, k, kb, vbk, qdec, kdecbd, decbd)


def _dn_scan_kernel(*refs, n_batch):
    ins = (refs[0:5], refs[5:10])
    o_refs = refs[10:12]
    s_scr = refs[12]
    t = pl.program_id(0)
    n_chunks = DN_BLOCK // DN_CHUNK
    C = DN_CHUNK

    @pl.when(t == 0)
    def _():
        s_scr[...] = jnp.zeros_like(s_scr)

    for step in range(n_chunks):
        for d in range(2):
            ci = step if d == 0 else n_chunks - 1 - step
            rows = slice(ci * C, (ci + 1) * C)
            qeff_ref, o0_ref, mk_ref, n_ref, egl_ref = ins[d]
            for b in range(n_batch):
                for h in range(DN_HEADS):
                    s = s_scr[d, b, h]
                    sb = s.astype(BF16)
                    o_refs[d][b, h, rows, :] = _dot(qeff_ref[b, h, rows, :], sb) + o0_ref[b, h, rows, :]
                    s_scr[d, b, h] = (s * egl_ref[b, h, ci:ci + 1, :] - _dot(mk_ref[b, h, rows, :], sb)
                                      + n_ref[b, h, rows, :])


def _dn_scan(maps, egl):
    qeff, o0, mk, nn = maps
    _, B, H, T, dk = qeff.shape
    tm = DN_BLOCK
    nblk = T // tm

    def blk(d, t):
        return jnp.where(t == 0, nblk - 1, nblk - 1 - t if d == 1 else t - 1)

    in_specs, args = [], []
    for d in range(2):
        for a in (qeff, o0, mk, nn):
            in_specs.append(pl.BlockSpec((None, B, H, tm, dk), lambda t, d=d: (d, 0, 0, blk(d, t), 0)))
            args.append(a)
        in_specs.append(pl.BlockSpec((None, B, None, H, tm // DN_CHUNK, dk),
                                     lambda t, d=d: (d, 0, blk(d, t), 0, 0, 0)))
        args.append(egl)
    out_specs = tuple(pl.BlockSpec((B, H, tm, dk), lambda t, d=d: (0, 0, blk(d, t), 0)) for d in range(2))
    return pl.pallas_call(
        functools.partial(_dn_scan_kernel, n_batch=B),
        out_shape=(jax.ShapeDtypeStruct((B, H, T, dk), F32),) * 2,
        grid=(nblk,),
        in_specs=in_specs,
        out_specs=out_specs,
        scratch_shapes=[pltpu.VMEM((2, B, H, dk, dk), F32)],
        compiler_params=_cparams(("arbitrary",)),
        name="dn_scan",
    )(*args)


def _dn_post_kernel(of_ref, ob_ref, z_ref, g_ref, o_ref):
    z = z_ref[...].astype(F32)
    for h in range(DN_HEADS):
        o = of_ref[h] + ob_ref[h]
        y = o * lax.rsqrt(jnp.mean(o * o, axis=-1, keepdims=True) + EPS) * g_ref[...]
        zz = z[:, h * DN_DK:(h + 1) * DN_DK]
        o_ref[:, h * DN_DK:(h + 1) * DN_DK] = (y * _silu(zz)).astype(o_ref.dtype)


def _dn_post(o_f, o_b, P, g_out):
    B, H, T, dk = o_f.shape
    tm = DN_BLOCK
    hm = pl.BlockSpec((None, H, tm, dk), lambda b, i: (b, 0, i, 0))
    return pl.pallas_call(
        _dn_post_kernel,
        out_shape=jax.ShapeDtypeStruct((B, T, BR_W), F32),
        grid=(B, T // tm),
        in_specs=[hm, hm,
                  pl.BlockSpec((None, tm, BR_W), lambda b, i: (b, i, COL_BZ // BR_W)),
                  pl.BlockSpec((1, dk), lambda b, i: (0, 0))],
        out_specs=pl.BlockSpec((None, tm, BR_W), lambda b, i: (b, i, 0)),
        compiler_params=_cparams(("arbitrary", "arbitrary")),
        name="dn_post",
    )(o_f, o_b, P, g_out.reshape(1, dk))


def _merge_kernel(a_ref, b_ref, c_ref, ga_ref, gb_ref, gc_ref, h_ref, mod_ref, g2_ref, wb_ref, wo_ref,
                  h_out_ref, v_out_ref, *, tm, n_lat):
    i = pl.program_id(1)
    D = h_ref.shape[-1]
    m = jax.nn.sigmoid(ga_ref[...].astype(F32)) * _dot(a_ref[...], wb_ref[0])
    m = m + jax.nn.sigmoid(gb_ref[...].astype(F32)) * _dot(b_ref[...].astype(BF16), wb_ref[1])
    m = m + jax.nn.sigmoid(gc_ref[...].astype(F32)) * _dot(c_ref[...], wb_ref[2])
    y = _dot(m.astype(BF16), wo_ref[...])
    gate = jnp.where(_is_ctx_row(i, tm, n_lat), mod_ref[0:1, 2 * D:3 * D], mod_ref[1:2, 2 * D:3 * D])
    h_new = h_ref[...] + gate * y
    h_out_ref[...] = h_new
    v_out_ref[...] = _norm_mod(h_new, g2_ref[...], mod_ref, i, tm, n_lat, 3, 4).astype(BF16)


def _merge(br_a, br_b, br_c, P, h, modtab, g_norm2, w_branch, w_out, n_lat, tm):
    B, T, D = h.shape
    br = pl.BlockSpec((None, tm, BR_W), lambda b, i: (b, i, 0))

    def gate(j):
        return pl.BlockSpec((None, tm, D), lambda b, i: (b, i, COL_GATE // D + j))

    tok = pl.BlockSpec((None, tm, D), lambda b, i: (b, i, 0))
    return pl.pallas_call(
        functools.partial(_merge_kernel, tm=tm, n_lat=n_lat),
        out_shape=(jax.ShapeDtypeStruct((B, T, D), F32), jax.ShapeDtypeStruct((B, T, D), BF16)),
        grid=(B, T // tm),
        in_specs=[br, br, br, gate(0), gate(1), gate(2), tok,
                  pl.BlockSpec((None, 2, 6 * D), lambda b, i: (b, 0, 0)),
                  pl.BlockSpec((1, D), lambda b, i: (0, 0)),
                  pl.BlockSpec((3, BR_W, D), lambda b, i: (0, 0, 0)),
                  pl.BlockSpec((D, D), lambda b, i: (0, 0))],
        out_specs=(tok, tok),
        compiler_params=_cparams(("arbitrary", "arbitrary")),
        name="merge",
    )(br_a, br_b, br_c, P, P, P, h, modtab, g_norm2.reshape(1, D), w_branch, w_out)


def _router_kernel(v_ref, wr_ref, br_ref, perm_ref, c_ref):
    tn = v_ref.shape[0]
    G = N_GROUPS
    scores = jax.nn.sigmoid(_dot_nt(wr_ref[...], v_ref[...]))
    sel = scores + br_ref[...]
    slabs = [sel[j * G:(j + 1) * G] for j in range(GROUP_SIZE)]
    m1, m2 = slabs[0], jnp.full((G, tn), -jnp.inf, F32)
    for j in range(1, GROUP_SIZE):
        m2 = jnp.maximum(m2, jnp.minimum(m1, slabs[j]))
        m1 = jnp.maximum(m1, slabs[j])
    gs = m1 + m2
    gidx = lax.broadcasted_iota(jnp.int32, (G, tn), 0)
    grank = jnp.zeros((G, tn), jnp.int32)
    for g in range(G):
        rowv = gs[g:g + 1]
        beats = jnp.where(rowv > gs, 1, jnp.where(jnp.logical_and(rowv == gs, g < gidx), 1, 0))
        grank = grank + beats
    gmask = grank < TOPK_GROUPS
    masked = jnp.concatenate([jnp.where(gmask, s, -jnp.inf) for s in slabs], axis=0)
    pos = lax.broadcasted_iota(jnp.int32, (N_EXPERTS, tn), 0)
    orig = (pos % G) * GROUP_SIZE + pos // G
    w = jnp.zeros((N_EXPERTS, tn), F32)
    for _ in range(TOP_K):
        best = jnp.max(masked, axis=0, keepdims=True)
        first = jnp.min(jnp.where(masked == best, orig, N_EXPERTS), axis=0, keepdims=True)
        pick = orig == first
        w = jnp.where(pick, scores, w)
        masked = jnp.where(pick, -jnp.inf, masked)
    denom = jnp.sum(w, axis=0, keepdims=True) + 1e-20
    c = w / denom * ROUTED_SCALE
    c_t = sum(_dot_tn(part, perm_ref[...]) for part in _split3(c))
    for g in range(N_EXPERTS // MOE_EPS):
        c_ref[g] = c_t[:, g * MOE_EPS:(g + 1) * MOE_EPS]


def _router(v_flat, w_router, b_router):
    N, D = v_flat.shape
    tn = 1280 if N % 1280 == 0 else 256
    perm = np.array([(p % N_GROUPS) * GROUP_SIZE + p // N_GROUPS for p in range(N_EXPERTS)])
    wr_t = w_router.T[perm].astype(BF16)
    br = b_router[perm].reshape(N_EXPERTS, 1)
    unperm = jnp.asarray(np.eye(N_EXPERTS)[perm], BF16)
    ngrp = N_EXPERTS // MOE_EPS
    return pl.pallas_call(
        _router_kernel,
        out_shape=jax.ShapeDtypeStruct((ngrp, N, MOE_EPS), F32),
        grid=(N // tn,),
        in_specs=[pl.BlockSpec((tn, D), lambda i: (i, 0)),
                  pl.BlockSpec((N_EXPERTS, D), lambda i: (0, 0)),
                  pl.BlockSpec((N_EXPERTS, 1), lambda i: (0, 0)),
                  pl.BlockSpec((N_EXPERTS, N_EXPERTS), lambda i: (0, 0))],
        out_specs=pl.BlockSpec((ngrp, tn, MOE_EPS), lambda i: (0, i, 0)),
        compiler_params=_cparams(("arbitrary",)),
        name="router",
    )(v_flat, wr_t, br, unperm)


MOE_EPS = 4


def _moe_kernel(x_ref, c_ref, wgu_ref, wd_ref, sgu_ref, sd_ref, h_ref, mod_ref,
                o_ref, acc_ref, *, tm, n_lat):
    i = pl.program_id(1)
    e = pl.program_id(2)
    x = x_ref[...]
    D = x.shape[-1]
    F = MOE_FF

    @pl.when(e == 0)
    def _():
        gu = _dot(x, sgu_ref[...])
        hs = _silu(gu[:, :F]) * gu[:, F:]
        acc_ref[...] = _dot(hs.astype(BF16), sd_ref[...])

    c = c_ref[...]
    parts = []
    for j in range(MOE_EPS):
        gu = _dot(x, wgu_ref[j])
        hj = _silu(gu[:, :F]) * gu[:, F:] * c[:, j:j + 1]
        parts.append(hj.astype(BF16))
    hcat = jnp.concatenate(parts, axis=-1)
    acc_ref[...] += _dot(hcat, wd_ref[...])

    @pl.when(e == pl.num_programs(2) - 1)
    def _():
        gate = jnp.where(_is_ctx_row(i, tm, n_lat), mod_ref[0:1, 5 * D:6 * D], mod_ref[1:2, 5 * D:6 * D])
        o_ref[...] = h_ref[...] + gate * acc_ref[...]


def _moe_weights(w_gate, w_up, w_down):
    E, D, F = w_gate.shape
    gu = jnp.concatenate([w_gate, w_up], axis=-1).astype(BF16)
    return gu, w_down.astype(BF16).reshape(E * F, D)


def _moe(v, c_grp, h, modtab, wgu, wd, sgu, sd, n_lat, tm):
    B, T, D = v.shape
    F = MOE_FF
    ngrp = N_EXPERTS // MOE_EPS
    tok = pl.BlockSpec((None, tm, D), lambda b, i, e: (b, i, 0))
    return pl.pallas_call(
        functools.partial(_moe_kernel, tm=tm, n_lat=n_lat),
        out_shape=jax.ShapeDtypeStruct((B, T, D), F32),
        grid=(B, T // tm, ngrp),
        in_specs=[tok,
                  pl.BlockSpec((None, None, tm, MOE_EPS), lambda b, i, e: (e, b, i, 0)),
                  pl.BlockSpec((MOE_EPS, D, 2 * F), lambda b, i, e: (e, 0, 0)),
                  pl.BlockSpec((MOE_EPS * F, D), lambda b, i, e: (e, 0)),
                  pl.BlockSpec((D, 2 * F), lambda b, i, e: (0, 0)),
                  pl.BlockSpec((F, D), lambda b, i, e: (0, 0)),
                  tok,
                  pl.BlockSpec((None, 2, 6 * D), lambda b, i, e: (b, 0, 0))],
        out_specs=tok,
        scratch_shapes=[pltpu.VMEM((tm, D), F32)],
        compiler_params=_cparams(("arbitrary", "arbitrary", "arbitrary")),
        name="moe",
    )(v, c_grp, wgu, wd, sgu, sd, h, modtab)


def _final_kernel(h_ref, g_ref, o_ref):
    x = h_ref[...]
    o_ref[...] = x * lax.rsqrt(jnp.mean(x * x, axis=-1, keepdims=True) + EPS) * g_ref[...]


def _final_norm(h, g_final, S):
    B, T, D = h.shape
    tm = 1024
    return pl.pallas_call(
        _final_kernel,
        out_shape=jax.ShapeDtypeStruct((B, S, D), F32),
        grid=(B, S // tm),
        in_specs=[pl.BlockSpec((None, tm, D), lambda b, i: (b, i, 0)),
                  pl.BlockSpec((1, D), lambda b, i: (0, 0))],
        out_specs=pl.BlockSpec((None, tm, D), lambda b, i: (b, i, 0)),
        compiler_params=_cparams(("arbitrary", "arbitrary")),
        name="final_norm",
    )(h, g_final.reshape(1, D))


def _rope_tables(S, n_ctx):
    rows = S // GRID_W
    row = jnp.repeat(jnp.arange(rows, dtype=F32), GRID_W)
    col = (jnp.arange(S) % GRID_W).astype(F32)
    axis_dim = HEAD_DIM // 2
    inv = 1.0 / (ROPE_THETA ** (jnp.arange(0, axis_dim, 2, dtype=F32) / axis_dim))
    ang = jnp.concatenate([row[:, None] * inv, col[:, None] * inv], axis=-1)
    ang = jnp.concatenate([ang, ang], axis=-1)
    cos = jnp.concatenate([jnp.cos(ang), jnp.ones((n_ctx, HEAD_DIM), F32)], axis=0)
    sin = jnp.concatenate([jnp.sin(ang), jnp.zeros((n_ctx, HEAD_DIM), F32)], axis=0)
    first = (jnp.arange(HEAD_DIM) < HEAD_DIM // 2)[None, :]
    sin_a = jnp.where(first, -sin, 0.0)
    sin_b = jnp.where(first, 0.0, sin)
    return tuple(jnp.tile(t, (1, 2)) for t in (cos, sin_a, sin_b))


def _permute_w_in(w):
    splits = np.cumsum([512, 128, 128, 1536, 512, 16, 16, 512, 512, 512, 3072])[:-1].tolist()
    aq, ak, av, bqkv, bz, ba, bb, cq, ck, cv, gate = jnp.split(w, splits, axis=-1)
    pad = jnp.zeros((w.shape[0], IN_W_PAD - COL_AB - 32), w.dtype)
    out = jnp.concatenate([aq, cq, ck, cv, bz, bqkv, gate, ak, av, ba, bb, pad], axis=-1)
    return out.astype(BF16)


def kernel(x, c, ctx, c_ctx, w_mod, b_mod, g_norm1, g_norm2, w_in, g_qnorm, g_knorm, w_conv, a_log, dt_bias, g_dn_out, lam_qk, g_subln, w_branch, w_out, w_router, b_router, w_e_gate, w_e_up, w_e_down, w_s_gate, w_s_up, w_s_down, g_final):
    B, S, D = x.shape
    n_ctx = ctx.shape[1]
    T = n_ctx + S
    L = w_mod.shape[0]
    N = B * T
    assert n_ctx == DN_BLOCK and S % 1024 == 0 and D == D_MODEL
    tm_tok = 1280 if T % 1280 == 0 else 256
    tm_merge = 640 if T % 640 == 0 else 256

    cond = jnp.zeros((8, D), F32).at[0].set(c_ctx).at[1:1 + B].set(c)
    mod = _mod_vectors(cond, w_mod, b_mod)
    tabs = _rope_tables(S, n_ctx)
    h = jnp.concatenate([x, ctx], axis=1)
    tk = 1280 if T % 1280 == 0 else 256

    for l in range(L):
        lambda_init = 0.8 - 0.6 * math.exp(-0.3 * l)
        modtab = jnp.stack([jnp.broadcast_to(mod[l, 0], (B, 6 * D)), mod[l, 1:1 + B]], axis=1)
        P, AB = _in_projection(h, modtab, g_norm1[l], _permute_w_in(w_in[l]), S, tm_tok)

        qa, ka, vta, qd, kd, vtd = _attn_prep(P, tabs, g_qnorm[l], g_knorm[l], tk)
        dx = (lam_qk[l], g_subln[l].reshape(LANES, 1))
        br_a = jnp.concatenate([_flash("gqa", qa, ka, vta, n_ctx, False),
                                _flash("gqa", qa, ka, vta, n_ctx, True)], axis=1)
        br_c = jnp.concatenate([_flash("diff", qd, kd, vtd, n_ctx, False, dx, lambda_init),
                                _flash("diff", qd, kd, vtd, n_ctx, True, dx, lambda_init)], axis=1)

        prep = _dn_prep(P, AB, w_conv[l], a_log[l], dt_bias[l])
        o_f, o_b = _dn_scan(_dn_chunk(prep), prep[-1])
        br_b = _dn_post(o_f, o_b, P, g_dn_out[l])

        h, v = _merge(br_a, br_b, br_c, P, h, modtab, g_norm2[l], w_branch[l].astype(BF16),
                      w_out[l].astype(BF16), S, tm_merge)

        c_grp = _router(v.reshape(N, D), w_router[l], b_router[l]).reshape(-1, B, T, MOE_EPS)
        wgu, wd = _moe_weights(w_e_gate[l], w_e_up[l], w_e_down[l])
        sgu = jnp.concatenate([w_s_gate[l], w_s_up[l]], axis=-1).astype(BF16)
        h = _moe(v, c_grp, h, modtab, wgu, wd, sgu, w_s_down[l].astype(BF16), S, tm_merge)

    return _final_norm(h, g_final, S)
```

```python
import functools
import math

import numpy as np
import jax
import jax.numpy as jnp
from jax import lax
from jax.experimental import pallas as pl
from jax.experimental.pallas import tpu as pltpu

F32 = jnp.float32
BF16 = jnp.bfloat16

D_MODEL = 1024
GRID_W = 64
EPS = 1e-6
ROPE_THETA = 10000.0
HEAD_DIM = 64
LANES = 128
A_HEADS = 8
A_KV_HEADS = 2
A_GROUP = A_HEADS // A_KV_HEADS
DN_HEADS = 8
DN_DK = 64
DN_CHUNK = 64
DN_BLOCK = 256
DF_HEADS = 4
N_EXPERTS = 64
TOP_K = 6
N_GROUPS = 8
TOPK_GROUPS = 4
GROUP_SIZE = N_EXPERTS // N_GROUPS
MOE_FF = 256
ROUTED_SCALE = 2.5
BR_W = 512

COL_AQ, COL_CQ, COL_CK, COL_CV, COL_BZ = 0, 512, 1024, 1536, 2048
COL_BQ, COL_BK, COL_BV = 2560, 3072, 3584
COL_GATE = 4096
COL_AK, COL_AV, COL_AB = 7168, 7296, 7424
IN_W_PAD = 7680
IN_TN = 1280

VMEM_LIMIT = 56 * 1024 * 1024


def _cparams(sem):
    return pltpu.CompilerParams(dimension_semantics=sem, vmem_limit_bytes=VMEM_LIMIT)


def _dot(a, b):
    return jnp.dot(a, b, preferred_element_type=F32)


def _dot_nt(a, b):
    return lax.dot_general(a, b, (((1,), (1,)), ((), ())), preferred_element_type=F32)


def _dot_tn(a, b):
    return lax.dot_general(a, b, (((0,), (0,)), ((), ())), preferred_element_type=F32)


def _split3(x):
    hi = x.astype(BF16)
    r = x - hi.astype(F32)
    mid = r.astype(BF16)
    lo = (r - mid.astype(F32)).astype(BF16)
    return hi, mid, lo


def _dot3_l(x, m):
    hi, mid, lo = _split3(x)
    return _dot(hi, m) + _dot(mid, m) + _dot(lo, m)


def _dot3_r(m, x):
    hi, mid, lo = _split3(x)
    return _dot(m, hi) + _dot(m, mid) + _dot(m, lo)


def _silu(x):
    return x * jax.nn.sigmoid(x)


def _mod_kernel(cond_ref, w_ref, b_ref, o_ref):
    a = _silu(cond_ref[...]).astype(BF16)
    o_ref[...] = _dot(a, w_ref[...].astype(BF16)) + b_ref[...]


def _mod_vectors(cond, w_mod, b_mod):
    L, D, W = w_mod.shape
    tn = 1024
    return pl.pallas_call(
        _mod_kernel,
        out_shape=jax.ShapeDtypeStruct((L, 8, W), F32),
        grid=(L, W // tn),
        in_specs=[pl.BlockSpec((8, D), lambda l, j: (0, 0)),
                  pl.BlockSpec((None, D, tn), lambda l, j: (l, 0, j)),
                  pl.BlockSpec((None, 1, tn), lambda l, j: (l, 0, j))],
        out_specs=pl.BlockSpec((None, 8, tn), lambda l, j: (l, 0, j)),
        compiler_params=_cparams(("arbitrary", "arbitrary")),
        name="mod_vectors",
    )(cond, w_mod, b_mod.reshape(L, 1, W))


def _is_ctx_row(blk, tm, n_lat):
    return blk * tm + lax.broadcasted_iota(jnp.int32, (tm, 1), 0) >= n_lat


def _norm_mod(x, g, mod_ref, blk, tm, n_lat, sh, sc):
    D = x.shape[-1]
    y = x * lax.rsqrt(jnp.mean(x * x, axis=-1, keepdims=True) + EPS) * g
    is_ctx = _is_ctx_row(blk, tm, n_lat)
    scale = jnp.where(is_ctx, mod_ref[0:1, sc * D:(sc + 1) * D], mod_ref[1:2, sc * D:(sc + 1) * D])
    shift = jnp.where(is_ctx, mod_ref[0:1, sh * D:(sh + 1) * D], mod_ref[1:2, sh * D:(sh + 1) * D])
    return y * (1.0 + scale) + shift


def _inproj_kernel(h_ref, mod_ref, g_ref, w_ref, p_ref, ab_ref, u_scr, *, tm, n_lat, ab_off):
    i = pl.program_id(1)
    j = pl.program_id(2)

    @pl.when(j == 0)
    def _():
        u_scr[...] = _norm_mod(h_ref[...], g_ref[...], mod_ref, i, tm, n_lat, 0, 1).astype(BF16)

    r = _dot(u_scr[...], w_ref[...])
    p_ref[...] = r.astype(BF16)

    @pl.when(j == pl.num_programs(2) - 1)
    def _():
        ab_ref[...] = r[:, ab_off:ab_off + LANES]


def _in_projection(h, modtab, g_norm, w_perm, n_lat, tm):
    B, T, D = h.shape
    ncol = IN_W_PAD // IN_TN
    kern = functools.partial(_inproj_kernel, tm=tm, n_lat=n_lat, ab_off=COL_AB - (ncol - 1) * IN_TN)
    return pl.pallas_call(
        kern,
        out_shape=(jax.ShapeDtypeStruct((B, T, IN_W_PAD), BF16),
                   jax.ShapeDtypeStruct((B, T, LANES), F32)),
        grid=(B, T // tm, ncol),
        in_specs=[pl.BlockSpec((None, tm, D), lambda b, i, j: (b, i, 0)),
                  pl.BlockSpec((None, 2, 6 * D), lambda b, i, j: (b, 0, 0)),
                  pl.BlockSpec((1, D), lambda b, i, j: (0, 0)),
                  pl.BlockSpec((D, IN_TN), lambda b, i, j: (0, j))],
        out_specs=(pl.BlockSpec((None, tm, IN_TN), lambda b, i, j: (b, i, j)),
                   pl.BlockSpec((None, tm, LANES), lambda b, i, j: (b, i, 0))),
        scratch_shapes=[pltpu.VMEM((tm, D), BF16)],
        compiler_params=_cparams(("arbitrary", "arbitrary", "arbitrary")),
        name="in_projection",
    )(h, modtab, g_norm.reshape(1, D), w_perm)


def _rope(x, cos, sin_a, sin_b):
    return x * cos + pltpu.roll(x, LANES - 32, 1) * sin_a + pltpu.roll(x, 32, 1) * sin_b


def _attn_prep_kernel(aq_ref, ak_ref, av_ref, cq_ref, ck_ref, cv_ref, cos_ref, sa_ref, sb_ref, gq_ref, gk_ref,
                      m_ref, qa_ref, ka_ref, vta_ref, qd_ref, kd_ref, vtd_ref):
    cos, sa, sb = cos_ref[...], sa_ref[...], sb_ref[...]
    vta_ref[...] = av_ref[...].astype(F32).T.astype(BF16)
    for h in range(DF_HEADS):
        vtd_ref[h] = cv_ref[:, h * LANES:(h + 1) * LANES].astype(F32).T.astype(BF16)
    m64 = m_ref[...]
    half = lax.broadcasted_iota(jnp.int32, cos.shape, 1) // HEAD_DIM
    scale = HEAD_DIM ** -0.5 * math.log2(math.e)

    def head_norm(x, g):
        ms = _dot3_l(x * x, m64) * (1.0 / HEAD_DIM)
        return x * lax.rsqrt(ms + EPS) * g

    for s in range(A_HEADS // 2):
        x = aq_ref[:, s * LANES:(s + 1) * LANES].astype(F32)
        y = _rope(head_norm(x, gq_ref[...]), cos, sa, sb) * scale
        y_sw = pltpu.roll(y, HEAD_DIM, 1)
        kvh = (2 * s) // A_GROUP
        for hh in range(2):
            src = y if hh == kvh else y_sw
            qa_ref[2 * s + hh] = jnp.where(half == kvh, src, 0.0).astype(BF16)
    xk = ak_ref[...].astype(F32)
    ka_ref[...] = _rope(head_norm(xk, gk_ref[...]), cos, sa, sb).astype(BF16)
    for h in range(DF_HEADS):
        x = cq_ref[:, h * LANES:(h + 1) * LANES].astype(F32)
        y = _rope(x, cos, sa, sb) * scale
        qd_ref[2 * h] = jnp.where(half == 0, y, 0.0).astype(BF16)
        qd_ref[2 * h + 1] = jnp.where(half == 1, y, 0.0).astype(BF16)
        xk = ck_ref[:, h * LANES:(h + 1) * LANES].astype(F32)
        kd_ref[:, h * LANES:(h + 1) * LANES] = _rope(xk, cos, sa, sb).astype(BF16)


def _attn_prep(P, tabs, g_q, g_k, tk):
    B, T, _ = P.shape
    tm = 256
    per = tk // tm
    cos2, sin_a, sin_b = tabs
    m64 = jnp.asarray(np.kron(np.eye(2), np.ones((HEAD_DIM, HEAD_DIM))), BF16)
    gq2 = jnp.tile(g_q, 2).reshape(1, LANES)
    gk2 = jnp.tile(g_k, 2).reshape(1, LANES)
    tab_spec = pl.BlockSpec((tm, LANES), lambda b, i: (i, 0))
    vec_spec = pl.BlockSpec((1, LANES), lambda b, i: (0, 0))
    return pl.pallas_call(
        _attn_prep_kernel,
        out_shape=(jax.ShapeDtypeStruct((B, A_HEADS, T, LANES), BF16),
                   jax.ShapeDtypeStruct((B, T, LANES), BF16),
                   jax.ShapeDtypeStruct((B, T // tk, LANES, tk), BF16),
                   jax.ShapeDtypeStruct((B, 2 * DF_HEADS, T, LANES), BF16),
                   jax.ShapeDtypeStruct((B, T, 512), BF16),
                   jax.ShapeDtypeStruct((B, DF_HEADS, T // tk, LANES, tk), BF16)),
        grid=(B, T // tm),
        in_specs=[pl.BlockSpec((None, tm, 512), lambda b, i: (b, i, COL_AQ // 512)),
                  pl.BlockSpec((None, tm, LANES), lambda b, i: (b, i, COL_AK // LANES)),
                  pl.BlockSpec((None, tm, LANES), lambda b, i: (b, i, COL_AV // LANES)),
                  pl.BlockSpec((None, tm, 512), lambda b, i: (b, i, COL_CQ // 512)),
                  pl.BlockSpec((None, tm, 512), lambda b, i: (b, i, COL_CK // 512)),
                  pl.BlockSpec((None, tm, 512), lambda b, i: (b, i, COL_CV // 512)),
                  tab_spec, tab_spec, tab_spec, vec_spec, vec_spec,
                  pl.BlockSpec((LANES, LANES), lambda b, i: (0, 0))],
        out_specs=(pl.BlockSpec((None, A_HEADS, tm, LANES), lambda b, i: (b, 0, i, 0)),
                   pl.BlockSpec((None, tm, LANES), lambda b, i: (b, i, 0)),
                   pl.BlockSpec((None, None, LANES, tm), lambda b, i: (b, i // per, 0, i % per)),
                   pl.BlockSpec((None, 2 * DF_HEADS, tm, LANES), lambda b, i: (b, 0, i, 0)),
                   pl.BlockSpec((None, tm, 512), lambda b, i: (b, i, 0)),
                   pl.BlockSpec((None, DF_HEADS, None, LANES, tm), lambda b, i: (b, 0, i // per, 0, i % per))),
        compiler_params=_cparams(("arbitrary", "arbitrary")),
        name="attn_prep",
    )(P, P, P, P, P, P, cos2, sin_a, sin_b, gq2, gk2, m64)


FLASH_TQ = 512
FLASH_PIECE = 256
FLASH_MAX_OVERSHOOT = 60.0
FLASH_GROUP = 2


def _flash_kernel(*refs, mode, cols, tk, n_ctx, n_chunks, ctx_only, lambda_init):
    refs = list(refs)
    q_ref, k_ref, vt_ref = refs[0:3]
    pos = 3
    if mode == "diff":
        lam_ref, g_ref = refs[pos], refs[pos + 1]
        pos += 2
    o_ref, m_scr, l_scr, over_scr, acc_scr = refs[pos:pos + 5]
    q = q_ref[...].reshape(cols, LANES)

    def reset():
        m_scr[...] = jnp.full((1, cols), -jnp.inf, F32)
        l_scr[...] = jnp.zeros((1, cols), F32)
        acc_scr[...] = jnp.zeros((LANES, cols), F32)

    def exact(k, vt):
        s = _dot_nt(k, q)
        m_old = m_scr[...]
        m_new = jnp.maximum(m_old, jnp.max(s, axis=0, keepdims=True))
        alpha = jnp.exp2(m_old - m_new)
        p = jnp.exp2(s - m_new)
        l_scr[...] = alpha * l_scr[...] + jnp.sum(p, axis=0, keepdims=True)
        acc_scr[...] = alpha * acc_scr[...] + _dot(vt, p.astype(BF16))
        m_scr[...] = m_new

    def lagged(k, vt):
        ref = m_scr[...]
        s = _dot_nt(k, q)
        p = jnp.exp2(s - ref)
        m_cur = jnp.max(s, axis=0, keepdims=True)
        m_new = jnp.maximum(ref, m_cur)
        alpha = jnp.exp2(ref - m_new)
        l_scr[...] = (l_scr[...] + jnp.sum(p, axis=0, keepdims=True)) * alpha
        acc_scr[...] = (acc_scr[...] + _dot(vt, p.astype(BF16))) * alpha
        over_scr[...] = jnp.maximum(over_scr[...], m_cur - ref)
        m_scr[...] = m_new

    def k_chunk(c):
        return k_ref[pl.ds(pl.multiple_of(c * tk, tk), tk), :]

    reset()
    if ctx_only:
        exact(k_ref[...], vt_ref[:, tk - n_ctx:])
    else:
        over_scr[...] = jnp.zeros((1, cols), F32)
        exact(k_ref[0:FLASH_PIECE, :], vt_ref[0, :, 0:FLASH_PIECE])
        if tk > FLASH_PIECE:
            lagged(k_ref[FLASH_PIECE:tk, :], vt_ref[0, :, FLASH_PIECE:tk])

        def lagged_group(c0):
            ref = m_scr[...]
            l, acc, over = l_scr[...], acc_scr[...], over_scr[...]
            pending = None
            for u in range(FLASH_GROUP + 1):
                if u < FLASH_GROUP:
                    s = _dot_nt(k_chunk(c0 + u), q)
                    m_cur = jnp.max(s, axis=0, keepdims=True)
                    over = jnp.maximum(over, m_cur - ref)
                    ref_next = jnp.maximum(ref, m_cur)
                if pending is not None:
                    s_p, ref_p, ref_after, c_p = pending
                    p = jnp.exp2(s_p - ref_p)
                    alpha = jnp.exp2(ref_p - ref_after)
                    l = (l + jnp.sum(p, axis=0, keepdims=True)) * alpha
                    acc = (acc + _dot(vt_ref[c_p], p.astype(BF16))) * alpha
                if u < FLASH_GROUP:
                    pending = (s, ref, ref_next, c0 + u)
                    ref = ref_next
            l_scr[...] = l
            acc_scr[...] = acc
            over_scr[...] = over
            m_scr[...] = ref

        def fast_body(it, carry):
            lagged_group(1 + FLASH_GROUP * it)
            return carry
        lax.fori_loop(0, (n_chunks - 1) // FLASH_GROUP, fast_body, 0)

        @pl.when(jnp.max(over_scr[...]) > FLASH_MAX_OVERSHOOT)
        def _():
            reset()

            def exact_body(c, carry):
                exact(k_chunk(c), vt_ref[c])
                return carry
            lax.fori_loop(0, n_chunks, exact_body, 0)

    if mode == "gqa":
        kvh = pl.program_id(1)
        o = acc_scr[pl.ds(pl.multiple_of(kvh * HEAD_DIM, HEAD_DIM), HEAD_DIM), :] / l_scr[...]
        tq = cols // A_GROUP
        pieces = [o[:, g * tq:(g + 1) * tq] for g in range(A_GROUP)]
        o_ref[...] = jnp.concatenate(pieces, axis=0).T.astype(o_ref.dtype)
    else:
        lq = lam_ref[...]
        lam = (jnp.exp(jnp.sum(lq[0:1] * lq[1:2], axis=-1, keepdims=True))
               - jnp.exp(jnp.sum(lq[2:3] * lq[3:4], axis=-1, keepdims=True)) + lambda_init)
        o = acc_scr[...] / l_scr[...]
        d = o[:, :cols // 2] - lam * o[:, cols // 2:]
        y = d * lax.rsqrt(jnp.mean(d * d, axis=0, keepdims=True) + EPS) * g_ref[...]
        o_ref[...] = (y * (1.0 - lambda_init)).T.astype(o_ref.dtype)


def _flash(mode, q, k, vt, n_ctx, ctx_only, extra=(), lambda_init=0.0):
    B, NQ, T, _ = q.shape
    S = T - n_ctx
    tk = vt.shape[-1]
    n_chunks = vt.shape[-3]
    assert (n_chunks - 1) % FLASH_GROUP == 0
    n_maps = A_GROUP if mode == "gqa" else 2
    n_heads = NQ // n_maps
    if ctx_only:
        tq, n_q, q0, rows = n_ctx, 1, S // n_ctx, n_ctx
        k_rows, k0 = n_ctx, S // n_ctx
        vt_blk, vt0 = None, n_chunks - 1
    else:
        tq = FLASH_TQ if mode == "gqa" else 2 * FLASH_TQ
        n_q, q0, rows = S // tq, 0, S
        k_rows, k0 = T, 0
        vt_blk, vt0 = n_chunks, 0
    cols = n_maps * tq
    kern = functools.partial(_flash_kernel, mode=mode, cols=cols, tk=tk, n_ctx=n_ctx,
                             n_chunks=n_chunks, ctx_only=ctx_only, lambda_init=lambda_init)
    in_specs = [pl.BlockSpec((None, n_maps, tq, LANES), lambda b, h, i: (b, h, i + q0, 0))]
    if mode == "gqa":
        in_specs += [pl.BlockSpec((None, k_rows, LANES), lambda b, h, i: (b, k0, 0)),
                     pl.BlockSpec((None, vt_blk, LANES, tk), lambda b, h, i: (b, vt0, 0, 0))]
        out_spec = pl.BlockSpec((None, tq, A_GROUP * HEAD_DIM), lambda b, h, i: (b, i, h))
    else:
        in_specs += [pl.BlockSpec((None, k_rows, LANES), lambda b, h, i: (b, k0, h)),
                     pl.BlockSpec((None, None, vt_blk, LANES, tk), lambda b, h, i: (b, h, vt0, 0, 0)),
                     pl.BlockSpec((4, HEAD_DIM), lambda b, h, i: (0, 0)),
                     pl.BlockSpec((LANES, 1), lambda b, h, i: (0, 0))]
        out_spec = pl.BlockSpec((None, tq, LANES), lambda b, h, i: (b, i, h))
    return pl.pallas_call(
        kern,
        out_shape=jax.ShapeDtypeStruct((B, rows, BR_W), BF16),
        grid=(B, n_heads, n_q),
        in_specs=in_specs,
        out_specs=out_spec,
        scratch_shapes=[pltpu.VMEM((1, cols), F32)] * 3 + [pltpu.VMEM((LANES, cols), F32)],
        compiler_params=_cparams(("arbitrary", "arbitrary", "arbitrary")),
        name="flash_" + mode + ("_ctx" if ctx_only else ""),
    )(q, k, vt, *extra)


def _dn_prep_kernel(q_ref, k_ref, v_ref, ql_ref, kl_ref, vl_ref, qr_ref, kr_ref, vr_ref, ab_ref,
                    wc_ref, alog_ref, dtb_ref, mh_ref, eg_ref, eb_ref, tm_ref, ones_ref,
                    qo_ref, ko_ref, kb_ref, vbk_ref, qdec_ref, kdecbd_ref, decbd_ref, egl_ref,
                    *, tm):
    i = pl.program_id(1)
    nblk = pl.num_programs(1)
    W = DN_HEADS * DN_DK
    row = lax.broadcasted_iota(jnp.int32, (tm, W), 0)
    col = lax.broadcasted_iota(jnp.int32, (tm, W), 1)
    left_ok = jnp.logical_and(i >= 1, i < nblk - 1).astype(F32)
    right_ok = (i < nblk - 2).astype(F32)

    def conv_silu(x_ref, xl_ref, xr_ref, c0):
        x = x_ref[...].astype(F32)
        prev_row = xl_ref[7:8, :].astype(F32) * left_ok
        next_row = xr_ref[0:1, :].astype(F32) * right_ok
        x_prev = jnp.where(row == 0, prev_row, pltpu.roll(x, 1, 0))
        x_next = jnp.where(row == tm - 1, next_row, pltpu.roll(x, tm - 1, 0))
        y = (x_prev * wc_ref[0:1, c0:c0 + W] + x * wc_ref[1:2, c0:c0 + W]
             + x_next * wc_ref[2:3, c0:c0 + W])
        return _silu(y)

    def l2norm(x):
        ss = _dot3_l(x * x, mh_ref[...])
        return x * lax.rsqrt(ss + EPS)

    q = l2norm(conv_silu(q_ref, ql_ref, qr_ref, 0)) * (DN_DK ** -0.5)
    k = l2norm(conv_silu(k_ref, kl_ref, kr_ref, W))
    v = conv_silu(v_ref, vl_ref, vr_ref, 2 * W)

    ab = ab_ref[...]
    z = ab + dtb_ref[...]
    softplus = jnp.maximum(z, 0.0) + jnp.log(1.0 + jnp.exp(-jnp.abs(z)))
    g_all = -jnp.exp(alog_ref[...]) * softplus
    beta_all = jax.nn.sigmoid(ab)

    t_in = row & (DN_CHUNK - 1)
    s_in = col & (DN_CHUNK - 1)

    def put(ref, d, val):
        for h in range(DN_HEADS):
            piece = val[:, h * DN_DK:(h + 1) * DN_DK]
            if d is None:
                ref[h] = piece.astype(ref.dtype)
            else:
                ref[d, h] = piece.astype(ref.dtype)

    lo = lax.broadcasted_iota(jnp.int32, (tm, LANES), 1) < DN_DK
    n_chunks = tm // DN_CHUNK
    bd_shape = (tm, n_chunks * DN_CHUNK)
    same_chunk = (lax.broadcasted_iota(jnp.int32, bd_shape, 0) // DN_CHUNK
                  == lax.broadcasted_iota(jnp.int32, bd_shape, 1) // DN_CHUNK)

    def pair(a, b, h):
        sa = a[:, (h // 2) * LANES:(h // 2 + 1) * LANES]
        sb = b[:, (h // 2) * LANES:(h // 2 + 1) * LANES]
        if h % 2 == 0:
            return jnp.where(lo, sa, pltpu.roll(sb, DN_DK, 1))
        return jnp.where(lo, pltpu.roll(sa, DN_DK, 1), sb)

    def put_block_diag(ref, d, val):
        for h in range(DN_HEADS):
            x = pair(val, val, h)
            ref[d, h] = jnp.where(same_chunk, jnp.concatenate([x] * (n_chunks // 2), axis=1), 0.0).astype(ref.dtype)

    put(qo_ref, None, q)
    put(ko_ref, None, k)
    for d in range(2):
        g_e = _dot3_l(g_all, eg_ref[d])
        beta_e = _dot3_l(beta_all, eb_ref[d])
        gc = _dot3_r(tm_ref[d], g_e)
        g_tot = _dot3_r(ones_ref[...], g_e)
        strict = (t_in > s_in) if d == 0 else (t_in < s_in)
        causal = (t_in >= s_in) if d == 0 else (t_in <= s_in)
        diff = _dot3_r(tm_ref[d], jnp.where(strict, g_e, 0.0))
        decay = jnp.where(causal, jnp.exp(diff), 0.0)
        e_g = jnp.exp(gc)
        kb = k * beta_e
        put(kb_ref, d, kb)
        vb, kbg = v * beta_e, kb * e_g
        for h in range(DN_HEADS):
            vbk_ref[d, h] = pair(vb, kbg, h).astype(vbk_ref.dtype)
        put(qdec_ref, d, q * e_g)
        put_block_diag(kdecbd_ref, d, k * jnp.exp(g_tot - gc))
        put_block_diag(decbd_ref, d, decay)
        e_tot = jnp.exp(g_tot)
        for h in range(DN_HEADS):
            for c in range(tm // DN_CHUNK):
                egl_ref[d, h, c:c + 1, :] = e_tot[c * DN_CHUNK:c * DN_CHUNK + 1, h * DN_DK:(h + 1) * DN_DK]


def _dn_consts():
    n = DN_BLOCK
    t = np.arange(n)
    same = (t[:, None] // DN_CHUNK) == (t[None, :] // DN_CHUNK)
    pre = same & (t[None, :] <= t[:, None])
    suf = same & (t[None, :] >= t[:, None])
    tmat = np.stack([pre, suf]).astype(np.float32)
    ones = same.astype(np.float32)
    W = DN_HEADS * DN_DK
    eg = np.zeros((2, LANES, W), np.float32)
    eb = np.zeros((2, LANES, W), np.float32)
    for d in range(2):
        for h in range(DN_HEADS):
            eg[d, d * DN_HEADS + h, h * DN_DK:(h + 1) * DN_DK] = 1.0
            eb[d, 2 * DN_HEADS + d * DN_HEADS + h, h * DN_DK:(h + 1) * DN_DK] = 1.0
    mh = np.kron(np.eye(DN_HEADS), np.ones((DN_DK, DN_DK))).astype(np.float32)
    return tuple(jnp.asarray(a, BF16) for a in (mh, eg, eb, tmat, ones))


def _dn_prep(P, AB, w_conv, a_log, dt_bias):
    B, T, _ = P.shape
    tm = DN_BLOCK
    nblk = T // tm
    W = DN_HEADS * DN_DK
    mh, eg, eb, tmat, ones = _dn_consts()
    alog_row = jnp.zeros((1, LANES), F32).at[0, :2 * DN_HEADS].set(a_log.reshape(-1))
    dtb_row = jnp.zeros((1, LANES), F32).at[0, :2 * DN_HEADS].set(dt_bias.reshape(-1))
    r8 = tm // 8

    def main(c0):
        return pl.BlockSpec((None, tm, W), lambda b, i: (b, i, c0 // W))

    def left(c0):
        return pl.BlockSpec((None, 8, W), lambda b, i: (b, jnp.maximum(i * r8 - 1, 0), c0 // W))

    def right(c0):
        return pl.BlockSpec((None, 8, W), lambda b, i: (b, jnp.minimum((i + 1) * r8, nblk * r8 - 1), c0 // W))

    def full(a):
        nd = a.ndim
        return pl.BlockSpec(a.shape, lambda b, i: (0,) * nd)

    hm = pl.BlockSpec((None, DN_HEADS, tm, DN_DK), lambda b, i: (b, 0, i, 0))

    def hm2(width):
        return pl.BlockSpec((2, None, DN_HEADS, tm, width), lambda b, i: (0, b, 0, i, 0))

    def sh_hm2(width):
        return jax.ShapeDtypeStruct((2, B, DN_HEADS, T, width), BF16)

    sh_hm = jax.ShapeDtypeStruct((B, DN_HEADS, T, DN_DK), BF16)
    return pl.pallas_call(
        functools.partial(_dn_prep_kernel, tm=tm),
        out_shape=(sh_hm, sh_hm, sh_hm2(DN_DK), sh_hm2(2 * DN_DK), sh_hm2(DN_DK), sh_hm2(tm), sh_hm2(tm),
                   jax.ShapeDtypeStruct((2, B, nblk, DN_HEADS, tm // DN_CHUNK, DN_DK), F32)),
        grid=(B, nblk),
        in_specs=[main(COL_BQ), main(COL_BK), main(COL_BV),
                  left(COL_BQ), left(COL_BK), left(COL_BV),
                  right(COL_BQ), right(COL_BK), right(COL_BV),
                  pl.BlockSpec((None, tm, LANES), lambda b, i: (b, i, 0)),
                  full(w_conv), full(alog_row), full(dtb_row), full(mh), full(eg), full(eb), full(tmat),
                  full(ones)],
        out_specs=(hm, hm, hm2(DN_DK), hm2(2 * DN_DK), hm2(DN_DK), hm2(tm), hm2(tm),
                   pl.BlockSpec((2, None, None, DN_HEADS, tm // DN_CHUNK, DN_DK),
                                lambda b, i: (0, b, i, 0, 0, 0))),
        compiler_params=_cparams(("arbitrary", "arbitrary")),
        name="dn_prep",
    )(P, P, P, P, P, P, P, P, P, AB, w_conv, alog_row, dtb_row, mh, eg, eb, tmat, ones)


def _dn_chunk_kernel(q_ref, k_ref, kb_ref, vbk_ref, qdec_ref, kdecbd_ref, decbd_ref,
                     qeff_ref, o0_ref, mk_ref, n_ref):
    n, C = DN_BLOCK, DN_CHUNK
    nc = n // C
    r = lax.broadcasted_iota(jnp.int32, (n, n), 0)
    c = lax.broadcasted_iota(jnp.int32, (n, n), 1)
    off_diag = r != c
    same_chunk = (r // C) == (c // C)
    rp = lax.broadcasted_iota(jnp.int32, (C, n), 0)
    cp = lax.broadcasted_iota(jnp.int32, (C, n), 1)
    eye_packed = (rp == cp % C).astype(F32)
    own_values = (lax.broadcasted_iota(jnp.int32, (n, 2 * nc * C), 0) // C
                  == lax.broadcasted_iota(jnp.int32, (n, 2 * nc * C), 1) // (2 * C))
    heads = range(DN_HEADS)

    def block_diag(xp):
        xb = xp.astype(BF16)
        return jnp.where(same_chunk, jnp.concatenate([xb] * nc, axis=0), jnp.zeros((), BF16))

    low, qk = [], []
    for h in heads:
        dec = decbd_ref[h].astype(F32)
        k = k_ref[h]
        a = jnp.where(off_diag, _dot_nt(kb_ref[h], k) * dec, 0.0)
        low.append(sum(a[i * C:(i + 1) * C] for i in range(nc)))
        qk.append((_dot_nt(q_ref[h], k) * dec).astype(BF16))
    p = [eye_packed - m for m in low]
    mpow = low
    for _ in range(5):
        mpow = [_dot(m.astype(BF16), block_diag(m)) for m in mpow]
        p = [a + _dot(a.astype(BF16), block_diag(m)) for a, m in zip(p, mpow)]
    uw = []
    for h in heads:
        vals = jnp.where(own_values, jnp.concatenate([vbk_ref[h]] * nc, axis=1), jnp.zeros((), BF16))
        packed = _dot(p[h].astype(BF16), vals)
        uw.append(jnp.concatenate([packed[:, i * 2 * C:(i + 1) * 2 * C] for i in range(nc)],
                                  axis=0).astype(BF16))
    qkuw = [_dot(qk[h], uw[h]) for h in heads]
    for h in heads:
        o0_ref[h] = qkuw[h][:, :DN_DK].astype(o0_ref.dtype)
        qeff_ref[h] = (qdec_ref[h].astype(F32) - qkuw[h][:, DN_DK:]).astype(qeff_ref.dtype)
        kt = _dot_tn(kdecbd_ref[h], uw[h])
        n_ref[h] = kt[:, :DN_DK].astype(n_ref.dtype)
        mk_ref[h] = kt[:, DN_DK:].astype(mk_ref.dtype)


def _dn_chunk(prep):
    q, k, kb, vbk, qdec, kdecbd, decbd, _ = prep
    B, H, T, dk = q.shape
    tm = DN_BLOCK
    hm = pl.BlockSpec((None, H, tm, dk), lambda d, b, i: (b, 0, i, 0))

    def hm2(width):
        return pl.BlockSpec((None, None, H, tm, width), lambda d, b, i: (d, b, 0, i, 0))

    def sh(dtype):
        return jax.ShapeDtypeStruct((2, B, H, T, dk), dtype)

    return pl.pallas_call(
        _dn_chunk_kernel,
        out_shape=(sh(BF16), sh(BF16), sh(BF16), sh(BF16)),
        grid=(2, B, T // tm),
        in_specs=[hm, hm, hm2(dk), hm2(2 * dk), hm2(dk), hm2(tm), hm2(tm)],
        out_specs=(hm2(dk), hm2(dk), hm2(dk), hm2(dk)),
        compiler_params=_cparams(("arbitrary", "arbitrary", "arbitrary")),
        name="dn_chunk",
    )(q, k, kb, vbk, qdec, kdecbd, decbd)


def _dn_scan_kernel(*refs, n_batch):
    ins = (refs[0:5], refs[5:10])
    o_refs = refs[10:12]
    s_scr = refs[12]
    t = pl.program_id(0)
    n_chunks = DN_BLOCK // DN_CHUNK
    C = DN_CHUNK

    @pl.when(t == 0)
    def _():
        s_scr[...] = jnp.zeros_like(s_scr)

    for step in range(n_chunks):
        for d in range(2):
            ci = step if d == 0 else n_chunks - 1 - step
            rows = slice(ci * C, (ci + 1) * C)
            qeff_ref, o0_ref, mk_ref, n_ref, egl_ref = ins[d]
            for b in range(n_batch):
                for h in range(DN_HEADS):
                    s = s_scr[d, b, h]
                    sb = s.astype(BF16)
                    o_refs[d][b, h, rows, :] = (_dot(qeff_ref[b, h, rows, :], sb)
                                                + o0_ref[b, h, rows, :]).astype(o_refs[d].dtype)
                    s_scr[d, b, h] = (s * egl_ref[b, h, ci:ci + 1, :] - _dot(mk_ref[b, h, rows, :], sb)
                                      + n_ref[b, h, rows, :])


def _dn_scan(maps, egl):
    qeff, o0, mk, nn = maps
    _, B, H, T, dk = qeff.shape
    tm = DN_BLOCK
    nblk = T // tm

    def blk(d, t):
        return jnp.where(t == 0, nblk - 1, nblk - 1 - t if d == 1 else t - 1)

    in_specs, args = [], []
    for d in range(2):
        for a in (qeff, o0, mk, nn):
            in_specs.append(pl.BlockSpec((None, B, H, tm, dk), lambda t, d=d: (d, 0, 0, blk(d, t), 0)))
            args.append(a)
        in_specs.append(pl.BlockSpec((None, B, None, H, tm // DN_CHUNK, dk),
                                     lambda t, d=d: (d, 0, blk(d, t), 0, 0, 0)))
        args.append(egl)
    out_specs = tuple(pl.BlockSpec((B, H, tm, dk), lambda t, d=d: (0, 0, blk(d, t), 0)) for d in range(2))
    return pl.pallas_call(
        functools.partial(_dn_scan_kernel, n_batch=B),
        out_shape=(jax.ShapeDtypeStruct((B, H, T, dk), BF16),) * 2,
        grid=(nblk,),
        in_specs=in_specs,
        out_specs=out_specs,
        scratch_shapes=[pltpu.VMEM((2, B, H, dk, dk), F32)],
        compiler_params=_cparams(("arbitrary",)),
        name="dn_scan",
    )(*args)


def _merge_kernel(a_ref, of_ref, ob_ref, z_ref, c_ref, ga_ref, gb_ref, gc_ref, h_ref, mod_ref, g2_ref, gdn_ref,
                  wb_ref, wo_ref, h_out_ref, v_out_ref, b_scr, *, tm, n_lat):
    i = pl.program_id(1)
    D = h_ref.shape[-1]
    z = z_ref[...].astype(F32)
    for hd in range(DN_HEADS):
        o = of_ref[hd].astype(F32) + ob_ref[hd].astype(F32)
        y = o * lax.rsqrt(jnp.mean(o * o, axis=-1, keepdims=True) + EPS) * gdn_ref[...]
        b_scr[:, hd * DN_DK:(hd + 1) * DN_DK] = y * _silu(z[:, hd * DN_DK:(hd + 1) * DN_DK])
    m = jax.nn.sigmoid(ga_ref[...].astype(F32)) * _dot(a_ref[...], wb_ref[0])
    m = m + jax.nn.sigmoid(gb_ref[...].astype(F32)) * _dot(b_scr[...].astype(BF16), wb_ref[1])
    m = m + jax.nn.sigmoid(gc_ref[...].astype(F32)) * _dot(c_ref[...], wb_ref[2])
    y = _dot(m.astype(BF16), wo_ref[...])
    gate = jnp.where(_is_ctx_row(i, tm, n_lat), mod_ref[0:1, 2 * D:3 * D], mod_ref[1:2, 2 * D:3 * D])
    h_new = h_ref[...] + gate * y
    h_out_ref[...] = h_new
    v_out_ref[...] = _norm_mod(h_new, g2_ref[...], mod_ref, i, tm, n_lat, 3, 4).astype(BF16)


def _merge(br_a, o_f, o_b, br_c, P, h, modtab, g_norm2, g_dn_out, w_branch, w_out, n_lat, tm):
    B, T, D = h.shape
    br = pl.BlockSpec((None, tm, BR_W), lambda b, i: (b, i, 0))
    hm = pl.BlockSpec((None, DN_HEADS, tm, DN_DK), lambda b, i: (b, 0, i, 0))

    def gate(j):
        return pl.BlockSpec((None, tm, D), lambda b, i: (b, i, COL_GATE // D + j))

    tok = pl.BlockSpec((None, tm, D), lambda b, i: (b, i, 0))
    return pl.pallas_call(
        functools.partial(_merge_kernel, tm=tm, n_lat=n_lat),
        out_shape=(jax.ShapeDtypeStruct((B, T, D), F32), jax.ShapeDtypeStruct((B, T, D), BF16)),
        grid=(B, T // tm),
        in_specs=[br, hm, hm,
                  pl.BlockSpec((None, tm, BR_W), lambda b, i: (b, i, COL_BZ // BR_W)),
                  br, gate(0), gate(1), gate(2), tok,
                  pl.BlockSpec((None, 2, 6 * D), lambda b, i: (b, 0, 0)),
                  pl.BlockSpec((1, D), lambda b, i: (0, 0)),
                  pl.BlockSpec((1, DN_DK), lambda b, i: (0, 0)),
                  pl.BlockSpec((3, BR_W, D), lambda b, i: (0, 0, 0)),
                  pl.BlockSpec((D, D), lambda b, i: (0, 0))],
        out_specs=(tok, tok),
        scratch_shapes=[pltpu.VMEM((tm, BR_W), F32)],
        compiler_params=_cparams(("arbitrary", "arbitrary")),
        name="merge",
    )(br_a, o_f, o_b, P, br_c, P, P, P, h, modtab, g_norm2.reshape(1, D), g_dn_out.reshape(1, DN_DK),
      w_branch, w_out)


def _router_kernel(v_ref, wr_ref, br_ref, perm_ref, c_ref):
    tn = v_ref.shape[0]
    G = N_GROUPS
    scores = jax.nn.sigmoid(_dot_nt(wr_ref[...], v_ref[...]))
    sel = scores + br_ref[...]
    slabs = [sel[j * G:(j + 1) * G] for j in range(GROUP_SIZE)]
    m1, m2 = slabs[0], jnp.full((G, tn), -jnp.inf, F32)
    for j in range(1, GROUP_SIZE):
        m2 = jnp.maximum(m2, jnp.minimum(m1, slabs[j]))
        m1 = jnp.maximum(m1, slabs[j])
    gs = m1 + m2
    gidx = lax.broadcasted_iota(jnp.int32, (G, tn), 0)
    grank = jnp.zeros((G, tn), jnp.int32)
    for g in range(G):
        rowv = gs[g:g + 1]
        beats = jnp.where(rowv > gs, 1, jnp.where(jnp.logical_and(rowv == gs, g < gidx), 1, 0))
        grank = grank + beats
    gmask = grank < TOPK_GROUPS
    masked = jnp.concatenate([jnp.where(gmask, s, -jnp.inf) for s in slabs], axis=0)
    pos = lax.broadcasted_iota(jnp.int32, (N_EXPERTS, tn), 0)
    orig = (pos % G) * GROUP_SIZE + pos // G
    w = jnp.zeros((N_EXPERTS, tn), F32)
    for _ in range(TOP_K):
        best = jnp.max(masked, axis=0, keepdims=True)
        first = jnp.min(jnp.where(masked == best, orig, N_EXPERTS), axis=0, keepdims=True)
        pick = orig == first
        w = jnp.where(pick, scores, w)
        masked = jnp.where(pick, -jnp.inf, masked)
    denom = jnp.sum(w, axis=0, keepdims=True) + 1e-20
    c = w / denom * ROUTED_SCALE
    c_t = sum(_dot_tn(part, perm_ref[...]) for part in _split3(c))
    for g in range(N_EXPERTS // MOE_EPS):
        c_ref[g] = c_t[:, g * MOE_EPS:(g + 1) * MOE_EPS]


def _router(v_flat, w_router, b_router):
    N, D = v_flat.shape
    tn = 1280 if N % 1280 == 0 else 256
    perm = np.array([(p % N_GROUPS) * GROUP_SIZE + p // N_GROUPS for p in range(N_EXPERTS)])
    wr_t = w_router.T[perm].astype(BF16)
    br = b_router[perm].reshape(N_EXPERTS, 1)
    unperm = jnp.asarray(np.eye(N_EXPERTS)[perm], BF16)
    ngrp = N_EXPERTS // MOE_EPS
    return pl.pallas_call(
        _router_kernel,
        out_shape=jax.ShapeDtypeStruct((ngrp, N, MOE_EPS), F32),
        grid=(N // tn,),
        in_specs=[pl.BlockSpec((tn, D), lambda i: (i, 0)),
                  pl.BlockSpec((N_EXPERTS, D), lambda i: (0, 0)),
                  pl.BlockSpec((N_EXPERTS, 1), lambda i: (0, 0)),
                  pl.BlockSpec((N_EXPERTS, N_EXPERTS), lambda i: (0, 0))],
        out_specs=pl.BlockSpec((ngrp, tn, MOE_EPS), lambda i: (0, i, 0)),
        compiler_params=_cparams(("arbitrary",)),
        name="router",
    )(v_flat, wr_t, br, unperm)


MOE_EPS = 4


def _moe_kernel(x_ref, c_ref, wgu_ref, wd_ref, sgu_ref, sd_ref, h_ref, mod_ref,
                o_ref, acc_ref, *, tm, n_lat):
    i = pl.program_id(1)
    e = pl.program_id(2)
    x = x_ref[...]
    D = x.shape[-1]
    F = MOE_FF

    @pl.when(e == 0)
    def _():
        gu = _dot(x, sgu_ref[...])
        hs = _silu(gu[:, :F]) * gu[:, F:]
        acc_ref[...] = _dot(hs.astype(BF16), sd_ref[...])

    c = c_ref[...]
    parts = []
    for j in range(MOE_EPS):
        gu = _dot(x, wgu_ref[j])
        hj = _silu(gu[:, :F]) * gu[:, F:] * c[:, j:j + 1]
        parts.append(hj.astype(BF16))
    hcat = jnp.concatenate(parts, axis=-1)
    acc_ref[...] += _dot(hcat, wd_ref[...])

    @pl.when(e == pl.num_programs(2) - 1)
    def _():
        gate = jnp.where(_is_ctx_row(i, tm, n_lat), mod_ref[0:1, 5 * D:6 * D], mod_ref[1:2, 5 * D:6 * D])
        o_ref[...] = h_ref[...] + gate * acc_ref[...]


def _moe_weights(w_gate, w_up, w_down):
    E, D, F = w_gate.shape
    gu = jnp.concatenate([w_gate, w_up], axis=-1).astype(BF16)
    return gu, w_down.astype(BF16).reshape(E * F, D)


def _moe(v, c_grp, h, modtab, wgu, wd, sgu, sd, n_lat, tm):
    B, T, D = v.shape
    F = MOE_FF
    ngrp = N_EXPERTS // MOE_EPS
    tok = pl.BlockSpec((None, tm, D), lambda b, i, e: (b, i, 0))
    return pl.pallas_call(
        functools.partial(_moe_kernel, tm=tm, n_lat=n_lat),
        out_shape=jax.ShapeDtypeStruct((B, T, D), F32),
        grid=(B, T // tm, ngrp),
        in_specs=[tok,
                  pl.BlockSpec((None, None, tm, MOE_EPS), lambda b, i, e: (e, b, i, 0)),
                  pl.BlockSpec((MOE_EPS, D, 2 * F), lambda b, i, e: (e, 0, 0)),
                  pl.BlockSpec((MOE_EPS * F, D), lambda b, i, e: (e, 0)),
                  pl.BlockSpec((D, 2 * F), lambda b, i, e: (0, 0)),
                  pl.BlockSpec((F, D), lambda b, i, e: (0, 0)),
                  tok,
                  pl.BlockSpec((None, 2, 6 * D), lambda b, i, e: (b, 0, 0))],
        out_specs=tok,
        scratch_shapes=[pltpu.VMEM((tm, D), F32)],
        compiler_params=_cparams(("arbitrary", "arbitrary", "arbitrary")),
        name="moe",
    )(v, c_grp, wgu, wd, sgu, sd, h, modtab)


def _final_kernel(h_ref, g_ref, o_ref):
    x = h_ref[...]
    o_ref[...] = x * lax.rsqrt(jnp.mean(x * x, axis=-1, keepdims=True) + EPS) * g_ref[...]


def _final_norm(h, g_final, S):
    B, T, D = h.shape
    tm = 1024
    return pl.pallas_call(
        _final_kernel,
        out_shape=jax.ShapeDtypeStruct((B, S, D), F32),
        grid=(B, S // tm),
        in_specs=[pl.BlockSpec((None, tm, D), lambda b, i: (b, i, 0)),
                  pl.BlockSpec((1, D), lambda b, i: (0, 0))],
        out_specs=pl.BlockSpec((None, tm, D), lambda b, i: (b, i, 0)),
        compiler_params=_cparams(("arbitrary", "arbitrary")),
        name="final_norm",
    )(h, g_final.reshape(1, D))


def _rope_tables(S, n_ctx):
    rows = S // GRID_W
    row = jnp.repeat(jnp.arange(rows, dtype=F32), GRID_W)
    col = (jnp.arange(S) % GRID_W).astype(F32)
    axis_dim = HEAD_DIM // 2
    inv = 1.0 / (ROPE_THETA ** (jnp.arange(0, axis_dim, 2, dtype=F32) / axis_dim))
    ang = jnp.concatenate([row[:, None] * inv, col[:, None] * inv], axis=-1)
    ang = jnp.concatenate([ang, ang], axis=-1)
    cos = jnp.concatenate([jnp.cos(ang), jnp.ones((n_ctx, HEAD_DIM), F32)], axis=0)
    sin = jnp.concatenate([jnp.sin(ang), jnp.zeros((n_ctx, HEAD_DIM), F32)], axis=0)
    first = (jnp.arange(HEAD_DIM) < HEAD_DIM // 2)[None, :]
    sin_a = jnp.where(first, -sin, 0.0)
    sin_b = jnp.where(first, 0.0, sin)
    return tuple(jnp.tile(t, (1, 2)) for t in (cos, sin_a, sin_b))


def _permute_w_in(w):
    splits = np.cumsum([512, 128, 128, 1536, 512, 16, 16, 512, 512, 512, 3072])[:-1].tolist()
    aq, ak, av, bqkv, bz, ba, bb, cq, ck, cv, gate = jnp.split(w, splits, axis=-1)
    pad = jnp.zeros((w.shape[0], IN_W_PAD - COL_AB - 32), w.dtype)
    out = jnp.concatenate([aq, cq, ck, cv, bz, bqkv, gate, ak, av, ba, bb, pad], axis=-1)
    return out.astype(BF16)


def kernel(x, c, ctx, c_ctx, w_mod, b_mod, g_norm1, g_norm2, w_in, g_qnorm, g_knorm, w_conv, a_log, dt_bias, g_dn_out, lam_qk, g_subln, w_branch, w_out, w_router, b_router, w_e_gate, w_e_up, w_e_down, w_s_gate, w_s_up, w_s_down, g_final):
    B, S, D = x.shape
    n_ctx = ctx.shape[1]
    T = n_ctx + S
    L = w_mod.shape[0]
    N = B * T
    assert n_ctx == DN_BLOCK and S % 1024 == 0 and D == D_MODEL
    tm_tok = 1280 if T % 1280 == 0 else 256
    tm_merge = 640 if T % 640 == 0 else 256

    cond = jnp.zeros((8, D), F32).at[0].set(c_ctx).at[1:1 + B].set(c)
    mod = _mod_vectors(cond, w_mod, b_mod)
    tabs = _rope_tables(S, n_ctx)
    h = jnp.concatenate([x, ctx], axis=1)
    tk = 1280 if T % 1280 == 0 else 256

    for l in range(L):
        lambda_init = 0.8 - 0.6 * math.exp(-0.3 * l)
        modtab = jnp.stack([jnp.broadcast_to(mod[l, 0], (B, 6 * D)), mod[l, 1:1 + B]], axis=1)
        P, AB = _in_projection(h, modtab, g_norm1[l], _permute_w_in(w_in[l]), S, tm_tok)

        qa, ka, vta, qd, kd, vtd = _attn_prep(P, tabs, g_qnorm[l], g_knorm[l], tk)
        dx = (lam_qk[l], g_subln[l].reshape(LANES, 1))
        br_a = jnp.concatenate([_flash("gqa", qa, ka, vta, n_ctx, False),
                                _flash("gqa", qa, ka, vta, n_ctx, True)], axis=1)
        br_c = jnp.concatenate([_flash("diff", qd, kd, vtd, n_ctx, False, dx, lambda_init),
                                _flash("diff", qd, kd, vtd, n_ctx, True, dx, lambda_init)], axis=1)

        prep = _dn_prep(P, AB, w_conv[l], a_log[l], dt_bias[l])
        o_f, o_b = _dn_scan(_dn_chunk(prep), prep[-1])

        h, v = _merge(br_a, o_f, o_b, br_c, P, h, modtab, g_norm2[l], g_dn_out[l], w_branch[l].astype(BF16),
                      w_out[l].astype(BF16), S, tm_merge)

        c_grp = _router(v.reshape(N, D), w_router[l], b_router[l]).reshape(-1, B, T, MOE_EPS)
        wgu, wd = _moe_weights(w_e_gate[l], w_e_up[l], w_e_down[l])
        sgu = jnp.concatenate([w_s_gate[l], w_s_up[l]], axis=-1).astype(BF16)
        h = _moe(v, c_grp, h, modtab, wgu, wd, sgu, w_s_down[l].astype(BF16), S, tm_merge)

    return _final_norm(h, g_final, S)
```

```python
import functools
import math

import numpy as np
import jax
import jax.numpy as jnp
from jax import lax
from jax.experimental import pallas as pl
from jax.experimental.pallas import tpu as pltpu

F32 = jnp.float32
BF16 = jnp.bfloat16

D_MODEL = 1024
GRID_W = 64
EPS = 1e-6
ROPE_THETA = 10000.0
HEAD_DIM = 64
LANES = 128
A_HEADS = 8
A_KV_HEADS = 2
A_GROUP = A_HEADS // A_KV_HEADS
DN_HEADS = 8
DN_DK = 64
DN_CHUNK = 64
DN_BLOCK = 256
DF_HEADS = 4
N_EXPERTS = 64
TOP_K = 6
N_GROUPS = 8
TOPK_GROUPS = 4
GROUP_SIZE = N_EXPERTS // N_GROUPS
MOE_FF = 256
ROUTED_SCALE = 2.5
BR_W = 512

COL_AQ, COL_CQ, COL_CK, COL_CV, COL_BZ = 0, 512, 1024, 1536, 2048
COL_BQ, COL_BK, COL_BV = 2560, 3072, 3584
COL_GATE = 4096
COL_AK, COL_AV, COL_AB = 7168, 7296, 7424
IN_W_PAD = 7680
IN_TN = 1280

VMEM_LIMIT = 56 * 1024 * 1024


def _cparams(sem):
    return pltpu.CompilerParams(dimension_semantics=sem, vmem_limit_bytes=VMEM_LIMIT)


def _dot(a, b):
    return jnp.dot(a, b, preferred_element_type=F32)


def _dot_nt(a, b):
    return lax.dot_general(a, b, (((1,), (1,)), ((), ())), preferred_element_type=F32)


def _dot_tn(a, b):
    return lax.dot_general(a, b, (((0,), (0,)), ((), ())), preferred_element_type=F32)


def _split3(x):
    hi = x.astype(BF16)
    r = x - hi.astype(F32)
    mid = r.astype(BF16)
    lo = (r - mid.astype(F32)).astype(BF16)
    return hi, mid, lo


def _dot3_l(x, m):
    hi, mid, lo = _split3(x)
    return _dot(hi, m) + _dot(mid, m) + _dot(lo, m)


def _dot3_r(m, x):
    hi, mid, lo = _split3(x)
    return _dot(m, hi) + _dot(m, mid) + _dot(m, lo)


def _silu(x):
    return x * jax.nn.sigmoid(x)


def _mod_kernel(cond_ref, w_ref, b_ref, o_ref):
    a = _silu(cond_ref[...]).astype(BF16)
    o_ref[...] = _dot(a, w_ref[...].astype(BF16)) + b_ref[...]


def _mod_vectors(cond, w_mod, b_mod):
    L, D, W = w_mod.shape
    tn = 1024
    return pl.pallas_call(
        _mod_kernel,
        out_shape=jax.ShapeDtypeStruct((L, 8, W), F32),
        grid=(L, W // tn),
        in_specs=[pl.BlockSpec((8, D), lambda l, j: (0, 0)),
                  pl.BlockSpec((None, D, tn), lambda l, j: (l, 0, j)),
                  pl.BlockSpec((None, 1, tn), lambda l, j: (l, 0, j))],
        out_specs=pl.BlockSpec((None, 8, tn), lambda l, j: (l, 0, j)),
        compiler_params=_cparams(("arbitrary", "arbitrary")),
        name="mod_vectors",
    )(cond, w_mod, b_mod.reshape(L, 1, W))


def _is_ctx_row(blk, tm, n_lat):
    return blk * tm + lax.broadcasted_iota(jnp.int32, (tm, 1), 0) >= n_lat


def _norm_mod(x, g, mod_ref, blk, tm, n_lat, sh, sc):
    D = x.shape[-1]
    y = x * lax.rsqrt(jnp.mean(x * x, axis=-1, keepdims=True) + EPS) * g
    is_ctx = _is_ctx_row(blk, tm, n_lat)
    scale = jnp.where(is_ctx, mod_ref[0:1, sc * D:(sc + 1) * D], mod_ref[1:2, sc * D:(sc + 1) * D])
    shift = jnp.where(is_ctx, mod_ref[0:1, sh * D:(sh + 1) * D], mod_ref[1:2, sh * D:(sh + 1) * D])
    return y * (1.0 + scale) + shift


def _inproj_kernel(h_ref, mod_ref, g_ref, w_ref, p_ref, ab_ref, u_scr, *, tm, n_lat, ab_off):
    i = pl.program_id(1)
    j = pl.program_id(2)

    @pl.when(j == 0)
    def _():
        u_scr[...] = _norm_mod(h_ref[...], g_ref[...], mod_ref, i, tm, n_lat, 0, 1).astype(BF16)

    r = _dot(u_scr[...], w_ref[...])
    p_ref[...] = r.astype(BF16)

    @pl.when(j == pl.num_programs(2) - 1)
    def _():
        ab_ref[...] = r[:, ab_off:ab_off + LANES]


def _in_projection(h, modtab, g_norm, w_perm, n_lat, tm):
    B, T, D = h.shape
    ncol = IN_W_PAD // IN_TN
    kern = functools.partial(_inproj_kernel, tm=tm, n_lat=n_lat, ab_off=COL_AB - (ncol - 1) * IN_TN)
    return pl.pallas_call(
        kern,
        out_shape=(jax.ShapeDtypeStruct((B, T, IN_W_PAD), BF16),
                   jax.ShapeDtypeStruct((B, T, LANES), F32)),
        grid=(B, T // tm, ncol),
        in_specs=[pl.BlockSpec((None, tm, D), lambda b, i, j: (b, i, 0)),
                  pl.BlockSpec((None, 2, 6 * D), lambda b, i, j: (b, 0, 0)),
                  pl.BlockSpec((1, D), lambda b, i, j: (0, 0)),
                  pl.BlockSpec((D, IN_TN), lambda b, i, j: (0, j))],
        out_specs=(pl.BlockSpec((None, tm, IN_TN), lambda b, i, j: (b, i, j)),
                   pl.BlockSpec((None, tm, LANES), lambda b, i, j: (b, i, 0))),
        scratch_shapes=[pltpu.VMEM((tm, D), BF16)],
        compiler_params=_cparams(("arbitrary", "arbitrary", "arbitrary")),
        name="in_projection",
    )(h, modtab, g_norm.reshape(1, D), w_perm)


def _rope(x, cos, sin_a, sin_b):
    return x * cos + pltpu.roll(x, LANES - 32, 1) * sin_a + pltpu.roll(x, 32, 1) * sin_b


def _attn_prep_kernel(aq_ref, ak_ref, av_ref, cq_ref, ck_ref, cv_ref, cos_ref, sa_ref, sb_ref, gq_ref, gk_ref,
                      m_ref, qa_ref, ka_ref, vta_ref, qd_ref, kd_ref, vtd_ref):
    cos, sa, sb = cos_ref[...], sa_ref[...], sb_ref[...]
    vta_ref[...] = av_ref[...].astype(F32).T.astype(BF16)
    for h in range(DF_HEADS):
        vtd_ref[h] = cv_ref[:, h * LANES:(h + 1) * LANES].astype(F32).T.astype(BF16)
    m64 = m_ref[...]
    half = lax.broadcasted_iota(jnp.int32, cos.shape, 1) // HEAD_DIM
    scale = HEAD_DIM ** -0.5 * math.log2(math.e)

    def head_norm(x, g):
        ms = _dot3_l(x * x, m64) * (1.0 / HEAD_DIM)
        return x * lax.rsqrt(ms + EPS) * g

    for s in range(A_HEADS // 2):
        x = aq_ref[:, s * LANES:(s + 1) * LANES].astype(F32)
        y = _rope(head_norm(x, gq_ref[...]), cos, sa, sb) * scale
        y_sw = pltpu.roll(y, HEAD_DIM, 1)
        kvh = (2 * s) // A_GROUP
        for hh in range(2):
            src = y if hh == kvh else y_sw
            qa_ref[2 * s + hh] = jnp.where(half == kvh, src, 0.0).astype(BF16)
    xk = ak_ref[...].astype(F32)
    ka_ref[...] = _rope(head_norm(xk, gk_ref[...]), cos, sa, sb).astype(BF16)
    for h in range(DF_HEADS):
        x = cq_ref[:, h * LANES:(h + 1) * LANES].astype(F32)
        y = _rope(x, cos, sa, sb) * scale
        qd_ref[2 * h] = jnp.where(half == 0, y, 0.0).astype(BF16)
        qd_ref[2 * h + 1] = jnp.where(half == 1, y, 0.0).astype(BF16)
        xk = ck_ref[:, h * LANES:(h + 1) * LANES].astype(F32)
        kd_ref[:, h * LANES:(h + 1) * LANES] = _rope(xk, cos, sa, sb).astype(BF16)


def _attn_prep(P, tabs, g_q, g_k, tk):
    B, T, _ = P.shape
    tm = 256
    per = tk // tm
    cos2, sin_a, sin_b = tabs
    m64 = jnp.asarray(np.kron(np.eye(2), np.ones((HEAD_DIM, HEAD_DIM))), BF16)
    gq2 = jnp.tile(g_q, 2).reshape(1, LANES)
    gk2 = jnp.tile(g_k, 2).reshape(1, LANES)
    tab_spec = pl.BlockSpec((tm, LANES), lambda b, i: (i, 0))
    vec_spec = pl.BlockSpec((1, LANES), lambda b, i: (0, 0))
    return pl.pallas_call(
        _attn_prep_kernel,
        out_shape=(jax.ShapeDtypeStruct((B, A_HEADS, T, LANES), BF16),
                   jax.ShapeDtypeStruct((B, T, LANES), BF16),
                   jax.ShapeDtypeStruct((B, T // tk, LANES, tk), BF16),
                   jax.ShapeDtypeStruct((B, 2 * DF_HEADS, T, LANES), BF16),
                   jax.ShapeDtypeStruct((B, T, 512), BF16),
                   jax.ShapeDtypeStruct((B, DF_HEADS, T // tk, LANES, tk), BF16)),
        grid=(B, T // tm),
        in_specs=[pl.BlockSpec((None, tm, 512), lambda b, i: (b, i, COL_AQ // 512)),
                  pl.BlockSpec((None, tm, LANES), lambda b, i: (b, i, COL_AK // LANES)),
                  pl.BlockSpec((None, tm, LANES), lambda b, i: (b, i, COL_AV // LANES)),
                  pl.BlockSpec((None, tm, 512), lambda b, i: (b, i, COL_CQ // 512)),
                  pl.BlockSpec((None, tm, 512), lambda b, i: (b, i, COL_CK // 512)),
                  pl.BlockSpec((None, tm, 512), lambda b, i: (b, i, COL_CV // 512)),
                  tab_spec, tab_spec, tab_spec, vec_spec, vec_spec,
                  pl.BlockSpec((LANES, LANES), lambda b, i: (0, 0))],
        out_specs=(pl.BlockSpec((None, A_HEADS, tm, LANES), lambda b, i: (b, 0, i, 0)),
                   pl.BlockSpec((None, tm, LANES), lambda b, i: (b, i, 0)),
                   pl.BlockSpec((None, None, LANES, tm), lambda b, i: (b, i // per, 0, i % per)),
                   pl.BlockSpec((None, 2 * DF_HEADS, tm, LANES), lambda b, i: (b, 0, i, 0)),
                   pl.BlockSpec((None, tm, 512), lambda b, i: (b, i, 0)),
                   pl.BlockSpec((None, DF_HEADS, None, LANES, tm), lambda b, i: (b, 0, i // per, 0, i % per))),
        compiler_params=_cparams(("arbitrary", "arbitrary")),
        name="attn_prep",
    )(P, P, P, P, P, P, cos2, sin_a, sin_b, gq2, gk2, m64)


FLASH_TQ = 512
FLASH_PIECE = 256
FLASH_MAX_OVERSHOOT = 60.0
FLASH_GROUP = 2


def _flash_kernel(*refs, mode, cols, tk, n_ctx, n_chunks, ctx_only, lambda_init):
    refs = list(refs)
    q_ref, k_ref, vt_ref = refs[0:3]
    pos = 3
    if mode == "diff":
        lam_ref, g_ref = refs[pos], refs[pos + 1]
        pos += 2
    o_ref, m_scr, l_scr, over_scr, acc_scr = refs[pos:pos + 5]
    q = q_ref[...].reshape(cols, LANES)

    def reset():
        m_scr[...] = jnp.full((1, cols), -jnp.inf, F32)
        l_scr[...] = jnp.zeros((1, cols), F32)
        acc_scr[...] = jnp.zeros((LANES, cols), F32)

    def exact(k, vt):
        s = _dot_nt(k, q)
        m_old = m_scr[...]
        m_new = jnp.maximum(m_old, jnp.max(s, axis=0, keepdims=True))
        alpha = jnp.exp2(m_old - m_new)
        p = jnp.exp2(s - m_new)
        l_scr[...] = alpha * l_scr[...] + jnp.sum(p, axis=0, keepdims=True)
        acc_scr[...] = alpha * acc_scr[...] + _dot(vt, p.astype(BF16))
        m_scr[...] = m_new

    def lagged(k, vt):
        ref = m_scr[...]
        s = _dot_nt(k, q)
        p = jnp.exp2(s - ref)
        m_cur = jnp.max(s, axis=0, keepdims=True)
        m_new = jnp.maximum(ref, m_cur)
        alpha = jnp.exp2(ref - m_new)
        l_scr[...] = (l_scr[...] + jnp.sum(p, axis=0, keepdims=True)) * alpha
        acc_scr[...] = (acc_scr[...] + _dot(vt, p.astype(BF16))) * alpha
        over_scr[...] = jnp.maximum(over_scr[...], m_cur - ref)
        m_scr[...] = m_new

    def k_chunk(c):
        return k_ref[pl.ds(pl.multiple_of(c * tk, tk), tk), :]

    reset()
    if ctx_only:
        exact(k_ref[...], vt_ref[:, tk - n_ctx:])
    else:
        over_scr[...] = jnp.zeros((1, cols), F32)
        exact(k_ref[0:FLASH_PIECE, :], vt_ref[0, :, 0:FLASH_PIECE])
        if tk > FLASH_PIECE:
            lagged(k_ref[FLASH_PIECE:tk, :], vt_ref[0, :, FLASH_PIECE:tk])

        def lagged_group(c0):
            ref = m_scr[...]
            l, acc, over = l_scr[...], acc_scr[...], over_scr[...]
            pending = None
            for u in range(FLASH_GROUP + 1):
                if u < FLASH_GROUP:
                    s = _dot_nt(k_chunk(c0 + u), q)
                    m_cur = jnp.max(s, axis=0, keepdims=True)
                    over = jnp.maximum(over, m_cur - ref)
                    ref_next = jnp.maximum(ref, m_cur)
                if pending is not None:
                    s_p, ref_p, ref_after, c_p = pending
                    p = jnp.exp2(s_p - ref_p)
                    alpha = jnp.exp2(ref_p - ref_after)
                    l = (l + jnp.sum(p, axis=0, keepdims=True)) * alpha
                    acc = (acc + _dot(vt_ref[c_p], p.astype(BF16))) * alpha
                if u < FLASH_GROUP:
                    pending = (s, ref, ref_next, c0 + u)
                    ref = ref_next
            l_scr[...] = l
            acc_scr[...] = acc
            over_scr[...] = over
            m_scr[...] = ref

        def fast_body(it, carry):
            lagged_group(1 + FLASH_GROUP * it)
            return carry
        lax.fori_loop(0, (n_chunks - 1) // FLASH_GROUP, fast_body, 0)

        @pl.when(jnp.max(over_scr[...]) > FLASH_MAX_OVERSHOOT)
        def _():
            reset()

            def exact_body(c, carry):
                exact(k_chunk(c), vt_ref[c])
                return carry
            lax.fori_loop(0, n_chunks, exact_body, 0)

    if mode == "gqa":
        kvh = pl.program_id(1)
        o = acc_scr[pl.ds(pl.multiple_of(kvh * HEAD_DIM, HEAD_DIM), HEAD_DIM), :] / l_scr[...]
        tq = cols // A_GROUP
        pieces = [o[:, g * tq:(g + 1) * tq] for g in range(A_GROUP)]
        o_ref[...] = jnp.concatenate(pieces, axis=0).T.astype(o_ref.dtype)
    else:
        lq = lam_ref[...]
        lam = (jnp.exp(jnp.sum(lq[0:1] * lq[1:2], axis=-1, keepdims=True))
               - jnp.exp(jnp.sum(lq[2:3] * lq[3:4], axis=-1, keepdims=True)) + lambda_init)
        o = acc_scr[...] / l_scr[...]
        d = o[:, :cols // 2] - lam * o[:, cols // 2:]
        y = d * lax.rsqrt(jnp.mean(d * d, axis=0, keepdims=True) + EPS) * g_ref[...]
        o_ref[...] = (y * (1.0 - lambda_init)).T.astype(o_ref.dtype)


def _flash(mode, q, k, vt, n_ctx, ctx_only, extra=(), lambda_init=0.0):
    B, NQ, T, _ = q.shape
    S = T - n_ctx
    tk = vt.shape[-1]
    n_chunks = vt.shape[-3]
    assert (n_chunks - 1) % FLASH_GROUP == 0
    n_maps = A_GROUP if mode == "gqa" else 2
    n_heads = NQ // n_maps
    if ctx_only:
        tq, n_q, q0, rows = n_ctx, 1, S // n_ctx, n_ctx
        k_rows, k0 = n_ctx, S // n_ctx
        vt_blk, vt0 = None, n_chunks - 1
    else:
        tq = FLASH_TQ if mode == "gqa" else 2 * FLASH_TQ
        n_q, q0, rows = S // tq, 0, S
        k_rows, k0 = T, 0
        vt_blk, vt0 = n_chunks, 0
    cols = n_maps * tq
    kern = functools.partial(_flash_kernel, mode=mode, cols=cols, tk=tk, n_ctx=n_ctx,
                             n_chunks=n_chunks, ctx_only=ctx_only, lambda_init=lambda_init)
    in_specs = [pl.BlockSpec((None, n_maps, tq, LANES), lambda b, h, i: (b, h, i + q0, 0))]
    if mode == "gqa":
        in_specs += [pl.BlockSpec((None, k_rows, LANES), lambda b, h, i: (b, k0, 0)),
                     pl.BlockSpec((None, vt_blk, LANES, tk), lambda b, h, i: (b, vt0, 0, 0))]
        out_spec = pl.BlockSpec((None, tq, A_GROUP * HEAD_DIM), lambda b, h, i: (b, i, h))
    else:
        in_specs += [pl.BlockSpec((None, k_rows, LANES), lambda b, h, i: (b, k0, h)),
                     pl.BlockSpec((None, None, vt_blk, LANES, tk), lambda b, h, i: (b, h, vt0, 0, 0)),
                     pl.BlockSpec((4, HEAD_DIM), lambda b, h, i: (0, 0)),
                     pl.BlockSpec((LANES, 1), lambda b, h, i: (0, 0))]
        out_spec = pl.BlockSpec((None, tq, LANES), lambda b, h, i: (b, i, h))
    return pl.pallas_call(
        kern,
        out_shape=jax.ShapeDtypeStruct((B, rows, BR_W), BF16),
        grid=(B, n_heads, n_q),
        in_specs=in_specs,
        out_specs=out_spec,
        scratch_shapes=[pltpu.VMEM((1, cols), F32)] * 3 + [pltpu.VMEM((LANES, cols), F32)],
        compiler_params=_cparams(("arbitrary", "arbitrary", "arbitrary")),
        name="flash_" + mode + ("_ctx" if ctx_only else ""),
    )(q, k, vt, *extra)


def _dn_prep_kernel(q_ref, k_ref, v_ref, ql_ref, kl_ref, vl_ref, qr_ref, kr_ref, vr_ref, ab_ref,
                    wc_ref, alog_ref, dtb_ref, mh_ref, eg_ref, eb_ref, tm_ref, ones_ref,
                    qo_ref, ko_ref, kb_ref, vbk_ref, qdec_ref, kdecbd_ref, decbd_ref, egl_ref,
                    *, tm):
    i = pl.program_id(1)
    nblk = pl.num_programs(1)
    W = DN_HEADS * DN_DK
    row = lax.broadcasted_iota(jnp.int32, (tm, W), 0)
    col = lax.broadcasted_iota(jnp.int32, (tm, W), 1)
    left_ok = jnp.logical_and(i >= 1, i < nblk - 1).astype(F32)
    right_ok = (i < nblk - 2).astype(F32)

    def conv_silu(x_ref, xl_ref, xr_ref, c0):
        x = x_ref[...].astype(F32)
        prev_row = xl_ref[7:8, :].astype(F32) * left_ok
        next_row = xr_ref[0:1, :].astype(F32) * right_ok
        x_prev = jnp.where(row == 0, prev_row, pltpu.roll(x, 1, 0))
        x_next = jnp.where(row == tm - 1, next_row, pltpu.roll(x, tm - 1, 0))
        y = (x_prev * wc_ref[0:1, c0:c0 + W] + x * wc_ref[1:2, c0:c0 + W]
             + x_next * wc_ref[2:3, c0:c0 + W])
        return _silu(y)

    def l2norm(x):
        ss = _dot3_l(x * x, mh_ref[...])
        return x * lax.rsqrt(ss + EPS)

    q = l2norm(conv_silu(q_ref, ql_ref, qr_ref, 0)) * (DN_DK ** -0.5)
    k = l2norm(conv_silu(k_ref, kl_ref, kr_ref, W))
    v = conv_silu(v_ref, vl_ref, vr_ref, 2 * W)

    ab = ab_ref[...]
    z = ab + dtb_ref[...]
    softplus = jnp.maximum(z, 0.0) + jnp.log(1.0 + jnp.exp(-jnp.abs(z)))
    g_all = -jnp.exp(alog_ref[...]) * softplus
    beta_all = jax.nn.sigmoid(ab)

    t_in = row & (DN_CHUNK - 1)
    s_in = col & (DN_CHUNK - 1)

    def put(ref, d, val):
        for h in range(DN_HEADS):
            piece = val[:, h * DN_DK:(h + 1) * DN_DK]
            if d is None:
                ref[h] = piece.astype(ref.dtype)
            else:
                ref[d, h] = piece.astype(ref.dtype)

    lo = lax.broadcasted_iota(jnp.int32, (tm, LANES), 1) < DN_DK
    n_chunks = tm // DN_CHUNK
    bd_shape = (tm, n_chunks * DN_CHUNK)
    same_chunk = (lax.broadcasted_iota(jnp.int32, bd_shape, 0) // DN_CHUNK
                  == lax.broadcasted_iota(jnp.int32, bd_shape, 1) // DN_CHUNK)

    def pair(a, b, h):
        sa = a[:, (h // 2) * LANES:(h // 2 + 1) * LANES]
        sb = b[:, (h // 2) * LANES:(h // 2 + 1) * LANES]
        if h % 2 == 0:
            return jnp.where(lo, sa, pltpu.roll(sb, DN_DK, 1))
        return jnp.where(lo, pltpu.roll(sa, DN_DK, 1), sb)

    def put_block_diag(ref, d, val):
        for h in range(DN_HEADS):
            x = pair(val, val, h)
            ref[d, h] = jnp.where(same_chunk, jnp.concatenate([x] * (n_chunks // 2), axis=1), 0.0).astype(ref.dtype)

    put(qo_ref, None, q)
    put(ko_ref, None, k)
    for d in range(2):
        g_e = _dot3_l(g_all, eg_ref[d])
        beta_e = _dot3_l(beta_all, eb_ref[d])
        gc = _dot3_r(tm_ref[d], g_e)
        g_tot = _dot3_r(ones_ref[...], g_e)
        strict = (t_in > s_in) if d == 0 else (t_in < s_in)
        causal = (t_in >= s_in) if d == 0 else (t_in <= s_in)
        diff = _dot3_r(tm_ref[d], jnp.where(strict, g_e, 0.0))
        decay = jnp.where(causal, jnp.exp(diff), 0.0)
        e_g = jnp.exp(gc)
        kb = k * beta_e
        put(kb_ref, d, kb)
        vb, kbg = v * beta_e, kb * e_g
        for h in range(DN_HEADS):
            vbk_ref[d, h] = pair(vb, kbg, h).astype(vbk_ref.dtype)
        put(qdec_ref, d, q * e_g)
        put_block_diag(kdecbd_ref, d, k * jnp.exp(g_tot - gc))
        put_block_diag(decbd_ref, d, decay)
        e_tot = jnp.exp(g_tot)
        for h in range(DN_HEADS):
            for c in range(tm // DN_CHUNK):
                egl_ref[d, h, c:c + 1, :] = e_tot[c * DN_CHUNK:c * DN_CHUNK + 1, h * DN_DK:(h + 1) * DN_DK]


def _dn_consts():
    n = DN_BLOCK
    t = np.arange(n)
    same = (t[:, None] // DN_CHUNK) == (t[None, :] // DN_CHUNK)
    pre = same & (t[None, :] <= t[:, None])
    suf = same & (t[None, :] >= t[:, None])
    tmat = np.stack([pre, suf]).astype(np.float32)
    ones = same.astype(np.float32)
    W = DN_HEADS * DN_DK
    eg = np.zeros((2, LANES, W), np.float32)
    eb = np.zeros((2, LANES, W), np.float32)
    for d in range(2):
        for h in range(DN_HEADS):
            eg[d, d * DN_HEADS + h, h * DN_DK:(h + 1) * DN_DK] = 1.0
            eb[d, 2 * DN_HEADS + d * DN_HEADS + h, h * DN_DK:(h + 1) * DN_DK] = 1.0
    mh = np.kron(np.eye(DN_HEADS), np.ones((DN_DK, DN_DK))).astype(np.float32)
    return tuple(jnp.asarray(a, BF16) for a in (mh, eg, eb, tmat, ones))


def _dn_prep(P, AB, w_conv, a_log, dt_bias):
    B, T, _ = P.shape
    tm = DN_BLOCK
    nblk = T // tm
    W = DN_HEADS * DN_DK
    mh, eg, eb, tmat, ones = _dn_consts()
    alog_row = jnp.zeros((1, LANES), F32).at[0, :2 * DN_HEADS].set(a_log.reshape(-1))
    dtb_row = jnp.zeros((1, LANES), F32).at[0, :2 * DN_HEADS].set(dt_bias.reshape(-1))
    r8 = tm // 8

    def main(c0):
        return pl.BlockSpec((None, tm, W), lambda b, i: (b, i, c0 // W))

    def left(c0):
        return pl.BlockSpec((None, 8, W), lambda b, i: (b, jnp.maximum(i * r8 - 1, 0), c0 // W))

    def right(c0):
        return pl.BlockSpec((None, 8, W), lambda b, i: (b, jnp.minimum((i + 1) * r8, nblk * r8 - 1), c0 // W))

    def full(a):
        nd = a.ndim
        return pl.BlockSpec(a.shape, lambda b, i: (0,) * nd)

    hm = pl.BlockSpec((None, DN_HEADS, tm, DN_DK), lambda b, i: (b, 0, i, 0))

    def hm2(width):
        return pl.BlockSpec((2, None, DN_HEADS, tm, width), lambda b, i: (0, b, 0, i, 0))

    def sh_hm2(width):
        return jax.ShapeDtypeStruct((2, B, DN_HEADS, T, width), BF16)

    sh_hm = jax.ShapeDtypeStruct((B, DN_HEADS, T, DN_DK), BF16)
    return pl.pallas_call(
        functools.partial(_dn_prep_kernel, tm=tm),
        out_shape=(sh_hm, sh_hm, sh_hm2(DN_DK), sh_hm2(2 * DN_DK), sh_hm2(DN_DK), sh_hm2(tm), sh_hm2(tm),
                   jax.ShapeDtypeStruct((2, B, nblk, DN_HEADS, tm // DN_CHUNK, DN_DK), F32)),
        grid=(B, nblk),
        in_specs=[main(COL_BQ), main(COL_BK), main(COL_BV),
                  left(COL_BQ), left(COL_BK), left(COL_BV),
                  right(COL_BQ), right(COL_BK), right(COL_BV),
                  pl.BlockSpec((None, tm, LANES), lambda b, i: (b, i, 0)),
                  full(w_conv), full(alog_row), full(dtb_row), full(mh), full(eg), full(eb), full(tmat),
                  full(ones)],
        out_specs=(hm, hm, hm2(DN_DK), hm2(2 * DN_DK), hm2(DN_DK), hm2(tm), hm2(tm),
                   pl.BlockSpec((2, None, None, DN_HEADS, tm // DN_CHUNK, DN_DK),
                                lambda b, i: (0, b, i, 0, 0, 0))),
        compiler_params=_cparams(("arbitrary", "arbitrary")),
        name="dn_prep",
    )(P, P, P, P, P, P, P, P, P, AB, w_conv, alog_row, dtb_row, mh, eg, eb, tmat, ones)


def _dn_chunk_kernel(q_ref, k_ref, kb_ref, vbk_ref, qdec_ref, kdecbd_ref, decbd_ref,
                     qeff_ref, o0_ref, mk_ref, n_ref):
    n, C = DN_BLOCK, DN_CHUNK
    nc = n // C
    r = lax.broadcasted_iota(jnp.int32, (n, n), 0)
    c = lax.broadcasted_iota(jnp.int32, (n, n), 1)
    off_diag = r != c
    same_chunk = (r // C) == (c // C)
    rp = lax.broadcasted_iota(jnp.int32, (C, n), 0)
    cp = lax.broadcasted_iota(jnp.int32, (C, n), 1)
    eye_packed = (rp == cp % C).astype(F32)
    own_values = (lax.broadcasted_iota(jnp.int32, (n, 2 * nc * C), 0) // C
                  == lax.broadcasted_iota(jnp.int32, (n, 2 * nc * C), 1) // (2 * C))
    heads = range(DN_HEADS)

    def block_diag(xp):
        xb = xp.astype(BF16)
        return jnp.where(same_chunk, jnp.concatenate([xb] * nc, axis=0), jnp.zeros((), BF16))

    low, qk = [], []
    for h in heads:
        dec = decbd_ref[h].astype(F32)
        k = k_ref[h]
        a = jnp.where(off_diag, _dot_nt(kb_ref[h], k) * dec, 0.0)
        low.append(sum(a[i * C:(i + 1) * C] for i in range(nc)))
        qk.append((_dot_nt(q_ref[h], k) * dec).astype(BF16))
    p = [eye_packed - m for m in low]
    mpow = low
    for _ in range(5):
        mpow = [_dot(m.astype(BF16), block_diag(m)) for m in mpow]
        p = [a + _dot(a.astype(BF16), block_diag(m)) for a, m in zip(p, mpow)]
    uw = []
    for h in heads:
        vals = jnp.where(own_values, jnp.concatenate([vbk_ref[h]] * nc, axis=1), jnp.zeros((), BF16))
        packed = _dot(p[h].astype(BF16), vals)
        uw.append(jnp.concatenate([packed[:, i * 2 * C:(i + 1) * 2 * C] for i in range(nc)],
                                  axis=0).astype(BF16))
    qkuw = [_dot(qk[h], uw[h]) for h in heads]
    for h in heads:
        o0_ref[h] = qkuw[h][:, :DN_DK].astype(o0_ref.dtype)
        qeff_ref[h] = (qdec_ref[h].astype(F32) - qkuw[h][:, DN_DK:]).astype(qeff_ref.dtype)
        kt = _dot_tn(kdecbd_ref[h], uw[h])
        n_ref[h] = kt[:, :DN_DK].astype(n_ref.dtype)
        mk_ref[h] = kt[:, DN_DK:].astype(mk_ref.dtype)


def _dn_chunk(prep):
    q, k, kb, vbk, qdec, kdecbd, decbd, _ = prep
    B, H, T, dk = q.shape
    tm = DN_BLOCK
    hm = pl.BlockSpec((None, H, tm, dk), lambda d, b, i: (b, 0, i, 0))

    def hm2(width):
        return pl.BlockSpec((None, None, H, tm, width), lambda d, b, i: (d, b, 0, i, 0))

    def sh(dtype):
        return jax.ShapeDtypeStruct((2, B, H, T, dk), dtype)

    return pl.pallas_call(
        _dn_chunk_kernel,
        out_shape=(sh(BF16), sh(BF16), sh(BF16), sh(BF16)),
        grid=(2, B, T // tm),
        in_specs=[hm, hm, hm2(dk), hm2(2 * dk), hm2(dk), hm2(tm), hm2(tm)],
        out_specs=(hm2(dk), hm2(dk), hm2(dk), hm2(dk)),
        compiler_params=_cparams(("arbitrary", "arbitrary", "arbitrary")),
        name="dn_chunk",
    )(q, k, kb, vbk, qdec, kdecbd, decbd)


def _dn_scan_kernel(*refs, n_batch):
    ins = (refs[0:5], refs[5:10])
    o_refs = refs[10:12]
    s_scr = refs[12]
    t = pl.program_id(0)
    n_chunks = DN_BLOCK // DN_CHUNK
    C = DN_CHUNK

    @pl.when(t == 0)
    def _():
        s_scr[...] = jnp.zeros_like(s_scr)

    for step in range(n_chunks):
        for d in range(2):
            ci = step if d == 0 else n_chunks - 1 - step
            rows = slice(ci * C, (ci + 1) * C)
            qeff_ref, o0_ref, mk_ref, n_ref, egl_ref = ins[d]
            for b in range(n_batch):
                for h in range(DN_HEADS):
                    s = s_scr[d, b, h]
                    sb = s.astype(BF16)
                    o_refs[d][b, h, rows, :] = (_dot(qeff_ref[b, h, rows, :], sb)
                                                + o0_ref[b, h, rows, :]).astype(o_refs[d].dtype)
                    s_scr[d, b, h] = (s * egl_ref[b, h, ci:ci + 1, :] - _dot(mk_ref[b, h, rows, :], sb)
                                      + n_ref[b, h, rows, :])


def _dn_scan(maps, egl):
    qeff, o0, mk, nn = maps
    _, B, H, T, dk = qeff.shape
    tm = DN_BLOCK
    nblk = T // tm

    def blk(d, t):
        return jnp.where(t == 0, nblk - 1, nblk - 1 - t if d == 1 else t - 1)

    in_specs, args = [], []
    for d in range(2):
        for a in (qeff, o0, mk, nn):
            in_specs.append(pl.BlockSpec((None, B, H, tm, dk), lambda t, d=d: (d, 0, 0, blk(d, t), 0)))
            args.append(a)
        in_specs.append(pl.BlockSpec((None, B, None, H, tm // DN_CHUNK, dk),
                                     lambda t, d=d: (d, 0, blk(d, t), 0, 0, 0)))
        args.append(egl)
    out_specs = tuple(pl.BlockSpec((B, H, tm, dk), lambda t, d=d: (0, 0, blk(d, t), 0)) for d in range(2))
    return pl.pallas_call(
        functools.partial(_dn_scan_kernel, n_batch=B),
        out_shape=(jax.ShapeDtypeStruct((B, H, T, dk), BF16),) * 2,
        grid=(nblk,),
        in_specs=in_specs,
        out_specs=out_specs,
        scratch_shapes=[pltpu.VMEM((2, B, H, dk, dk), F32)],
        compiler_params=_cparams(("arbitrary",)),
        name="dn_scan",
    )(*args)


def _merge_kernel(a_ref, of_ref, ob_ref, z_ref, c_ref, ga_ref, gb_ref, gc_ref, h_ref, mod_ref, g2_ref, gdn_ref,
                  wb_ref, wo_ref, h_out_ref, v_out_ref, b_scr, *, tm, n_lat):
    i = pl.program_id(1)
    D = h_ref.shape[-1]
    z = z_ref[...].astype(F32)
    for hd in range(DN_HEADS):
        o = of_ref[hd].astype(F32) + ob_ref[hd].astype(F32)
        y = o * lax.rsqrt(jnp.mean(o * o, axis=-1, keepdims=True) + EPS) * gdn_ref[...]
        b_scr[:, hd * DN_DK:(hd + 1) * DN_DK] = y * _silu(z[:, hd * DN_DK:(hd + 1) * DN_DK])
    m = jax.nn.sigmoid(ga_ref[...].astype(F32)) * _dot(a_ref[...], wb_ref[0])
    m = m + jax.nn.sigmoid(gb_ref[...].astype(F32)) * _dot(b_scr[...].astype(BF16), wb_ref[1])
    m = m + jax.nn.sigmoid(gc_ref[...].astype(F32)) * _dot(c_ref[...], wb_ref[2])
    y = _dot(m.astype(BF16), wo_ref[...])
    gate = jnp.where(_is_ctx_row(i, tm, n_lat), mod_ref[0:1, 2 * D:3 * D], mod_ref[1:2, 2 * D:3 * D])
    h_new = h_ref[...] + gate * y
    h_out_ref[...] = h_new
    v_out_ref[...] = _norm_mod(h_new, g2_ref[...], mod_ref, i, tm, n_lat, 3, 4).astype(BF16)


def _merge(br_a, o_f, o_b, br_c, P, h, modtab, g_norm2, g_dn_out, w_branch, w_out, n_lat, tm):
    B, T, D = h.shape
    br = pl.BlockSpec((None, tm, BR_W), lambda b, i: (b, i, 0))
    hm = pl.BlockSpec((None, DN_HEADS, tm, DN_DK), lambda b, i: (b, 0, i, 0))

    def gate(j):
        return pl.BlockSpec((None, tm, D), lambda b, i: (b, i, COL_GATE // D + j))

    tok = pl.BlockSpec((None, tm, D), lambda b, i: (b, i, 0))
    return pl.pallas_call(
        functools.partial(_merge_kernel, tm=tm, n_lat=n_lat),
        out_shape=(jax.ShapeDtypeStruct((B, T, D), F32), jax.ShapeDtypeStruct((B, T, D), BF16)),
        grid=(B, T // tm),
        in_specs=[br, hm, hm,
                  pl.BlockSpec((None, tm, BR_W), lambda b, i: (b, i, COL_BZ // BR_W)),
                  br, gate(0), gate(1), gate(2), tok,
                  pl.BlockSpec((None, 2, 6 * D), lambda b, i: (b, 0, 0)),
                  pl.BlockSpec((1, D), lambda b, i: (0, 0)),
                  pl.BlockSpec((1, DN_DK), lambda b, i: (0, 0)),
                  pl.BlockSpec((3, BR_W, D), lambda b, i: (0, 0, 0)),
                  pl.BlockSpec((D, D), lambda b, i: (0, 0))],
        out_specs=(tok, tok),
        scratch_shapes=[pltpu.VMEM((tm, BR_W), F32)],
        compiler_params=_cparams(("arbitrary", "arbitrary")),
        name="merge",
    )(br_a, o_f, o_b, P, br_c, P, P, P, h, modtab, g_norm2.reshape(1, D), g_dn_out.reshape(1, DN_DK),
      w_branch, w_out)


def _router_kernel(v_ref, wr_ref, br_ref, perm_ref, c_ref):
    tn = v_ref.shape[0]
    G = N_GROUPS
    scores = jax.nn.sigmoid(_dot_nt(wr_ref[...], v_ref[...]))
    sel = scores + br_ref[...]
    slabs = [sel[j * G:(j + 1) * G] for j in range(GROUP_SIZE)]
    m1, m2 = slabs[0], jnp.full((G, tn), -jnp.inf, F32)
    for j in range(1, GROUP_SIZE):
        m2 = jnp.maximum(m2, jnp.minimum(m1, slabs[j]))
        m1 = jnp.maximum(m1, slabs[j])
    gs = m1 + m2
    gidx = lax.broadcasted_iota(jnp.int32, (G, tn), 0)
    grank = jnp.zeros((G, tn), jnp.int32)
    for g in range(G):
        rowv = gs[g:g + 1]
        beats = jnp.where(rowv > gs, 1, jnp.where(jnp.logical_and(rowv == gs, g < gidx), 1, 0))
        grank = grank + beats
    gmask = grank < TOPK_GROUPS
    masked = jnp.concatenate([jnp.where(gmask, s, -jnp.inf) for s in slabs], axis=0)
    pos = lax.broadcasted_iota(jnp.int32, (N_EXPERTS, tn), 0)
    orig = (pos % G) * GROUP_SIZE + pos // G
    w = jnp.zeros((N_EXPERTS, tn), F32)
    for _ in range(TOP_K):
        best = jnp.max(masked, axis=0, keepdims=True)
        first = jnp.min(jnp.where(masked == best, orig, N_EXPERTS), axis=0, keepdims=True)
        pick = orig == first
        w = jnp.where(pick, scores, w)
        masked = jnp.where(pick, -jnp.inf, masked)
    denom = jnp.sum(w, axis=0, keepdims=True) + 1e-20
    c = w / denom * ROUTED_SCALE
    c_t = sum(_dot_tn(part, perm_ref[...]) for part in _split3(c))
    for g in range(N_EXPERTS // MOE_EPS):
        c_ref[g] = c_t[:, g * MOE_EPS:(g + 1) * MOE_EPS]


def _router(v_flat, w_router, b_router):
    N, D = v_flat.shape
    tn = 1280 if N % 1280 == 0 else 256
    perm = np.array([(p % N_GROUPS) * GROUP_SIZE + p // N_GROUPS for p in range(N_EXPERTS)])
    wr_t = w_router.T[perm].astype(BF16)
    br = b_router[perm].reshape(N_EXPERTS, 1)
    unperm = jnp.asarray(np.eye(N_EXPERTS)[perm], BF16)
    ngrp = N_EXPERTS // MOE_EPS
    return pl.pallas_call(
        _router_kernel,
        out_shape=jax.ShapeDtypeStruct((ngrp, N, MOE_EPS), F32),
        grid=(N // tn,),
        in_specs=[pl.BlockSpec((tn, D), lambda i: (i, 0)),
                  pl.BlockSpec((N_EXPERTS, D), lambda i: (0, 0)),
                  pl.BlockSpec((N_EXPERTS, 1), lambda i: (0, 0)),
                  pl.BlockSpec((N_EXPERTS, N_EXPERTS), lambda i: (0, 0))],
        out_specs=pl.BlockSpec((ngrp, tn, MOE_EPS), lambda i: (0, i, 0)),
        compiler_params=_cparams(("arbitrary",)),
        name="router",
    )(v_flat, wr_t, br, unperm)


MOE_EPS = 8


def _moe_kernel(x_ref, c_ref, wgu_ref, wd_ref, sgu_ref, sd_ref, h_ref, mod_ref,
                o_ref, acc_ref, *, tm, n_lat):
    i = pl.program_id(1)
    e = pl.program_id(2)
    x = x_ref[...]
    D = x.shape[-1]
    F = MOE_FF

    @pl.when(e == 0)
    def _():
        gu = _dot(x, sgu_ref[...])
        hs = _silu(gu[:, :F]) * gu[:, F:]
        acc_ref[...] = _dot(hs.astype(BF16), sd_ref[...])

    c = c_ref[...]
    parts = []
    for j in range(MOE_EPS):
        gu = _dot(x, wgu_ref[j])
        hj = _silu(gu[:, :F]) * gu[:, F:] * c[:, j:j + 1]
        parts.append(hj.astype(BF16))
    hcat = jnp.concatenate(parts, axis=-1)
    acc_ref[...] += _dot(hcat, wd_ref[...])

    @pl.when(e == pl.num_programs(2) - 1)
    def _():
        gate = jnp.where(_is_ctx_row(i, tm, n_lat), mod_ref[0:1, 5 * D:6 * D], mod_ref[1:2, 5 * D:6 * D])
        o_ref[...] = h_ref[...] + gate * acc_ref[...]


def _moe_weights(w_gate, w_up, w_down):
    E, D, F = w_gate.shape
    gu = jnp.concatenate([w_gate, w_up], axis=-1).astype(BF16)
    return gu, w_down.astype(BF16).reshape(E * F, D)


def _moe(v, c_grp, h, modtab, wgu, wd, sgu, sd, n_lat, tm):
    B, T, D = v.shape
    F = MOE_FF
    ngrp = N_EXPERTS // MOE_EPS
    tok = pl.BlockSpec((None, tm, D), lambda b, i, e: (b, i, 0))
    return pl.pallas_call(
        functools.partial(_moe_kernel, tm=tm, n_lat=n_lat),
        out_shape=jax.ShapeDtypeStruct((B, T, D), F32),
        grid=(B, T // tm, ngrp),
        in_specs=[tok,
                  pl.BlockSpec((None, None, tm, MOE_EPS), lambda b, i, e: (e, b, i, 0)),
                  pl.BlockSpec((MOE_EPS, D, 2 * F), lambda b, i, e: (e, 0, 0)),
                  pl.BlockSpec((MOE_EPS * F, D), lambda b, i, e: (e, 0)),
                  pl.BlockSpec((D, 2 * F), lambda b, i, e: (0, 0)),
                  pl.BlockSpec((F, D), lambda b, i, e: (0, 0)),
                  tok,
                  pl.BlockSpec((None, 2, 6 * D), lambda b, i, e: (b, 0, 0))],
        out_specs=tok,
        scratch_shapes=[pltpu.VMEM((tm, D), F32)],
        compiler_params=_cparams(("arbitrary", "arbitrary", "arbitrary")),
        name="moe",
    )(v, c_grp, wgu, wd, sgu, sd, h, modtab)


def _final_kernel(h_ref, g_ref, o_ref):
    x = h_ref[...]
    o_ref[...] = x * lax.rsqrt(jnp.mean(x * x, axis=-1, keepdims=True) + EPS) * g_ref[...]


def _final_norm(h, g_final, S):
    B, T, D = h.shape
    tm = 1024
    return pl.pallas_call(
        _final_kernel,
        out_shape=jax.ShapeDtypeStruct((B, S, D), F32),
        grid=(B, S // tm),
        in_specs=[pl.BlockSpec((None, tm, D), lambda b, i: (b, i, 0)),
                  pl.BlockSpec((1, D), lambda b, i: (0, 0))],
        out_specs=pl.BlockSpec((None, tm, D), lambda b, i: (b, i, 0)),
        compiler_params=_cparams(("arbitrary", "arbitrary")),
        name="final_norm",
    )(h, g_final.reshape(1, D))


def _rope_tables(S, n_ctx):
    rows = S // GRID_W
    row = jnp.repeat(jnp.arange(rows, dtype=F32), GRID_W)
    col = (jnp.arange(S) % GRID_W).astype(F32)
    axis_dim = HEAD_DIM // 2
    inv = 1.0 / (ROPE_THETA ** (jnp.arange(0, axis_dim, 2, dtype=F32) / axis_dim))
    ang = jnp.concatenate([row[:, None] * inv, col[:, None] * inv], axis=-1)
    ang = jnp.concatenate([ang, ang], axis=-1)
    cos = jnp.concatenate([jnp.cos(ang), jnp.ones((n_ctx, HEAD_DIM), F32)], axis=0)
    sin = jnp.concatenate([jnp.sin(ang), jnp.zeros((n_ctx, HEAD_DIM), F32)], axis=0)
    first = (jnp.arange(HEAD_DIM) < HEAD_DIM // 2)[None, :]
    sin_a = jnp.where(first, -sin, 0.0)
    sin_b = jnp.where(first, 0.0, sin)
    return tuple(jnp.tile(t, (1, 2)) for t in (cos, sin_a, sin_b))


def _permute_w_in(w):
    splits = np.cumsum([512, 128, 128, 1536, 512, 16, 16, 512, 512, 512, 3072])[:-1].tolist()
    aq, ak, av, bqkv, bz, ba, bb, cq, ck, cv, gate = jnp.split(w, splits, axis=-1)
    pad = jnp.zeros((w.shape[0], IN_W_PAD - COL_AB - 32), w.dtype)
    out = jnp.concatenate([aq, cq, ck, cv, bz, bqkv, gate, ak, av, ba, bb, pad], axis=-1)
    return out.astype(BF16)


def kernel(x, c, ctx, c_ctx, w_mod, b_mod, g_norm1, g_norm2, w_in, g_qnorm, g_knorm, w_conv, a_log, dt_bias, g_dn_out, lam_qk, g_subln, w_branch, w_out, w_router, b_router, w_e_gate, w_e_up, w_e_down, w_s_gate, w_s_up, w_s_down, g_final):
    B, S, D = x.shape
    n_ctx = ctx.shape[1]
    T = n_ctx + S
    L = w_mod.shape[0]
    N = B * T
    assert n_ctx == DN_BLOCK and S % 1024 == 0 and D == D_MODEL
    tm_tok = 1280 if T % 1280 == 0 else 256
    tm_merge = 640 if T % 640 == 0 else 256

    cond = jnp.zeros((8, D), F32).at[0].set(c_ctx).at[1:1 + B].set(c)
    mod = _mod_vectors(cond, w_mod, b_mod)
    tabs = _rope_tables(S, n_ctx)
    h = jnp.concatenate([x, ctx], axis=1)
    tk = 1280 if T % 1280 == 0 else 256

    for l in range(L):
        lambda_init = 0.8 - 0.6 * math.exp(-0.3 * l)
        modtab = jnp.stack([jnp.broadcast_to(mod[l, 0], (B, 6 * D)), mod[l, 1:1 + B]], axis=1)
        P, AB = _in_projection(h, modtab, g_norm1[l], _permute_w_in(w_in[l]), S, tm_tok)

        qa, ka, vta, qd, kd, vtd = _attn_prep(P, tabs, g_qnorm[l], g_knorm[l], tk)
        dx = (lam_qk[l], g_subln[l].reshape(LANES, 1))
        br_a = jnp.concatenate([_flash("gqa", qa, ka, vta, n_ctx, False),
                                _flash("gqa", qa, ka, vta, n_ctx, True)], axis=1)
        br_c = jnp.concatenate([_flash("diff", qd, kd, vtd, n_ctx, False, dx, lambda_init),
                                _flash("diff", qd, kd, vtd, n_ctx, True, dx, lambda_init)], axis=1)

        prep = _dn_prep(P, AB, w_conv[l], a_log[l], dt_bias[l])
        o_f, o_b = _dn_scan(_dn_chunk(prep), prep[-1])

        h, v = _merge(br_a, o_f, o_b, br_c, P, h, modtab, g_norm2[l], g_dn_out[l], w_branch[l].astype(BF16),
                      w_out[l].astype(BF16), S, tm_merge)

        c_grp = _router(v.reshape(N, D), w_router[l], b_router[l]).reshape(-1, B, T, MOE_EPS)
        wgu, wd = _moe_weights(w_e_gate[l], w_e_up[l], w_e_down[l])
        sgu = jnp.concatenate([w_s_gate[l], w_s_up[l]], axis=-1).astype(BF16)
        h = _moe(v, c_grp, h, modtab, wgu, wd, sgu, w_s_down[l].astype(BF16), S, tm_merge)

    return _final_norm(h, g_final, S)
```

```python
import functools
import math

import numpy as np
import jax
import jax.numpy as jnp
from jax import lax
from jax.experimental import pallas as pl
from jax.experimental.pallas import tpu as pltpu

F32 = jnp.float32
BF16 = jnp.bfloat16

D_MODEL = 1024
GRID_W = 64
EPS = 1e-6
ROPE_THETA = 10000.0
HEAD_DIM = 64
LANES = 128
A_HEADS = 8
A_KV_HEADS = 2
A_GROUP = A_HEADS // A_KV_HEADS
DN_HEADS = 8
DN_DK = 64
DN_CHUNK = 64
DN_BLOCK = 256
DF_HEADS = 4
N_EXPERTS = 64
TOP_K = 6
N_GROUPS = 8
TOPK_GROUPS = 4
GROUP_SIZE = N_EXPERTS // N_GROUPS
MOE_FF = 256
ROUTED_SCALE = 2.5
BR_W = 512

COL_AQ, COL_CQ, COL_CK, COL_CV, COL_BZ = 0, 512, 1024, 1536, 2048
COL_BQ, COL_BK, COL_BV = 2560, 3072, 3584
COL_GATE = 4096
COL_AK, COL_AV, COL_AB = 7168, 7296, 7424
IN_W_PAD = 7680
IN_TN = 1536

VMEM_LIMIT = 56 * 1024 * 1024


def _cparams(sem):
    return pltpu.CompilerParams(dimension_semantics=sem, vmem_limit_bytes=VMEM_LIMIT)


def _dot(a, b):
    return jnp.dot(a, b, preferred_element_type=F32)


def _dot_nt(a, b):
    return lax.dot_general(a, b, (((1,), (1,)), ((), ())), preferred_element_type=F32)


def _dot_tn(a, b):
    return lax.dot_general(a, b, (((0,), (0,)), ((), ())), preferred_element_type=F32)


def _split3(x):
    hi = x.astype(BF16)
    r = x - hi.astype(F32)
    mid = r.astype(BF16)
    lo = (r - mid.astype(F32)).astype(BF16)
    return hi, mid, lo


def _dot3_l(x, m):
    hi, mid, lo = _split3(x)
    return _dot(hi, m) + _dot(mid, m) + _dot(lo, m)


def _dot3_r(m, x):
    hi, mid, lo = _split3(x)
    return _dot(m, hi) + _dot(m, mid) + _dot(m, lo)


def _silu(x):
    return x * jax.nn.sigmoid(x)


def _mod_kernel(cond_ref, w_ref, b_ref, o_ref):
    a = _silu(cond_ref[...]).astype(BF16)
    o_ref[...] = _dot(a, w_ref[...].astype(BF16)) + b_ref[...]


def _mod_vectors(cond, w_mod, b_mod):
    L, D, W = w_mod.shape
    tn = 1024
    return pl.pallas_call(
        _mod_kernel,
        out_shape=jax.ShapeDtypeStruct((L, 8, W), F32),
        grid=(L, W // tn),
        in_specs=[pl.BlockSpec((8, D), lambda l, j: (0, 0)),
                  pl.BlockSpec((None, D, tn), lambda l, j: (l, 0, j)),
                  pl.BlockSpec((None, 1, tn), lambda l, j: (l, 0, j))],
        out_specs=pl.BlockSpec((None, 8, tn), lambda l, j: (l, 0, j)),
        compiler_params=_cparams(("arbitrary", "arbitrary")),
        name="mod_vectors",
    )(cond, w_mod, b_mod.reshape(L, 1, W))


def _is_ctx_row(blk, tm, n_lat):
    return blk * tm + lax.broadcasted_iota(jnp.int32, (tm, 1), 0) >= n_lat


def _norm_mod(x, g, mod_ref, blk, tm, n_lat, sh, sc):
    D = x.shape[-1]
    y = x * lax.rsqrt(jnp.mean(x * x, axis=-1, keepdims=True) + EPS) * g
    is_ctx = _is_ctx_row(blk, tm, n_lat)
    scale = jnp.where(is_ctx, mod_ref[0:1, sc * D:(sc + 1) * D], mod_ref[1:2, sc * D:(sc + 1) * D])
    shift = jnp.where(is_ctx, mod_ref[0:1, sh * D:(sh + 1) * D], mod_ref[1:2, sh * D:(sh + 1) * D])
    return y * (1.0 + scale) + shift


def _inproj_kernel(h_ref, mod_ref, g_ref, w_ref, p_ref, ab_ref, u_scr, *, tm, n_lat, ab_off):
    i = pl.program_id(1)
    j = pl.program_id(2)

    @pl.when(j == 0)
    def _():
        u_scr[...] = _norm_mod(h_ref[...], g_ref[...], mod_ref, i, tm, n_lat, 0, 1).astype(BF16)

    r = _dot(u_scr[...], w_ref[...])
    p_ref[...] = r.astype(BF16)

    @pl.when(j == pl.num_programs(2) - 1)
    def _():
        ab_ref[...] = r[:, ab_off:ab_off + LANES]


def _in_projection(h, modtab, g_norm, w_perm, n_lat, tm):
    B, T, D = h.shape
    ncol = IN_W_PAD // IN_TN
    kern = functools.partial(_inproj_kernel, tm=tm, n_lat=n_lat, ab_off=COL_AB - (ncol - 1) * IN_TN)
    return pl.pallas_call(
        kern,
        out_shape=(jax.ShapeDtypeStruct((B, T, IN_W_PAD), BF16),
                   jax.ShapeDtypeStruct((B, T, LANES), F32)),
        grid=(B, T // tm, ncol),
        in_specs=[pl.BlockSpec((None, tm, D), lambda b, i, j: (b, i, 0)),
                  pl.BlockSpec((None, 2, 6 * D), lambda b, i, j: (b, 0, 0)),
                  pl.BlockSpec((1, D), lambda b, i, j: (0, 0)),
                  pl.BlockSpec((D, IN_TN), lambda b, i, j: (0, j))],
        out_specs=(pl.BlockSpec((None, tm, IN_TN), lambda b, i, j: (b, i, j)),
                   pl.BlockSpec((None, tm, LANES), lambda b, i, j: (b, i, 0))),
        scratch_shapes=[pltpu.VMEM((tm, D), BF16)],
        compiler_params=_cparams(("arbitrary", "arbitrary", "arbitrary")),
        name="in_projection",
    )(h, modtab, g_norm.reshape(1, D), w_perm)


def _rope(x, cos, sin_a, sin_b):
    return x * cos + pltpu.roll(x, LANES - 32, 1) * sin_a + pltpu.roll(x, 32, 1) * sin_b


def _attn_prep_kernel(aq_ref, ak_ref, av_ref, cq_ref, ck_ref, cv_ref, cos_ref, sa_ref, sb_ref, gq_ref, gk_ref,
                      m_ref, qa_ref, ka_ref, vta_ref, qd_ref, kd_ref, vtd_ref):
    cos, sa, sb = cos_ref[...], sa_ref[...], sb_ref[...]
    vta_ref[...] = av_ref[...].astype(F32).T.astype(BF16)
    for h in range(DF_HEADS):
        vtd_ref[h] = cv_ref[:, h * LANES:(h + 1) * LANES].astype(F32).T.astype(BF16)
    m64 = m_ref[...]
    half = lax.broadcasted_iota(jnp.int32, cos.shape, 1) // HEAD_DIM
    scale = HEAD_DIM ** -0.5 * math.log2(math.e)

    def head_norm(x, g):
        ms = _dot3_l(x * x, m64) * (1.0 / HEAD_DIM)
        return x * lax.rsqrt(ms + EPS) * g

    for s in range(A_HEADS // 2):
        x = aq_ref[:, s * LANES:(s + 1) * LANES].astype(F32)
        y = _rope(head_norm(x, gq_ref[...]), cos, sa, sb) * scale
        y_sw = pltpu.roll(y, HEAD_DIM, 1)
        kvh = (2 * s) // A_GROUP
        for hh in range(2):
            src = y if hh == kvh else y_sw
            qa_ref[2 * s + hh] = jnp.where(half == kvh, src, 0.0).astype(BF16)
    xk = ak_ref[...].astype(F32)
    ka_ref[...] = _rope(head_norm(xk, gk_ref[...]), cos, sa, sb).astype(BF16)
    for h in range(DF_HEADS):
        x = cq_ref[:, h * LANES:(h + 1) * LANES].astype(F32)
        y = _rope(x, cos, sa, sb) * scale
        qd_ref[2 * h] = jnp.where(half == 0, y, 0.0).astype(BF16)
        qd_ref[2 * h + 1] = jnp.where(half == 1, y, 0.0).astype(BF16)
        xk = ck_ref[:, h * LANES:(h + 1) * LANES].astype(F32)
        kd_ref[:, h * LANES:(h + 1) * LANES] = _rope(xk, cos, sa, sb).astype(BF16)


def _attn_prep(P, tabs, g_q, g_k, tk):
    B, T, _ = P.shape
    tm = 256
    per = tk // tm
    cos2, sin_a, sin_b = tabs
    m64 = jnp.asarray(np.kron(np.eye(2), np.ones((HEAD_DIM, HEAD_DIM))), BF16)
    gq2 = jnp.tile(g_q, 2).reshape(1, LANES)
    gk2 = jnp.tile(g_k, 2).reshape(1, LANES)
    tab_spec = pl.BlockSpec((tm, LANES), lambda b, i: (i, 0))
    vec_spec = pl.BlockSpec((1, LANES), lambda b, i: (0, 0))
    return pl.pallas_call(
        _attn_prep_kernel,
        out_shape=(jax.ShapeDtypeStruct((B, A_HEADS, T, LANES), BF16),
                   jax.ShapeDtypeStruct((B, T, LANES), BF16),
                   jax.ShapeDtypeStruct((B, T // tk, LANES, tk), BF16),
                   jax.ShapeDtypeStruct((B, 2 * DF_HEADS, T, LANES), BF16),
                   jax.ShapeDtypeStruct((B, T, 512), BF16),
                   jax.ShapeDtypeStruct((B, DF_HEADS, T // tk, LANES, tk), BF16)),
        grid=(B, T // tm),
        in_specs=[pl.BlockSpec((None, tm, 512), lambda b, i: (b, i, COL_AQ // 512)),
                  pl.BlockSpec((None, tm, LANES), lambda b, i: (b, i, COL_AK // LANES)),
                  pl.BlockSpec((None, tm, LANES), lambda b, i: (b, i, COL_AV // LANES)),
                  pl.BlockSpec((None, tm, 512), lambda b, i: (b, i, COL_CQ // 512)),
                  pl.BlockSpec((None, tm, 512), lambda b, i: (b, i, COL_CK // 512)),
                  pl.BlockSpec((None, tm, 512), lambda b, i: (b, i, COL_CV // 512)),
                  tab_spec, tab_spec, tab_spec, vec_spec, vec_spec,
                  pl.BlockSpec((LANES, LANES), lambda b, i: (0, 0))],
        out_specs=(pl.BlockSpec((None, A_HEADS, tm, LANES), lambda b, i: (b, 0, i, 0)),
                   pl.BlockSpec((None, tm, LANES), lambda b, i: (b, i, 0)),
                   pl.BlockSpec((None, None, LANES, tm), lambda b, i: (b, i // per, 0, i % per)),
                   pl.BlockSpec((None, 2 * DF_HEADS, tm, LANES), lambda b, i: (b, 0, i, 0)),
                   pl.BlockSpec((None, tm, 512), lambda b, i: (b, i, 0)),
                   pl.BlockSpec((None, DF_HEADS, None, LANES, tm), lambda b, i: (b, 0, i // per, 0, i % per))),
        compiler_params=_cparams(("arbitrary", "arbitrary")),
        name="attn_prep",
    )(P, P, P, P, P, P, cos2, sin_a, sin_b, gq2, gk2, m64)


FLASH_TQ = 512
FLASH_PIECE = 256
FLASH_MAX_OVERSHOOT = 60.0
FLASH_GROUP = 2


def _flash_kernel(*refs, mode, cols, tk, n_ctx, n_chunks, ctx_only, lambda_init):
    refs = list(refs)
    q_ref, k_ref, vt_ref = refs[0:3]
    pos = 3
    if mode == "diff":
        lam_ref, g_ref = refs[pos], refs[pos + 1]
        pos += 2
    o_ref, m_scr, l_scr, over_scr, acc_scr = refs[pos:pos + 5]
    q = q_ref[...].reshape(cols, LANES)

    def reset():
        m_scr[...] = jnp.full((1, cols), -jnp.inf, F32)
        l_scr[...] = jnp.zeros((1, cols), F32)
        acc_scr[...] = jnp.zeros((LANES, cols), F32)

    def exact(k, vt):
        s = _dot_nt(k, q)
        m_old = m_scr[...]
        m_new = jnp.maximum(m_old, jnp.max(s, axis=0, keepdims=True))
        alpha = jnp.exp2(m_old - m_new)
        p = jnp.exp2(s - m_new)
        l_scr[...] = alpha * l_scr[...] + jnp.sum(p, axis=0, keepdims=True)
        acc_scr[...] = alpha * acc_scr[...] + _dot(vt, p.astype(BF16))
        m_scr[...] = m_new

    def lagged(k, vt):
        ref = m_scr[...]
        s = _dot_nt(k, q)
        p = jnp.exp2(s - ref)
        m_cur = jnp.max(s, axis=0, keepdims=True)
        m_new = jnp.maximum(ref, m_cur)
        alpha = jnp.exp2(ref - m_new)
        l_scr[...] = (l_scr[...] + jnp.sum(p, axis=0, keepdims=True)) * alpha
        acc_scr[...] = (acc_scr[...] + _dot(vt, p.astype(BF16))) * alpha
        over_scr[...] = jnp.maximum(over_scr[...], m_cur - ref)
        m_scr[...] = m_new

    def k_chunk(c):
        return k_ref[pl.ds(pl.multiple_of(c * tk, tk), tk), :]

    reset()
    if ctx_only:
        exact(k_ref[...], vt_ref[:, tk - n_ctx:])
    else:
        over_scr[...] = jnp.zeros((1, cols), F32)
        exact(k_ref[0:FLASH_PIECE, :], vt_ref[0, :, 0:FLASH_PIECE])
        if tk > FLASH_PIECE:
            lagged(k_ref[FLASH_PIECE:tk, :], vt_ref[0, :, FLASH_PIECE:tk])

        def lagged_group(c0):
            ref = m_scr[...]
            l, acc, over = l_scr[...], acc_scr[...], over_scr[...]
            pending = None
            for u in range(FLASH_GROUP + 1):
                if u < FLASH_GROUP:
                    s = _dot_nt(k_chunk(c0 + u), q)
                    m_cur = jnp.max(s, axis=0, keepdims=True)
                    over = jnp.maximum(over, m_cur - ref)
                    ref_next = jnp.maximum(ref, m_cur)
                if pending is not None:
                    s_p, ref_p, ref_after, c_p = pending
                    p = jnp.exp2(s_p - ref_p)
                    alpha = jnp.exp2(ref_p - ref_after)
                    l = (l + jnp.sum(p, axis=0, keepdims=True)) * alpha
                    acc = (acc + _dot(vt_ref[c_p], p.astype(BF16))) * alpha
                if u < FLASH_GROUP:
                    pending = (s, ref, ref_next, c0 + u)
                    ref = ref_next
            l_scr[...] = l
            acc_scr[...] = acc
            over_scr[...] = over
            m_scr[...] = ref

        def fast_body(it, carry):
            lagged_group(1 + FLASH_GROUP * it)
            return carry
        lax.fori_loop(0, (n_chunks - 1) // FLASH_GROUP, fast_body, 0)

        @pl.when(jnp.max(over_scr[...]) > FLASH_MAX_OVERSHOOT)
        def _():
            reset()

            def exact_body(c, carry):
                exact(k_chunk(c), vt_ref[c])
                return carry
            lax.fori_loop(0, n_chunks, exact_body, 0)

    if mode == "gqa":
        kvh = pl.program_id(1)
        o = acc_scr[pl.ds(pl.multiple_of(kvh * HEAD_DIM, HEAD_DIM), HEAD_DIM), :] / l_scr[...]
        tq = cols // A_GROUP
        pieces = [o[:, g * tq:(g + 1) * tq] for g in range(A_GROUP)]
        o_ref[...] = jnp.concatenate(pieces, axis=0).T.astype(o_ref.dtype)
    else:
        lq = lam_ref[...]
        lam = (jnp.exp(jnp.sum(lq[0:1] * lq[1:2], axis=-1, keepdims=True))
               - jnp.exp(jnp.sum(lq[2:3] * lq[3:4], axis=-1, keepdims=True)) + lambda_init)
        o = acc_scr[...] / l_scr[...]
        d = o[:, :cols // 2] - lam * o[:, cols // 2:]
        y = d * lax.rsqrt(jnp.mean(d * d, axis=0, keepdims=True) + EPS) * g_ref[...]
        o_ref[...] = (y * (1.0 - lambda_init)).T.astype(o_ref.dtype)


def _flash(mode, q, k, vt, n_ctx, ctx_only, extra=(), lambda_init=0.0):
    B, NQ, T, _ = q.shape
    S = T - n_ctx
    tk = vt.shape[-1]
    n_chunks = vt.shape[-3]
    assert (n_chunks - 1) % FLASH_GROUP == 0
    n_maps = A_GROUP if mode == "gqa" else 2
    n_heads = NQ // n_maps
    if ctx_only:
        tq, n_q, q0, rows = n_ctx, 1, S // n_ctx, n_ctx
        k_rows, k0 = n_ctx, S // n_ctx
        vt_blk, vt0 = None, n_chunks - 1
    else:
        tq = FLASH_TQ if mode == "gqa" else 2 * FLASH_TQ
        n_q, q0, rows = S // tq, 0, S
        k_rows, k0 = T, 0
        vt_blk, vt0 = n_chunks, 0
    cols = n_maps * tq
    kern = functools.partial(_flash_kernel, mode=mode, cols=cols, tk=tk, n_ctx=n_ctx,
                             n_chunks=n_chunks, ctx_only=ctx_only, lambda_init=lambda_init)
    in_specs = [pl.BlockSpec((None, n_maps, tq, LANES), lambda b, h, i: (b, h, i + q0, 0))]
    if mode == "gqa":
        in_specs += [pl.BlockSpec((None, k_rows, LANES), lambda b, h, i: (b, k0, 0)),
                     pl.BlockSpec((None, vt_blk, LANES, tk), lambda b, h, i: (b, vt0, 0, 0))]
        out_spec = pl.BlockSpec((None, tq, A_GROUP * HEAD_DIM), lambda b, h, i: (b, i, h))
    else:
        in_specs += [pl.BlockSpec((None, k_rows, LANES), lambda b, h, i: (b, k0, h)),
                     pl.BlockSpec((None, None, vt_blk, LANES, tk), lambda b, h, i: (b, h, vt0, 0, 0)),
                     pl.BlockSpec((4, HEAD_DIM), lambda b, h, i: (0, 0)),
                     pl.BlockSpec((LANES, 1), lambda b, h, i: (0, 0))]
        out_spec = pl.BlockSpec((None, tq, LANES), lambda b, h, i: (b, i, h))
    return pl.pallas_call(
        kern,
        out_shape=jax.ShapeDtypeStruct((B, rows, BR_W), BF16),
        grid=(B, n_heads, n_q),
        in_specs=in_specs,
        out_specs=out_spec,
        scratch_shapes=[pltpu.VMEM((1, cols), F32)] * 3 + [pltpu.VMEM((LANES, cols), F32)],
        compiler_params=_cparams(("arbitrary", "arbitrary", "arbitrary")),
        name="flash_" + mode + ("_ctx" if ctx_only else ""),
    )(q, k, vt, *extra)


def _dn_prep_kernel(q_ref, k_ref, v_ref, ql_ref, kl_ref, vl_ref, qr_ref, kr_ref, vr_ref, ab_ref,
                    wc_ref, alog_ref, dtb_ref, mh_ref, eg_ref, eb_ref, tm_ref, ones_ref,
                    qo_ref, ko_ref, kb_ref, vbk_ref, qdec_ref, kdecbd_ref, decbd_ref, egl_ref,
                    *, tm):
    i = pl.program_id(1)
    nblk = pl.num_programs(1)
    W = DN_HEADS * DN_DK
    row = lax.broadcasted_iota(jnp.int32, (tm, W), 0)
    col = lax.broadcasted_iota(jnp.int32, (tm, W), 1)
    left_ok = jnp.logical_and(i >= 1, i < nblk - 1).astype(F32)
    right_ok = (i < nblk - 2).astype(F32)

    def conv_silu(x_ref, xl_ref, xr_ref, c0):
        x = x_ref[...].astype(F32)
        prev_row = xl_ref[7:8, :].astype(F32) * left_ok
        next_row = xr_ref[0:1, :].astype(F32) * right_ok
        x_prev = jnp.where(row == 0, prev_row, pltpu.roll(x, 1, 0))
        x_next = jnp.where(row == tm - 1, next_row, pltpu.roll(x, tm - 1, 0))
        y = (x_prev * wc_ref[0:1, c0:c0 + W] + x * wc_ref[1:2, c0:c0 + W]
             + x_next * wc_ref[2:3, c0:c0 + W])
        return _silu(y)

    def l2norm(x):
        ss = _dot3_l(x * x, mh_ref[...])
        return x * lax.rsqrt(ss + EPS)

    q = l2norm(conv_silu(q_ref, ql_ref, qr_ref, 0)) * (DN_DK ** -0.5)
    k = l2norm(conv_silu(k_ref, kl_ref, kr_ref, W))
    v = conv_silu(v_ref, vl_ref, vr_ref, 2 * W)

    ab = ab_ref[...]
    z = ab + dtb_ref[...]
    softplus = jnp.maximum(z, 0.0) + jnp.log(1.0 + jnp.exp(-jnp.abs(z)))
    g_all = -jnp.exp(alog_ref[...]) * softplus
    beta_all = jax.nn.sigmoid(ab)

    t_in = row & (DN_CHUNK - 1)
    s_in = col & (DN_CHUNK - 1)

    def put(ref, d, val):
        for h in range(DN_HEADS):
            piece = val[:, h * DN_DK:(h + 1) * DN_DK]
            if d is None:
                ref[h] = piece.astype(ref.dtype)
            else:
                ref[d, h] = piece.astype(ref.dtype)

    lo = lax.broadcasted_iota(jnp.int32, (tm, LANES), 1) < DN_DK
    n_chunks = tm // DN_CHUNK
    bd_shape = (tm, n_chunks * DN_CHUNK)
    same_chunk = (lax.broadcasted_iota(jnp.int32, bd_shape, 0) // DN_CHUNK
                  == lax.broadcasted_iota(jnp.int32, bd_shape, 1) // DN_CHUNK)

    def pair(a, b, h):
        sa = a[:, (h // 2) * LANES:(h // 2 + 1) * LANES]
        sb = b[:, (h // 2) * LANES:(h // 2 + 1) * LANES]
        if h % 2 == 0:
            return jnp.where(lo, sa, pltpu.roll(sb, DN_DK, 1))
        return jnp.where(lo, pltpu.roll(sa, DN_DK, 1), sb)

    def put_block_diag(ref, d, val):
        for h in range(DN_HEADS):
            x = pair(val, val, h)
            ref[d, h] = jnp.where(same_chunk, jnp.concatenate([x] * (n_chunks // 2), axis=1), 0.0).astype(ref.dtype)

    put(qo_ref, None, q)
    put(ko_ref, None, k)
    for d in range(2):
        g_e = _dot3_l(g_all, eg_ref[d])
        beta_e = _dot3_l(beta_all, eb_ref[d])
        gc = _dot3_r(tm_ref[d], g_e)
        g_tot = _dot3_r(ones_ref[...], g_e)
        strict = (t_in > s_in) if d == 0 else (t_in < s_in)
        causal = (t_in >= s_in) if d == 0 else (t_in <= s_in)
        diff = _dot3_r(tm_ref[d], jnp.where(strict, g_e, 0.0))
        decay = jnp.where(causal, jnp.exp(diff), 0.0)
        e_g = jnp.exp(gc)
        kb = k * beta_e
        put(kb_ref, d, kb)
        vb, kbg = v * beta_e, kb * e_g
        for h in range(DN_HEADS):
            vbk_ref[d, h] = pair(vb, kbg, h).astype(vbk_ref.dtype)
        put(qdec_ref, d, q * e_g)
        put_block_diag(kdecbd_ref, d, k * jnp.exp(g_tot - gc))
        put_block_diag(decbd_ref, d, decay)
        e_tot = jnp.exp(g_tot)
        for h in range(DN_HEADS):
            for c in range(tm // DN_CHUNK):
                egl_ref[d, h, c:c + 1, :] = e_tot[c * DN_CHUNK:c * DN_CHUNK + 1, h * DN_DK:(h + 1) * DN_DK]


def _dn_consts():
    n = DN_BLOCK
    t = np.arange(n)
    same = (t[:, None] // DN_CHUNK) == (t[None, :] // DN_CHUNK)
    pre = same & (t[None, :] <= t[:, None])
    suf = same & (t[None, :] >= t[:, None])
    tmat = np.stack([pre, suf]).astype(np.float32)
    ones = same.astype(np.float32)
    W = DN_HEADS * DN_DK
    eg = np.zeros((2, LANES, W), np.float32)
    eb = np.zeros((2, LANES, W), np.float32)
    for d in range(2):
        for h in range(DN_HEADS):
            eg[d, d * DN_HEADS + h, h * DN_DK:(h + 1) * DN_DK] = 1.0
            eb[d, 2 * DN_HEADS + d * DN_HEADS + h, h * DN_DK:(h + 1) * DN_DK] = 1.0
    mh = np.kron(np.eye(DN_HEADS), np.ones((DN_DK, DN_DK))).astype(np.float32)
    return tuple(jnp.asarray(a, BF16) for a in (mh, eg, eb, tmat, ones))


def _dn_prep(P, AB, w_conv, a_log, dt_bias):
    B, T, _ = P.shape
    tm = DN_BLOCK
    nblk = T // tm
    W = DN_HEADS * DN_DK
    mh, eg, eb, tmat, ones = _dn_consts()
    alog_row = jnp.zeros((1, LANES), F32).at[0, :2 * DN_HEADS].set(a_log.reshape(-1))
    dtb_row = jnp.zeros((1, LANES), F32).at[0, :2 * DN_HEADS].set(dt_bias.reshape(-1))
    r8 = tm // 8

    def main(c0):
        return pl.BlockSpec((None, tm, W), lambda b, i: (b, i, c0 // W))

    def left(c0):
        return pl.BlockSpec((None, 8, W), lambda b, i: (b, jnp.maximum(i * r8 - 1, 0), c0 // W))

    def right(c0):
        return pl.BlockSpec((None, 8, W), lambda b, i: (b, jnp.minimum((i + 1) * r8, nblk * r8 - 1), c0 // W))

    def full(a):
        nd = a.ndim
        return pl.BlockSpec(a.shape, lambda b, i: (0,) * nd)

    hm = pl.BlockSpec((None, DN_HEADS, tm, DN_DK), lambda b, i: (b, 0, i, 0))

    def hm2(width):
        return pl.BlockSpec((2, None, DN_HEADS, tm, width), lambda b, i: (0, b, 0, i, 0))

    def sh_hm2(width):
        return jax.ShapeDtypeStruct((2, B, DN_HEADS, T, width), BF16)

    sh_hm = jax.ShapeDtypeStruct((B, DN_HEADS, T, DN_DK), BF16)
    return pl.pallas_call(
        functools.partial(_dn_prep_kernel, tm=tm),
        out_shape=(sh_hm, sh_hm, sh_hm2(DN_DK), sh_hm2(2 * DN_DK), sh_hm2(DN_DK), sh_hm2(tm), sh_hm2(tm),
                   jax.ShapeDtypeStruct((2, B, nblk, DN_HEADS, tm // DN_CHUNK, DN_DK), F32)),
        grid=(B, nblk),
        in_specs=[main(COL_BQ), main(COL_BK), main(COL_BV),
                  left(COL_BQ), left(COL_BK), left(COL_BV),
                  right(COL_BQ), right(COL_BK), right(COL_BV),
                  pl.BlockSpec((None, tm, LANES), lambda b, i: (b, i, 0)),
                  full(w_conv), full(alog_row), full(dtb_row), full(mh), full(eg), full(eb), full(tmat),
                  full(ones)],
        out_specs=(hm, hm, hm2(DN_DK), hm2(2 * DN_DK), hm2(DN_DK), hm2(tm), hm2(tm),
                   pl.BlockSpec((2, None, None, DN_HEADS, tm // DN_CHUNK, DN_DK),
                                lambda b, i: (0, b, i, 0, 0, 0))),
        compiler_params=_cparams(("arbitrary", "arbitrary")),
        name="dn_prep",
    )(P, P, P, P, P, P, P, P, P, AB, w_conv, alog_row, dtb_row, mh, eg, eb, tmat, ones)


def _dn_chunk_kernel(q_ref, k_ref, kb_ref, vbk_ref, qdec_ref, kdecbd_ref, decbd_ref,
                     qeff_ref, o0_ref, mk_ref, n_ref):
    n, C = DN_BLOCK, DN_CHUNK
    nc = n // C
    r = lax.broadcasted_iota(jnp.int32, (n, n), 0)
    c = lax.broadcasted_iota(jnp.int32, (n, n), 1)
    off_diag = r != c
    same_chunk = (r // C) == (c // C)
    rp = lax.broadcasted_iota(jnp.int32, (C, n), 0)
    cp = lax.broadcasted_iota(jnp.int32, (C, n), 1)
    eye_packed = (rp == cp % C).astype(F32)
    own_values = (lax.broadcasted_iota(jnp.int32, (n, 2 * nc * C), 0) // C
                  == lax.broadcasted_iota(jnp.int32, (n, 2 * nc * C), 1) // (2 * C))
    heads = range(DN_HEADS)

    def block_diag(xp):
        xb = xp.astype(BF16)
        return jnp.where(same_chunk, jnp.concatenate([xb] * nc, axis=0), jnp.zeros((), BF16))

    low, qk = [], []
    for h in heads:
        dec = decbd_ref[h].astype(F32)
        k = k_ref[h]
        a = jnp.where(off_diag, _dot_nt(kb_ref[h], k) * dec, 0.0)
        low.append(sum(a[i * C:(i + 1) * C] for i in range(nc)))
        qk.append((_dot_nt(q_ref[h], k) * dec).astype(BF16))
    p = [eye_packed - m for m in low]
    mpow = low
    for _ in range(5):
        mpow = [_dot(m.astype(BF16), block_diag(m)) for m in mpow]
        p = [a + _dot(a.astype(BF16), block_diag(m)) for a, m in zip(p, mpow)]
    uw = []
    for h in heads:
        vals = jnp.where(own_values, jnp.concatenate([vbk_ref[h]] * nc, axis=1), jnp.zeros((), BF16))
        packed = _dot(p[h].astype(BF16), vals)
        uw.append(jnp.concatenate([packed[:, i * 2 * C:(i + 1) * 2 * C] for i in range(nc)],
                                  axis=0).astype(BF16))
    qkuw = [_dot(qk[h], uw[h]) for h in heads]
    for h in heads:
        o0_ref[h] = qkuw[h][:, :DN_DK].astype(o0_ref.dtype)
        qeff_ref[h] = (qdec_ref[h].astype(F32) - qkuw[h][:, DN_DK:]).astype(qeff_ref.dtype)
        kt = _dot_tn(kdecbd_ref[h], uw[h])
        n_ref[h] = kt[:, :DN_DK].astype(n_ref.dtype)
        mk_ref[h] = kt[:, DN_DK:].astype(mk_ref.dtype)


def _dn_chunk(prep):
    q, k, kb, vbk, qdec, kdecbd, decbd, _ = prep
    B, H, T, dk = q.shape
    tm = DN_BLOCK
    hm = pl.BlockSpec((None, H, tm, dk), lambda d, b, i: (b, 0, i, 0))

    def hm2(width):
        return pl.BlockSpec((None, None, H, tm, width), lambda d, b, i: (d, b, 0, i, 0))

    def sh(dtype):
        return jax.ShapeDtypeStruct((2, B, H, T, dk), dtype)

    return pl.pallas_call(
        _dn_chunk_kernel,
        out_shape=(sh(BF16), sh(BF16), sh(BF16), sh(BF16)),
        grid=(2, B, T // tm),
        in_specs=[hm, hm, hm2(dk), hm2(2 * dk), hm2(dk), hm2(tm), hm2(tm)],
        out_specs=(hm2(dk), hm2(dk), hm2(dk), hm2(dk)),
        compiler_params=_cparams(("arbitrary", "arbitrary", "arbitrary")),
        name="dn_chunk",
    )(q, k, kb, vbk, qdec, kdecbd, decbd)


def _dn_scan_kernel(*refs, n_batch):
    ins = (refs[0:5], refs[5:10])
    o_refs = refs[10:12]
    s_scr = refs[12]
    t = pl.program_id(0)
    n_chunks = DN_BLOCK // DN_CHUNK
    C = DN_CHUNK

    @pl.when(t == 0)
    def _():
        s_scr[...] = jnp.zeros_like(s_scr)

    for step in range(n_chunks):
        for d in range(2):
            ci = step if d == 0 else n_chunks - 1 - step
            rows = slice(ci * C, (ci + 1) * C)
            qeff_ref, o0_ref, mk_ref, n_ref, egl_ref = ins[d]
            for b in range(n_batch):
                for h in range(DN_HEADS):
                    s = s_scr[d, b, h]
                    sb = s.astype(BF16)
                    o_refs[d][b, h, rows, :] = (_dot(qeff_ref[b, h, rows, :], sb)
                                                + o0_ref[b, h, rows, :]).astype(o_refs[d].dtype)
                    s_scr[d, b, h] = (s * egl_ref[b, h, ci:ci + 1, :] - _dot(mk_ref[b, h, rows, :], sb)
                                      + n_ref[b, h, rows, :])


def _dn_scan(maps, egl):
    qeff, o0, mk, nn = maps
    _, B, H, T, dk = qeff.shape
    tm = DN_BLOCK
    nblk = T // tm

    def blk(d, t):
        return jnp.where(t == 0, nblk - 1, nblk - 1 - t if d == 1 else t - 1)

    in_specs, args = [], []
    for d in range(2):
        for a in (qeff, o0, mk, nn):
            in_specs.append(pl.BlockSpec((None, B, H, tm, dk), lambda t, d=d: (d, 0, 0, blk(d, t), 0)))
            args.append(a)
        in_specs.append(pl.BlockSpec((None, B, None, H, tm // DN_CHUNK, dk),
                                     lambda t, d=d: (d, 0, blk(d, t), 0, 0, 0)))
        args.append(egl)
    out_specs = tuple(pl.BlockSpec((B, H, tm, dk), lambda t, d=d: (0, 0, blk(d, t), 0)) for d in range(2))
    return pl.pallas_call(
        functools.partial(_dn_scan_kernel, n_batch=B),
        out_shape=(jax.ShapeDtypeStruct((B, H, T, dk), BF16),) * 2,
        grid=(nblk,),
        in_specs=in_specs,
        out_specs=out_specs,
        scratch_shapes=[pltpu.VMEM((2, B, H, dk, dk), F32)],
        compiler_params=_cparams(("arbitrary",)),
        name="dn_scan",
    )(*args)


def _merge_kernel(a_ref, of_ref, ob_ref, z_ref, c_ref, ga_ref, gb_ref, gc_ref, h_ref, mod_ref, g2_ref, gdn_ref,
                  wb_ref, wo_ref, h_out_ref, v_out_ref, b_scr, *, tm, n_lat):
    i = pl.program_id(1)
    D = h_ref.shape[-1]
    z = z_ref[...].astype(F32)
    for hd in range(DN_HEADS):
        o = of_ref[hd].astype(F32) + ob_ref[hd].astype(F32)
        y = o * lax.rsqrt(jnp.mean(o * o, axis=-1, keepdims=True) + EPS) * gdn_ref[...]
        b_scr[:, hd * DN_DK:(hd + 1) * DN_DK] = y * _silu(z[:, hd * DN_DK:(hd + 1) * DN_DK])
    m = jax.nn.sigmoid(ga_ref[...].astype(F32)) * _dot(a_ref[...], wb_ref[0])
    m = m + jax.nn.sigmoid(gb_ref[...].astype(F32)) * _dot(b_scr[...].astype(BF16), wb_ref[1])
    m = m + jax.nn.sigmoid(gc_ref[...].astype(F32)) * _dot(c_ref[...], wb_ref[2])
    y = _dot(m.astype(BF16), wo_ref[...])
    gate = jnp.where(_is_ctx_row(i, tm, n_lat), mod_ref[0:1, 2 * D:3 * D], mod_ref[1:2, 2 * D:3 * D])
    h_new = h_ref[...] + gate * y
    h_out_ref[...] = h_new
    v_out_ref[...] = _norm_mod(h_new, g2_ref[...], mod_ref, i, tm, n_lat, 3, 4).astype(BF16)


def _merge(br_a, o_f, o_b, br_c, P, h, modtab, g_norm2, g_dn_out, w_branch, w_out, n_lat, tm):
    B, T, D = h.shape
    br = pl.BlockSpec((None, tm, BR_W), lambda b, i: (b, i, 0))
    hm = pl.BlockSpec((None, DN_HEADS, tm, DN_DK), lambda b, i: (b, 0, i, 0))

    def gate(j):
        return pl.BlockSpec((None, tm, D), lambda b, i: (b, i, COL_GATE // D + j))

    tok = pl.BlockSpec((None, tm, D), lambda b, i: (b, i, 0))
    return pl.pallas_call(
        functools.partial(_merge_kernel, tm=tm, n_lat=n_lat),
        out_shape=(jax.ShapeDtypeStruct((B, T, D), F32), jax.ShapeDtypeStruct((B, T, D), BF16)),
        grid=(B, T // tm),
        in_specs=[br, hm, hm,
                  pl.BlockSpec((None, tm, BR_W), lambda b, i: (b, i, COL_BZ // BR_W)),
                  br, gate(0), gate(1), gate(2), tok,
                  pl.BlockSpec((None, 2, 6 * D), lambda b, i: (b, 0, 0)),
                  pl.BlockSpec((1, D), lambda b, i: (0, 0)),
                  pl.BlockSpec((1, DN_DK), lambda b, i: (0, 0)),
                  pl.BlockSpec((3, BR_W, D), lambda b, i: (0, 0, 0)),
                  pl.BlockSpec((D, D), lambda b, i: (0, 0))],
        out_specs=(tok, tok),
        scratch_shapes=[pltpu.VMEM((tm, BR_W), F32)],
        compiler_params=_cparams(("arbitrary", "arbitrary")),
        name="merge",
    )(br_a, o_f, o_b, P, br_c, P, P, P, h, modtab, g_norm2.reshape(1, D), g_dn_out.reshape(1, DN_DK),
      w_branch, w_out)


def _router_kernel(v_ref, wr_ref, br_ref, perm_ref, c_ref):
    tn = v_ref.shape[0]
    G = N_GROUPS
    scores = jax.nn.sigmoid(_dot_nt(wr_ref[...], v_ref[...]))
    sel = scores + br_ref[...]
    slabs = [sel[j * G:(j + 1) * G] for j in range(GROUP_SIZE)]
    m1, m2 = slabs[0], jnp.full((G, tn), -jnp.inf, F32)
    for j in range(1, GROUP_SIZE):
        m2 = jnp.maximum(m2, jnp.minimum(m1, slabs[j]))
        m1 = jnp.maximum(m1, slabs[j])
    gs = m1 + m2
    gidx = lax.broadcasted_iota(jnp.int32, (G, tn), 0)
    grank = jnp.zeros((G, tn), jnp.int32)
    for g in range(G):
        rowv = gs[g:g + 1]
        beats = jnp.where(rowv > gs, 1, jnp.where(jnp.logical_and(rowv == gs, g < gidx), 1, 0))
        grank = grank + beats
    gmask = grank < TOPK_GROUPS
    masked = jnp.concatenate([jnp.where(gmask, s, -jnp.inf) for s in slabs], axis=0)
    pos = lax.broadcasted_iota(jnp.int32, (N_EXPERTS, tn), 0)
    orig = (pos % G) * GROUP_SIZE + pos // G
    w = jnp.zeros((N_EXPERTS, tn), F32)
    for _ in range(TOP_K):
        best = jnp.max(masked, axis=0, keepdims=True)
        first = jnp.min(jnp.where(masked == best, orig, N_EXPERTS), axis=0, keepdims=True)
        pick = orig == first
        w = jnp.where(pick, scores, w)
        masked = jnp.where(pick, -jnp.inf, masked)
    denom = jnp.sum(w, axis=0, keepdims=True) + 1e-20
    c = w / denom * ROUTED_SCALE
    c_t = sum(_dot_tn(part, perm_ref[...]) for part in _split3(c))
    for g in range(N_EXPERTS // MOE_EPS):
        c_ref[g] = c_t[:, g * MOE_EPS:(g + 1) * MOE_EPS]


def _router(v_flat, w_router, b_router):
    N, D = v_flat.shape
    tn = 1280 if N % 1280 == 0 else 256
    perm = np.array([(p % N_GROUPS) * GROUP_SIZE + p // N_GROUPS for p in range(N_EXPERTS)])
    wr_t = w_router.T[perm].astype(BF16)
    br = b_router[perm].reshape(N_EXPERTS, 1)
    unperm = jnp.asarray(np.eye(N_EXPERTS)[perm], BF16)
    ngrp = N_EXPERTS // MOE_EPS
    return pl.pallas_call(
        _router_kernel,
        out_shape=jax.ShapeDtypeStruct((ngrp, N, MOE_EPS), F32),
        grid=(N // tn,),
        in_specs=[pl.BlockSpec((tn, D), lambda i: (i, 0)),
                  pl.BlockSpec((N_EXPERTS, D), lambda i: (0, 0)),
                  pl.BlockSpec((N_EXPERTS, 1), lambda i: (0, 0)),
                  pl.BlockSpec((N_EXPERTS, N_EXPERTS), lambda i: (0, 0))],
        out_specs=pl.BlockSpec((ngrp, tn, MOE_EPS), lambda i: (0, i, 0)),
        compiler_params=_cparams(("arbitrary",)),
        name="router",
    )(v_flat, wr_t, br, unperm)


MOE_EPS = 8


def _moe_kernel(x_ref, c_ref, wgu_ref, wd_ref, sgu_ref, sd_ref, h_ref, mod_ref,
                o_ref, acc_ref, *, tm, n_lat):
    i = pl.program_id(1)
    e = pl.program_id(2)
    x = x_ref[...]
    D = x.shape[-1]
    F = MOE_FF

    @pl.when(e == 0)
    def _():
        gu = _dot(x, sgu_ref[...])
        hs = _silu(gu[:, :F]) * gu[:, F:]
        acc_ref[...] = _dot(hs.astype(BF16), sd_ref[...])

    c = c_ref[...]
    parts = []
    for j in range(MOE_EPS):
        gu = _dot(x, wgu_ref[j])
        hj = _silu(gu[:, :F]) * gu[:, F:] * c[:, j:j + 1]
        parts.append(hj.astype(BF16))
    hcat = jnp.concatenate(parts, axis=-1)
    acc_ref[...] += _dot(hcat, wd_ref[...])

    @pl.when(e == pl.num_programs(2) - 1)
    def _():
        gate = jnp.where(_is_ctx_row(i, tm, n_lat), mod_ref[0:1, 5 * D:6 * D], mod_ref[1:2, 5 * D:6 * D])
        o_ref[...] = h_ref[...] + gate * acc_ref[...]


def _moe_weights(w_gate, w_up, w_down):
    E, D, F = w_gate.shape
    gu = jnp.concatenate([w_gate, w_up], axis=-1).astype(BF16)
    return gu, w_down.astype(BF16).reshape(E * F, D)


def _moe(v, c_grp, h, modtab, wgu, wd, sgu, sd, n_lat, tm):
    B, T, D = v.shape
    F = MOE_FF
    ngrp = N_EXPERTS // MOE_EPS
    tok = pl.BlockSpec((None, tm, D), lambda b, i, e: (b, i, 0))
    return pl.pallas_call(
        functools.partial(_moe_kernel, tm=tm, n_lat=n_lat),
        out_shape=jax.ShapeDtypeStruct((B, T, D), F32),
        grid=(B, T // tm, ngrp),
        in_specs=[tok,
                  pl.BlockSpec((None, None, tm, MOE_EPS), lambda b, i, e: (e, b, i, 0)),
                  pl.BlockSpec((MOE_EPS, D, 2 * F), lambda b, i, e: (e, 0, 0)),
                  pl.BlockSpec((MOE_EPS * F, D), lambda b, i, e: (e, 0)),
                  pl.BlockSpec((D, 2 * F), lambda b, i, e: (0, 0)),
                  pl.BlockSpec((F, D), lambda b, i, e: (0, 0)),
                  tok,
                  pl.BlockSpec((None, 2, 6 * D), lambda b, i, e: (b, 0, 0))],
        out_specs=tok,
        scratch_shapes=[pltpu.VMEM((tm, D), F32)],
        compiler_params=_cparams(("arbitrary", "arbitrary", "arbitrary")),
        name="moe",
    )(v, c_grp, wgu, wd, sgu, sd, h, modtab)


def _final_kernel(h_ref, g_ref, o_ref):
    x = h_ref[...]
    o_ref[...] = x * lax.rsqrt(jnp.mean(x * x, axis=-1, keepdims=True) + EPS) * g_ref[...]


def _final_norm(h, g_final, S):
    B, T, D = h.shape
    tm = 1024
    return pl.pallas_call(
        _final_kernel,
        out_shape=jax.ShapeDtypeStruct((B, S, D), F32),
        grid=(B, S // tm),
        in_specs=[pl.BlockSpec((None, tm, D), lambda b, i: (b, i, 0)),
                  pl.BlockSpec((1, D), lambda b, i: (0, 0))],
        out_specs=pl.BlockSpec((None, tm, D), lambda b, i: (b, i, 0)),
        compiler_params=_cparams(("arbitrary", "arbitrary")),
        name="final_norm",
    )(h, g_final.reshape(1, D))


def _rope_tables(S, n_ctx):
    rows = S // GRID_W
    row = jnp.repeat(jnp.arange(rows, dtype=F32), GRID_W)
    col = (jnp.arange(S) % GRID_W).astype(F32)
    axis_dim = HEAD_DIM // 2
    inv = 1.0 / (ROPE_THETA ** (jnp.arange(0, axis_dim, 2, dtype=F32) / axis_dim))
    ang = jnp.concatenate([row[:, None] * inv, col[:, None] * inv], axis=-1)
    ang = jnp.concatenate([ang, ang], axis=-1)
    cos = jnp.concatenate([jnp.cos(ang), jnp.ones((n_ctx, HEAD_DIM), F32)], axis=0)
    sin = jnp.concatenate([jnp.sin(ang), jnp.zeros((n_ctx, HEAD_DIM), F32)], axis=0)
    first = (jnp.arange(HEAD_DIM) < HEAD_DIM // 2)[None, :]
    sin_a = jnp.where(first, -sin, 0.0)
    sin_b = jnp.where(first, 0.0, sin)
    return tuple(jnp.tile(t, (1, 2)) for t in (cos, sin_a, sin_b))


def _permute_w_in(w):
    splits = np.cumsum([512, 128, 128, 1536, 512, 16, 16, 512, 512, 512, 3072])[:-1].tolist()
    aq, ak, av, bqkv, bz, ba, bb, cq, ck, cv, gate = jnp.split(w, splits, axis=-1)
    pad = jnp.zeros((w.shape[0], IN_W_PAD - COL_AB - 32), w.dtype)
    out = jnp.concatenate([aq, cq, ck, cv, bz, bqkv, gate, ak, av, ba, bb, pad], axis=-1)
    return out.astype(BF16)


def kernel(x, c, ctx, c_ctx, w_mod, b_mod, g_norm1, g_norm2, w_in, g_qnorm, g_knorm, w_conv, a_log, dt_bias, g_dn_out, lam_qk, g_subln, w_branch, w_out, w_router, b_router, w_e_gate, w_e_up, w_e_down, w_s_gate, w_s_up, w_s_down, g_final):
    B, S, D = x.shape
    n_ctx = ctx.shape[1]
    T = n_ctx + S
    L = w_mod.shape[0]
    N = B * T
    assert n_ctx == DN_BLOCK and S % 1024 == 0 and D == D_MODEL
    tm_tok = 1280 if T % 1280 == 0 else 256
    tm_merge = 640 if T % 640 == 0 else 256

    cond = jnp.zeros((8, D), F32).at[0].set(c_ctx).at[1:1 + B].set(c)
    mod = _mod_vectors(cond, w_mod, b_mod)
    tabs = _rope_tables(S, n_ctx)
    h = jnp.concatenate([x, ctx], axis=1)
    tk = 1280 if T % 1280 == 0 else 256

    for l in range(L):
        lambda_init = 0.8 - 0.6 * math.exp(-0.3 * l)
        modtab = jnp.stack([jnp.broadcast_to(mod[l, 0], (B, 6 * D)), mod[l, 1:1 + B]], axis=1)
        P, AB = _in_projection(h, modtab, g_norm1[l], _permute_w_in(w_in[l]), S, tm_tok)

        qa, ka, vta, qd, kd, vtd = _attn_prep(P, tabs, g_qnorm[l], g_knorm[l], tk)
        dx = (lam_qk[l], g_subln[l].reshape(LANES, 1))
        br_a = jnp.concatenate([_flash("gqa", qa, ka, vta, n_ctx, False),
                                _flash("gqa", qa, ka, vta, n_ctx, True)], axis=1)
        br_c = jnp.concatenate([_flash("diff", qd, kd, vtd, n_ctx, False, dx, lambda_init),
                                _flash("diff", qd, kd, vtd, n_ctx, True, dx, lambda_init)], axis=1)

        prep = _dn_prep(P, AB, w_conv[l], a_log[l], dt_bias[l])
        o_f, o_b = _dn_scan(_dn_chunk(prep), prep[-1])

        h, v = _merge(br_a, o_f, o_b, br_c, P, h, modtab, g_norm2[l], g_dn_out[l], w_branch[l].astype(BF16),
                      w_out[l].astype(BF16), S, tm_merge)

        c_grp = _router(v.reshape(N, D), w_router[l], b_router[l]).reshape(-1, B, T, MOE_EPS)
        wgu, wd = _moe_weights(w_e_gate[l], w_e_up[l], w_e_down[l])
        sgu = jnp.concatenate([w_s_gate[l], w_s_up[l]], axis=-1).astype(BF16)
        h = _moe(v, c_grp, h, modtab, wgu, wd, sgu, w_s_down[l].astype(BF16), S, tm_merge)

    return _final_norm(h, g_final, S)
```

```python
import functools
import math

import numpy as np
import jax
import jax.numpy as jnp
from jax import lax
from jax.experimental import pallas as pl
from jax.experimental.pallas import tpu as pltpu

F32 = jnp.float32
BF16 = jnp.bfloat16

D_MODEL = 1024
GRID_W = 64
EPS = 1e-6
ROPE_THETA = 10000.0
HEAD_DIM = 64
LANES = 128
A_HEADS = 8
A_KV_HEADS = 2
A_GROUP = A_HEADS // A_KV_HEADS
DN_HEADS = 8
DN_DK = 64
DN_CHUNK = 64
DN_BLOCK = 256
DF_HEADS = 4
N_EXPERTS = 64
TOP_K = 6
N_GROUPS = 8
TOPK_GROUPS = 4
GROUP_SIZE = N_EXPERTS // N_GROUPS
MOE_FF = 256
ROUTED_SCALE = 2.5
BR_W = 512

COL_AQ, COL_CQ, COL_CK, COL_CV, COL_BZ = 0, 512, 1024, 1536, 2048
COL_BQ, COL_BK, COL_BV = 2560, 3072, 3584
COL_GATE = 4096
COL_AK, COL_AV, COL_AB = 7168, 7296, 7424
IN_W_PAD = 7680
IN_TN = 1280

VMEM_LIMIT = 56 * 1024 * 1024


def _cparams(sem):
    return pltpu.CompilerParams(dimension_semantics=sem, vmem_limit_bytes=VMEM_LIMIT)


def _dot(a, b):
    return jnp.dot(a, b, preferred_element_type=F32)


def _dot_nt(a, b):
    return lax.dot_general(a, b, (((1,), (1,)), ((), ())), preferred_element_type=F32)


def _dot_tn(a, b):
    return lax.dot_general(a, b, (((0,), (0,)), ((), ())), preferred_element_type=F32)


def _split3(x):
    hi = x.astype(BF16)
    r = x - hi.astype(F32)
    mid = r.astype(BF16)
    lo = (r - mid.astype(F32)).astype(BF16)
    return hi, mid, lo


def _dot3_l(x, m):
    hi, mid, lo = _split3(x)
    return _dot(hi, m) + _dot(mid, m) + _dot(lo, m)


def _dot3_r(m, x):
    hi, mid, lo = _split3(x)
    return _dot(m, hi) + _dot(m, mid) + _dot(m, lo)


def _silu(x):
    return x * _sigmoid(x)


def _sigmoid(x):
    return 0.5 * jnp.tanh(0.5 * x) + 0.5


def _mod_kernel(cond_ref, w_ref, b_ref, o_ref):
    a = _silu(cond_ref[...]).astype(BF16)
    o_ref[...] = _dot(a, w_ref[...].astype(BF16)) + b_ref[...]


def _mod_vectors(cond, w_mod, b_mod):
    L, D, W = w_mod.shape
    tn = 1024
    return pl.pallas_call(
        _mod_kernel,
        out_shape=jax.ShapeDtypeStruct((L, 8, W), F32),
        grid=(L, W // tn),
        in_specs=[pl.BlockSpec((8, D), lambda l, j: (0, 0)),
                  pl.BlockSpec((None, D, tn), lambda l, j: (l, 0, j)),
                  pl.BlockSpec((None, 1, tn), lambda l, j: (l, 0, j))],
        out_specs=pl.BlockSpec((None, 8, tn), lambda l, j: (l, 0, j)),
        compiler_params=_cparams(("arbitrary", "arbitrary")),
        name="mod_vectors",
    )(cond, w_mod, b_mod.reshape(L, 1, W))


def _is_ctx_row(blk, tm, n_lat):
    return blk * tm + lax.broadcasted_iota(jnp.int32, (tm, 1), 0) >= n_lat


def _norm_mod(x, g, mod_ref, blk, tm, n_lat, sh, sc):
    D = x.shape[-1]
    y = x * lax.rsqrt(jnp.mean(x * x, axis=-1, keepdims=True) + EPS) * g
    is_ctx = _is_ctx_row(blk, tm, n_lat)
    scale = jnp.where(is_ctx, mod_ref[0:1, sc * D:(sc + 1) * D], mod_ref[1:2, sc * D:(sc + 1) * D])
    shift = jnp.where(is_ctx, mod_ref[0:1, sh * D:(sh + 1) * D], mod_ref[1:2, sh * D:(sh + 1) * D])
    return y * (1.0 + scale) + shift


def _inproj_kernel(h_ref, mod_ref, g_ref, w_ref, p_ref, ab_ref, u_scr, *, tm, n_lat, ab_off):
    i = pl.program_id(1)
    j = pl.program_id(2)

    @pl.when(j == 0)
    def _():
        u_scr[...] = _norm_mod(h_ref[...], g_ref[...], mod_ref, i, tm, n_lat, 0, 1).astype(BF16)

    r = _dot(u_scr[...], w_ref[...])
    p_ref[...] = r.astype(BF16)

    @pl.when(j == pl.num_programs(2) - 1)
    def _():
        ab_ref[...] = r[:, ab_off:ab_off + LANES]


def _in_projection(h, modtab, g_norm, w_perm, n_lat, tm):
    B, T, D = h.shape
    ncol = IN_W_PAD // IN_TN
    kern = functools.partial(_inproj_kernel, tm=tm, n_lat=n_lat, ab_off=COL_AB - (ncol - 1) * IN_TN)
    return pl.pallas_call(
        kern,
        out_shape=(jax.ShapeDtypeStruct((B, T, IN_W_PAD), BF16),
                   jax.ShapeDtypeStruct((B, T, LANES), F32)),
        grid=(B, T // tm, ncol),
        in_specs=[pl.BlockSpec((None, tm, D), lambda b, i, j: (b, i, 0)),
                  pl.BlockSpec((None, 2, 6 * D), lambda b, i, j: (b, 0, 0)),
                  pl.BlockSpec((1, D), lambda b, i, j: (0, 0)),
                  pl.BlockSpec((D, IN_TN), lambda b, i, j: (0, j))],
        out_specs=(pl.BlockSpec((None, tm, IN_TN), lambda b, i, j: (b, i, j)),
                   pl.BlockSpec((None, tm, LANES), lambda b, i, j: (b, i, 0))),
        scratch_shapes=[pltpu.VMEM((tm, D), BF16)],
        compiler_params=_cparams(("arbitrary", "arbitrary", "arbitrary")),
        name="in_projection",
    )(h, modtab, g_norm.reshape(1, D), w_perm)


def _rope(x, cos, sin_a, sin_b):
    return x * cos + pltpu.roll(x, LANES - 32, 1) * sin_a + pltpu.roll(x, 32, 1) * sin_b


def _attn_prep_kernel(aq_ref, ak_ref, av_ref, cq_ref, ck_ref, cv_ref, cos_ref, sa_ref, sb_ref, gq_ref, gk_ref,
                      m_ref, qa_ref, ka_ref, vta_ref, qd_ref, kd_ref, vtd_ref):
    cos, sa, sb = cos_ref[...], sa_ref[...], sb_ref[...]
    vta_ref[...] = av_ref[...].astype(F32).T.astype(BF16)
    for h in range(DF_HEADS):
        vtd_ref[h] = cv_ref[:, h * LANES:(h + 1) * LANES].astype(F32).T.astype(BF16)
    m64 = m_ref[...]
    half = lax.broadcasted_iota(jnp.int32, cos.shape, 1) // HEAD_DIM
    scale = HEAD_DIM ** -0.5 * math.log2(math.e)

    def head_norm(x, g):
        ms = _dot3_l(x * x, m64) * (1.0 / HEAD_DIM)
        return x * lax.rsqrt(ms + EPS) * g

    for s in range(A_HEADS // 2):
        x = aq_ref[:, s * LANES:(s + 1) * LANES].astype(F32)
        y = _rope(head_norm(x, gq_ref[...]), cos, sa, sb) * scale
        y_sw = pltpu.roll(y, HEAD_DIM, 1)
        kvh = (2 * s) // A_GROUP
        for hh in range(2):
            src = y if hh == kvh else y_sw
            qa_ref[2 * s + hh] = jnp.where(half == kvh, src, 0.0).astype(BF16)
    xk = ak_ref[...].astype(F32)
    ka_ref[...] = _rope(head_norm(xk, gk_ref[...]), cos, sa, sb).astype(BF16)
    for h in range(DF_HEADS):
        x = cq_ref[:, h * LANES:(h + 1) * LANES].astype(F32)
        y = _rope(x, cos, sa, sb) * scale
        qd_ref[2 * h] = jnp.where(half == 0, y, 0.0).astype(BF16)
        qd_ref[2 * h + 1] = jnp.where(half == 1, y, 0.0).astype(BF16)
        xk = ck_ref[:, h * LANES:(h + 1) * LANES].astype(F32)
        kd_ref[:, h * LANES:(h + 1) * LANES] = _rope(xk, cos, sa, sb).astype(BF16)


def _attn_prep(P, tabs, g_q, g_k, tk):
    B, T, _ = P.shape
    tm = 256
    per = tk // tm
    cos2, sin_a, sin_b = tabs
    m64 = jnp.asarray(np.kron(np.eye(2), np.ones((HEAD_DIM, HEAD_DIM))), BF16)
    gq2 = jnp.tile(g_q, 2).reshape(1, LANES)
    gk2 = jnp.tile(g_k, 2).reshape(1, LANES)
    tab_spec = pl.BlockSpec((tm, LANES), lambda b, i: (i, 0))
    vec_spec = pl.BlockSpec((1, LANES), lambda b, i: (0, 0))
    return pl.pallas_call(
        _attn_prep_kernel,
        out_shape=(jax.ShapeDtypeStruct((B, A_HEADS, T, LANES), BF16),
                   jax.ShapeDtypeStruct((B, T, LANES), BF16),
                   jax.ShapeDtypeStruct((B, T // tk, LANES, tk), BF16),
                   jax.ShapeDtypeStruct((B, 2 * DF_HEADS, T, LANES), BF16),
                   jax.ShapeDtypeStruct((B, T, 512), BF16),
                   jax.ShapeDtypeStruct((B, DF_HEADS, T // tk, LANES, tk), BF16)),
        grid=(B, T // tm),
        in_specs=[pl.BlockSpec((None, tm, 512), lambda b, i: (b, i, COL_AQ // 512)),
                  pl.BlockSpec((None, tm, LANES), lambda b, i: (b, i, COL_AK // LANES)),
                  pl.BlockSpec((None, tm, LANES), lambda b, i: (b, i, COL_AV // LANES)),
                  pl.BlockSpec((None, tm, 512), lambda b, i: (b, i, COL_CQ // 512)),
                  pl.BlockSpec((None, tm, 512), lambda b, i: (b, i, COL_CK // 512)),
                  pl.BlockSpec((None, tm, 512), lambda b, i: (b, i, COL_CV // 512)),
                  tab_spec, tab_spec, tab_spec, vec_spec, vec_spec,
                  pl.BlockSpec((LANES, LANES), lambda b, i: (0, 0))],
        out_specs=(pl.BlockSpec((None, A_HEADS, tm, LANES), lambda b, i: (b, 0, i, 0)),
                   pl.BlockSpec((None, tm, LANES), lambda b, i: (b, i, 0)),
                   pl.BlockSpec((None, None, LANES, tm), lambda b, i: (b, i // per, 0, i % per)),
                   pl.BlockSpec((None, 2 * DF_HEADS, tm, LANES), lambda b, i: (b, 0, i, 0)),
                   pl.BlockSpec((None, tm, 512), lambda b, i: (b, i, 0)),
                   pl.BlockSpec((None, DF_HEADS, None, LANES, tm), lambda b, i: (b, 0, i // per, 0, i % per))),
        compiler_params=_cparams(("arbitrary", "arbitrary")),
        name="attn_prep",
    )(P, P, P, P, P, P, cos2, sin_a, sin_b, gq2, gk2, m64)


FLASH_TQ = 512
FLASH_PIECE = 256
FLASH_MAX_OVERSHOOT = 60.0
FLASH_GROUP = 2


def _flash_kernel(*refs, mode, cols, tk, n_ctx, n_chunks, ctx_only, lambda_init):
    refs = list(refs)
    q_ref, k_ref, vt_ref = refs[0:3]
    pos = 3
    if mode == "diff":
        lam_ref, g_ref = refs[pos], refs[pos + 1]
        pos += 2
    o_ref, m_scr, l_scr, over_scr, acc_scr = refs[pos:pos + 5]
    q = q_ref[...].reshape(cols, LANES)

    def reset():
        m_scr[...] = jnp.full((1, cols), -jnp.inf, F32)
        l_scr[...] = jnp.zeros((1, cols), F32)
        acc_scr[...] = jnp.zeros((LANES, cols), F32)

    def exact(k, vt):
        s = _dot_nt(k, q)
        m_old = m_scr[...]
        m_new = jnp.maximum(m_old, jnp.max(s, axis=0, keepdims=True))
        alpha = jnp.exp2(m_old - m_new)
        p = jnp.exp2(s - m_new)
        l_scr[...] = alpha * l_scr[...] + jnp.sum(p, axis=0, keepdims=True)
        acc_scr[...] = alpha * acc_scr[...] + _dot(vt, p.astype(BF16))
        m_scr[...] = m_new

    def lagged(k, vt):
        ref = m_scr[...]
        s = _dot_nt(k, q)
        p = jnp.exp2(s - ref)
        m_cur = jnp.max(s, axis=0, keepdims=True)
        m_new = jnp.maximum(ref, m_cur)
        alpha = jnp.exp2(ref - m_new)
        l_scr[...] = (l_scr[...] + jnp.sum(p, axis=0, keepdims=True)) * alpha
        acc_scr[...] = (acc_scr[...] + _dot(vt, p.astype(BF16))) * alpha
        over_scr[...] = jnp.maximum(over_scr[...], m_cur - ref)
        m_scr[...] = m_new

    def k_chunk(c):
        return k_ref[pl.ds(pl.multiple_of(c * tk, tk), tk), :]

    reset()
    if ctx_only:
        exact(k_ref[...], vt_ref[:, tk - n_ctx:])
    else:
        over_scr[...] = jnp.zeros((1, cols), F32)
        exact(k_ref[0:FLASH_PIECE, :], vt_ref[0, :, 0:FLASH_PIECE])
        if tk > FLASH_PIECE:
            lagged(k_ref[FLASH_PIECE:tk, :], vt_ref[0, :, FLASH_PIECE:tk])

        def lagged_group(c0):
            ref = m_scr[...]
            l, acc, over = l_scr[...], acc_scr[...], over_scr[...]
            pending = None
            for u in range(FLASH_GROUP + 1):
                if u < FLASH_GROUP:
                    s = _dot_nt(k_chunk(c0 + u), q)
                    m_cur = jnp.max(s, axis=0, keepdims=True)
                    over = jnp.maximum(over, m_cur - ref)
                    ref_next = jnp.maximum(ref, m_cur)
                if pending is not None:
                    s_p, ref_p, ref_after, c_p = pending
                    p = jnp.exp2(s_p - ref_p)
                    alpha = jnp.exp2(ref_p - ref_after)
                    l = (l + jnp.sum(p, axis=0, keepdims=True)) * alpha
                    acc = (acc + _dot(vt_ref[c_p], p.astype(BF16))) * alpha
                if u < FLASH_GROUP:
                    pending = (s, ref, ref_next, c0 + u)
                    ref = ref_next
            l_scr[...] = l
            acc_scr[...] = acc
            over_scr[...] = over
            m_scr[...] = ref

        def fast_body(it, carry):
            lagged_group(1 + FLASH_GROUP * it)
            return carry
        lax.fori_loop(0, (n_chunks - 1) // FLASH_GROUP, fast_body, 0)

        @pl.when(jnp.max(over_scr[...]) > FLASH_MAX_OVERSHOOT)
        def _():
            reset()

            def exact_body(c, carry):
                exact(k_chunk(c), vt_ref[c])
                return carry
            lax.fori_loop(0, n_chunks, exact_body, 0)

    if mode == "gqa":
        kvh = pl.program_id(1)
        o = acc_scr[pl.ds(pl.multiple_of(kvh * HEAD_DIM, HEAD_DIM), HEAD_DIM), :] / l_scr[...]
        tq = cols // A_GROUP
        pieces = [o[:, g * tq:(g + 1) * tq] for g in range(A_GROUP)]
        o_ref[...] = jnp.concatenate(pieces, axis=0).T.astype(o_ref.dtype)
    else:
        lq = lam_ref[...]
        lam = (jnp.exp(jnp.sum(lq[0:1] * lq[1:2], axis=-1, keepdims=True))
               - jnp.exp(jnp.sum(lq[2:3] * lq[3:4], axis=-1, keepdims=True)) + lambda_init)
        o = acc_scr[...] / l_scr[...]
        d = o[:, :cols // 2] - lam * o[:, cols // 2:]
        y = d * lax.rsqrt(jnp.mean(d * d, axis=0, keepdims=True) + EPS) * g_ref[...]
        o_ref[...] = (y * (1.0 - lambda_init)).T.astype(o_ref.dtype)


def _flash(mode, q, k, vt, n_ctx, ctx_only, extra=(), lambda_init=0.0):
    B, NQ, T, _ = q.shape
    S = T - n_ctx
    tk = vt.shape[-1]
    n_chunks = vt.shape[-3]
    assert (n_chunks - 1) % FLASH_GROUP == 0
    n_maps = A_GROUP if mode == "gqa" else 2
    n_heads = NQ // n_maps
    if ctx_only:
        tq, n_q, q0, rows = n_ctx, 1, S // n_ctx, n_ctx
        k_rows, k0 = n_ctx, S // n_ctx
        vt_blk, vt0 = None, n_chunks - 1
    else:
        tq = FLASH_TQ if mode == "gqa" else 2 * FLASH_TQ
        n_q, q0, rows = S // tq, 0, S
        k_rows, k0 = T, 0
        vt_blk, vt0 = n_chunks, 0
    cols = n_maps * tq
    kern = functools.partial(_flash_kernel, mode=mode, cols=cols, tk=tk, n_ctx=n_ctx,
                             n_chunks=n_chunks, ctx_only=ctx_only, lambda_init=lambda_init)
    in_specs = [pl.BlockSpec((None, n_maps, tq, LANES), lambda b, h, i: (b, h, i + q0, 0))]
    if mode == "gqa":
        in_specs += [pl.BlockSpec((None, k_rows, LANES), lambda b, h, i: (b, k0, 0)),
                     pl.BlockSpec((None, vt_blk, LANES, tk), lambda b, h, i: (b, vt0, 0, 0))]
        out_spec = pl.BlockSpec((None, tq, A_GROUP * HEAD_DIM), lambda b, h, i: (b, i, h))
    else:
        in_specs += [pl.BlockSpec((None, k_rows, LANES), lambda b, h, i: (b, k0, h)),
                     pl.BlockSpec((None, None, vt_blk, LANES, tk), lambda b, h, i: (b, h, vt0, 0, 0)),
                     pl.BlockSpec((4, HEAD_DIM), lambda b, h, i: (0, 0)),
                     pl.BlockSpec((LANES, 1), lambda b, h, i: (0, 0))]
        out_spec = pl.BlockSpec((None, tq, LANES), lambda b, h, i: (b, i, h))
    return pl.pallas_call(
        kern,
        out_shape=jax.ShapeDtypeStruct((B, rows, BR_W), BF16),
        grid=(B, n_heads, n_q),
        in_specs=in_specs,
        out_specs=out_spec,
        scratch_shapes=[pltpu.VMEM((1, cols), F32)] * 3 + [pltpu.VMEM((LANES, cols), F32)],
        compiler_params=_cparams(("arbitrary", "arbitrary", "arbitrary")),
        name="flash_" + mode + ("_ctx" if ctx_only else ""),
    )(q, k, vt, *extra)


def _dn_prep_kernel(q_ref, k_ref, v_ref, ql_ref, kl_ref, vl_ref, qr_ref, kr_ref, vr_ref, ab_ref,
                    wc_ref, alog_ref, dtb_ref, mh_ref, eg_ref, eb_ref, tm_ref, ones_ref,
                    qo_ref, ko_ref, kb_ref, vbk_ref, qdec_ref, kdecbd_ref, decbd_ref, egl_ref,
                    *, tm):
    i = pl.program_id(1)
    nblk = pl.num_programs(1)
    W = DN_HEADS * DN_DK
    row = lax.broadcasted_iota(jnp.int32, (tm, W), 0)
    col = lax.broadcasted_iota(jnp.int32, (tm, W), 1)
    left_ok = jnp.logical_and(i >= 1, i < nblk - 1).astype(F32)
    right_ok = (i < nblk - 2).astype(F32)

    def conv_silu(x_ref, xl_ref, xr_ref, c0):
        x = x_ref[...].astype(F32)
        prev_row = xl_ref[7:8, :].astype(F32) * left_ok
        next_row = xr_ref[0:1, :].astype(F32) * right_ok
        x_prev = jnp.where(row == 0, prev_row, pltpu.roll(x, 1, 0))
        x_next = jnp.where(row == tm - 1, next_row, pltpu.roll(x, tm - 1, 0))
        y = (x_prev * wc_ref[0:1, c0:c0 + W] + x * wc_ref[1:2, c0:c0 + W]
             + x_next * wc_ref[2:3, c0:c0 + W])
        return _silu(y)

    def l2norm(x):
        ss = _dot3_l(x * x, mh_ref[...])
        return x * lax.rsqrt(ss + EPS)

    q = l2norm(conv_silu(q_ref, ql_ref, qr_ref, 0)) * (DN_DK ** -0.5)
    k = l2norm(conv_silu(k_ref, kl_ref, kr_ref, W))
    v = conv_silu(v_ref, vl_ref, vr_ref, 2 * W)

    ab = ab_ref[...]
    z = ab + dtb_ref[...]
    softplus = jnp.maximum(z, 0.0) + jnp.log(1.0 + jnp.exp(-jnp.abs(z)))
    g_all = -jnp.exp(alog_ref[...]) * softplus
    beta_all = jax.nn.sigmoid(ab)

    t_in = row & (DN_CHUNK - 1)
    s_in = col & (DN_CHUNK - 1)

    def put(ref, d, val):
        for h in range(DN_HEADS):
            piece = val[:, h * DN_DK:(h + 1) * DN_DK]
            if d is None:
                ref[h] = piece.astype(ref.dtype)
            else:
                ref[d, h] = piece.astype(ref.dtype)

    lo = lax.broadcasted_iota(jnp.int32, (tm, LANES), 1) < DN_DK
    n_chunks = tm // DN_CHUNK
    bd_shape = (tm, n_chunks * DN_CHUNK)
    same_chunk = (lax.broadcasted_iota(jnp.int32, bd_shape, 0) // DN_CHUNK
                  == lax.broadcasted_iota(jnp.int32, bd_shape, 1) // DN_CHUNK)

    def pair(a, b, h):
        sa = a[:, (h // 2) * LANES:(h // 2 + 1) * LANES]
        sb = b[:, (h // 2) * LANES:(h // 2 + 1) * LANES]
        if h % 2 == 0:
            return jnp.where(lo, sa, pltpu.roll(sb, DN_DK, 1))
        return jnp.where(lo, pltpu.roll(sa, DN_DK, 1), sb)

    def put_block_diag(ref, d, val):
        for h in range(DN_HEADS):
            x = pair(val, val, h)
            ref[d, h] = jnp.where(same_chunk, jnp.concatenate([x] * (n_chunks // 2), axis=1), 0.0).astype(ref.dtype)

    put(qo_ref, None, q)
    put(ko_ref, None, k)
    for d in range(2):
        g_e = _dot3_l(g_all, eg_ref[d])
        beta_e = _dot3_l(beta_all, eb_ref[d])
        gc = _dot3_r(tm_ref[d], g_e)
        g_tot = _dot3_r(ones_ref[...], g_e)
        strict = (t_in > s_in) if d == 0 else (t_in < s_in)
        causal = (t_in >= s_in) if d == 0 else (t_in <= s_in)
        diff = _dot3_r(tm_ref[d], jnp.where(strict, g_e, 0.0))
        decay = jnp.where(causal, jnp.exp(diff), 0.0)
        e_g = jnp.exp(gc)
        kb = k * beta_e
        put(kb_ref, d, kb)
        vb, kbg = v * beta_e, kb * e_g
        for h in range(DN_HEADS):
            vbk_ref[d, h] = pair(vb, kbg, h).astype(vbk_ref.dtype)
        put(qdec_ref, d, q * e_g)
        put_block_diag(kdecbd_ref, d, k * jnp.exp(g_tot - gc))
        put_block_diag(decbd_ref, d, decay)
        e_tot = jnp.exp(g_tot)
        for h in range(DN_HEADS):
            for c in range(tm // DN_CHUNK):
                egl_ref[d, h, c:c + 1, :] = e_tot[c * DN_CHUNK:c * DN_CHUNK + 1, h * DN_DK:(h + 1) * DN_DK]


def _dn_consts():
    n = DN_BLOCK
    t = np.arange(n)
    same = (t[:, None] // DN_CHUNK) == (t[None, :] // DN_CHUNK)
    pre = same & (t[None, :] <= t[:, None])
    suf = same & (t[None, :] >= t[:, None])
    tmat = np.stack([pre, suf]).astype(np.float32)
    ones = same.astype(np.float32)
    W = DN_HEADS * DN_DK
    eg = np.zeros((2, LANES, W), np.float32)
    eb = np.zeros((2, LANES, W), np.float32)
    for d in range(2):
        for h in range(DN_HEADS):
            eg[d, d * DN_HEADS + h, h * DN_DK:(h + 1) * DN_DK] = 1.0
            eb[d, 2 * DN_HEADS + d * DN_HEADS + h, h * DN_DK:(h + 1) * DN_DK] = 1.0
    mh = np.kron(np.eye(DN_HEADS), np.ones((DN_DK, DN_DK))).astype(np.float32)
    return tuple(jnp.asarray(a, BF16) for a in (mh, eg, eb, tmat, ones))


def _dn_prep(P, AB, w_conv, a_log, dt_bias):
    B, T, _ = P.shape
    tm = DN_BLOCK
    nblk = T // tm
    W = DN_HEADS * DN_DK
    mh, eg, eb, tmat, ones = _dn_consts()
    alog_row = jnp.zeros((1, LANES), F32).at[0, :2 * DN_HEADS].set(a_log.reshape(-1))
    dtb_row = jnp.zeros((1, LANES), F32).at[0, :2 * DN_HEADS].set(dt_bias.reshape(-1))
    r8 = tm // 8

    def main(c0):
        return pl.BlockSpec((None, tm, W), lambda b, i: (b, i, c0 // W))

    def left(c0):
        return pl.BlockSpec((None, 8, W), lambda b, i: (b, jnp.maximum(i * r8 - 1, 0), c0 // W))

    def right(c0):
        return pl.BlockSpec((None, 8, W), lambda b, i: (b, jnp.minimum((i + 1) * r8, nblk * r8 - 1), c0 // W))

    def full(a):
        nd = a.ndim
        return pl.BlockSpec(a.shape, lambda b, i: (0,) * nd)

    hm = pl.BlockSpec((None, DN_HEADS, tm, DN_DK), lambda b, i: (b, 0, i, 0))

    def hm2(width):
        return pl.BlockSpec((2, None, DN_HEADS, tm, width), lambda b, i: (0, b, 0, i, 0))

    def sh_hm2(width):
        return jax.ShapeDtypeStruct((2, B, DN_HEADS, T, width), BF16)

    sh_hm = jax.ShapeDtypeStruct((B, DN_HEADS, T, DN_DK), BF16)
    return pl.pallas_call(
        functools.partial(_dn_prep_kernel, tm=tm),
        out_shape=(sh_hm, sh_hm, sh_hm2(DN_DK), sh_hm2(2 * DN_DK), sh_hm2(DN_DK), sh_hm2(tm), sh_hm2(tm),
                   jax.ShapeDtypeStruct((2, B, nblk, DN_HEADS, tm // DN_CHUNK, DN_DK), F32)),
        grid=(B, nblk),
        in_specs=[main(COL_BQ), main(COL_BK), main(COL_BV),
                  left(COL_BQ), left(COL_BK), left(COL_BV),
                  right(COL_BQ), right(COL_BK), right(COL_BV),
                  pl.BlockSpec((None, tm, LANES), lambda b, i: (b, i, 0)),
                  full(w_conv), full(alog_row), full(dtb_row), full(mh), full(eg), full(eb), full(tmat),
                  full(ones)],
        out_specs=(hm, hm, hm2(DN_DK), hm2(2 * DN_DK), hm2(DN_DK), hm2(tm), hm2(tm),
                   pl.BlockSpec((2, None, None, DN_HEADS, tm // DN_CHUNK, DN_DK),
                                lambda b, i: (0, b, i, 0, 0, 0))),
        compiler_params=_cparams(("arbitrary", "arbitrary")),
        name="dn_prep",
    )(P, P, P, P, P, P, P, P, P, AB, w_conv, alog_row, dtb_row, mh, eg, eb, tmat, ones)


def _dn_chunk_kernel(q_ref, k_ref, kb_ref, vbk_ref, qdec_ref, kdecbd_ref, decbd_ref,
                     qeff_ref, o0_ref, mk_ref, n_ref):
    n, C = DN_BLOCK, DN_CHUNK
    nc = n // C
    r = lax.broadcasted_iota(jnp.int32, (n, n), 0)
    c = lax.broadcasted_iota(jnp.int32, (n, n), 1)
    off_diag = r != c
    same_chunk = (r // C) == (c // C)
    rp = lax.broadcasted_iota(jnp.int32, (C, n), 0)
    cp = lax.broadcasted_iota(jnp.int32, (C, n), 1)
    eye_packed = (rp == cp % C).astype(F32)
    own_values = (lax.broadcasted_iota(jnp.int32, (n, 2 * nc * C), 0) // C
                  == lax.broadcasted_iota(jnp.int32, (n, 2 * nc * C), 1) // (2 * C))
    heads = range(DN_HEADS)

    def block_diag(xp):
        xb = xp.astype(BF16)
        return jnp.where(same_chunk, jnp.concatenate([xb] * nc, axis=0), jnp.zeros((), BF16))

    low, qk = [], []
    for h in heads:
        dec = decbd_ref[h].astype(F32)
        k = k_ref[h]
        a = jnp.where(off_diag, _dot_nt(kb_ref[h], k) * dec, 0.0)
        low.append(sum(a[i * C:(i + 1) * C] for i in range(nc)))
        qk.append((_dot_nt(q_ref[h], k) * dec).astype(BF16))
    p = [eye_packed - m for m in low]
    mpow = low
    for _ in range(5):
        mpow = [_dot(m.astype(BF16), block_diag(m)) for m in mpow]
        p = [a + _dot(a.astype(BF16), block_diag(m)) for a, m in zip(p, mpow)]
    uw = []
    for h in heads:
        vals = jnp.where(own_values, jnp.concatenate([vbk_ref[h]] * nc, axis=1), jnp.zeros((), BF16))
        packed = _dot(p[h].astype(BF16), vals)
        uw.append(jnp.concatenate([packed[:, i * 2 * C:(i + 1) * 2 * C] for i in range(nc)],
                                  axis=0).astype(BF16))
    qkuw = [_dot(qk[h], uw[h]) for h in heads]
    for h in heads:
        o0_ref[h] = qkuw[h][:, :DN_DK].astype(o0_ref.dtype)
        qeff_ref[h] = (qdec_ref[h].astype(F32) - qkuw[h][:, DN_DK:]).astype(qeff_ref.dtype)
        kt = _dot_tn(kdecbd_ref[h], uw[h])
        n_ref[h] = kt[:, :DN_DK].astype(n_ref.dtype)
        mk_ref[h] = kt[:, DN_DK:].astype(mk_ref.dtype)


def _dn_chunk(prep):
    q, k, kb, vbk, qdec, kdecbd, decbd, _ = prep
    B, H, T, dk = q.shape
    tm = DN_BLOCK
    hm = pl.BlockSpec((None, H, tm, dk), lambda d, b, i: (b, 0, i, 0))

    def hm2(width):
        return pl.BlockSpec((None, None, H, tm, width), lambda d, b, i: (d, b, 0, i, 0))

    def sh(dtype):
        return jax.ShapeDtypeStruct((2, B, H, T, dk), dtype)

    return pl.pallas_call(
        _dn_chunk_kernel,
        out_shape=(sh(BF16), sh(BF16), sh(BF16), sh(BF16)),
        grid=(2, B, T // tm),
        in_specs=[hm, hm, hm2(dk), hm2(2 * dk), hm2(dk), hm2(tm), hm2(tm)],
        out_specs=(hm2(dk), hm2(dk), hm2(dk), hm2(dk)),
        compiler_params=_cparams(("arbitrary", "arbitrary", "arbitrary")),
        name="dn_chunk",
    )(q, k, kb, vbk, qdec, kdecbd, decbd)


def _dn_scan_kernel(*refs, n_batch):
    ins = (refs[0:5], refs[5:10])
    o_refs = refs[10:12]
    s_scr = refs[12]
    t = pl.program_id(0)
    n_chunks = DN_BLOCK // DN_CHUNK
    C = DN_CHUNK

    @pl.when(t == 0)
    def _():
        s_scr[...] = jnp.zeros_like(s_scr)

    for step in range(n_chunks):
        for d in range(2):
            ci = step if d == 0 else n_chunks - 1 - step
            rows = slice(ci * C, (ci + 1) * C)
            qeff_ref, o0_ref, mk_ref, n_ref, egl_ref = ins[d]
            for b in range(n_batch):
                for h in range(DN_HEADS):
                    s = s_scr[d, b, h]
                    sb = s.astype(BF16)
                    o_refs[d][b, h, rows, :] = (_dot(qeff_ref[b, h, rows, :], sb)
                                                + o0_ref[b, h, rows, :]).astype(o_refs[d].dtype)
                    s_scr[d, b, h] = (s * egl_ref[b, h, ci:ci + 1, :] - _dot(mk_ref[b, h, rows, :], sb)
                                      + n_ref[b, h, rows, :])


def _dn_scan(maps, egl):
    qeff, o0, mk, nn = maps
    _, B, H, T, dk = qeff.shape
    tm = DN_BLOCK
    nblk = T // tm

    def blk(d, t):
        return jnp.where(t == 0, nblk - 1, nblk - 1 - t if d == 1 else t - 1)

    in_specs, args = [], []
    for d in range(2):
        for a in (qeff, o0, mk, nn):
            in_specs.append(pl.BlockSpec((None, B, H, tm, dk), lambda t, d=d: (d, 0, 0, blk(d, t), 0)))
            args.append(a)
        in_specs.append(pl.BlockSpec((None, B, None, H, tm // DN_CHUNK, dk),
                                     lambda t, d=d: (d, 0, blk(d, t), 0, 0, 0)))
        args.append(egl)
    out_specs = tuple(pl.BlockSpec((B, H, tm, dk), lambda t, d=d: (0, 0, blk(d, t), 0)) for d in range(2))
    return pl.pallas_call(
        functools.partial(_dn_scan_kernel, n_batch=B),
        out_shape=(jax.ShapeDtypeStruct((B, H, T, dk), BF16),) * 2,
        grid=(nblk,),
        in_specs=in_specs,
        out_specs=out_specs,
        scratch_shapes=[pltpu.VMEM((2, B, H, dk, dk), F32)],
        compiler_params=_cparams(("arbitrary",)),
        name="dn_scan",
    )(*args)


def _merge_kernel(a_ref, of_ref, ob_ref, z_ref, c_ref, ga_ref, gb_ref, gc_ref, h_ref, mod_ref, g2_ref, gdn_ref,
                  wb_ref, wo_ref, h_out_ref, v_out_ref, b_scr, *, tm, n_lat):
    i = pl.program_id(1)
    D = h_ref.shape[-1]
    z = z_ref[...].astype(F32)
    for hd in range(DN_HEADS):
        o = of_ref[hd].astype(F32) + ob_ref[hd].astype(F32)
        y = o * lax.rsqrt(jnp.mean(o * o, axis=-1, keepdims=True) + EPS) * gdn_ref[...]
        b_scr[:, hd * DN_DK:(hd + 1) * DN_DK] = y * _silu(z[:, hd * DN_DK:(hd + 1) * DN_DK])
    m = _sigmoid(ga_ref[...].astype(F32)) * _dot(a_ref[...], wb_ref[0])
    m = m + _sigmoid(gb_ref[...].astype(F32)) * _dot(b_scr[...].astype(BF16), wb_ref[1])
    m = m + _sigmoid(gc_ref[...].astype(F32)) * _dot(c_ref[...], wb_ref[2])
    y = _dot(m.astype(BF16), wo_ref[...])
    gate = jnp.where(_is_ctx_row(i, tm, n_lat), mod_ref[0:1, 2 * D:3 * D], mod_ref[1:2, 2 * D:3 * D])
    h_new = h_ref[...] + gate * y
    h_out_ref[...] = h_new
    v_out_ref[...] = _norm_mod(h_new, g2_ref[...], mod_ref, i, tm, n_lat, 3, 4).astype(BF16)


def _merge(br_a, o_f, o_b, br_c, P, h, modtab, g_norm2, g_dn_out, w_branch, w_out, n_lat, tm):
    B, T, D = h.shape
    br = pl.BlockSpec((None, tm, BR_W), lambda b, i: (b, i, 0))
    hm = pl.BlockSpec((None, DN_HEADS, tm, DN_DK), lambda b, i: (b, 0, i, 0))

    def gate(j):
        return pl.BlockSpec((None, tm, D), lambda b, i: (b, i, COL_GATE // D + j))

    tok = pl.BlockSpec((None, tm, D), lambda b, i: (b, i, 0))
    return pl.pallas_call(
        functools.partial(_merge_kernel, tm=tm, n_lat=n_lat),
        out_shape=(jax.ShapeDtypeStruct((B, T, D), F32), jax.ShapeDtypeStruct((B, T, D), BF16)),
        grid=(B, T // tm),
        in_specs=[br, hm, hm,
                  pl.BlockSpec((None, tm, BR_W), lambda b, i: (b, i, COL_BZ // BR_W)),
                  br, gate(0), gate(1), gate(2), tok,
                  pl.BlockSpec((None, 2, 6 * D), lambda b, i: (b, 0, 0)),
                  pl.BlockSpec((1, D), lambda b, i: (0, 0)),
                  pl.BlockSpec((1, DN_DK), lambda b, i: (0, 0)),
                  pl.BlockSpec((3, BR_W, D), lambda b, i: (0, 0, 0)),
                  pl.BlockSpec((D, D), lambda b, i: (0, 0))],
        out_specs=(tok, tok),
        scratch_shapes=[pltpu.VMEM((tm, BR_W), F32)],
        compiler_params=_cparams(("arbitrary", "arbitrary")),
        name="merge",
    )(br_a, o_f, o_b, P, br_c, P, P, P, h, modtab, g_norm2.reshape(1, D), g_dn_out.reshape(1, DN_DK),
      w_branch, w_out)


def _router_kernel(v_ref, wr_ref, br_ref, perm_ref, c_ref):
    tn = v_ref.shape[0]
    G = N_GROUPS
    scores = jax.nn.sigmoid(_dot_nt(wr_ref[...], v_ref[...]))
    sel = scores + br_ref[...]
    slabs = [sel[j * G:(j + 1) * G] for j in range(GROUP_SIZE)]
    m1, m2 = slabs[0], jnp.full((G, tn), -jnp.inf, F32)
    for j in range(1, GROUP_SIZE):
        m2 = jnp.maximum(m2, jnp.minimum(m1, slabs[j]))
        m1 = jnp.maximum(m1, slabs[j])
    gs = m1 + m2
    gidx = lax.broadcasted_iota(jnp.int32, (G, tn), 0)
    grank = jnp.zeros((G, tn), jnp.int32)
    for g in range(G):
        rowv = gs[g:g + 1]
        beats = jnp.where(rowv > gs, 1, jnp.where(jnp.logical_and(rowv == gs, g < gidx), 1, 0))
        grank = grank + beats
    gmask = grank < TOPK_GROUPS
    masked = jnp.concatenate([jnp.where(gmask, s, -jnp.inf) for s in slabs], axis=0)
    pos = lax.broadcasted_iota(jnp.int32, (N_EXPERTS, tn), 0)
    orig = (pos % G) * GROUP_SIZE + pos // G
    w = jnp.zeros((N_EXPERTS, tn), F32)
    for _ in range(TOP_K):
        best = jnp.max(masked, axis=0, keepdims=True)
        first = jnp.min(jnp.where(masked == best, orig, N_EXPERTS), axis=0, keepdims=True)
        pick = orig == first
        w = jnp.where(pick, scores, w)
        masked = jnp.where(pick, -jnp.inf, masked)
    denom = jnp.sum(w, axis=0, keepdims=True) + 1e-20
    c = w / denom * ROUTED_SCALE
    c_t = sum(_dot_tn(part, perm_ref[...]) for part in _split3(c))
    for g in range(N_EXPERTS // MOE_EPS):
        c_ref[g] = c_t[:, g * MOE_EPS:(g + 1) * MOE_EPS]


def _router(v_flat, w_router, b_router):
    N, D = v_flat.shape
    tn = 1280 if N % 1280 == 0 else 256
    perm = np.array([(p % N_GROUPS) * GROUP_SIZE + p // N_GROUPS for p in range(N_EXPERTS)])
    wr_t = w_router.T[perm].astype(BF16)
    br = b_router[perm].reshape(N_EXPERTS, 1)
    unperm = jnp.asarray(np.eye(N_EXPERTS)[perm], BF16)
    ngrp = N_EXPERTS // MOE_EPS
    return pl.pallas_call(
        _router_kernel,
        out_shape=jax.ShapeDtypeStruct((ngrp, N, MOE_EPS), F32),
        grid=(N // tn,),
        in_specs=[pl.BlockSpec((tn, D), lambda i: (i, 0)),
                  pl.BlockSpec((N_EXPERTS, D), lambda i: (0, 0)),
                  pl.BlockSpec((N_EXPERTS, 1), lambda i: (0, 0)),
                  pl.BlockSpec((N_EXPERTS, N_EXPERTS), lambda i: (0, 0))],
        out_specs=pl.BlockSpec((ngrp, tn, MOE_EPS), lambda i: (0, i, 0)),
        compiler_params=_cparams(("arbitrary",)),
        name="router",
    )(v_flat, wr_t, br, unperm)


MOE_EPS = 8


def _moe_kernel(x_ref, c_ref, wgu_ref, wd_ref, sgu_ref, sd_ref, h_ref, mod_ref,
                o_ref, acc_ref, *, tm, n_lat):
    i = pl.program_id(1)
    e = pl.program_id(2)
    x = x_ref[...]
    D = x.shape[-1]
    F = MOE_FF

    @pl.when(e == 0)
    def _():
        gu = _dot(x, sgu_ref[...])
        hs = _silu(gu[:, :F]) * gu[:, F:]
        acc_ref[...] = _dot(hs.astype(BF16), sd_ref[...])

    c = c_ref[...]
    parts = []
    for j in range(MOE_EPS):
        gu = _dot(x, wgu_ref[j])
        hj = _silu(gu[:, :F]) * gu[:, F:] * c[:, j:j + 1]
        parts.append(hj.astype(BF16))
    hcat = jnp.concatenate(parts, axis=-1)
    acc_ref[...] += _dot(hcat, wd_ref[...])

    @pl.when(e == pl.num_programs(2) - 1)
    def _():
        gate = jnp.where(_is_ctx_row(i, tm, n_lat), mod_ref[0:1, 5 * D:6 * D], mod_ref[1:2, 5 * D:6 * D])
        o_ref[...] = h_ref[...] + gate * acc_ref[...]


def _moe_weights(w_gate, w_up, w_down):
    E, D, F = w_gate.shape
    gu = jnp.concatenate([w_gate, w_up], axis=-1).astype(BF16)
    return gu, w_down.astype(BF16).reshape(E * F, D)


def _moe(v, c_grp, h, modtab, wgu, wd, sgu, sd, n_lat, tm):
    B, T, D = v.shape
    F = MOE_FF
    ngrp = N_EXPERTS // MOE_EPS
    tok = pl.BlockSpec((None, tm, D), lambda b, i, e: (b, i, 0))
    return pl.pallas_call(
        functools.partial(_moe_kernel, tm=tm, n_lat=n_lat),
        out_shape=jax.ShapeDtypeStruct((B, T, D), F32),
        grid=(B, T // tm, ngrp),
        in_specs=[tok,
                  pl.BlockSpec((None, None, tm, MOE_EPS), lambda b, i, e: (e, b, i, 0)),
                  pl.BlockSpec((MOE_EPS, D, 2 * F), lambda b, i, e: (e, 0, 0)),
                  pl.BlockSpec((MOE_EPS * F, D), lambda b, i, e: (e, 0)),
                  pl.BlockSpec((D, 2 * F), lambda b, i, e: (0, 0)),
                  pl.BlockSpec((F, D), lambda b, i, e: (0, 0)),
                  tok,
                  pl.BlockSpec((None, 2, 6 * D), lambda b, i, e: (b, 0, 0))],
        out_specs=tok,
        scratch_shapes=[pltpu.VMEM((tm, D), F32)],
        compiler_params=_cparams(("arbitrary", "arbitrary", "arbitrary")),
        name="moe",
    )(v, c_grp, wgu, wd, sgu, sd, h, modtab)


def _final_kernel(h_ref, g_ref, o_ref):
    x = h_ref[...]
    o_ref[...] = x * lax.rsqrt(jnp.mean(x * x, axis=-1, keepdims=True) + EPS) * g_ref[...]


def _final_norm(h, g_final, S):
    B, T, D = h.shape
    tm = 1024
    return pl.pallas_call(
        _final_kernel,
        out_shape=jax.ShapeDtypeStruct((B, S, D), F32),
        grid=(B, S // tm),
        in_specs=[pl.BlockSpec((None, tm, D), lambda b, i: (b, i, 0)),
                  pl.BlockSpec((1, D), lambda b, i: (0, 0))],
        out_specs=pl.BlockSpec((None, tm, D), lambda b, i: (b, i, 0)),
        compiler_params=_cparams(("arbitrary", "arbitrary")),
        name="final_norm",
    )(h, g_final.reshape(1, D))


def _rope_tables(S, n_ctx):
    rows = S // GRID_W
    row = jnp.repeat(jnp.arange(rows, dtype=F32), GRID_W)
    col = (jnp.arange(S) % GRID_W).astype(F32)
    axis_dim = HEAD_DIM // 2
    inv = 1.0 / (ROPE_THETA ** (jnp.arange(0, axis_dim, 2, dtype=F32) / axis_dim))
    ang = jnp.concatenate([row[:, None] * inv, col[:, None] * inv], axis=-1)
    ang = jnp.concatenate([ang, ang], axis=-1)
    cos = jnp.concatenate([jnp.cos(ang), jnp.ones((n_ctx, HEAD_DIM), F32)], axis=0)
    sin = jnp.concatenate([jnp.sin(ang), jnp.zeros((n_ctx, HEAD_DIM), F32)], axis=0)
    first = (jnp.arange(HEAD_DIM) < HEAD_DIM // 2)[None, :]
    sin_a = jnp.where(first, -sin, 0.0)
    sin_b = jnp.where(first, 0.0, sin)
    return tuple(jnp.tile(t, (1, 2)) for t in (cos, sin_a, sin_b))


def _permute_w_in(w):
    splits = np.cumsum([512, 128, 128, 1536, 512, 16, 16, 512, 512, 512, 3072])[:-1].tolist()
    aq, ak, av, bqkv, bz, ba, bb, cq, ck, cv, gate = jnp.split(w, splits, axis=-1)
    pad = jnp.zeros((w.shape[0], IN_W_PAD - COL_AB - 32), w.dtype)
    out = jnp.concatenate([aq, cq, ck, cv, bz, bqkv, gate, ak, av, ba, bb, pad], axis=-1)
    return out.astype(BF16)


def kernel(x, c, ctx, c_ctx, w_mod, b_mod, g_norm1, g_norm2, w_in, g_qnorm, g_knorm, w_conv, a_log, dt_bias, g_dn_out, lam_qk, g_subln, w_branch, w_out, w_router, b_router, w_e_gate, w_e_up, w_e_down, w_s_gate, w_s_up, w_s_down, g_final):
    B, S, D = x.shape
    n_ctx = ctx.shape[1]
    T = n_ctx + S
    L = w_mod.shape[0]
    N = B * T
    assert n_ctx == DN_BLOCK and S % 1024 == 0 and D == D_MODEL
    tm_tok = 1280 if T % 1280 == 0 else 256
    tm_merge = 640 if T % 640 == 0 else 256

    cond = jnp.zeros((8, D), F32).at[0].set(c_ctx).at[1:1 + B].set(c)
    mod = _mod_vectors(cond, w_mod, b_mod)
    tabs = _rope_tables(S, n_ctx)
    h = jnp.concatenate([x, ctx], axis=1)
    tk = 1280 if T % 1280 == 0 else 256

    for l in range(L):
        lambda_init = 0.8 - 0.6 * math.exp(-0.3 * l)
        modtab = jnp.stack([jnp.broadcast_to(mod[l, 0], (B, 6 * D)), mod[l, 1:1 + B]], axis=1)
        P, AB = _in_projection(h, modtab, g_norm1[l], _permute_w_in(w_in[l]), S, tm_tok)

        qa, ka, vta, qd, kd, vtd = _attn_prep(P, tabs, g_qnorm[l], g_knorm[l], tk)
        dx = (lam_qk[l], g_subln[l].reshape(LANES, 1))
        br_a = jnp.concatenate([_flash("gqa", qa, ka, vta, n_ctx, False),
                                _flash("gqa", qa, ka, vta, n_ctx, True)], axis=1)
        br_c = jnp.concatenate([_flash("diff", qd, kd, vtd, n_ctx, False, dx, lambda_init),
                                _flash("diff", qd, kd, vtd, n_ctx, True, dx, lambda_init)], axis=1)

        prep = _dn_prep(P, AB, w_conv[l], a_log[l], dt_bias[l])
        o_f, o_b = _dn_scan(_dn_chunk(prep), prep[-1])

        h, v = _merge(br_a, o_f, o_b, br_c, P, h, modtab, g_norm2[l], g_dn_out[l], w_branch[l].astype(BF16),
                      w_out[l].astype(BF16), S, tm_merge)

        c_grp = _router(v.reshape(N, D), w_router[l], b_router[l]).reshape(-1, B, T, MOE_EPS)
        wgu, wd = _moe_weights(w_e_gate[l], w_e_up[l], w_e_down[l])
        sgu = jnp.concatenate([w_s_gate[l], w_s_up[l]], axis=-1).astype(BF16)
        h = _moe(v, c_grp, h, modtab, wgu, wd, sgu, w_s_down[l].astype(BF16), S, tm_merge)

    return _final_norm(h, g_final, S)
```
